```python
import math
import jax, jax.numpy as jnp
from jax import lax
import numpy as np

D_MODEL = 2048
BATCH = 1
SEQ = 16384
DEPTH = 1

GDN_HEADS = 8
GDN_DK = 128
GDN_DV = 128
GDN_CONV = 4
GDN_CHUNK = 64
GDN_QK = GDN_HEADS * GDN_DK
GDN_V = GDN_HEADS * GDN_DV
GDN_CONV_CH = 2 * GDN_QK + GDN_V
SWA_HEADS = 16
SWA_KV_HEADS = 4
SWA_HEAD_DIM = 64
SWA_WINDOW = 128
SWA_BLOCK = 128
SWA_Q = SWA_HEADS * SWA_HEAD_DIM
SWA_KV = SWA_KV_HEADS * SWA_HEAD_DIM
REL_BUCKETS = 32
REL_MAX_DIST = 128
EPS = 1e-6
IN_SPLITS = (GDN_QK, GDN_QK, GDN_V, GDN_V, GDN_HEADS, GDN_HEADS,
             SWA_Q, SWA_KV, SWA_KV, SWA_Q, D_MODEL, D_MODEL)
IN_COLS = sum(IN_SPLITS)

kernel_name = "hybrid_gdn_swa_gated_merge_block"


def split_cols(a, sizes):
    offs = np.cumsum(sizes)[:-1].tolist()
    return jnp.split(a, offs, axis=-1)


def rms_norm(x, gain):
    xf = x.astype(jnp.float32)
    y = xf * lax.rsqrt(jnp.mean(xf * xf, axis=-1, keepdims=True) + EPS)
    return (y * gain.astype(jnp.float32)).astype(x.dtype)


def l2_norm(x):
    xf = x.astype(jnp.float32)
    return xf * lax.rsqrt(jnp.sum(xf * xf, axis=-1, keepdims=True) + EPS)


def causal_depthwise_conv(x, w):
    K, C = w.shape
    return lax.conv_general_dilated(
        x, w[:, None, :].astype(x.dtype), window_strides=(1,), padding=[(K - 1, 0)],
        dimension_numbers=("NWC", "WIO", "NWC"), feature_group_count=C)


def gated_delta_rule_chunked(q, k, v, g, beta):
    B, T, H, dk = q.shape
    dv = v.shape[-1]
    C = GDN_CHUNK
    N = T // C

    def chunk(a):
        return a.reshape(B, N, C, H, *a.shape[3:]).swapaxes(2, 3)

    q, k, v, beta = chunk(q), chunk(k), chunk(v), chunk(beta)
    g = jnp.cumsum(chunk(g), axis=-1)
    causal = jnp.tril(jnp.ones((C, C), bool))
    strict = jnp.tril(jnp.ones((C, C), bool), -1)
    gdiff = g[..., :, None] - g[..., None, :]
    decay = jnp.where(causal, jnp.exp(jnp.where(causal, gdiff, 0.0)), 0.0)
    k_beta = k * beta[..., None]
    v_beta = v * beta[..., None]
    L = jnp.where(strict, jnp.einsum("bnhid,bnhjd->bnhij", k_beta, k) * decay, 0.0)
    eye = jnp.eye(C, dtype=jnp.float32)
    t_inv = lax.linalg.triangular_solve(eye + L, jnp.broadcast_to(eye, L.shape),
                                        left_side=True, lower=True, unit_diagonal=True)
    u = jnp.einsum("bnhij,bnhje->bnhie", t_inv, v_beta)
    w = jnp.einsum("bnhij,bnhjd->bnhid", t_inv, k_beta * jnp.exp(g)[..., None])
    attn_intra = jnp.where(causal, jnp.einsum("bnhid,bnhjd->bnhij", q, k) * decay, 0.0)
    g_last = g[..., -1]
    q_dec = q * jnp.exp(g)[..., None]
    k_dec = k * jnp.exp(g_last[..., None] - g)[..., None]

    def step(S, inp):
        qd, kd, u_c, w_c, a_c, gl = inp
        v_new = u_c - jnp.einsum("bhcd,bhde->bhce", w_c, S)
        o = jnp.einsum("bhcd,bhde->bhce", qd, S) + jnp.einsum("bhij,bhje->bhie", a_c, v_new)
        S = S * jnp.exp(gl)[..., None, None] + jnp.einsum("bhcd,bhce->bhde", kd, v_new)
        return S, o

    xs = (jnp.moveaxis(q_dec, 1, 0), jnp.moveaxis(k_dec, 1, 0), jnp.moveaxis(u, 1, 0),
          jnp.moveaxis(w, 1, 0), jnp.moveaxis(attn_intra, 1, 0), jnp.moveaxis(g_last, 1, 0))
    S0 = jnp.zeros((B, H, dk, dv), jnp.float32)
    _, o = lax.scan(step, S0, xs)
    return jnp.moveaxis(o, 0, 1).swapaxes(2, 3).reshape(B, T, H, dv)


def t5_bucket(dist):
    max_exact = REL_BUCKETS // 2
    d = jnp.maximum(dist, 1).astype(jnp.float32)
    large = max_exact + (jnp.log(d / max_exact) / math.log(REL_MAX_DIST / max_exact)
                         * (REL_BUCKETS - max_exact)).astype(jnp.int32)
    large = jnp.minimum(large, REL_BUCKETS - 1)
    return jnp.where(dist < max_exact, dist, large)


def sliding_window_attention(q, k, v, sinks, rel_bias):
    B, T, Hq, d = q.shape
    Hkv = k.shape[2]
    G = Hq // Hkv
    Q = SWA_BLOCK
    N = T // Q
    qb = q.reshape(B, N, Q, Hkv, G, d)

    def band(a):
        a = a.reshape(B, N, Q, Hkv, d)
        prev = jnp.pad(a, ((0, 0), (1, 0), (0, 0), (0, 0), (0, 0)))[:, :-1]
        return jnp.concatenate([prev, a], axis=2)

    kb, vb = band(k), band(v)
    logits = jnp.einsum("bnqhgd,bnkhd->bnhgqk", qb, kb).astype(jnp.float32) * (d ** -0.5)
    qpos = jnp.arange(Q)[:, None] + Q
    kpos = jnp.arange(2 * Q)[None, :]
    dist = qpos - kpos
    in_window = (dist >= 0) & (dist < SWA_WINDOW)
    blk = jnp.arange(N)[:, None, None]
    valid = in_window[None] & ((blk * Q + kpos[None] - Q) >= 0)
    bias = rel_bias[t5_bucket(jnp.maximum(dist, 0))]
    bias = bias.transpose(2, 0, 1).reshape(Hkv, G, Q, 2 * Q).astype(jnp.float32)
    logits = jnp.where(valid[None, :, None, None], logits + bias[None, None], -jnp.inf)
    sink = sinks.astype(jnp.float32).reshape(Hkv, G)[None, None, :, :, None, None]
    m = jnp.maximum(jnp.max(logits, axis=-1, keepdims=True), sink)
    p = jnp.exp(logits - m)
    probs = (p / (jnp.sum(p, axis=-1, keepdims=True) + jnp.exp(sink - m))).astype(v.dtype)
    out = jnp.einsum("bnhgqk,bnkhd->bnqhgd", probs, vb)
    return out.reshape(B, T, Hq * d)


def setup_inputs(seed: int = 0) -> dict:
    key = jax.random.key(seed)
    ks = jax.random.split(key, 20)
    f32 = jnp.float32

    def normal(k, shape, scale):
        return jax.random.normal(k, shape, f32) * scale

    x = normal(ks[0], (BATCH, SEQ, D_MODEL), 1.0)
    c = normal(ks[1], (BATCH, D_MODEL), 1.0)
    w_ada = normal(ks[2], (DEPTH, D_MODEL, 3 * D_MODEL), 0.1 * D_MODEL ** -0.5)
    b_ada = normal(ks[3], (DEPTH, 3 * D_MODEL), 0.02)
    norm_gain = 1.0 + normal(ks[4], (DEPTH, D_MODEL), 0.02)
    w_in = normal(ks[5], (DEPTH, D_MODEL, IN_COLS), D_MODEL ** -0.5)
    conv_w = normal(ks[6], (DEPTH, GDN_CONV, GDN_CONV_CH), GDN_CONV ** -0.5)
    a_log = jnp.log(jax.random.uniform(ks[7], (DEPTH, GDN_HEADS), f32, 1.0, 16.0))
    dt = jnp.exp(jax.random.uniform(ks[8], (DEPTH, GDN_HEADS), f32, math.log(1e-3), math.log(1e-1)))
    dt_bias = dt + jnp.log(-jnp.expm1(-dt))
    gdn_norm_gain = 1.0 + normal(ks[9], (DEPTH, GDN_DV), 0.02)
    q_norm_gain = 1.0 + normal(ks[10], (DEPTH, SWA_HEAD_DIM), 0.02)
    k_norm_gain = 1.0 + normal(ks[11], (DEPTH, SWA_HEAD_DIM), 0.02)
    sinks = normal(ks[12], (DEPTH, SWA_HEADS), 0.5)
    rel_bias = normal(ks[13], (REL_BUCKETS, SWA_HEADS), 0.5)
    w_branch_gdn = normal(ks[14], (DEPTH, GDN_V, D_MODEL), GDN_V ** -0.5)
    w_branch_swa = normal(ks[15], (DEPTH, SWA_Q, D_MODEL), SWA_Q ** -0.5)
    w_out = normal(ks[16], (DEPTH, D_MODEL, D_MODEL), D_MODEL ** -0.5)
    return {"x": x, "c": c, "w_ada": w_ada, "b_ada": b_ada, "norm_gain": norm_gain,
            "w_in": w_in, "conv_w": conv_w, "a_log": a_log, "dt_bias": dt_bias,
            "gdn_norm_gain": gdn_norm_gain, "q_norm_gain": q_norm_gain,
            "k_norm_gain": k_norm_gain, "sinks": sinks, "rel_bias": rel_bias,
            "w_branch_gdn": w_branch_gdn, "w_branch_swa": w_branch_swa, "w_out": w_out}


def reference(x, c, w_ada, b_ada, norm_gain, w_in, conv_w, a_log, dt_bias, gdn_norm_gain,
              q_norm_gain, k_norm_gain, sinks, rel_bias, w_branch_gdn, w_branch_swa, w_out):
    B, T, _ = x.shape
    c_act = jax.nn.silu(c)
    for l in range(DEPTH):
        shift, scale, gate = jnp.split(c_act @ w_ada[l] + b_ada[l], 3, axis=-1)
        h = rms_norm(x, norm_gain[l]) * (1.0 + scale[:, None, :]) + shift[:, None, :]
        (q_a, k_a, v_a, z_a, b_a, a_a, q_b, k_b, v_b, z_b, gl_a, gl_b) = split_cols(h @ w_in[l], IN_SPLITS)

        qkv = jax.nn.silu(causal_depthwise_conv(jnp.concatenate([q_a, k_a, v_a], axis=-1), conv_w[l]))
        qa, ka, va = split_cols(qkv, (GDN_QK, GDN_QK, GDN_V))
        qa = l2_norm(qa.reshape(B, T, GDN_HEADS, GDN_DK)) * (GDN_DK ** -0.5)
        ka = l2_norm(ka.reshape(B, T, GDN_HEADS, GDN_DK))
        va = va.reshape(B, T, GDN_HEADS, GDN_DV).astype(jnp.float32)
        beta = jax.nn.sigmoid(b_a.astype(jnp.float32))
        g = -jnp.exp(a_log[l].astype(jnp.float32)) * jax.nn.softplus(
            a_a.astype(jnp.float32) + dt_bias[l].astype(jnp.float32))
        o_a = gated_delta_rule_chunked(qa, ka, va, g, beta)
        o_a = rms_norm(o_a, gdn_norm_gain[l]).reshape(B, T, GDN_V).astype(x.dtype) * jax.nn.silu(z_a)
        y_gdn = o_a @ w_branch_gdn[l]

        qb = rms_norm(q_b.reshape(B, T, SWA_HEADS, SWA_HEAD_DIM), q_norm_gain[l])
        kb = rms_norm(k_b.reshape(B, T, SWA_KV_HEADS, SWA_HEAD_DIM), k_norm_gain[l])
        vb = v_b.reshape(B, T, SWA_KV_HEADS, SWA_HEAD_DIM)
        o_b = sliding_window_attention(qb, kb, vb, sinks[l], rel_bias)
        y_swa = (o_b * jax.nn.silu(z_b)) @ w_branch_swa[l]

        mixed = jax.nn.sigmoid(gl_a) * y_gdn + jax.nn.sigmoid(gl_b) * y_swa
        x = x + gate[:, None, :] * (mixed @ w_out[l])
    return x
```

```python
import functools
import math

import jax
import jax.numpy as jnp
import numpy as np
from jax import lax
from jax.experimental import pallas as pl
from jax.experimental.pallas import tpu as pltpu

F32 = jnp.float32
BF16 = jnp.bfloat16

LANES = 128
D_MODEL = 2048
GDN_HEADS = 8
GDN_DK = 128
GDN_DV = 128
GDN_CONV = 4
GDN_QK = GDN_HEADS * GDN_DK
GDN_V = GDN_HEADS * GDN_DV
GDN_CONV_CH = 2 * GDN_QK + GDN_V
GDN_BLOCK = 128
SWA_HEADS = 16
SWA_KV_HEADS = 4
SWA_HEAD_DIM = 64
SWA_WINDOW = 128
SWA_BLOCK = 128
SWA_Q = SWA_HEADS * SWA_HEAD_DIM
SWA_KV = SWA_KV_HEADS * SWA_HEAD_DIM
REL_BUCKETS = 32
REL_MAX_DIST = 128
EPS = 1e-6
NEG_BIG = -1e30

COL_QKV = 0
COL_ZA = GDN_CONV_CH
COL_QB = COL_ZA + GDN_V
COL_ZB = COL_QB + SWA_Q
COL_GA = COL_ZB + SWA_Q
COL_GB = COL_GA + D_MODEL
COL_KB = COL_GB + D_MODEL
COL_VB = COL_KB + SWA_KV
PROJ_COLS = COL_VB + SWA_KV
for _col, _width in ((COL_ZA, GDN_V), (COL_QB, SWA_Q), (COL_ZB, SWA_Q), (COL_GA, D_MODEL),
                     (COL_GB, D_MODEL), (COL_KB, SWA_KV), (COL_VB, SWA_KV)):
    assert _col % _width == 0

VMEM_LIMIT = 56 * 1024 * 1024


def _sigmoid(x):
    return 1.0 / (1.0 + jnp.exp(-x))


def _silu(x):
    return x * _sigmoid(x)


def _params(sem):
    return pltpu.CompilerParams(dimension_semantics=sem, vmem_limit_bytes=VMEM_LIMIT)


def _ada_mod_kernel(c_ref, w_ref, b_ref, o_ref):
    c = c_ref[...]
    o_ref[...] = jnp.sum(_silu(c) * w_ref[...], axis=0, keepdims=True) + b_ref[...]


def _ada_mod(c_col, w_ada, b_ada):
    d, n = w_ada.shape
    tn = 512
    return pl.pallas_call(
        _ada_mod_kernel,
        out_shape=jax.ShapeDtypeStruct((1, n), F32),
        grid=(n // tn,),
        in_specs=[pl.BlockSpec((d, 1), lambda j: (0, 0)),
                  pl.BlockSpec((d, tn), lambda j: (0, j)),
                  pl.BlockSpec((1, tn), lambda j: (0, j))],
        out_specs=pl.BlockSpec((1, tn), lambda j: (0, j)),
        compiler_params=_params(("arbitrary",)),
        name="ada_mod",
    )(c_col, w_ada, b_ada)


def _in_proj_kernel(x_ref, gain_ref, scale_ref, shift_ref, w_ref, ws_ref, o_ref, ba_ref, h_ref,
                    *, row_chunk):
    j = pl.program_id(1)

    @pl.when(j == 0)
    def _():
        gs = gain_ref[...] * (1.0 + scale_ref[...])
        sh = shift_ref[...]
        tm = x_ref.shape[0]

        def body(r, carry):
            rows = pl.ds(pl.multiple_of(r * row_chunk, row_chunk), row_chunk)
            x = x_ref[rows, :]
            ms = jnp.mean(x * x, axis=-1, keepdims=True)
            h = (x * lax.rsqrt(ms + EPS)) * gs + sh
            h_ref[rows, :] = h.astype(BF16)
            return carry

        lax.fori_loop(0, tm // row_chunk, body, 0)
        ba_ref[...] = jnp.dot(h_ref[...], ws_ref[...], preferred_element_type=F32)

    o_ref[...] = jnp.dot(h_ref[...], w_ref[...], preferred_element_type=F32).astype(o_ref.dtype)


def _in_proj(x2d, gain, scale, shift, w_big, w_small, *, tm, tn):
    t, d = x2d.shape
    n = w_big.shape[1]
    return pl.pallas_call(
        functools.partial(_in_proj_kernel, row_chunk=128),
        out_shape=(jax.ShapeDtypeStruct((t, n), BF16), jax.ShapeDtypeStruct((t, LANES), F32)),
        grid=(t // tm, n // tn),
        in_specs=[pl.BlockSpec((tm, d), lambda i, j: (i, 0)),
                  pl.BlockSpec((1, d), lambda i, j: (0, 0)),
                  pl.BlockSpec((1, d), lambda i, j: (0, 0)),
                  pl.BlockSpec((1, d), lambda i, j: (0, 0)),
                  pl.BlockSpec((d, tn), lambda i, j: (0, j)),
                  pl.BlockSpec((d, LANES), lambda i, j: (0, 0))],
        out_specs=(pl.BlockSpec((tm, tn), lambda i, j: (i, j)),
                   pl.BlockSpec((tm, LANES), lambda i, j: (i, 0))),
        scratch_shapes=[pltpu.VMEM((tm, d), BF16)],
        compiler_params=_params(("arbitrary", "arbitrary")),
        name="in_proj",
    )(x2d, gain, scale, shift, w_big, w_small)


def _swa_bias_kernel(tab_ref, o_ref):
    h = pl.program_id(0)
    q = SWA_BLOCK
    qpos = lax.broadcasted_iota(jnp.int32, (q, 2 * q), 0) + q
    kpos = lax.broadcasted_iota(jnp.int32, (q, 2 * q), 1)
    dist = qpos - kpos
    in_window = (dist >= 0) & (dist < SWA_WINDOW)
    d = jnp.maximum(dist, 0)
    max_exact = REL_BUCKETS // 2
    df = jnp.maximum(d, 1).astype(F32)
    large = max_exact + (jnp.log(df / max_exact) / math.log(REL_MAX_DIST / max_exact)
                         * (REL_BUCKETS - max_exact)).astype(jnp.int32)
    large = jnp.minimum(large, REL_BUCKETS - 1)
    bucket = jnp.where(d < max_exact, d, large)
    acc = jnp.zeros((q, 2 * q), F32)
    for b in range(REL_BUCKETS):
        acc = jnp.where(bucket == b, tab_ref[h, b], acc)
    o_ref[0] = jnp.where(in_window, acc, NEG_BIG)


def _swa_bias(rel_bias_t):
    q = SWA_BLOCK
    return pl.pallas_call(
        _swa_bias_kernel,
        out_shape=jax.ShapeDtypeStruct((SWA_HEADS, q, 2 * q), F32),
        grid=(SWA_HEADS,),
        in_specs=[pl.BlockSpec(memory_space=pltpu.SMEM)],
        out_specs=pl.BlockSpec((1, q, 2 * q), lambda h: (h, 0, 0)),
        compiler_params=_params(("arbitrary",)),
        name="swa_bias",
    )(rel_bias_t)


def _chunk_cumsum_rows(x):
    n = x.shape[0]
    row = lax.broadcasted_iota(jnp.int32, x.shape, 0)
    s = 1
    while s < n:
        x = x + jnp.where(row >= s, pltpu.roll(x, s, axis=0), 0.0)
        s *= 2
    return x


def _unit_lower_inverse(l_mat, masks_ref):
    eye = masks_ref[0]
    m0 = l_mat * masks_ref[1]
    x = eye - m0
    p = jnp.dot(m0, m0, preferred_element_type=F32)
    x = x + jnp.dot(x, p, preferred_element_type=F32)
    p = jnp.dot(p, p, preferred_element_type=F32)
    x = x + jnp.dot(x, p, preferred_element_type=F32)
    for lvl in range(2, masks_ref.shape[0]):
        n = l_mat * masks_ref[lvl]
        x = x - jnp.dot(x, jnp.dot(n, x, preferred_element_type=F32), preferred_element_type=F32)
    return x


def _gdn_kernel(cur_ref, prev_ref, z_ref, ba_ref, convw_ref, alog_ref, dtb_ref, gain_ref, masks_ref,
                o_ref, xcat_ref, s_ref):
    t = pl.program_id(0)
    c = GDN_BLOCK
    pad = prev_ref.shape[0]

    @pl.when(t == 0)
    def _():
        s_ref[...] = jnp.zeros_like(s_ref)

    keep_prev = (t > 0).astype(F32)
    xcat_ref[0:pad, :] = prev_ref[...].astype(F32) * keep_prev
    xcat_ref[pad:pad + c, :] = cur_ref[...].astype(F32)

    ba = ba_ref[...]
    beta_all = _sigmoid(ba)
    xg = ba + dtb_ref[...]
    softplus = jnp.maximum(xg, 0.0) + jnp.log(1.0 + jnp.exp(-jnp.abs(xg)))
    g_all = -jnp.exp(alog_ref[...]) * softplus
    gc = _chunk_cumsum_rows(g_all)
    gc_t = gc.T
    g_last = gc[c - 1:c, :]
    eg_all = jnp.exp(gc)
    ekd_all = jnp.exp(g_last - gc)
    elast_all = jnp.exp(g_last)

    row = lax.broadcasted_iota(jnp.int32, (c, c), 0)
    col = lax.broadcasted_iota(jnp.int32, (c, c), 1)
    causal = row >= col
    strict = row > col

    def conv_tile(col0):
        acc = None
        for i in range(GDN_CONV):
            xs = xcat_ref[pad - (GDN_CONV - 1) + i: pad - (GDN_CONV - 1) + i + c, col0:col0 + LANES]
            term = xs * convw_ref[i:i + 1, col0:col0 + LANES]
            acc = term if acc is None else acc + term
        return _silu(acc)

    def l2n(a):
        return a * lax.rsqrt(jnp.sum(a * a, axis=-1, keepdims=True) + EPS)

    for h in range(GDN_HEADS):
        q = l2n(conv_tile(h * GDN_DK)) * (GDN_DK ** -0.5)
        k = l2n(conv_tile(GDN_QK + h * GDN_DK))
        v = conv_tile(2 * GDN_QK + h * GDN_DV)
        beta = beta_all[:, h:h + 1]
        gcol = gc[:, GDN_HEADS + h:GDN_HEADS + h + 1]
        grow = gc_t[GDN_HEADS + h:GDN_HEADS + h + 1, :]
        eg = eg_all[:, GDN_HEADS + h:GDN_HEADS + h + 1]
        ekd = ekd_all[:, GDN_HEADS + h:GDN_HEADS + h + 1]
        elast = elast_all[:, GDN_HEADS + h:GDN_HEADS + h + 1]

        decay = jnp.where(causal, jnp.exp(jnp.where(causal, gcol - grow, 0.0)), 0.0)
        kb = k * beta
        vb = v * beta
        kk = lax.dot_general(kb, k, (((1,), (1,)), ((), ())), preferred_element_type=F32)
        qk = lax.dot_general(q, k, (((1,), (1,)), ((), ())), preferred_element_type=F32)
        l_mat = jnp.where(strict, kk * decay, 0.0)
        attn = qk * decay
        t_inv = _unit_lower_inverse(l_mat, masks_ref)
        uw = jnp.dot(t_inv, jnp.concatenate([vb, kb * eg], axis=1), preferred_element_type=F32)
        u = uw[:, :GDN_DV]
        w = uw[:, GDN_DV:]
        s_old = s_ref[h]
        v_new = u - jnp.dot(w, s_old, preferred_element_type=F32)
        o = (jnp.dot(q * eg, s_old, preferred_element_type=F32)
             + jnp.dot(attn, v_new, preferred_element_type=F32))
        s_ref[h] = s_old * elast + lax.dot_general(
            k * ekd, v_new, (((0,), (0,)), ((), ())), preferred_element_type=F32)

        on = o * lax.rsqrt(jnp.mean(o * o, axis=-1, keepdims=True) + EPS) * gain_ref[...]
        z = z_ref[:, h * GDN_DV:(h + 1) * GDN_DV].astype(F32)
        o_ref[:, h * GDN_DV:(h + 1) * GDN_DV] = (on * _silu(z)).astype(o_ref.dtype)


def _gdn_masks():
    c = GDN_BLOCK
    r = np.arange(c)[:, None]
    k = np.arange(c)[None, :]
    mats = [np.eye(c), (r // 8 == k // 8) & (r > k)]
    b = 8
    while b < c:
        mats.append((r // (2 * b) == k // (2 * b)) & ((r // b) % 2 == 1) & ((k // b) % 2 == 0))
        b *= 2
    return jnp.asarray(np.stack([np.asarray(m, np.float32) for m in mats]))


def _gdn(proj, ba, conv_w, alog_lane, dtb_lane, gain):
    t = proj.shape[0]
    c = GDN_BLOCK
    pad = 16
    masks = _gdn_masks()
    nm = masks.shape[0]
    return pl.pallas_call(
        _gdn_kernel,
        out_shape=jax.ShapeDtypeStruct((t, GDN_V), BF16),
        grid=(t // c,),
        in_specs=[pl.BlockSpec((c, GDN_CONV_CH), lambda i: (i, 0)),
                  pl.BlockSpec((pad, GDN_CONV_CH), lambda i: (jnp.maximum(i * (c // pad) - 1, 0), 0)),
                  pl.BlockSpec((c, GDN_V), lambda i: (i, COL_ZA // GDN_V)),
                  pl.BlockSpec((c, LANES), lambda i: (i, 0)),
                  pl.BlockSpec((GDN_CONV, GDN_CONV_CH), lambda i: (0, 0)),
                  pl.BlockSpec((1, LANES), lambda i: (0, 0)),
                  pl.BlockSpec((1, LANES), lambda i: (0, 0)),
                  pl.BlockSpec((1, GDN_DV), lambda i: (0, 0)),
                  pl.BlockSpec((nm, c, c), lambda i: (0, 0, 0))],
        out_specs=pl.BlockSpec((c, GDN_V), lambda i: (i, 0)),
        scratch_shapes=[pltpu.VMEM((pad + c, GDN_CONV_CH), F32),
                        pltpu.VMEM((GDN_HEADS, GDN_DK, GDN_DV), F32)],
        compiler_params=_params(("arbitrary",)),
        name="gdn",
    )(proj, proj, proj, ba, conv_w, alog_lane, dtb_lane, gain, masks)


def _swa_kernel(q_ref, kc_ref, kp_ref, vc_ref, vp_ref, z_ref, bias_ref, sink_ref, qg_ref, kg_ref, seg_ref,
                o_ref):
    n = pl.program_id(0)
    qb = SWA_BLOCK
    hd = SWA_HEAD_DIM
    lane = lax.broadcasted_iota(jnp.int32, (1, LANES), 1)
    lo = lane < hd
    seg = seg_ref[...]

    def qk_norm(a, gain):
        ms = jnp.dot(a * a, seg, preferred_element_type=F32)
        return a * lax.rsqrt(ms + EPS) * gain

    kcat = jnp.concatenate([kp_ref[...], kc_ref[...]], axis=0).astype(F32)
    vcat = jnp.concatenate([vp_ref[...], vc_ref[...]], axis=0).astype(F32)
    kcol = lax.broadcasted_iota(jnp.int32, (1, 2 * qb), 1)
    first_pen = jnp.where((kcol < qb) & (n == 0), NEG_BIG, 0.0)

    kv_tiles = []
    for u in range(SWA_KV // LANES):
        kt = qk_norm(kcat[:, u * LANES:(u + 1) * LANES], kg_ref[...])
        vt = vcat[:, u * LANES:(u + 1) * LANES]
        kv_tiles.append((kt, pltpu.roll(kt, hd, axis=1), vt, pltpu.roll(vt, hd, axis=1)))

    for tq in range(SWA_Q // LANES):
        g = (2 * tq) // (SWA_HEADS // SWA_KV_HEADS)
        kt, kt_r, vt, vt_r = kv_tiles[g // 2]
        if g % 2 == 0:
            k_lo, k_hi, v_lo, v_hi = kt, kt_r, vt, vt_r
        else:
            k_lo, k_hi, v_lo, v_hi = kt_r, kt, vt_r, vt
        k2 = jnp.concatenate([jnp.where(lo, k_lo, 0.0), jnp.where(lo, 0.0, k_hi)], axis=0)
        v2 = jnp.concatenate([jnp.where(lo, v_lo, 0.0), jnp.where(lo, 0.0, v_hi)], axis=0)
        qt = qk_norm(q_ref[:, tq * LANES:(tq + 1) * LANES].astype(F32), qg_ref[...])
        logits = lax.dot_general(qt, k2, (((1,), (1,)), ((), ())), preferred_element_type=F32)
        ps, dens = [], []
        for e in range(2):
            head = 2 * tq + e
            s = logits[:, e * 2 * qb:(e + 1) * 2 * qb] * (hd ** -0.5) + bias_ref[head] + first_pen
            sink = sink_ref[head]
            m = jnp.maximum(jnp.max(s, axis=-1, keepdims=True), sink)
            p = jnp.exp(s - m)
            dens.append(jnp.sum(p, axis=-1, keepdims=True) + jnp.exp(sink - m))
            ps.append(p)
        pv = jnp.dot(jnp.concatenate(ps, axis=1), v2, preferred_element_type=F32)
        out = pv * jnp.where(lo, 1.0 / dens[0], 1.0 / dens[1])
        z = z_ref[:, tq * LANES:(tq + 1) * LANES].astype(F32)
        o_ref[:, tq * LANES:(tq + 1) * LANES] = (out * _silu(z)).astype(o_ref.dtype)


def _swa(proj, bias, sinks, qg2, kg2):
    t = proj.shape[0]
    qb = SWA_BLOCK
    seg = np.kron(np.eye(LANES // SWA_HEAD_DIM), np.ones((SWA_HEAD_DIM, SWA_HEAD_DIM))) / SWA_HEAD_DIM
    seg = jnp.asarray(seg, F32)
    kcol = COL_KB // SWA_KV
    vcol = COL_VB // SWA_KV
    prev = lambda i: jnp.maximum(i - 1, 0)
    return pl.pallas_call(
        _swa_kernel,
        out_shape=jax.ShapeDtypeStruct((t, SWA_Q), BF16),
        grid=(t // qb,),
        in_specs=[pl.BlockSpec((qb, SWA_Q), lambda i: (i, COL_QB // SWA_Q)),
                  pl.BlockSpec((qb, SWA_KV), lambda i: (i, kcol)),
                  pl.BlockSpec((qb, SWA_KV), lambda i: (prev(i), kcol)),
                  pl.BlockSpec((qb, SWA_KV), lambda i: (i, vcol)),
                  pl.BlockSpec((qb, SWA_KV), lambda i: (prev(i), vcol)),
                  pl.BlockSpec((qb, SWA_Q), lambda i: (i, COL_ZB // SWA_Q)),
                  pl.BlockSpec((SWA_HEADS, qb, 2 * qb), lambda i: (0, 0, 0)),
                  pl.BlockSpec(memory_space=pltpu.SMEM),
                  pl.BlockSpec((1, LANES), lambda i: (0, 0)),
                  pl.BlockSpec((1, LANES), lambda i: (0, 0)),
                  pl.BlockSpec((LANES, LANES), lambda i: (0, 0))],
        out_specs=pl.BlockSpec((qb, SWA_Q), lambda i: (i, 0)),
        compiler_params=_params(("arbitrary",)),
        name="swa",
    )(proj, proj, proj, proj, proj, proj, bias, sinks, qg2, kg2, seg)


def _merge_out_kernel(x_ref, oa_ref, ob_ref, ga_ref, gb_ref, gate_ref, wa_ref, wb_ref, wo_ref, o_ref):
    ya = jnp.dot(oa_ref[...], wa_ref[...], preferred_element_type=F32)
    yb = jnp.dot(ob_ref[...], wb_ref[...], preferred_element_type=F32)
    mixed = _sigmoid(ga_ref[...].astype(F32)) * ya + _sigmoid(gb_ref[...].astype(F32)) * yb
    y = jnp.dot(mixed.astype(BF16), wo_ref[...], preferred_element_type=F32)
    o_ref[...] = x_ref[...] + gate_ref[...] * y


def _merge_out(x2d, o_a, o_b, proj, gate, w_a, w_b, w_o, *, tm):
    t, d = x2d.shape
    const = lambda shape: pl.BlockSpec(shape, lambda i: (0, 0), pipeline_mode=pl.Buffered(1))
    return pl.pallas_call(
        _merge_out_kernel,
        out_shape=jax.ShapeDtypeStruct((t, d), F32),
        grid=(t // tm,),
        in_specs=[pl.BlockSpec((tm, d), lambda i: (i, 0)),
                  pl.BlockSpec((tm, GDN_V), lambda i: (i, 0)),
                  pl.BlockSpec((tm, SWA_Q), lambda i: (i, 0)),
                  pl.BlockSpec((tm, d), lambda i: (i, COL_GA // D_MODEL)),
                  pl.BlockSpec((tm, d), lambda i: (i, COL_GB // D_MODEL)),
                  pl.BlockSpec((1, d), lambda i: (0, 0)),
                  const((GDN_V, d)), const((SWA_Q, d)), const((d, d))],
        out_specs=pl.BlockSpec((tm, d), lambda i: (i, 0)),
        compiler_params=_params(("arbitrary",)),
        name="merge_out",
    )(x2d, o_a, o_b, proj, proj, gate, w_a, w_b, w_o)


def _repack_w_in(w):
    sizes = (GDN_QK, GDN_QK, GDN_V, GDN_V, GDN_HEADS, GDN_HEADS, SWA_Q, SWA_KV, SWA_KV, SWA_Q, D_MODEL, D_MODEL)
    offs = np.concatenate([[0], np.cumsum(sizes)])
    seg = lambda i: w[:, offs[i]:offs[i + 1]]
    big = jnp.concatenate([seg(0), seg(1), seg(2), seg(3), seg(6), seg(9), seg(10), seg(11), seg(7), seg(8)],
                          axis=1).astype(BF16)
    small = jnp.concatenate([seg(4), seg(5)], axis=1)
    small = jnp.pad(small, ((0, 0), (0, LANES - small.shape[1]))).astype(BF16)
    return big, small


def _lane_row(vec, offset):
    return jnp.pad(vec.astype(F32), (offset, LANES - offset - vec.shape[0]))[None, :]


def kernel(x, c, w_ada, b_ada, norm_gain, w_in, conv_w, a_log, dt_bias, gdn_norm_gain, q_norm_gain,
           k_norm_gain, sinks, rel_bias, w_branch_gdn, w_branch_swa, w_out):
    bsz, t, d = x.shape
    depth = w_in.shape[0]
    bias = _swa_bias(rel_bias.T.astype(F32))
    outs = []
    for b in range(bsz):
        xb = x[b]
        c_col = c[b].astype(F32)[:, None]
        for l in range(depth):
            mod = _ada_mod(c_col, w_ada[l], b_ada[l][None, :])
            shift, scale, gate = mod[:, :d], mod[:, d:2 * d], mod[:, 2 * d:]
            w_big, w_small = _repack_w_in(w_in[l])
            proj, ba = _in_proj(xb, norm_gain[l][None, :], scale, shift, w_big, w_small,
                                tm=min(1024, t), tn=768)
            o_a = _gdn(proj, ba, conv_w[l], _lane_row(a_log[l], GDN_HEADS), _lane_row(dt_bias[l], GDN_HEADS),
                       gdn_norm_gain[l][None, :])
            o_b = _swa(proj, bias, sinks[l].astype(F32),
                       jnp.tile(q_norm_gain[l], LANES // SWA_HEAD_DIM)[None, :],
                       jnp.tile(k_norm_gain[l], LANES // SWA_HEAD_DIM)[None, :])
            xb = _merge_out(xb, o_a, o_b, proj, gate, w_branch_gdn[l].astype(BF16),
                            w_branch_swa[l].astype(BF16), w_out[l].astype(BF16), tm=min(512, t))
        outs.append(xb)
    return jnp.stack(outs, axis=0)
```

```python
import functools
import math

import jax
import jax.numpy as jnp
import numpy as np
from jax import lax
from jax.experimental import pallas as pl
from jax.experimental.pallas import tpu as pltpu

F32 = jnp.float32
BF16 = jnp.bfloat16

LANES = 128
D_MODEL = 2048
GDN_HEADS = 8
GDN_DK = 128
GDN_DV = 128
GDN_CONV = 4
GDN_QK = GDN_HEADS * GDN_DK
GDN_V = GDN_HEADS * GDN_DV
GDN_CONV_CH = 2 * GDN_QK + GDN_V
GDN_BLOCK = 128
SWA_HEADS = 16
SWA_KV_HEADS = 4
SWA_HEAD_DIM = 64
SWA_WINDOW = 128
SWA_BLOCK = 128
SWA_Q = SWA_HEADS * SWA_HEAD_DIM
SWA_KV = SWA_KV_HEADS * SWA_HEAD_DIM
REL_BUCKETS = 32
REL_MAX_DIST = 128
EPS = 1e-6
NEG_BIG = -1e30

COL_QKV = 0
COL_ZA = GDN_CONV_CH
COL_QB = COL_ZA + GDN_V
COL_ZB = COL_QB + SWA_Q
COL_GA = COL_ZB + SWA_Q
COL_GB = COL_GA + D_MODEL
COL_KB = COL_GB + D_MODEL
COL_VB = COL_KB + SWA_KV
PROJ_COLS = COL_VB + SWA_KV
for _col, _width in ((COL_ZA, GDN_V), (COL_QB, SWA_Q), (COL_ZB, SWA_Q), (COL_GA, D_MODEL),
                     (COL_GB, D_MODEL), (COL_KB, SWA_KV), (COL_VB, SWA_KV)):
    assert _col % _width == 0

VMEM_LIMIT = 56 * 1024 * 1024


def _sigmoid(x):
    return 1.0 / (1.0 + jnp.exp(-x))


def _silu(x):
    return x * _sigmoid(x)


def _params(sem):
    return pltpu.CompilerParams(dimension_semantics=sem, vmem_limit_bytes=VMEM_LIMIT)


def _ada_mod_kernel(c_ref, w_ref, b_ref, o_ref):
    c = c_ref[...]
    o_ref[...] = jnp.sum(_silu(c) * w_ref[...], axis=0, keepdims=True) + b_ref[...]


def _ada_mod(c_col, w_ada, b_ada):
    d, n = w_ada.shape
    tn = 512
    return pl.pallas_call(
        _ada_mod_kernel,
        out_shape=jax.ShapeDtypeStruct((1, n), F32),
        grid=(n // tn,),
        in_specs=[pl.BlockSpec((d, 1), lambda j: (0, 0)),
                  pl.BlockSpec((d, tn), lambda j: (0, j)),
                  pl.BlockSpec((1, tn), lambda j: (0, j))],
        out_specs=pl.BlockSpec((1, tn), lambda j: (0, j)),
        compiler_params=_params(("arbitrary",)),
        name="ada_mod",
    )(c_col, w_ada, b_ada)


def _in_proj_kernel(x_ref, gain_ref, scale_ref, shift_ref, w_ref, ws_ref, o_ref, ba_ref, h_ref,
                    *, row_chunk):
    j = pl.program_id(1)

    @pl.when(j == 0)
    def _():
        gs = gain_ref[...] * (1.0 + scale_ref[...])
        sh = shift_ref[...]
        tm = x_ref.shape[0]

        def body(r, carry):
            rows = pl.ds(pl.multiple_of(r * row_chunk, row_chunk), row_chunk)
            x = x_ref[rows, :]
            ms = jnp.mean(x * x, axis=-1, keepdims=True)
            h = (x * lax.rsqrt(ms + EPS)) * gs + sh
            h_ref[rows, :] = h.astype(BF16)
            return carry

        lax.fori_loop(0, tm // row_chunk, body, 0)
        ba_ref[...] = jnp.dot(h_ref[...], ws_ref[...], preferred_element_type=F32)

    o_ref[...] = jnp.dot(h_ref[...], w_ref[...], preferred_element_type=F32).astype(o_ref.dtype)


def _in_proj(x2d, gain, scale, shift, w_big, w_small, *, tm, tn):
    t, d = x2d.shape
    n = w_big.shape[1]
    return pl.pallas_call(
        functools.partial(_in_proj_kernel, row_chunk=128),
        out_shape=(jax.ShapeDtypeStruct((t, n), BF16), jax.ShapeDtypeStruct((t, LANES), F32)),
        grid=(t // tm, n // tn),
        in_specs=[pl.BlockSpec((tm, d), lambda i, j: (i, 0)),
                  pl.BlockSpec((1, d), lambda i, j: (0, 0)),
                  pl.BlockSpec((1, d), lambda i, j: (0, 0)),
                  pl.BlockSpec((1, d), lambda i, j: (0, 0)),
                  pl.BlockSpec((d, tn), lambda i, j: (0, j)),
                  pl.BlockSpec((d, LANES), lambda i, j: (0, 0))],
        out_specs=(pl.BlockSpec((tm, tn), lambda i, j: (i, j)),
                   pl.BlockSpec((tm, LANES), lambda i, j: (i, 0))),
        scratch_shapes=[pltpu.VMEM((tm, d), BF16)],
        compiler_params=_params(("arbitrary", "arbitrary")),
        name="in_proj",
    )(x2d, gain, scale, shift, w_big, w_small)


def _swa_bias_kernel(tab_ref, o_ref):
    h = pl.program_id(0)
    q = SWA_BLOCK
    qpos = lax.broadcasted_iota(jnp.int32, (q, 2 * q), 0) + q
    kpos = lax.broadcasted_iota(jnp.int32, (q, 2 * q), 1)
    dist = qpos - kpos
    in_window = (dist >= 0) & (dist < SWA_WINDOW)
    d = jnp.maximum(dist, 0)
    max_exact = REL_BUCKETS // 2
    df = jnp.maximum(d, 1).astype(F32)
    large = max_exact + (jnp.log(df / max_exact) / math.log(REL_MAX_DIST / max_exact)
                         * (REL_BUCKETS - max_exact)).astype(jnp.int32)
    large = jnp.minimum(large, REL_BUCKETS - 1)
    bucket = jnp.where(d < max_exact, d, large)
    acc = jnp.zeros((q, 2 * q), F32)
    for b in range(REL_BUCKETS):
        acc = jnp.where(bucket == b, tab_ref[h, b], acc)
    o_ref[0] = jnp.where(in_window, acc, NEG_BIG)


def _swa_bias(rel_bias_t):
    q = SWA_BLOCK
    return pl.pallas_call(
        _swa_bias_kernel,
        out_shape=jax.ShapeDtypeStruct((SWA_HEADS, q, 2 * q), F32),
        grid=(SWA_HEADS,),
        in_specs=[pl.BlockSpec(memory_space=pltpu.SMEM)],
        out_specs=pl.BlockSpec((1, q, 2 * q), lambda h: (h, 0, 0)),
        compiler_params=_params(("arbitrary",)),
        name="swa_bias",
    )(rel_bias_t)


def _chunk_cumsum_rows(x):
    n = x.shape[0]
    row = lax.broadcasted_iota(jnp.int32, x.shape, 0)
    s = 1
    while s < n:
        x = x + jnp.where(row >= s, pltpu.roll(x, s, axis=0), 0.0)
        s *= 2
    return x


def _mm(a, b):
    return jnp.dot(a.astype(BF16), b.astype(BF16), preferred_element_type=F32)


def _mm_nt(a, b):
    return lax.dot_general(a.astype(BF16), b.astype(BF16), (((1,), (1,)), ((), ())),
                           preferred_element_type=F32)


def _mm_tn(a, b):
    return lax.dot_general(a.astype(BF16), b.astype(BF16), (((0,), (0,)), ((), ())),
                           preferred_element_type=F32)


def _unit_lower_inverses(l_mats, masks_ref):
    eye = masks_ref[0]
    m0 = [l * masks_ref[1] for l in l_mats]
    x = [eye - m for m in m0]
    p = [_mm(m, m) for m in m0]
    x = [xi + _mm(xi, pi) for xi, pi in zip(x, p)]
    p = [_mm(pi, pi) for pi in p]
    x = [xi + _mm(xi, pi) for xi, pi in zip(x, p)]
    for lvl in range(2, masks_ref.shape[0]):
        nx = [_mm(l * masks_ref[lvl], xi) for l, xi in zip(l_mats, x)]
        x = [xi - _mm(xi, ni) for xi, ni in zip(x, nx)]
    return x


def _gdn_kernel(cur_ref, prev_ref, z_ref, ba_ref, convw_ref, alog_ref, dtb_ref, gain_ref, masks_ref,
                o_ref, xcat_ref, s_ref):
    t = pl.program_id(0)
    c = GDN_BLOCK
    pad = prev_ref.shape[0]

    @pl.when(t == 0)
    def _():
        s_ref[...] = jnp.zeros_like(s_ref)

    keep_prev = (t > 0).astype(F32)
    xcat_ref[0:pad, :] = prev_ref[...].astype(F32) * keep_prev
    xcat_ref[pad:pad + c, :] = cur_ref[...].astype(F32)

    ba = ba_ref[...]
    beta_all = _sigmoid(ba)
    xg = ba + dtb_ref[...]
    softplus = jnp.maximum(xg, 0.0) + jnp.log(1.0 + jnp.exp(-jnp.abs(xg)))
    g_all = -jnp.exp(alog_ref[...]) * softplus
    gc = _chunk_cumsum_rows(g_all)
    gc_t = gc.T
    g_last = gc[c - 1:c, :]
    eg_all = jnp.exp(gc)
    ekd_all = jnp.exp(g_last - gc)
    elast_all = jnp.exp(g_last)

    row = lax.broadcasted_iota(jnp.int32, (c, c), 0)
    col = lax.broadcasted_iota(jnp.int32, (c, c), 1)
    causal = row >= col
    strict = row > col

    def conv_tile(col0):
        acc = None
        for i in range(GDN_CONV):
            xs = xcat_ref[pad - (GDN_CONV - 1) + i: pad - (GDN_CONV - 1) + i + c, col0:col0 + LANES]
            term = xs * convw_ref[i:i + 1, col0:col0 + LANES]
            acc = term if acc is None else acc + term
        return _silu(acc)

    def l2n(a):
        return a * lax.rsqrt(jnp.sum(a * a, axis=-1, keepdims=True) + EPS)

    heads = range(GDN_HEADS)
    lane_of = lambda a, h: a[:, GDN_HEADS + h:GDN_HEADS + h + 1]
    q = [l2n(conv_tile(h * GDN_DK)) * (GDN_DK ** -0.5) for h in heads]
    k = [l2n(conv_tile(GDN_QK + h * GDN_DK)) for h in heads]
    v = [conv_tile(2 * GDN_QK + h * GDN_DV) for h in heads]
    beta = [beta_all[:, h:h + 1] for h in heads]
    eg = [lane_of(eg_all, h) for h in heads]
    s_old = [s_ref[h] for h in heads]
    o_inter = [_mm(q[h] * eg[h], s_old[h]) for h in heads]
    kb = [k[h] * beta[h] for h in heads]
    kk = [_mm_nt(kb[h], k[h]) for h in heads]
    qk = [_mm_nt(q[h], k[h]) for h in heads]
    decay = []
    for h in heads:
        gdiff = lane_of(gc, h) - gc_t[GDN_HEADS + h:GDN_HEADS + h + 1, :]
        decay.append(jnp.where(causal, jnp.exp(jnp.where(causal, gdiff, 0.0)), 0.0))
    l_mat = [jnp.where(strict, kk[h] * decay[h], 0.0) for h in heads]
    attn = [qk[h] * decay[h] for h in heads]
    t_inv = _unit_lower_inverses(l_mat, masks_ref)
    uw = [_mm(t_inv[h], jnp.concatenate([v[h] * beta[h], kb[h] * eg[h]], axis=1)) for h in heads]
    v_new = [uw[h][:, :GDN_DV] - _mm(uw[h][:, GDN_DV:], s_old[h]) for h in heads]
    o = [o_inter[h] + _mm(attn[h], v_new[h]) for h in heads]
    for h in heads:
        s_ref[h] = s_old[h] * lane_of(elast_all, h) + _mm_tn(k[h] * lane_of(ekd_all, h), v_new[h])
    for h in heads:
        on = o[h] * lax.rsqrt(jnp.mean(o[h] * o[h], axis=-1, keepdims=True) + EPS) * gain_ref[...]
        z = z_ref[:, h * GDN_DV:(h + 1) * GDN_DV].astype(F32)
        o_ref[:, h * GDN_DV:(h + 1) * GDN_DV] = (on * _silu(z)).astype(o_ref.dtype)


def _gdn_masks():
    c = GDN_BLOCK
    r = np.arange(c)[:, None]
    k = np.arange(c)[None, :]
    mats = [np.eye(c), (r // 8 == k // 8) & (r > k)]
    b = 8
    while b < c:
        mats.append((r // (2 * b) == k // (2 * b)) & ((r // b) % 2 == 1) & ((k // b) % 2 == 0))
        b *= 2
    return jnp.asarray(np.stack([np.asarray(m, np.float32) for m in mats]))


def _gdn(proj, ba, conv_w, alog_lane, dtb_lane, gain):
    t = proj.shape[0]
    c = GDN_BLOCK
    pad = 16
    masks = _gdn_masks()
    nm = masks.shape[0]
    return pl.pallas_call(
        _gdn_kernel,
        out_shape=jax.ShapeDtypeStruct((t, GDN_V), BF16),
        grid=(t // c,),
        in_specs=[pl.BlockSpec((c, GDN_CONV_CH), lambda i: (i, 0)),
                  pl.BlockSpec((pad, GDN_CONV_CH), lambda i: (jnp.maximum(i * (c // pad) - 1, 0), 0)),
                  pl.BlockSpec((c, GDN_V), lambda i: (i, COL_ZA // GDN_V)),
                  pl.BlockSpec((c, LANES), lambda i: (i, 0)),
                  pl.BlockSpec((GDN_CONV, GDN_CONV_CH), lambda i: (0, 0)),
                  pl.BlockSpec((1, LANES), lambda i: (0, 0)),
                  pl.BlockSpec((1, LANES), lambda i: (0, 0)),
                  pl.BlockSpec((1, GDN_DV), lambda i: (0, 0)),
                  pl.BlockSpec((nm, c, c), lambda i: (0, 0, 0))],
        out_specs=pl.BlockSpec((c, GDN_V), lambda i: (i, 0)),
        scratch_shapes=[pltpu.VMEM((pad + c, GDN_CONV_CH), F32),
                        pltpu.VMEM((GDN_HEADS, GDN_DK, GDN_DV), F32)],
        compiler_params=_params(("arbitrary",)),
        name="gdn",
    )(proj, proj, proj, ba, conv_w, alog_lane, dtb_lane, gain, masks)


def _swa_kernel(q_ref, kc_ref, kp_ref, vc_ref, vp_ref, z_ref, bias_ref, sink_ref, qg_ref, kg_ref, seg_ref,
                o_ref):
    n = pl.program_id(0)
    qb = SWA_BLOCK
    hd = SWA_HEAD_DIM
    lane = lax.broadcasted_iota(jnp.int32, (1, LANES), 1)
    lo = lane < hd
    seg = seg_ref[...]

    def qk_norm(a, gain):
        ms = jnp.dot(a * a, seg, preferred_element_type=F32)
        return a * lax.rsqrt(ms + EPS) * gain

    kcat = jnp.concatenate([kp_ref[...], kc_ref[...]], axis=0).astype(F32)
    vcat = jnp.concatenate([vp_ref[...], vc_ref[...]], axis=0).astype(F32)
    kcol = lax.broadcasted_iota(jnp.int32, (1, 2 * qb), 1)
    first_pen = jnp.where((kcol < qb) & (n == 0), NEG_BIG, 0.0)

    kv_tiles = []
    for u in range(SWA_KV // LANES):
        kt = qk_norm(kcat[:, u * LANES:(u + 1) * LANES], kg_ref[...])
        vt = vcat[:, u * LANES:(u + 1) * LANES]
        kv_tiles.append((kt, pltpu.roll(kt, hd, axis=1), vt, pltpu.roll(vt, hd, axis=1)))

    for tq in range(SWA_Q // LANES):
        g = (2 * tq) // (SWA_HEADS // SWA_KV_HEADS)
        kt, kt_r, vt, vt_r = kv_tiles[g // 2]
        if g % 2 == 0:
            k_lo, k_hi, v_lo, v_hi = kt, kt_r, vt, vt_r
        else:
            k_lo, k_hi, v_lo, v_hi = kt_r, kt, vt_r, vt
        k2 = jnp.concatenate([jnp.where(lo, k_lo, 0.0), jnp.where(lo, 0.0, k_hi)], axis=0)
        v2 = jnp.concatenate([jnp.where(lo, v_lo, 0.0), jnp.where(lo, 0.0, v_hi)], axis=0)
        qt = qk_norm(q_ref[:, tq * LANES:(tq + 1) * LANES].astype(F32), qg_ref[...])
        logits = lax.dot_general(qt, k2, (((1,), (1,)), ((), ())), preferred_element_type=F32)
        ps, dens = [], []
        for e in range(2):
            head = 2 * tq + e
            s = logits[:, e * 2 * qb:(e + 1) * 2 * qb] * (hd ** -0.5) + bias_ref[head] + first_pen
            sink = sink_ref[head]
            m = jnp.maximum(jnp.max(s, axis=-1, keepdims=True), sink)
            p = jnp.exp(s - m)
            dens.append(jnp.sum(p, axis=-1, keepdims=True) + jnp.exp(sink - m))
            ps.append(p)
        pv = jnp.dot(jnp.concatenate(ps, axis=1), v2, preferred_element_type=F32)
        out = pv * jnp.where(lo, 1.0 / dens[0], 1.0 / dens[1])
        z = z_ref[:, tq * LANES:(tq + 1) * LANES].astype(F32)
        o_ref[:, tq * LANES:(tq + 1) * LANES] = (out * _silu(z)).astype(o_ref.dtype)


def _swa(proj, bias, sinks, qg2, kg2):
    t = proj.shape[0]
    qb = SWA_BLOCK
    seg = np.kron(np.eye(LANES // SWA_HEAD_DIM), np.ones((SWA_HEAD_DIM, SWA_HEAD_DIM))) / SWA_HEAD_DIM
    seg = jnp.asarray(seg, F32)
    kcol = COL_KB // SWA_KV
    vcol = COL_VB // SWA_KV
    prev = lambda i: jnp.maximum(i - 1, 0)
    return pl.pallas_call(
        _swa_kernel,
        out_shape=jax.ShapeDtypeStruct((t, SWA_Q), BF16),
        grid=(t // qb,),
        in_specs=[pl.BlockSpec((qb, SWA_Q), lambda i: (i, COL_QB // SWA_Q)),
                  pl.BlockSpec((qb, SWA_KV), lambda i: (i, kcol)),
                  pl.BlockSpec((qb, SWA_KV), lambda i: (prev(i), kcol)),
                  pl.BlockSpec((qb, SWA_KV), lambda i: (i, vcol)),
                  pl.BlockSpec((qb, SWA_KV), lambda i: (prev(i), vcol)),
                  pl.BlockSpec((qb, SWA_Q), lambda i: (i, COL_ZB // SWA_Q)),
                  pl.BlockSpec((SWA_HEADS, qb, 2 * qb), lambda i: (0, 0, 0)),
                  pl.BlockSpec(memory_space=pltpu.SMEM),
                  pl.BlockSpec((1, LANES), lambda i: (0, 0)),
                  pl.BlockSpec((1, LANES), lambda i: (0, 0)),
                  pl.BlockSpec((LANES, LANES), lambda i: (0, 0))],
        out_specs=pl.BlockSpec((qb, SWA_Q), lambda i: (i, 0)),
        compiler_params=_params(("arbitrary",)),
        name="swa",
    )(proj, proj, proj, proj, proj, proj, bias, sinks, qg2, kg2, seg)


def _merge_out_kernel(x_ref, oa_ref, ob_ref, ga_ref, gb_ref, gate_ref, wa_ref, wb_ref, wo_ref, o_ref):
    ya = jnp.dot(oa_ref[...], wa_ref[...], preferred_element_type=F32)
    yb = jnp.dot(ob_ref[...], wb_ref[...], preferred_element_type=F32)
    mixed = _sigmoid(ga_ref[...].astype(F32)) * ya + _sigmoid(gb_ref[...].astype(F32)) * yb
    y = jnp.dot(mixed.astype(BF16), wo_ref[...], preferred_element_type=F32)
    o_ref[...] = x_ref[...] + gate_ref[...] * y


def _merge_out(x2d, o_a, o_b, proj, gate, w_a, w_b, w_o, *, tm):
    t, d = x2d.shape
    const = lambda shape: pl.BlockSpec(shape, lambda i: (0, 0), pipeline_mode=pl.Buffered(1))
    return pl.pallas_call(
        _merge_out_kernel,
        out_shape=jax.ShapeDtypeStruct((t, d), F32),
        grid=(t // tm,),
        in_specs=[pl.BlockSpec((tm, d), lambda i: (i, 0)),
                  pl.BlockSpec((tm, GDN_V), lambda i: (i, 0)),
                  pl.BlockSpec((tm, SWA_Q), lambda i: (i, 0)),
                  pl.BlockSpec((tm, d), lambda i: (i, COL_GA // D_MODEL)),
                  pl.BlockSpec((tm, d), lambda i: (i, COL_GB // D_MODEL)),
                  pl.BlockSpec((1, d), lambda i: (0, 0)),
                  const((GDN_V, d)), const((SWA_Q, d)), const((d, d))],
        out_specs=pl.BlockSpec((tm, d), lambda i: (i, 0)),
        compiler_params=_params(("arbitrary",)),
        name="merge_out",
    )(x2d, o_a, o_b, proj, proj, gate, w_a, w_b, w_o)


def _repack_w_in(w):
    sizes = (GDN_QK, GDN_QK, GDN_V, GDN_V, GDN_HEADS, GDN_HEADS, SWA_Q, SWA_KV, SWA_KV, SWA_Q, D_MODEL, D_MODEL)
    offs = np.concatenate([[0], np.cumsum(sizes)])
    seg = lambda i: w[:, offs[i]:offs[i + 1]]
    big = jnp.concatenate([seg(0), seg(1), seg(2), seg(3), seg(6), seg(9), seg(10), seg(11), seg(7), seg(8)],
                          axis=1).astype(BF16)
    small = jnp.concatenate([seg(4), seg(5)], axis=1)
    small = jnp.pad(small, ((0, 0), (0, LANES - small.shape[1]))).astype(BF16)
    return big, small


def _lane_row(vec, offset):
    return jnp.pad(vec.astype(F32), (offset, LANES - offset - vec.shape[0]))[None, :]


def kernel(x, c, w_ada, b_ada, norm_gain, w_in, conv_w, a_log, dt_bias, gdn_norm_gain, q_norm_gain,
           k_norm_gain, sinks, rel_bias, w_branch_gdn, w_branch_swa, w_out):
    bsz, t, d = x.shape
    depth = w_in.shape[0]
    bias = _swa_bias(rel_bias.T.astype(F32))
    outs = []
    for b in range(bsz):
        xb = x[b]
        c_col = c[b].astype(F32)[:, None]
        for l in range(depth):
            mod = _ada_mod(c_col, w_ada[l], b_ada[l][None, :])
            shift, scale, gate = mod[:, :d], mod[:, d:2 * d], mod[:, 2 * d:]
            w_big, w_small = _repack_w_in(w_in[l])
            proj, ba = _in_proj(xb, norm_gain[l][None, :], scale, shift, w_big, w_small,
                                tm=min(1024, t), tn=768)
            o_a = _gdn(proj, ba, conv_w[l], _lane_row(a_log[l], GDN_HEADS), _lane_row(dt_bias[l], GDN_HEADS),
                       gdn_norm_gain[l][None, :])
            o_b = _swa(proj, bias, sinks[l].astype(F32),
                       jnp.tile(q_norm_gain[l], LANES // SWA_HEAD_DIM)[None, :],
                       jnp.tile(k_norm_gain[l], LANES // SWA_HEAD_DIM)[None, :])
            xb = _merge_out(xb, o_a, o_b, proj, gate, w_branch_gdn[l].astype(BF16),
                            w_branch_swa[l].astype(BF16), w_out[l].astype(BF16), tm=min(512, t))
        outs.append(xb)
    return jnp.stack(outs, axis=0)
```

```python
import functools
import math

import jax
import jax.numpy as jnp
import numpy as np
from jax import lax
from jax.experimental import pallas as pl
from jax.experimental.pallas import tpu as pltpu

F32 = jnp.float32
BF16 = jnp.bfloat16

LANES = 128
D_MODEL = 2048
GDN_HEADS = 8
GDN_DK = 128
GDN_DV = 128
GDN_CONV = 4
GDN_QK = GDN_HEADS * GDN_DK
GDN_V = GDN_HEADS * GDN_DV
GDN_CONV_CH = 2 * GDN_QK + GDN_V
GDN_BLOCK = 128
SWA_HEADS = 16
SWA_KV_HEADS = 4
SWA_HEAD_DIM = 64
SWA_WINDOW = 128
SWA_BLOCK = 128
SWA_Q = SWA_HEADS * SWA_HEAD_DIM
SWA_KV = SWA_KV_HEADS * SWA_HEAD_DIM
REL_BUCKETS = 32
REL_MAX_DIST = 128
EPS = 1e-6
NEG_BIG = -1e30
LOG2E = math.log2(math.e)

COL_QKV = 0
COL_ZA = GDN_CONV_CH
COL_QB = COL_ZA + GDN_V
COL_ZB = COL_QB + SWA_Q
COL_GA = COL_ZB + SWA_Q
COL_GB = COL_GA + D_MODEL
COL_KB = COL_GB + D_MODEL
COL_VB = COL_KB + SWA_KV
PROJ_COLS = COL_VB + SWA_KV
for _col, _width in ((COL_ZA, GDN_V), (COL_QB, SWA_Q), (COL_ZB, SWA_Q), (COL_GA, D_MODEL),
                     (COL_GB, D_MODEL), (COL_KB, SWA_KV), (COL_VB, SWA_KV)):
    assert _col % _width == 0

VMEM_LIMIT = 56 * 1024 * 1024


def _sigmoid(x):
    return 1.0 / (1.0 + jnp.exp(-x))


def _silu(x):
    return x * _sigmoid(x)


def _params(sem):
    return pltpu.CompilerParams(dimension_semantics=sem, vmem_limit_bytes=VMEM_LIMIT)


def _ada_mod_kernel(c_ref, w_ref, b_ref, o_ref):
    c = c_ref[...]
    o_ref[...] = jnp.sum(_silu(c) * w_ref[...], axis=0, keepdims=True) + b_ref[...]


def _ada_mod(c_col, w_ada, b_ada):
    d, n = w_ada.shape
    tn = 512
    return pl.pallas_call(
        _ada_mod_kernel,
        out_shape=jax.ShapeDtypeStruct((1, n), F32),
        grid=(n // tn,),
        in_specs=[pl.BlockSpec((d, 1), lambda j: (0, 0)),
                  pl.BlockSpec((d, tn), lambda j: (0, j)),
                  pl.BlockSpec((1, tn), lambda j: (0, j))],
        out_specs=pl.BlockSpec((1, tn), lambda j: (0, j)),
        compiler_params=_params(("arbitrary",)),
        name="ada_mod",
    )(c_col, w_ada, b_ada)


def _in_proj_kernel(x_ref, gain_ref, scale_ref, shift_ref, w_ref, ws_ref, o_ref, ba_ref, h_ref,
                    *, row_chunk):
    j = pl.program_id(1)

    @pl.when(j == 0)
    def _():
        gs = gain_ref[...] * (1.0 + scale_ref[...])
        sh = shift_ref[...]
        tm = x_ref.shape[0]

        def body(r, carry):
            rows = pl.ds(pl.multiple_of(r * row_chunk, row_chunk), row_chunk)
            x = x_ref[rows, :]
            ms = jnp.mean(x * x, axis=-1, keepdims=True)
            h = (x * lax.rsqrt(ms + EPS)) * gs + sh
            h_ref[rows, :] = h.astype(BF16)
            return carry

        lax.fori_loop(0, tm // row_chunk, body, 0)
        ba_ref[...] = jnp.dot(h_ref[...], ws_ref[...], preferred_element_type=F32)

    o_ref[...] = jnp.dot(h_ref[...], w_ref[...], preferred_element_type=F32).astype(o_ref.dtype)


def _in_proj(x2d, gain, scale, shift, w_big, w_small, *, tm, tn):
    t, d = x2d.shape
    n = w_big.shape[1]
    return pl.pallas_call(
        functools.partial(_in_proj_kernel, row_chunk=128),
        out_shape=(jax.ShapeDtypeStruct((t, n), BF16), jax.ShapeDtypeStruct((t, LANES), F32)),
        grid=(t // tm, n // tn),
        in_specs=[pl.BlockSpec((tm, d), lambda i, j: (i, 0)),
                  pl.BlockSpec((1, d), lambda i, j: (0, 0)),
                  pl.BlockSpec((1, d), lambda i, j: (0, 0)),
                  pl.BlockSpec((1, d), lambda i, j: (0, 0)),
                  pl.BlockSpec((d, tn), lambda i, j: (0, j)),
                  pl.BlockSpec((d, LANES), lambda i, j: (0, 0))],
        out_specs=(pl.BlockSpec((tm, tn), lambda i, j: (i, j)),
                   pl.BlockSpec((tm, LANES), lambda i, j: (i, 0))),
        scratch_shapes=[pltpu.VMEM((tm, d), BF16)],
        compiler_params=_params(("arbitrary", "arbitrary")),
        name="in_proj",
    )(x2d, gain, scale, shift, w_big, w_small)


def _swa_bias_kernel(tab_ref, sink_ref, o_ref):
    h = pl.program_id(0)
    q = SWA_BLOCK
    qpos = lax.broadcasted_iota(jnp.int32, (q, 2 * q), 0) + q
    kpos = lax.broadcasted_iota(jnp.int32, (q, 2 * q), 1)
    dist = qpos - kpos
    in_window = (dist >= 0) & (dist < SWA_WINDOW)
    d = jnp.maximum(dist, 0)
    max_exact = REL_BUCKETS // 2
    df = jnp.maximum(d, 1).astype(F32)
    large = max_exact + (jnp.log(df / max_exact) / math.log(REL_MAX_DIST / max_exact)
                         * (REL_BUCKETS - max_exact)).astype(jnp.int32)
    large = jnp.minimum(large, REL_BUCKETS - 1)
    bucket = jnp.where(d < max_exact, d, large)
    acc = jnp.zeros((q, 2 * q), F32)
    for b in range(REL_BUCKETS):
        acc = jnp.where(bucket == b, tab_ref[h, b], acc)
    table = jnp.where(in_window, acc, NEG_BIG)
    o_ref[0] = jnp.where(kpos == 0, sink_ref[h], table) * LOG2E


def _swa_bias(rel_bias_t, sinks):
    q = SWA_BLOCK
    return pl.pallas_call(
        _swa_bias_kernel,
        out_shape=jax.ShapeDtypeStruct((SWA_HEADS, q, 2 * q), F32),
        grid=(SWA_HEADS,),
        in_specs=[pl.BlockSpec(memory_space=pltpu.SMEM), pl.BlockSpec(memory_space=pltpu.SMEM)],
        out_specs=pl.BlockSpec((1, q, 2 * q), lambda h: (h, 0, 0)),
        compiler_params=_params(("arbitrary",)),
        name="swa_bias",
    )(rel_bias_t, sinks)


def _chunk_cumsum_rows(x):
    n = x.shape[0]
    row = lax.broadcasted_iota(jnp.int32, x.shape, 0)
    s = 1
    while s < n:
        x = x + jnp.where(row >= s, pltpu.roll(x, s, axis=0), 0.0)
        s *= 2
    return x


def _mm(a, b):
    return jnp.dot(a.astype(BF16), b.astype(BF16), preferred_element_type=F32)


def _mm_nt(a, b):
    return lax.dot_general(a.astype(BF16), b.astype(BF16), (((1,), (1,)), ((), ())),
                           preferred_element_type=F32)


def _mm_tn(a, b):
    return lax.dot_general(a.astype(BF16), b.astype(BF16), (((0,), (0,)), ((), ())),
                           preferred_element_type=F32)


def _unit_lower_inverses(l_mats, masks_ref):
    eye = masks_ref[0]
    m0 = [l * masks_ref[1] for l in l_mats]
    x = [eye - m for m in m0]
    p = [_mm(m, m) for m in m0]
    x = [xi + _mm(xi, pi) for xi, pi in zip(x, p)]
    p = [_mm(pi, pi) for pi in p]
    x = [xi + _mm(xi, pi) for xi, pi in zip(x, p)]
    for lvl in range(2, masks_ref.shape[0]):
        nx = [_mm(l * masks_ref[lvl], xi) for l, xi in zip(l_mats, x)]
        x = [xi - _mm(xi, ni) for xi, ni in zip(x, nx)]
    return x


def _gdn_kernel(cur_ref, prev_ref, z_ref, ba_ref, convw_ref, band_ref, alog_ref, dtb_ref, gain_ref,
                masks_ref, o_ref, s_ref):
    t = pl.program_id(0)
    c = GDN_BLOCK

    @pl.when(t == 0)
    def _():
        s_ref[...] = jnp.zeros_like(s_ref)

    prev = jnp.where(t > 0, prev_ref[...], jnp.zeros_like(prev_ref))
    xcat = jnp.concatenate([prev, cur_ref[...]], axis=0)
    conv_w = convw_ref[...].astype(BF16)
    taps = jnp.concatenate([xcat * conv_w[i:i + 1, :] for i in range(GDN_CONV)], axis=0)
    conv = jnp.dot(band_ref[...], taps, preferred_element_type=F32)

    ba = ba_ref[...]
    beta_all = _sigmoid(ba)
    xg = ba + dtb_ref[...]
    softplus = jnp.maximum(xg, 0.0) + jnp.log(1.0 + jnp.exp(-jnp.abs(xg)))
    g_all = -jnp.exp(alog_ref[...]) * softplus
    gc = _chunk_cumsum_rows(g_all)
    gc_t = gc.T
    g_last = gc[c - 1:c, :]
    eg_all = jnp.exp(gc)
    ekd_all = jnp.exp(g_last - gc)
    elast_all = jnp.exp(g_last)

    row = lax.broadcasted_iota(jnp.int32, (c, c), 0)
    col = lax.broadcasted_iota(jnp.int32, (c, c), 1)
    causal = row >= col
    strict = row > col

    def conv_tile(col0):
        return _silu(conv[:, col0:col0 + LANES])

    def l2n(a):
        return a * lax.rsqrt(jnp.sum(a * a, axis=-1, keepdims=True) + EPS)

    heads = range(GDN_HEADS)
    lane_of = lambda a, h: a[:, GDN_HEADS + h:GDN_HEADS + h + 1]
    q = [l2n(conv_tile(h * GDN_DK)) * (GDN_DK ** -0.5) for h in heads]
    k = [l2n(conv_tile(GDN_QK + h * GDN_DK)) for h in heads]
    v = [conv_tile(2 * GDN_QK + h * GDN_DV) for h in heads]
    beta = [beta_all[:, h:h + 1] for h in heads]
    eg = [lane_of(eg_all, h) for h in heads]
    s_old = [s_ref[h] for h in heads]
    o_inter = [_mm(q[h] * eg[h], s_old[h]) for h in heads]
    kb = [k[h] * beta[h] for h in heads]
    kk = [_mm_nt(kb[h], k[h]) for h in heads]
    qk = [_mm_nt(q[h], k[h]) for h in heads]
    decay = []
    for h in heads:
        gdiff = lane_of(gc, h) - gc_t[GDN_HEADS + h:GDN_HEADS + h + 1, :]
        decay.append(jnp.where(causal, jnp.exp(jnp.where(causal, gdiff, 0.0)), 0.0))
    l_mat = [jnp.where(strict, kk[h] * decay[h], 0.0) for h in heads]
    attn = [qk[h] * decay[h] for h in heads]
    t_inv = _unit_lower_inverses(l_mat, masks_ref)
    uw = [_mm(t_inv[h], jnp.concatenate([v[h] * beta[h], kb[h] * eg[h]], axis=1)) for h in heads]
    v_new = [uw[h][:, :GDN_DV] - _mm(uw[h][:, GDN_DV:], s_old[h]) for h in heads]
    o = [o_inter[h] + _mm(attn[h], v_new[h]) for h in heads]
    for h in heads:
        s_ref[h] = s_old[h] * lane_of(elast_all, h) + _mm_tn(k[h] * lane_of(ekd_all, h), v_new[h])
    for h in heads:
        on = o[h] * lax.rsqrt(jnp.mean(o[h] * o[h], axis=-1, keepdims=True) + EPS) * gain_ref[...]
        z = z_ref[:, h * GDN_DV:(h + 1) * GDN_DV].astype(F32)
        o_ref[:, h * GDN_DV:(h + 1) * GDN_DV] = (on * _silu(z)).astype(o_ref.dtype)


def _gdn_masks():
    c = GDN_BLOCK
    r = np.arange(c)[:, None]
    k = np.arange(c)[None, :]
    mats = [np.eye(c), (r // 8 == k // 8) & (r > k)]
    b = 8
    while b < c:
        mats.append((r // (2 * b) == k // (2 * b)) & ((r // b) % 2 == 1) & ((k // b) % 2 == 0))
        b *= 2
    return jnp.asarray(np.stack([np.asarray(m, np.float32) for m in mats]))


def _conv_band(c, pad):
    band = np.zeros((c, GDN_CONV * (pad + c)), np.float32)
    for i in range(GDN_CONV):
        band[np.arange(c), i * (pad + c) + pad + np.arange(c) - (GDN_CONV - 1) + i] = 1.0
    return jnp.asarray(band, BF16)


def _gdn(proj, ba, conv_w, alog_lane, dtb_lane, gain):
    t = proj.shape[0]
    c = GDN_BLOCK
    pad = 16
    masks = _gdn_masks()
    nm = masks.shape[0]
    band = _conv_band(c, pad)
    return pl.pallas_call(
        _gdn_kernel,
        out_shape=jax.ShapeDtypeStruct((t, GDN_V), BF16),
        grid=(t // c,),
        in_specs=[pl.BlockSpec((c, GDN_CONV_CH), lambda i: (i, 0)),
                  pl.BlockSpec((pad, GDN_CONV_CH), lambda i: (jnp.maximum(i * (c // pad) - 1, 0), 0)),
                  pl.BlockSpec((c, GDN_V), lambda i: (i, COL_ZA // GDN_V)),
                  pl.BlockSpec((c, LANES), lambda i: (i, 0)),
                  pl.BlockSpec((GDN_CONV, GDN_CONV_CH), lambda i: (0, 0)),
                  pl.BlockSpec(band.shape, lambda i: (0, 0)),
                  pl.BlockSpec((1, LANES), lambda i: (0, 0)),
                  pl.BlockSpec((1, LANES), lambda i: (0, 0)),
                  pl.BlockSpec((1, GDN_DV), lambda i: (0, 0)),
                  pl.BlockSpec((nm, c, c), lambda i: (0, 0, 0))],
        out_specs=pl.BlockSpec((c, GDN_V), lambda i: (i, 0)),
        scratch_shapes=[pltpu.VMEM((GDN_HEADS, GDN_DK, GDN_DV), F32)],
        compiler_params=_params(("arbitrary",)),
        name="gdn",
    )(proj, proj, proj, ba, conv_w, band, alog_lane, dtb_lane, gain, masks)


def _swa_kernel(q_ref, kc_ref, kp_ref, vc_ref, vp_ref, z_ref, bias_ref, qg_ref, kg_ref, seg_ref, o_ref):
    n = pl.program_id(0)
    qb = SWA_BLOCK
    hd = SWA_HEAD_DIM
    n_tiles = SWA_Q // LANES
    lane = lax.broadcasted_iota(jnp.int32, (1, LANES), 1)
    lo = lane < hd
    seg = seg_ref[...].astype(BF16)

    def qk_norm(a, gain):
        ms = jnp.dot((a * a).astype(BF16), seg, preferred_element_type=F32)
        return a * (lax.rsqrt(ms + EPS) * gain)

    krow = lax.broadcasted_iota(jnp.int32, (2 * qb, 1), 0)
    is_sink = krow == 0
    key_live = (krow >= qb) | (n > 0)
    ones_col = jnp.where(key_live | is_sink, 1.0, 0.0)
    ones2 = jnp.concatenate([jnp.where(lo, ones_col, 0.0), jnp.where(lo, 0.0, ones_col)], axis=0)

    kcat = jnp.concatenate([kp_ref[...], kc_ref[...]], axis=0).astype(F32)
    vcat = jnp.concatenate([vp_ref[...], vc_ref[...]], axis=0).astype(F32)
    k2s, rhs = [], []
    for u in range(SWA_KV // LANES):
        kt = jnp.where(is_sink, 0.0, qk_norm(kcat[:, u * LANES:(u + 1) * LANES], kg_ref[...]))
        vt = jnp.where(key_live & ~is_sink, vcat[:, u * LANES:(u + 1) * LANES], 0.0)
        kt_r = pltpu.roll(kt, hd, axis=1)
        vt_r = pltpu.roll(vt, hd, axis=1)
        for half in range(2):
            k_lo, k_hi, v_lo, v_hi = (kt, kt_r, vt, vt_r) if half == 0 else (kt_r, kt, vt_r, vt)
            k2s.append(jnp.concatenate([jnp.where(lo, k_lo, 0.0), jnp.where(lo, 0.0, k_hi)],
                                       axis=0).astype(BF16))
            v2 = jnp.concatenate([jnp.where(lo, v_lo, 0.0), jnp.where(lo, 0.0, v_hi)], axis=0)
            rhs.append(jnp.concatenate([v2, ones2], axis=1).astype(BF16))

    q_gain = qg_ref[...] * (hd ** -0.5 * LOG2E)
    kv_of = lambda tq: (2 * tq) // (SWA_HEADS // SWA_KV_HEADS)
    qt = [qk_norm(q_ref[:, tq * LANES:(tq + 1) * LANES].astype(F32), q_gain) for tq in range(n_tiles)]
    logits = [_mm_nt(qt[tq], k2s[kv_of(tq)]) for tq in range(n_tiles)]
    probs = []
    for tq in range(n_tiles):
        ps = []
        for e in range(2):
            s = logits[tq][:, e * 2 * qb:(e + 1) * 2 * qb] + bias_ref[2 * tq + e]
            ps.append(jnp.exp2(s - jnp.max(s, axis=-1, keepdims=True)).astype(BF16))
        probs.append(jnp.concatenate(ps, axis=1))
    pv = [jnp.dot(probs[tq], rhs[kv_of(tq)], preferred_element_type=F32) for tq in range(n_tiles)]
    for tq in range(n_tiles):
        out = pv[tq][:, :LANES] * (1.0 / pv[tq][:, LANES:])
        z = z_ref[:, tq * LANES:(tq + 1) * LANES].astype(F32)
        o_ref[:, tq * LANES:(tq + 1) * LANES] = (out * _silu(z)).astype(o_ref.dtype)


def _swa(proj, bias, qg2, kg2):
    t = proj.shape[0]
    qb = SWA_BLOCK
    seg = np.kron(np.eye(LANES // SWA_HEAD_DIM), np.ones((SWA_HEAD_DIM, SWA_HEAD_DIM))) / SWA_HEAD_DIM
    seg = jnp.asarray(seg, F32)
    kcol = COL_KB // SWA_KV
    vcol = COL_VB // SWA_KV
    prev = lambda i: jnp.maximum(i - 1, 0)
    return pl.pallas_call(
        _swa_kernel,
        out_shape=jax.ShapeDtypeStruct((t, SWA_Q), BF16),
        grid=(t // qb,),
        in_specs=[pl.BlockSpec((qb, SWA_Q), lambda i: (i, COL_QB // SWA_Q)),
                  pl.BlockSpec((qb, SWA_KV), lambda i: (i, kcol)),
                  pl.BlockSpec((qb, SWA_KV), lambda i: (prev(i), kcol)),
                  pl.BlockSpec((qb, SWA_KV), lambda i: (i, vcol)),
                  pl.BlockSpec((qb, SWA_KV), lambda i: (prev(i), vcol)),
                  pl.BlockSpec((qb, SWA_Q), lambda i: (i, COL_ZB // SWA_Q)),
                  pl.BlockSpec((SWA_HEADS, qb, 2 * qb), lambda i: (0, 0, 0)),
                  pl.BlockSpec((1, LANES), lambda i: (0, 0)),
                  pl.BlockSpec((1, LANES), lambda i: (0, 0)),
                  pl.BlockSpec((LANES, LANES), lambda i: (0, 0))],
        out_specs=pl.BlockSpec((qb, SWA_Q), lambda i: (i, 0)),
        compiler_params=_params(("arbitrary",)),
        name="swa",
    )(proj, proj, proj, proj, proj, proj, bias, qg2, kg2, seg)


def _merge_out_kernel(x_ref, oa_ref, ob_ref, ga_ref, gb_ref, gate_ref, wa_ref, wb_ref, wo_ref, o_ref):
    ya = jnp.dot(oa_ref[...], wa_ref[...], preferred_element_type=F32)
    yb = jnp.dot(ob_ref[...], wb_ref[...], preferred_element_type=F32)
    mixed = _sigmoid(ga_ref[...].astype(F32)) * ya + _sigmoid(gb_ref[...].astype(F32)) * yb
    y = jnp.dot(mixed.astype(BF16), wo_ref[...], preferred_element_type=F32)
    o_ref[...] = x_ref[...] + gate_ref[...] * y


def _merge_out(x2d, o_a, o_b, proj, gate, w_a, w_b, w_o, *, tm):
    t, d = x2d.shape
    const = lambda shape: pl.BlockSpec(shape, lambda i: (0, 0), pipeline_mode=pl.Buffered(1))
    return pl.pallas_call(
        _merge_out_kernel,
        out_shape=jax.ShapeDtypeStruct((t, d), F32),
        grid=(t // tm,),
        in_specs=[pl.BlockSpec((tm, d), lambda i: (i, 0)),
                  pl.BlockSpec((tm, GDN_V), lambda i: (i, 0)),
                  pl.BlockSpec((tm, SWA_Q), lambda i: (i, 0)),
                  pl.BlockSpec((tm, d), lambda i: (i, COL_GA // D_MODEL)),
                  pl.BlockSpec((tm, d), lambda i: (i, COL_GB // D_MODEL)),
                  pl.BlockSpec((1, d), lambda i: (0, 0)),
                  const((GDN_V, d)), const((SWA_Q, d)), const((d, d))],
        out_specs=pl.BlockSpec((tm, d), lambda i: (i, 0)),
        compiler_params=_params(("arbitrary",)),
        name="merge_out",
    )(x2d, o_a, o_b, proj, proj, gate, w_a, w_b, w_o)


def _repack_w_in(w):
    sizes = (GDN_QK, GDN_QK, GDN_V, GDN_V, GDN_HEADS, GDN_HEADS, SWA_Q, SWA_KV, SWA_KV, SWA_Q, D_MODEL, D_MODEL)
    offs = np.concatenate([[0], np.cumsum(sizes)])
    seg = lambda i: w[:, offs[i]:offs[i + 1]]
    big = jnp.concatenate([seg(0), seg(1), seg(2), seg(3), seg(6), seg(9), seg(10), seg(11), seg(7), seg(8)],
                          axis=1).astype(BF16)
    small = jnp.concatenate([seg(4), seg(5)], axis=1)
    small = jnp.pad(small, ((0, 0), (0, LANES - small.shape[1]))).astype(BF16)
    return big, small


def _lane_row(vec, offset):
    return jnp.pad(vec.astype(F32), (offset, LANES - offset - vec.shape[0]))[None, :]


def kernel(x, c, w_ada, b_ada, norm_gain, w_in, conv_w, a_log, dt_bias, gdn_norm_gain, q_norm_gain,
           k_norm_gain, sinks, rel_bias, w_branch_gdn, w_branch_swa, w_out):
    bsz, t, d = x.shape
    depth = w_in.shape[0]
    outs = []
    for b in range(bsz):
        xb = x[b]
        c_col = c[b].astype(F32)[:, None]
        for l in range(depth):
            mod = _ada_mod(c_col, w_ada[l], b_ada[l][None, :])
            shift, scale, gate = mod[:, :d], mod[:, d:2 * d], mod[:, 2 * d:]
            w_big, w_small = _repack_w_in(w_in[l])
            proj, ba = _in_proj(xb, norm_gain[l][None, :], scale, shift, w_big, w_small,
                                tm=min(1024, t), tn=768)
            o_a = _gdn(proj, ba, conv_w[l], _lane_row(a_log[l], GDN_HEADS), _lane_row(dt_bias[l], GDN_HEADS),
                       gdn_norm_gain[l][None, :])
            bias = _swa_bias(rel_bias.T.astype(F32), sinks[l].astype(F32))
            o_b = _swa(proj, bias,
                       jnp.tile(q_norm_gain[l], LANES // SWA_HEAD_DIM)[None, :],
                       jnp.tile(k_norm_gain[l], LANES // SWA_HEAD_DIM)[None, :])
            xb = _merge_out(xb, o_a, o_b, proj, gate, w_branch_gdn[l].astype(BF16),
                            w_branch_swa[l].astype(BF16), w_out[l].astype(BF16), tm=min(512, t))
        outs.append(xb)
    return jnp.stack(outs, axis=0)
```

```python
import functools
import math

import jax
import jax.numpy as jnp
import numpy as np
from jax import lax
from jax.experimental import pallas as pl
from jax.experimental.pallas import tpu as pltpu

F32 = jnp.float32
BF16 = jnp.bfloat16

LANES = 128
D_MODEL = 2048
GDN_HEADS = 8
GDN_DK = 128
GDN_DV = 128
GDN_CONV = 4
GDN_QK = GDN_HEADS * GDN_DK
GDN_V = GDN_HEADS * GDN_DV
GDN_CONV_CH = 2 * GDN_QK + GDN_V
GDN_BLOCK = 128
SWA_HEADS = 16
SWA_KV_HEADS = 4
SWA_HEAD_DIM = 64
SWA_WINDOW = 128
SWA_BLOCK = 128
SWA_Q = SWA_HEADS * SWA_HEAD_DIM
SWA_KV = SWA_KV_HEADS * SWA_HEAD_DIM
REL_BUCKETS = 32
REL_MAX_DIST = 128
EPS = 1e-6
NEG_BIG = -1e30
LOG2E = math.log2(math.e)

COL_QKV = 0
COL_ZA = GDN_CONV_CH
COL_QB = COL_ZA + GDN_V
COL_ZB = COL_QB + SWA_Q
COL_GA = COL_ZB + SWA_Q
COL_GB = COL_GA + D_MODEL
COL_KB = COL_GB + D_MODEL
COL_VB = COL_KB + SWA_KV
PROJ_COLS = COL_VB + SWA_KV
for _col, _width in ((COL_ZA, GDN_V), (COL_QB, SWA_Q), (COL_ZB, SWA_Q), (COL_GA, D_MODEL),
                     (COL_GB, D_MODEL), (COL_KB, SWA_KV), (COL_VB, SWA_KV)):
    assert _col % _width == 0

VMEM_LIMIT = 56 * 1024 * 1024


def _sigmoid(x):
    return 1.0 / (1.0 + jnp.exp(-x))


def _silu(x):
    return x * _sigmoid(x)


def _params(sem):
    return pltpu.CompilerParams(dimension_semantics=sem, vmem_limit_bytes=VMEM_LIMIT)


def _ada_mod_kernel(c_ref, w_ref, b_ref, o_ref):
    c = c_ref[...]
    o_ref[...] = jnp.sum(_silu(c) * w_ref[...], axis=0, keepdims=True) + b_ref[...]


def _ada_mod(c_col, w_ada, b_ada):
    d, n = w_ada.shape
    tn = 512
    return pl.pallas_call(
        _ada_mod_kernel,
        out_shape=jax.ShapeDtypeStruct((1, n), F32),
        grid=(n // tn,),
        in_specs=[pl.BlockSpec((d, 1), lambda j: (0, 0)),
                  pl.BlockSpec((d, tn), lambda j: (0, j)),
                  pl.BlockSpec((1, tn), lambda j: (0, j))],
        out_specs=pl.BlockSpec((1, tn), lambda j: (0, j)),
        compiler_params=_params(("arbitrary",)),
        name="ada_mod",
    )(c_col, w_ada, b_ada)


def _in_proj_kernel(x_ref, gain_ref, scale_ref, shift_ref, w_ref, ws_ref, o_ref, ba_ref, h_ref,
                    *, row_chunk):
    j = pl.program_id(1)

    @pl.when(j == 0)
    def _():
        gs = gain_ref[...] * (1.0 + scale_ref[...])
        sh = shift_ref[...]
        tm = x_ref.shape[0]

        def body(r, carry):
            rows = pl.ds(pl.multiple_of(r * row_chunk, row_chunk), row_chunk)
            x = x_ref[rows, :]
            ms = jnp.mean(x * x, axis=-1, keepdims=True)
            h = (x * lax.rsqrt(ms + EPS)) * gs + sh
            h_ref[rows, :] = h.astype(BF16)
            return carry

        lax.fori_loop(0, tm // row_chunk, body, 0)
        ba_ref[...] = jnp.dot(h_ref[...], ws_ref[...], preferred_element_type=F32)

    o_ref[...] = jnp.dot(h_ref[...], w_ref[...], preferred_element_type=F32).astype(o_ref.dtype)


def _in_proj(x2d, gain, scale, shift, w_big, w_small, *, tm, tn):
    t, d = x2d.shape
    n = w_big.shape[1]
    return pl.pallas_call(
        functools.partial(_in_proj_kernel, row_chunk=128),
        out_shape=(jax.ShapeDtypeStruct((t, n), BF16), jax.ShapeDtypeStruct((t, LANES), F32)),
        grid=(t // tm, n // tn),
        in_specs=[pl.BlockSpec((tm, d), lambda i, j: (i, 0)),
                  pl.BlockSpec((1, d), lambda i, j: (0, 0)),
                  pl.BlockSpec((1, d), lambda i, j: (0, 0)),
                  pl.BlockSpec((1, d), lambda i, j: (0, 0)),
                  pl.BlockSpec((d, tn), lambda i, j: (0, j)),
                  pl.BlockSpec((d, LANES), lambda i, j: (0, 0))],
        out_specs=(pl.BlockSpec((tm, tn), lambda i, j: (i, j)),
                   pl.BlockSpec((tm, LANES), lambda i, j: (i, 0))),
        scratch_shapes=[pltpu.VMEM((tm, d), BF16)],
        compiler_params=_params(("arbitrary", "arbitrary")),
        name="in_proj",
    )(x2d, gain, scale, shift, w_big, w_small)


def _swa_bias_kernel(tab_ref, sink_ref, o_ref):
    h = pl.program_id(0)
    q = SWA_BLOCK
    qpos = lax.broadcasted_iota(jnp.int32, (q, 2 * q), 0) + q
    kpos = lax.broadcasted_iota(jnp.int32, (q, 2 * q), 1)
    dist = qpos - kpos
    in_window = (dist >= 0) & (dist < SWA_WINDOW)
    d = jnp.maximum(dist, 0)
    max_exact = REL_BUCKETS // 2
    df = jnp.maximum(d, 1).astype(F32)
    large = max_exact + (jnp.log(df / max_exact) / math.log(REL_MAX_DIST / max_exact)
                         * (REL_BUCKETS - max_exact)).astype(jnp.int32)
    large = jnp.minimum(large, REL_BUCKETS - 1)
    bucket = jnp.where(d < max_exact, d, large)
    acc = jnp.zeros((q, 2 * q), F32)
    for b in range(REL_BUCKETS):
        acc = jnp.where(bucket == b, tab_ref[h, b], acc)
    table = jnp.where(in_window, acc, NEG_BIG)
    o_ref[0] = jnp.where(kpos == 0, sink_ref[h], table) * LOG2E


def _swa_bias(rel_bias_t, sinks):
    q = SWA_BLOCK
    return pl.pallas_call(
        _swa_bias_kernel,
        out_shape=jax.ShapeDtypeStruct((SWA_HEADS, q, 2 * q), F32),
        grid=(SWA_HEADS,),
        in_specs=[pl.BlockSpec(memory_space=pltpu.SMEM), pl.BlockSpec(memory_space=pltpu.SMEM)],
        out_specs=pl.BlockSpec((1, q, 2 * q), lambda h: (h, 0, 0)),
        compiler_params=_params(("arbitrary",)),
        name="swa_bias",
    )(rel_bias_t, sinks)


def _chunk_cumsum_rows(x):
    n = x.shape[0]
    row = lax.broadcasted_iota(jnp.int32, x.shape, 0)
    s = 1
    while s < n:
        x = x + jnp.where(row >= s, pltpu.roll(x, s, axis=0), 0.0)
        s *= 2
    return x


def _mm(a, b):
    return jnp.dot(a.astype(BF16), b.astype(BF16), preferred_element_type=F32)


def _mm_nt(a, b):
    return lax.dot_general(a.astype(BF16), b.astype(BF16), (((1,), (1,)), ((), ())),
                           preferred_element_type=F32)


def _mm_tn(a, b):
    return lax.dot_general(a.astype(BF16), b.astype(BF16), (((0,), (0,)), ((), ())),
                           preferred_element_type=F32)


def _unit_lower_inverses(l_mats, masks_ref):
    eye = masks_ref[0]
    m0 = [l * masks_ref[1] for l in l_mats]
    x = [eye - m for m in m0]
    p = [_mm(m, m) for m in m0]
    x = [xi + _mm(xi, pi) for xi, pi in zip(x, p)]
    p = [_mm(pi, pi) for pi in p]
    x = [xi + _mm(xi, pi) for xi, pi in zip(x, p)]
    for lvl in range(2, masks_ref.shape[0]):
        nx = [_mm(l * masks_ref[lvl], xi) for l, xi in zip(l_mats, x)]
        x = [xi - _mm(xi, ni) for xi, ni in zip(x, nx)]
    return x


def _gdn_kernel(cur_ref, prev_ref, z_ref, ba_ref, convw_ref, band_ref, alog_ref, dtb_ref, gain_ref,
                masks_ref, o_ref, s_ref):
    t = pl.program_id(0)
    c = GDN_BLOCK

    @pl.when(t == 0)
    def _():
        s_ref[...] = jnp.zeros_like(s_ref)

    prev = jnp.where(t > 0, prev_ref[...], jnp.zeros_like(prev_ref))
    xcat = jnp.concatenate([prev, cur_ref[...]], axis=0)
    conv_w = convw_ref[...].astype(BF16)
    taps = jnp.concatenate([xcat * conv_w[i:i + 1, :] for i in range(GDN_CONV)], axis=0)
    conv = jnp.dot(band_ref[...], taps, preferred_element_type=F32)

    ba = ba_ref[...]
    beta_all = _sigmoid(ba)
    xg = ba + dtb_ref[...]
    softplus = jnp.maximum(xg, 0.0) + jnp.log(1.0 + jnp.exp(-jnp.abs(xg)))
    g_all = -jnp.exp(alog_ref[...]) * softplus
    gc = _chunk_cumsum_rows(g_all)
    gc_t = gc.T
    g_last = gc[c - 1:c, :]
    eg_all = jnp.exp(gc)
    ekd_all = jnp.exp(g_last - gc)
    elast_all = jnp.exp(g_last)

    row = lax.broadcasted_iota(jnp.int32, (c, c), 0)
    col = lax.broadcasted_iota(jnp.int32, (c, c), 1)
    causal = row >= col
    strict = row > col

    def conv_tile(col0):
        return _silu(conv[:, col0:col0 + LANES])

    def l2n(a):
        return a * lax.rsqrt(jnp.sum(a * a, axis=-1, keepdims=True) + EPS)

    heads = range(GDN_HEADS)
    lane_of = lambda a, h: a[:, GDN_HEADS + h:GDN_HEADS + h + 1]
    q = [l2n(conv_tile(h * GDN_DK)) * (GDN_DK ** -0.5) for h in heads]
    k = [l2n(conv_tile(GDN_QK + h * GDN_DK)) for h in heads]
    v = [conv_tile(2 * GDN_QK + h * GDN_DV) for h in heads]
    beta = [beta_all[:, h:h + 1] for h in heads]
    eg = [lane_of(eg_all, h) for h in heads]
    s_old = [s_ref[h] for h in heads]
    o_inter = [_mm(q[h] * eg[h], s_old[h]) for h in heads]
    kb = [k[h] * beta[h] for h in heads]
    kk = [_mm_nt(kb[h], k[h]) for h in heads]
    qk = [_mm_nt(q[h], k[h]) for h in heads]
    decay = []
    for h in heads:
        gdiff = lane_of(gc, h) - gc_t[GDN_HEADS + h:GDN_HEADS + h + 1, :]
        decay.append(jnp.where(causal, jnp.exp(jnp.where(causal, gdiff, 0.0)), 0.0))
    l_mat = [jnp.where(strict, kk[h] * decay[h], 0.0) for h in heads]
    attn = [qk[h] * decay[h] for h in heads]
    t_inv = _unit_lower_inverses(l_mat, masks_ref)
    uw = [_mm(t_inv[h], jnp.concatenate([v[h] * beta[h], kb[h] * eg[h]], axis=1)) for h in heads]
    v_new = [uw[h][:, :GDN_DV] - _mm(uw[h][:, GDN_DV:], s_old[h]) for h in heads]
    o = [o_inter[h] + _mm(attn[h], v_new[h]) for h in heads]
    for h in heads:
        s_ref[h] = s_old[h] * lane_of(elast_all, h) + _mm_tn(k[h] * lane_of(ekd_all, h), v_new[h])
    for h in heads:
        on = o[h] * lax.rsqrt(jnp.mean(o[h] * o[h], axis=-1, keepdims=True) + EPS) * gain_ref[...]
        z = z_ref[:, h * GDN_DV:(h + 1) * GDN_DV].astype(F32)
        o_ref[:, h * GDN_DV:(h + 1) * GDN_DV] = (on * _silu(z)).astype(o_ref.dtype)


def _gdn_masks():
    c = GDN_BLOCK
    r = np.arange(c)[:, None]
    k = np.arange(c)[None, :]
    mats = [np.eye(c), (r // 8 == k // 8) & (r > k)]
    b = 8
    while b < c:
        mats.append((r // (2 * b) == k // (2 * b)) & ((r // b) % 2 == 1) & ((k // b) % 2 == 0))
        b *= 2
    return jnp.asarray(np.stack([np.asarray(m, np.float32) for m in mats]))


def _conv_band(c, pad):
    band = np.zeros((c, GDN_CONV * (pad + c)), np.float32)
    for i in range(GDN_CONV):
        band[np.arange(c), i * (pad + c) + pad + np.arange(c) - (GDN_CONV - 1) + i] = 1.0
    return jnp.asarray(band, BF16)


def _gdn(proj, ba, conv_w, alog_lane, dtb_lane, gain):
    t = proj.shape[0]
    c = GDN_BLOCK
    pad = 16
    masks = _gdn_masks()
    nm = masks.shape[0]
    band = _conv_band(c, pad)
    return pl.pallas_call(
        _gdn_kernel,
        out_shape=jax.ShapeDtypeStruct((t, GDN_V), BF16),
        grid=(t // c,),
        in_specs=[pl.BlockSpec((c, GDN_CONV_CH), lambda i: (i, 0)),
                  pl.BlockSpec((pad, GDN_CONV_CH), lambda i: (jnp.maximum(i * (c // pad) - 1, 0), 0)),
                  pl.BlockSpec((c, GDN_V), lambda i: (i, COL_ZA // GDN_V)),
                  pl.BlockSpec((c, LANES), lambda i: (i, 0)),
                  pl.BlockSpec((GDN_CONV, GDN_CONV_CH), lambda i: (0, 0)),
                  pl.BlockSpec(band.shape, lambda i: (0, 0)),
                  pl.BlockSpec((1, LANES), lambda i: (0, 0)),
                  pl.BlockSpec((1, LANES), lambda i: (0, 0)),
                  pl.BlockSpec((1, GDN_DV), lambda i: (0, 0)),
                  pl.BlockSpec((nm, c, c), lambda i: (0, 0, 0))],
        out_specs=pl.BlockSpec((c, GDN_V), lambda i: (i, 0)),
        scratch_shapes=[pltpu.VMEM((GDN_HEADS, GDN_DK, GDN_DV), F32)],
        compiler_params=_params(("arbitrary",)),
        name="gdn",
    )(proj, proj, proj, ba, conv_w, band, alog_lane, dtb_lane, gain, masks)


def _swa_kernel(q_ref, kc_ref, kp_ref, vc_ref, vp_ref, z_ref, bias_ref, qg_ref, kg_ref, seg_ref, o_ref):
    n = pl.program_id(0)
    qb = SWA_BLOCK
    hd = SWA_HEAD_DIM
    n_tiles = SWA_Q // LANES
    lane = lax.broadcasted_iota(jnp.int32, (1, LANES), 1)
    lo = lane < hd
    seg = seg_ref[...].astype(BF16)

    def qk_norm(a, gain):
        ms = jnp.dot((a * a).astype(BF16), seg, preferred_element_type=F32)
        return a * (lax.rsqrt(ms + EPS) * gain)

    krow = lax.broadcasted_iota(jnp.int32, (2 * qb, 1), 0)
    is_sink = krow == 0
    key_live = (krow >= qb) | (n > 0)
    ones_col = jnp.where(key_live | is_sink, 1.0, 0.0)
    ones2 = jnp.concatenate([jnp.where(lo, ones_col, 0.0), jnp.where(lo, 0.0, ones_col)], axis=0)

    kcat = jnp.concatenate([kp_ref[...], kc_ref[...]], axis=0).astype(F32)
    vcat = jnp.concatenate([vp_ref[...], vc_ref[...]], axis=0).astype(F32)
    k2s, rhs = [], []
    for u in range(SWA_KV // LANES):
        kt = jnp.where(is_sink, 0.0, qk_norm(kcat[:, u * LANES:(u + 1) * LANES], kg_ref[...]))
        vt = jnp.where(key_live & ~is_sink, vcat[:, u * LANES:(u + 1) * LANES], 0.0)
        kt_r = pltpu.roll(kt, hd, axis=1)
        vt_r = pltpu.roll(vt, hd, axis=1)
        for half in range(2):
            k_lo, k_hi, v_lo, v_hi = (kt, kt_r, vt, vt_r) if half == 0 else (kt_r, kt, vt_r, vt)
            k2s.append(jnp.concatenate([jnp.where(lo, k_lo, 0.0), jnp.where(lo, 0.0, k_hi)],
                                       axis=0).astype(BF16))
            v2 = jnp.concatenate([jnp.where(lo, v_lo, 0.0), jnp.where(lo, 0.0, v_hi)], axis=0)
            rhs.append(jnp.concatenate([v2, ones2], axis=1).astype(BF16))

    q_gain = qg_ref[...] * (hd ** -0.5 * LOG2E)
    kv_of = lambda tq: (2 * tq) // (SWA_HEADS // SWA_KV_HEADS)
    qt = [qk_norm(q_ref[:, tq * LANES:(tq + 1) * LANES].astype(F32), q_gain) for tq in range(n_tiles)]
    logits = [_mm_nt(qt[tq], k2s[kv_of(tq)]) for tq in range(n_tiles)]
    probs = []
    for tq in range(n_tiles):
        ps = []
        for e in range(2):
            s = logits[tq][:, e * 2 * qb:(e + 1) * 2 * qb] + bias_ref[2 * tq + e]
            ps.append(jnp.exp2(s - jnp.max(s, axis=-1, keepdims=True)).astype(BF16))
        probs.append(jnp.concatenate(ps, axis=1))
    pv = [jnp.dot(probs[tq], rhs[kv_of(tq)], preferred_element_type=F32) for tq in range(n_tiles)]
    for tq in range(n_tiles):
        out = pv[tq][:, :LANES] * (1.0 / pv[tq][:, LANES:])
        z = z_ref[:, tq * LANES:(tq + 1) * LANES].astype(F32)
        o_ref[:, tq * LANES:(tq + 1) * LANES] = (out * _silu(z)).astype(o_ref.dtype)


def _swa(proj, bias, qg2, kg2):
    t = proj.shape[0]
    qb = SWA_BLOCK
    seg = np.kron(np.eye(LANES // SWA_HEAD_DIM), np.ones((SWA_HEAD_DIM, SWA_HEAD_DIM))) / SWA_HEAD_DIM
    seg = jnp.asarray(seg, F32)
    kcol = COL_KB // SWA_KV
    vcol = COL_VB // SWA_KV
    prev = lambda i: jnp.maximum(i - 1, 0)
    return pl.pallas_call(
        _swa_kernel,
        out_shape=jax.ShapeDtypeStruct((t, SWA_Q), BF16),
        grid=(t // qb,),
        in_specs=[pl.BlockSpec((qb, SWA_Q), lambda i: (i, COL_QB // SWA_Q)),
                  pl.BlockSpec((qb, SWA_KV), lambda i: (i, kcol)),
                  pl.BlockSpec((qb, SWA_KV), lambda i: (prev(i), kcol)),
                  pl.BlockSpec((qb, SWA_KV), lambda i: (i, vcol)),
                  pl.BlockSpec((qb, SWA_KV), lambda i: (prev(i), vcol)),
                  pl.BlockSpec((qb, SWA_Q), lambda i: (i, COL_ZB // SWA_Q)),
                  pl.BlockSpec((SWA_HEADS, qb, 2 * qb), lambda i: (0, 0, 0)),
                  pl.BlockSpec((1, LANES), lambda i: (0, 0)),
                  pl.BlockSpec((1, LANES), lambda i: (0, 0)),
                  pl.BlockSpec((LANES, LANES), lambda i: (0, 0))],
        out_specs=pl.BlockSpec((qb, SWA_Q), lambda i: (i, 0)),
        compiler_params=_params(("arbitrary",)),
        name="swa",
    )(proj, proj, proj, proj, proj, proj, bias, qg2, kg2, seg)


def _merge_out_kernel(x_ref, oa_ref, ob_ref, ga_ref, gb_ref, gate_ref, wa_ref, wb_ref, wo_ref, o_ref):
    ya = jnp.dot(oa_ref[...], wa_ref[...], preferred_element_type=F32)
    yb = jnp.dot(ob_ref[...], wb_ref[...], preferred_element_type=F32)
    mixed = _sigmoid(ga_ref[...].astype(F32)) * ya + _sigmoid(gb_ref[...].astype(F32)) * yb
    y = jnp.dot(mixed.astype(BF16), wo_ref[...], preferred_element_type=F32)
    o_ref[...] = x_ref[...] + gate_ref[...] * y


def _merge_out(x2d, o_a, o_b, proj, gate, w_a, w_b, w_o, *, tm):
    t, d = x2d.shape
    const = lambda shape: pl.BlockSpec(shape, lambda i: (0, 0), pipeline_mode=pl.Buffered(1))
    return pl.pallas_call(
        _merge_out_kernel,
        out_shape=jax.ShapeDtypeStruct((t, d), F32),
        grid=(t // tm,),
        in_specs=[pl.BlockSpec((tm, d), lambda i: (i, 0)),
                  pl.BlockSpec((tm, GDN_V), lambda i: (i, 0)),
                  pl.BlockSpec((tm, SWA_Q), lambda i: (i, 0)),
                  pl.BlockSpec((tm, d), lambda i: (i, COL_GA // D_MODEL)),
                  pl.BlockSpec((tm, d), lambda i: (i, COL_GB // D_MODEL)),
                  pl.BlockSpec((1, d), lambda i: (0, 0)),
                  const((GDN_V, d)), const((SWA_Q, d)), const((d, d))],
        out_specs=pl.BlockSpec((tm, d), lambda i: (i, 0)),
        compiler_params=_params(("arbitrary",)),
        name="merge_out",
    )(x2d, o_a, o_b, proj, proj, gate, w_a, w_b, w_o)


REPACK_TILE = 512
N_GATE_COLS = 2 * GDN_HEADS


def _repack_plan():
    src_of = {COL_QKV: 0, COL_ZA: GDN_CONV_CH}
    after_gates = COL_QB + N_GATE_COLS
    src_of.update({COL_QB: after_gates, COL_KB: after_gates + SWA_Q, COL_VB: after_gates + SWA_Q + SWA_KV,
                   COL_ZB: after_gates + SWA_Q + 2 * SWA_KV, COL_GA: after_gates + 2 * SWA_Q + 2 * SWA_KV,
                   COL_GB: after_gates + 2 * SWA_Q + 2 * SWA_KV + D_MODEL})
    starts = sorted(src_of)
    src = []
    for dst in range(0, PROJ_COLS, REPACK_TILE):
        grp = max(s for s in starts if s <= dst)
        src.append(src_of[grp] + dst - grp)
    return np.asarray(src, np.int32)


def _repack_kernel(tab_ref, a_ref, b_ref, big_ref, small_ref, *, n_aligned):
    del tab_ref
    o = pl.program_id(0)
    a = a_ref[...]

    @pl.when(o < n_aligned)
    def _():
        big_ref[...] = a.astype(BF16)

    @pl.when(o >= n_aligned)
    def _():
        x = jnp.concatenate([a, b_ref[...]], axis=1)
        big_ref[...] = x[:, N_GATE_COLS:N_GATE_COLS + REPACK_TILE].astype(BF16)

    @pl.when(o == n_aligned)
    def _():
        lane = lax.broadcasted_iota(jnp.int32, (1, LANES), 1)
        small_ref[...] = jnp.where(lane < N_GATE_COLS, a[:, :LANES], 0.0).astype(BF16)


def _repack_w_in(w):
    d = w.shape[0]
    src = _repack_plan()
    n_aligned = int(np.sum(src % LANES == 0))
    assert np.all(src[:n_aligned] % REPACK_TILE == 0)
    assert np.all((src[n_aligned:] - N_GATE_COLS) % REPACK_TILE == 0) and src[n_aligned] == COL_QB + N_GATE_COLS
    a_idx = src // REPACK_TILE
    b_idx = (a_idx + 1) * (REPACK_TILE // LANES)
    tab = jnp.asarray(np.stack([a_idx, b_idx]), jnp.int32)
    return pl.pallas_call(
        functools.partial(_repack_kernel, n_aligned=n_aligned),
        out_shape=(jax.ShapeDtypeStruct((d, PROJ_COLS), BF16), jax.ShapeDtypeStruct((d, LANES), BF16)),
        grid_spec=pltpu.PrefetchScalarGridSpec(
            num_scalar_prefetch=1,
            grid=(PROJ_COLS // REPACK_TILE,),
            in_specs=[pl.BlockSpec((d, REPACK_TILE), lambda o, tab: (0, tab[0, o])),
                      pl.BlockSpec((d, LANES), lambda o, tab: (0, tab[1, o]))],
            out_specs=(pl.BlockSpec((d, REPACK_TILE), lambda o, tab: (0, o)),
                       pl.BlockSpec((d, LANES), lambda o, tab: (0, 0)))),
        compiler_params=_params(("arbitrary",)),
        name="repack_w_in",
    )(tab, w, w)


def _lane_row(vec, offset):
    return jnp.pad(vec.astype(F32), (offset, LANES - offset - vec.shape[0]))[None, :]


def kernel(x, c, w_ada, b_ada, norm_gain, w_in, conv_w, a_log, dt_bias, gdn_norm_gain, q_norm_gain,
           k_norm_gain, sinks, rel_bias, w_branch_gdn, w_branch_swa, w_out):
    bsz, t, d = x.shape
    depth = w_in.shape[0]
    outs = []
    for b in range(bsz):
        xb = x[b]
        c_col = c[b].astype(F32)[:, None]
        for l in range(depth):
            mod = _ada_mod(c_col, w_ada[l], b_ada[l][None, :])
            shift, scale, gate = mod[:, :d], mod[:, d:2 * d], mod[:, 2 * d:]
            w_big, w_small = _repack_w_in(w_in[l])
            proj, ba = _in_proj(xb, norm_gain[l][None, :], scale, shift, w_big, w_small,
                                tm=min(1024, t), tn=1792)
            o_a = _gdn(proj, ba, conv_w[l], _lane_row(a_log[l], GDN_HEADS), _lane_row(dt_bias[l], GDN_HEADS),
                       gdn_norm_gain[l][None, :])
            bias = _swa_bias(rel_bias.T.astype(F32), sinks[l].astype(F32))
            o_b = _swa(proj, bias,
                       jnp.tile(q_norm_gain[l], LANES // SWA_HEAD_DIM)[None, :],
                       jnp.tile(k_norm_gain[l], LANES // SWA_HEAD_DIM)[None, :])
            xb = _merge_out(xb, o_a, o_b, proj, gate, w_branch_gdn[l].astype(BF16),
                            w_branch_swa[l].astype(BF16), w_out[l].astype(BF16), tm=min(512, t))
        outs.append(xb)
    return jnp.stack(outs, axis=0)
```

```python
import functools
import math

import jax
import jax.numpy as jnp
import numpy as np
from jax import lax
from jax.experimental import pallas as pl
from jax.experimental.pallas import tpu as pltpu

F32 = jnp.float32
BF16 = jnp.bfloat16

LANES = 128
D_MODEL = 2048
GDN_HEADS = 8
GDN_DK = 128
GDN_DV = 128
GDN_CONV = 4
GDN_QK = GDN_HEADS * GDN_DK
GDN_V = GDN_HEADS * GDN_DV
GDN_CONV_CH = 2 * GDN_QK + GDN_V
GDN_BLOCK = 128
SWA_HEADS = 16
SWA_KV_HEADS = 4
SWA_HEAD_DIM = 64
SWA_WINDOW = 128
SWA_BLOCK = 128
SWA_Q = SWA_HEADS * SWA_HEAD_DIM
SWA_KV = SWA_KV_HEADS * SWA_HEAD_DIM
REL_BUCKETS = 32
REL_MAX_DIST = 128
EPS = 1e-6
NEG_BIG = -1e30
LOG2E = math.log2(math.e)

COL_QKV = 0
COL_ZA = GDN_CONV_CH
COL_QB = COL_ZA + GDN_V
COL_ZB = COL_QB + SWA_Q
COL_GA = COL_ZB + SWA_Q
COL_GB = COL_GA + D_MODEL
COL_KB = COL_GB + D_MODEL
COL_VB = COL_KB + SWA_KV
PROJ_COLS = COL_VB + SWA_KV
for _col, _width in ((COL_ZA, GDN_V), (COL_QB, SWA_Q), (COL_ZB, SWA_Q), (COL_GA, D_MODEL),
                     (COL_GB, D_MODEL), (COL_KB, SWA_KV), (COL_VB, SWA_KV)):
    assert _col % _width == 0

VMEM_LIMIT = 56 * 1024 * 1024


def _sigmoid(x):
    return 1.0 / (1.0 + jnp.exp(-x))


def _silu(x):
    return x * _sigmoid(x)


def _params(sem):
    return pltpu.CompilerParams(dimension_semantics=sem, vmem_limit_bytes=VMEM_LIMIT)


def _ada_mod_kernel(c_ref, w_ref, b_ref, o_ref):
    c = c_ref[...]
    o_ref[...] = jnp.sum(_silu(c) * w_ref[...], axis=0, keepdims=True) + b_ref[...]


def _ada_mod(c_col, w_ada, b_ada):
    d, n = w_ada.shape
    tn = 512
    return pl.pallas_call(
        _ada_mod_kernel,
        out_shape=jax.ShapeDtypeStruct((1, n), F32),
        grid=(n // tn,),
        in_specs=[pl.BlockSpec((d, 1), lambda j: (0, 0)),
                  pl.BlockSpec((d, tn), lambda j: (0, j)),
                  pl.BlockSpec((1, tn), lambda j: (0, j))],
        out_specs=pl.BlockSpec((1, tn), lambda j: (0, j)),
        compiler_params=_params(("arbitrary",)),
        name="ada_mod",
    )(c_col, w_ada, b_ada)


def _in_proj_kernel(x_ref, gain_ref, scale_ref, shift_ref, w_ref, ws_ref, o_ref, ba_ref, h_ref,
                    *, row_chunk):
    j = pl.program_id(1)

    @pl.when(j == 0)
    def _():
        gs = gain_ref[...] * (1.0 + scale_ref[...])
        sh = shift_ref[...]
        tm = x_ref.shape[0]

        def body(r, carry):
            rows = pl.ds(pl.multiple_of(r * row_chunk, row_chunk), row_chunk)
            x = x_ref[rows, :]
            ms = jnp.mean(x * x, axis=-1, keepdims=True)
            h = (x * lax.rsqrt(ms + EPS)) * gs + sh
            h_ref[rows, :] = h.astype(BF16)
            return carry

        lax.fori_loop(0, tm // row_chunk, body, 0)
        ba_ref[...] = _mm_nt(h_ref[...], ws_ref[...])

    o_ref[...] = _mm_nt(h_ref[...], w_ref[...]).astype(o_ref.dtype)


def _in_proj(x2d, gain, scale, shift, w_big, w_small, *, tm, tn):
    t, d = x2d.shape
    n = w_big.shape[0]
    return pl.pallas_call(
        functools.partial(_in_proj_kernel, row_chunk=128),
        out_shape=(jax.ShapeDtypeStruct((t, n), BF16), jax.ShapeDtypeStruct((t, LANES), F32)),
        grid=(t // tm, n // tn),
        in_specs=[pl.BlockSpec((tm, d), lambda i, j: (i, 0)),
                  pl.BlockSpec((1, d), lambda i, j: (0, 0)),
                  pl.BlockSpec((1, d), lambda i, j: (0, 0)),
                  pl.BlockSpec((1, d), lambda i, j: (0, 0)),
                  pl.BlockSpec((tn, d), lambda i, j: (j, 0)),
                  pl.BlockSpec((LANES, d), lambda i, j: (0, 0))],
        out_specs=(pl.BlockSpec((tm, tn), lambda i, j: (i, j)),
                   pl.BlockSpec((tm, LANES), lambda i, j: (i, 0))),
        scratch_shapes=[pltpu.VMEM((tm, d), BF16)],
        compiler_params=_params(("arbitrary", "arbitrary")),
        name="in_proj",
    )(x2d, gain, scale, shift, w_big, w_small)


def _swa_bias_kernel(tab_ref, sink_ref, o_ref):
    h = pl.program_id(0)
    q = SWA_BLOCK
    qpos = lax.broadcasted_iota(jnp.int32, (q, 2 * q), 0) + q
    kpos = lax.broadcasted_iota(jnp.int32, (q, 2 * q), 1)
    dist = qpos - kpos
    in_window = (dist >= 0) & (dist < SWA_WINDOW)
    d = jnp.maximum(dist, 0)
    max_exact = REL_BUCKETS // 2
    df = jnp.maximum(d, 1).astype(F32)
    large = max_exact + (jnp.log(df / max_exact) / math.log(REL_MAX_DIST / max_exact)
                         * (REL_BUCKETS - max_exact)).astype(jnp.int32)
    large = jnp.minimum(large, REL_BUCKETS - 1)
    bucket = jnp.where(d < max_exact, d, large)
    acc = jnp.zeros((q, 2 * q), F32)
    for b in range(REL_BUCKETS):
        acc = jnp.where(bucket == b, tab_ref[h, b], acc)
    table = jnp.where(in_window, acc, NEG_BIG)
    o_ref[0] = jnp.where(kpos == 0, sink_ref[h], table) * LOG2E


def _swa_bias(rel_bias_t, sinks):
    q = SWA_BLOCK
    return pl.pallas_call(
        _swa_bias_kernel,
        out_shape=jax.ShapeDtypeStruct((SWA_HEADS, q, 2 * q), F32),
        grid=(SWA_HEADS,),
        in_specs=[pl.BlockSpec(memory_space=pltpu.SMEM), pl.BlockSpec(memory_space=pltpu.SMEM)],
        out_specs=pl.BlockSpec((1, q, 2 * q), lambda h: (h, 0, 0)),
        compiler_params=_params(("arbitrary",)),
        name="swa_bias",
    )(rel_bias_t, sinks)


def _chunk_cumsum_rows(x):
    n = x.shape[0]
    row = lax.broadcasted_iota(jnp.int32, x.shape, 0)
    s = 1
    while s < n:
        x = x + jnp.where(row >= s, pltpu.roll(x, s, axis=0), 0.0)
        s *= 2
    return x


def _mm(a, b):
    return jnp.dot(a.astype(BF16), b.astype(BF16), preferred_element_type=F32)


def _mm_nt(a, b):
    return lax.dot_general(a.astype(BF16), b.astype(BF16), (((1,), (1,)), ((), ())),
                           preferred_element_type=F32)


def _mm_tn(a, b):
    return lax.dot_general(a.astype(BF16), b.astype(BF16), (((0,), (0,)), ((), ())),
                           preferred_element_type=F32)


def _unit_lower_inverses(l_mats, masks_ref):
    eye = masks_ref[0]
    m0 = [l * masks_ref[1] for l in l_mats]
    x = [eye - m for m in m0]
    p = [_mm(m, m) for m in m0]
    x = [xi + _mm(xi, pi) for xi, pi in zip(x, p)]
    p = [_mm(pi, pi) for pi in p]
    x = [xi + _mm(xi, pi) for xi, pi in zip(x, p)]
    for lvl in range(2, masks_ref.shape[0]):
        nx = [_mm(l * masks_ref[lvl], xi) for l, xi in zip(l_mats, x)]
        x = [xi - _mm(xi, ni) for xi, ni in zip(x, nx)]
    return x


def _gdn_kernel(cur_ref, prev_ref, z_ref, ba_ref, convw_ref, band_ref, alog_ref, dtb_ref, gain_ref,
                masks_ref, o_ref, s_ref):
    t = pl.program_id(0)
    c = GDN_BLOCK

    @pl.when(t == 0)
    def _():
        s_ref[...] = jnp.zeros_like(s_ref)

    prev = jnp.where(t > 0, prev_ref[...], jnp.zeros_like(prev_ref))
    xcat = jnp.concatenate([prev, cur_ref[...]], axis=0)
    conv_w = convw_ref[...].astype(BF16)
    taps = jnp.concatenate([xcat * conv_w[i:i + 1, :] for i in range(GDN_CONV)], axis=0)
    conv = jnp.dot(band_ref[...], taps, preferred_element_type=F32)

    ba = ba_ref[...]
    beta_all = _sigmoid(ba)
    xg = ba + dtb_ref[...]
    softplus = jnp.maximum(xg, 0.0) + jnp.log(1.0 + jnp.exp(-jnp.abs(xg)))
    g_all = -jnp.exp(alog_ref[...]) * softplus
    gc = _chunk_cumsum_rows(g_all)
    gc_t = gc.T
    g_last = gc[c - 1:c, :]
    eg_all = jnp.exp(gc)
    ekd_all = jnp.exp(g_last - gc)
    elast_all = jnp.exp(g_last)

    row = lax.broadcasted_iota(jnp.int32, (c, c), 0)
    col = lax.broadcasted_iota(jnp.int32, (c, c), 1)
    causal = row >= col
    strict = row > col

    def conv_tile(col0):
        return _silu(conv[:, col0:col0 + LANES])

    def l2n(a):
        return a * lax.rsqrt(jnp.sum(a * a, axis=-1, keepdims=True) + EPS)

    heads = range(GDN_HEADS)
    lane_of = lambda a, h: a[:, GDN_HEADS + h:GDN_HEADS + h + 1]
    q = [l2n(conv_tile(h * GDN_DK)) * (GDN_DK ** -0.5) for h in heads]
    k = [l2n(conv_tile(GDN_QK + h * GDN_DK)) for h in heads]
    v = [conv_tile(2 * GDN_QK + h * GDN_DV) for h in heads]
    beta = [beta_all[:, h:h + 1] for h in heads]
    eg = [lane_of(eg_all, h) for h in heads]
    s_old = [s_ref[h] for h in heads]
    o_inter = [_mm(q[h] * eg[h], s_old[h]) for h in heads]
    kb = [k[h] * beta[h] for h in heads]
    kk = [_mm_nt(kb[h], k[h]) for h in heads]
    qk = [_mm_nt(q[h], k[h]) for h in heads]
    decay = []
    for h in heads:
        gdiff = lane_of(gc, h) - gc_t[GDN_HEADS + h:GDN_HEADS + h + 1, :]
        decay.append(jnp.where(causal, jnp.exp(jnp.where(causal, gdiff, 0.0)), 0.0))
    l_mat = [jnp.where(strict, kk[h] * decay[h], 0.0) for h in heads]
    attn = [qk[h] * decay[h] for h in heads]
    t_inv = _unit_lower_inverses(l_mat, masks_ref)
    uw = [_mm(t_inv[h], jnp.concatenate([v[h] * beta[h], kb[h] * eg[h]], axis=1)) for h in heads]
    v_new = [uw[h][:, :GDN_DV] - _mm(uw[h][:, GDN_DV:], s_old[h]) for h in heads]
    o = [o_inter[h] + _mm(attn[h], v_new[h]) for h in heads]
    for h in heads:
        s_ref[h] = s_old[h] * lane_of(elast_all, h) + _mm_tn(k[h] * lane_of(ekd_all, h), v_new[h])
    for h in heads:
        on = o[h] * lax.rsqrt(jnp.mean(o[h] * o[h], axis=-1, keepdims=True) + EPS) * gain_ref[...]
        z = z_ref[:, h * GDN_DV:(h + 1) * GDN_DV].astype(F32)
        o_ref[:, h * GDN_DV:(h + 1) * GDN_DV] = (on * _silu(z)).astype(o_ref.dtype)


def _gdn_masks():
    c = GDN_BLOCK
    r = np.arange(c)[:, None]
    k = np.arange(c)[None, :]
    mats = [np.eye(c), (r // 8 == k // 8) & (r > k)]
    b = 8
    while b < c:
        mats.append((r // (2 * b) == k // (2 * b)) & ((r // b) % 2 == 1) & ((k // b) % 2 == 0))
        b *= 2
    return jnp.asarray(np.stack([np.asarray(m, np.float32) for m in mats]))


def _conv_band(c, pad):
    band = np.zeros((c, GDN_CONV * (pad + c)), np.float32)
    for i in range(GDN_CONV):
        band[np.arange(c), i * (pad + c) + pad + np.arange(c) - (GDN_CONV - 1) + i] = 1.0
    return jnp.asarray(band, BF16)


def _gdn(proj, ba, conv_w, alog_lane, dtb_lane, gain):
    t = proj.shape[0]
    c = GDN_BLOCK
    pad = 16
    masks = _gdn_masks()
    nm = masks.shape[0]
    band = _conv_band(c, pad)
    return pl.pallas_call(
        _gdn_kernel,
        out_shape=jax.ShapeDtypeStruct((t, GDN_V), BF16),
        grid=(t // c,),
        in_specs=[pl.BlockSpec((c, GDN_CONV_CH), lambda i: (i, 0)),
                  pl.BlockSpec((pad, GDN_CONV_CH), lambda i: (jnp.maximum(i * (c // pad) - 1, 0), 0)),
                  pl.BlockSpec((c, GDN_V), lambda i: (i, COL_ZA // GDN_V)),
                  pl.BlockSpec((c, LANES), lambda i: (i, 0)),
                  pl.BlockSpec((GDN_CONV, GDN_CONV_CH), lambda i: (0, 0)),
                  pl.BlockSpec(band.shape, lambda i: (0, 0)),
                  pl.BlockSpec((1, LANES), lambda i: (0, 0)),
                  pl.BlockSpec((1, LANES), lambda i: (0, 0)),
                  pl.BlockSpec((1, GDN_DV), lambda i: (0, 0)),
                  pl.BlockSpec((nm, c, c), lambda i: (0, 0, 0))],
        out_specs=pl.BlockSpec((c, GDN_V), lambda i: (i, 0)),
        scratch_shapes=[pltpu.VMEM((GDN_HEADS, GDN_DK, GDN_DV), F32)],
        compiler_params=_params(("arbitrary",)),
        name="gdn",
    )(proj, proj, proj, ba, conv_w, band, alog_lane, dtb_lane, gain, masks)


def _swa_kernel(q_ref, kc_ref, kp_ref, vc_ref, vp_ref, z_ref, bias_ref, qg_ref, kg_ref, seg_ref, o_ref):
    n = pl.program_id(0)
    qb = SWA_BLOCK
    hd = SWA_HEAD_DIM
    n_tiles = SWA_Q // LANES
    lane = lax.broadcasted_iota(jnp.int32, (1, LANES), 1)
    lo = lane < hd
    seg = seg_ref[...].astype(BF16)

    def qk_norm(a, gain):
        ms = jnp.dot((a * a).astype(BF16), seg, preferred_element_type=F32)
        return a * (lax.rsqrt(ms + EPS) * gain)

    krow = lax.broadcasted_iota(jnp.int32, (2 * qb, 1), 0)
    is_sink = krow == 0
    key_live = (krow >= qb) | (n > 0)
    ones_col = jnp.where(key_live | is_sink, 1.0, 0.0)
    ones2 = jnp.concatenate([jnp.where(lo, ones_col, 0.0), jnp.where(lo, 0.0, ones_col)], axis=0)

    kcat = jnp.concatenate([kp_ref[...], kc_ref[...]], axis=0).astype(F32)
    vcat = jnp.concatenate([vp_ref[...], vc_ref[...]], axis=0).astype(F32)
    k2s, rhs = [], []
    for u in range(SWA_KV // LANES):
        kt = jnp.where(is_sink, 0.0, qk_norm(kcat[:, u * LANES:(u + 1) * LANES], kg_ref[...]))
        vt = jnp.where(key_live & ~is_sink, vcat[:, u * LANES:(u + 1) * LANES], 0.0)
        kt_r = pltpu.roll(kt, hd, axis=1)
        vt_r = pltpu.roll(vt, hd, axis=1)
        for half in range(2):
            k_lo, k_hi, v_lo, v_hi = (kt, kt_r, vt, vt_r) if half == 0 else (kt_r, kt, vt_r, vt)
            k2s.append(jnp.concatenate([jnp.where(lo, k_lo, 0.0), jnp.where(lo, 0.0, k_hi)],
                                       axis=0).astype(BF16))
            v2 = jnp.concatenate([jnp.where(lo, v_lo, 0.0), jnp.where(lo, 0.0, v_hi)], axis=0)
            rhs.append(jnp.concatenate([v2, ones2], axis=1).astype(BF16))

    q_gain = qg_ref[...] * (hd ** -0.5 * LOG2E)
    kv_of = lambda tq: (2 * tq) // (SWA_HEADS // SWA_KV_HEADS)
    qt = [qk_norm(q_ref[:, tq * LANES:(tq + 1) * LANES].astype(F32), q_gain) for tq in range(n_tiles)]
    logits = [_mm_nt(qt[tq], k2s[kv_of(tq)]) for tq in range(n_tiles)]
    probs = []
    for tq in range(n_tiles):
        ps = []
        for e in range(2):
            s = logits[tq][:, e * 2 * qb:(e + 1) * 2 * qb] + bias_ref[2 * tq + e]
            ps.append(jnp.exp2(s - jnp.max(s, axis=-1, keepdims=True)).astype(BF16))
        probs.append(jnp.concatenate(ps, axis=1))
    pv = [jnp.dot(probs[tq], rhs[kv_of(tq)], preferred_element_type=F32) for tq in range(n_tiles)]
    for tq in range(n_tiles):
        out = pv[tq][:, :LANES] * (1.0 / pv[tq][:, LANES:])
        z = z_ref[:, tq * LANES:(tq + 1) * LANES].astype(F32)
        o_ref[:, tq * LANES:(tq + 1) * LANES] = (out * _silu(z)).astype(o_ref.dtype)


def _swa(proj, bias, qg2, kg2):
    t = proj.shape[0]
    qb = SWA_BLOCK
    seg = np.kron(np.eye(LANES // SWA_HEAD_DIM), np.ones((SWA_HEAD_DIM, SWA_HEAD_DIM))) / SWA_HEAD_DIM
    seg = jnp.asarray(seg, F32)
    kcol = COL_KB // SWA_KV
    vcol = COL_VB // SWA_KV
    prev = lambda i: jnp.maximum(i - 1, 0)
    return pl.pallas_call(
        _swa_kernel,
        out_shape=jax.ShapeDtypeStruct((t, SWA_Q), BF16),
        grid=(t // qb,),
        in_specs=[pl.BlockSpec((qb, SWA_Q), lambda i: (i, COL_QB // SWA_Q)),
                  pl.BlockSpec((qb, SWA_KV), lambda i: (i, kcol)),
                  pl.BlockSpec((qb, SWA_KV), lambda i: (prev(i), kcol)),
                  pl.BlockSpec((qb, SWA_KV), lambda i: (i, vcol)),
                  pl.BlockSpec((qb, SWA_KV), lambda i: (prev(i), vcol)),
                  pl.BlockSpec((qb, SWA_Q), lambda i: (i, COL_ZB // SWA_Q)),
                  pl.BlockSpec((SWA_HEADS, qb, 2 * qb), lambda i: (0, 0, 0)),
                  pl.BlockSpec((1, LANES), lambda i: (0, 0)),
                  pl.BlockSpec((1, LANES), lambda i: (0, 0)),
                  pl.BlockSpec((LANES, LANES), lambda i: (0, 0))],
        out_specs=pl.BlockSpec((qb, SWA_Q), lambda i: (i, 0)),
        compiler_params=_params(("arbitrary",)),
        name="swa",
    )(proj, proj, proj, proj, proj, proj, bias, qg2, kg2, seg)


def _merge_out_kernel(x_ref, oa_ref, ob_ref, ga_ref, gb_ref, gate_ref, wa_ref, wb_ref, wo_ref, o_ref):
    ya = jnp.dot(oa_ref[...], wa_ref[...], preferred_element_type=F32)
    yb = jnp.dot(ob_ref[...], wb_ref[...], preferred_element_type=F32)
    mixed = _sigmoid(ga_ref[...].astype(F32)) * ya + _sigmoid(gb_ref[...].astype(F32)) * yb
    y = jnp.dot(mixed.astype(BF16), wo_ref[...], preferred_element_type=F32)
    o_ref[...] = x_ref[...] + gate_ref[...] * y


def _merge_out(x2d, o_a, o_b, proj, gate, w_a, w_b, w_o, *, tm):
    t, d = x2d.shape
    const = lambda shape: pl.BlockSpec(shape, lambda i: (0, 0), pipeline_mode=pl.Buffered(1))
    return pl.pallas_call(
        _merge_out_kernel,
        out_shape=jax.ShapeDtypeStruct((t, d), F32),
        grid=(t // tm,),
        in_specs=[pl.BlockSpec((tm, d), lambda i: (i, 0)),
                  pl.BlockSpec((tm, GDN_V), lambda i: (i, 0)),
                  pl.BlockSpec((tm, SWA_Q), lambda i: (i, 0)),
                  pl.BlockSpec((tm, d), lambda i: (i, COL_GA // D_MODEL)),
                  pl.BlockSpec((tm, d), lambda i: (i, COL_GB // D_MODEL)),
                  pl.BlockSpec((1, d), lambda i: (0, 0)),
                  const((GDN_V, d)), const((SWA_Q, d)), const((d, d))],
        out_specs=pl.BlockSpec((tm, d), lambda i: (i, 0)),
        compiler_params=_params(("arbitrary",)),
        name="merge_out",
    )(x2d, o_a, o_b, proj, proj, gate, w_a, w_b, w_o)


REPACK_TILE = 512
N_GATE_COLS = 2 * GDN_HEADS


def _repack_plan():
    src_of = {COL_QKV: 0, COL_ZA: GDN_CONV_CH}
    after_gates = COL_QB + N_GATE_COLS
    src_of.update({COL_QB: after_gates, COL_KB: after_gates + SWA_Q, COL_VB: after_gates + SWA_Q + SWA_KV,
                   COL_ZB: after_gates + SWA_Q + 2 * SWA_KV, COL_GA: after_gates + 2 * SWA_Q + 2 * SWA_KV,
                   COL_GB: after_gates + 2 * SWA_Q + 2 * SWA_KV + D_MODEL})
    starts = sorted(src_of)
    src = []
    for dst in range(0, PROJ_COLS, REPACK_TILE):
        grp = max(s for s in starts if s <= dst)
        src.append(src_of[grp] + dst - grp)
    return np.asarray(src, np.int32)


def _repack_kernel(tab_ref, w_ref, g_ref, big_ref, small_ref):
    del tab_ref
    big_ref[...] = w_ref[...].astype(BF16)

    @pl.when(pl.program_id(0) == 0)
    def _():
        row = lax.broadcasted_iota(jnp.int32, (LANES, 1), 0)
        small_ref[...] = jnp.where(row < N_GATE_COLS, g_ref[...], 0.0).astype(BF16)


def _repack_w_in(w_t):
    d = w_t.shape[1]
    src = _repack_plan()
    gate_row = COL_QB
    assert gate_row % LANES == 0 and np.all(src % N_GATE_COLS == 0)
    return pl.pallas_call(
        _repack_kernel,
        out_shape=(jax.ShapeDtypeStruct((PROJ_COLS, d), BF16), jax.ShapeDtypeStruct((LANES, d), BF16)),
        grid_spec=pltpu.PrefetchScalarGridSpec(
            num_scalar_prefetch=1,
            grid=(PROJ_COLS // REPACK_TILE,),
            in_specs=[pl.BlockSpec((pl.Element(REPACK_TILE), pl.Element(d)),
                                   lambda o, tab: (tab[o] * N_GATE_COLS, 0)),
                      pl.BlockSpec((LANES, d), lambda o, tab: (gate_row // LANES, 0))],
            out_specs=(pl.BlockSpec((REPACK_TILE, d), lambda o, tab: (o, 0)),
                       pl.BlockSpec((LANES, d), lambda o, tab: (0, 0)))),
        compiler_params=_params(("arbitrary",)),
        name="repack_w_in",
    )(jnp.asarray(src // N_GATE_COLS), w_t, w_t)


def _lane_row(vec, offset):
    return jnp.pad(vec.astype(F32), (offset, LANES - offset - vec.shape[0]))[None, :]


def kernel(x, c, w_ada, b_ada, norm_gain, w_in, conv_w, a_log, dt_bias, gdn_norm_gain, q_norm_gain,
           k_norm_gain, sinks, rel_bias, w_branch_gdn, w_branch_swa, w_out):
    bsz, t, d = x.shape
    depth = w_in.shape[0]
    outs = []
    for b in range(bsz):
        xb = x[b]
        c_col = c[b].astype(F32)[:, None]
        for l in range(depth):
            mod = _ada_mod(c_col, w_ada[l], b_ada[l][None, :])
            shift, scale, gate = mod[:, :d], mod[:, d:2 * d], mod[:, 2 * d:]
            w_big, w_small = _repack_w_in(w_in[l].T)
            proj, ba = _in_proj(xb, norm_gain[l][None, :], scale, shift, w_big, w_small,
                                tm=min(1024, t), tn=1792)
            o_a = _gdn(proj, ba, conv_w[l], _lane_row(a_log[l], GDN_HEADS), _lane_row(dt_bias[l], GDN_HEADS),
                       gdn_norm_gain[l][None, :])
            bias = _swa_bias(rel_bias.T.astype(F32), sinks[l].astype(F32))
            o_b = _swa(proj, bias,
                       jnp.tile(q_norm_gain[l], LANES // SWA_HEAD_DIM)[None, :],
                       jnp.tile(k_norm_gain[l], LANES // SWA_HEAD_DIM)[None, :])
            xb = _merge_out(xb, o_a, o_b, proj, gate, w_branch_gdn[l].astype(BF16),
                            w_branch_swa[l].astype(BF16), w_out[l].astype(BF16), tm=min(512, t))
        outs.append(xb)
    return jnp.stack(outs, axis=0)
```

```python
import functools
import math

import jax
import jax.numpy as jnp
import numpy as np
from jax import lax
from jax.experimental import pallas as pl
from jax.experimental.pallas import tpu as pltpu

F32 = jnp.float32
BF16 = jnp.bfloat16

LANES = 128
D_MODEL = 2048
GDN_HEADS = 8
GDN_DK = 128
GDN_DV = 128
GDN_CONV = 4
GDN_QK = GDN_HEADS * GDN_DK
GDN_V = GDN_HEADS * GDN_DV
GDN_CONV_CH = 2 * GDN_QK + GDN_V
GDN_BLOCK = 128
GDN_STEP_BLOCKS = 2
SWA_HEADS = 16
SWA_KV_HEADS = 4
SWA_HEAD_DIM = 64
SWA_WINDOW = 128
SWA_BLOCK = 128
SWA_Q = SWA_HEADS * SWA_HEAD_DIM
SWA_KV = SWA_KV_HEADS * SWA_HEAD_DIM
REL_BUCKETS = 32
REL_MAX_DIST = 128
EPS = 1e-6
NEG_BIG = -1e30
LOG2E = math.log2(math.e)

COL_QKV = 0
COL_ZA = GDN_CONV_CH
COL_QB = COL_ZA + GDN_V
COL_ZB = COL_QB + SWA_Q
COL_GA = COL_ZB + SWA_Q
COL_GB = COL_GA + D_MODEL
COL_KB = COL_GB + D_MODEL
COL_VB = COL_KB + SWA_KV
PROJ_COLS = COL_VB + SWA_KV
for _col, _width in ((COL_ZA, GDN_V), (COL_QB, SWA_Q), (COL_ZB, SWA_Q), (COL_GA, D_MODEL),
                     (COL_GB, D_MODEL), (COL_KB, SWA_KV), (COL_VB, SWA_KV)):
    assert _col % _width == 0

VMEM_LIMIT = 56 * 1024 * 1024


def _sigmoid(x):
    return 1.0 / (1.0 + jnp.exp(-x))


def _silu(x):
    return x * _sigmoid(x)


def _params(sem):
    return pltpu.CompilerParams(dimension_semantics=sem, vmem_limit_bytes=VMEM_LIMIT)


def _ada_mod_kernel(c_ref, w_ref, b_ref, o_ref):
    c = c_ref[...]
    o_ref[...] = jnp.sum(_silu(c) * w_ref[...], axis=0, keepdims=True) + b_ref[...]


def _ada_mod(c_col, w_ada, b_ada):
    d, n = w_ada.shape
    tn = 512
    return pl.pallas_call(
        _ada_mod_kernel,
        out_shape=jax.ShapeDtypeStruct((1, n), F32),
        grid=(n // tn,),
        in_specs=[pl.BlockSpec((d, 1), lambda j: (0, 0)),
                  pl.BlockSpec((d, tn), lambda j: (0, j)),
                  pl.BlockSpec((1, tn), lambda j: (0, j))],
        out_specs=pl.BlockSpec((1, tn), lambda j: (0, j)),
        compiler_params=_params(("arbitrary",)),
        name="ada_mod",
    )(c_col, w_ada, b_ada)


def _in_proj_kernel(x_ref, gain_ref, scale_ref, shift_ref, w_ref, ws_ref, o_ref, ba_ref, h_ref,
                    *, row_chunk):
    j = pl.program_id(1)

    @pl.when(j == 0)
    def _():
        gs = gain_ref[...] * (1.0 + scale_ref[...])
        sh = shift_ref[...]
        tm = x_ref.shape[0]

        def body(r, carry):
            rows = pl.ds(pl.multiple_of(r * row_chunk, row_chunk), row_chunk)
            x = x_ref[rows, :]
            ms = jnp.mean(x * x, axis=-1, keepdims=True)
            h = (x * lax.rsqrt(ms + EPS)) * gs + sh
            h_ref[rows, :] = h.astype(BF16)
            return carry

        lax.fori_loop(0, tm // row_chunk, body, 0)
        ba_ref[...] = _mm_nt(h_ref[...], ws_ref[...])

    o_ref[...] = _mm_nt(h_ref[...], w_ref[...]).astype(o_ref.dtype)


def _in_proj(x2d, gain, scale, shift, w_big, w_small, *, tm, tn):
    t, d = x2d.shape
    n = w_big.shape[0]
    return pl.pallas_call(
        functools.partial(_in_proj_kernel, row_chunk=128),
        out_shape=(jax.ShapeDtypeStruct((t, n), BF16), jax.ShapeDtypeStruct((t, LANES), F32)),
        grid=(t // tm, n // tn),
        in_specs=[pl.BlockSpec((tm, d), lambda i, j: (i, 0)),
                  pl.BlockSpec((1, d), lambda i, j: (0, 0)),
                  pl.BlockSpec((1, d), lambda i, j: (0, 0)),
                  pl.BlockSpec((1, d), lambda i, j: (0, 0)),
                  pl.BlockSpec((tn, d), lambda i, j: (j, 0)),
                  pl.BlockSpec((LANES, d), lambda i, j: (0, 0))],
        out_specs=(pl.BlockSpec((tm, tn), lambda i, j: (i, j)),
                   pl.BlockSpec((tm, LANES), lambda i, j: (i, 0))),
        scratch_shapes=[pltpu.VMEM((tm, d), BF16)],
        compiler_params=_params(("arbitrary", "arbitrary")),
        name="in_proj",
    )(x2d, gain, scale, shift, w_big, w_small)


def _swa_bias_kernel(tab_ref, sink_ref, o_ref):
    h = pl.program_id(0)
    q = SWA_BLOCK
    qpos = lax.broadcasted_iota(jnp.int32, (q, 2 * q), 0) + q
    kpos = lax.broadcasted_iota(jnp.int32, (q, 2 * q), 1)
    dist = qpos - kpos
    in_window = (dist >= 0) & (dist < SWA_WINDOW)
    d = jnp.maximum(dist, 0)
    max_exact = REL_BUCKETS // 2
    df = jnp.maximum(d, 1).astype(F32)
    large = max_exact + (jnp.log(df / max_exact) / math.log(REL_MAX_DIST / max_exact)
                         * (REL_BUCKETS - max_exact)).astype(jnp.int32)
    large = jnp.minimum(large, REL_BUCKETS - 1)
    bucket = jnp.where(d < max_exact, d, large)
    acc = jnp.zeros((q, 2 * q), F32)
    for b in range(REL_BUCKETS):
        acc = jnp.where(bucket == b, tab_ref[h, b], acc)
    table = jnp.where(in_window, acc, NEG_BIG)
    o_ref[0] = jnp.where(kpos == 0, sink_ref[h], table) * LOG2E


def _swa_bias(rel_bias_t, sinks):
    q = SWA_BLOCK
    return pl.pallas_call(
        _swa_bias_kernel,
        out_shape=jax.ShapeDtypeStruct((SWA_HEADS, q, 2 * q), F32),
        grid=(SWA_HEADS,),
        in_specs=[pl.BlockSpec(memory_space=pltpu.SMEM), pl.BlockSpec(memory_space=pltpu.SMEM)],
        out_specs=pl.BlockSpec((1, q, 2 * q), lambda h: (h, 0, 0)),
        compiler_params=_params(("arbitrary",)),
        name="swa_bias",
    )(rel_bias_t, sinks)


def _chunk_cumsum_rows(x):
    n = x.shape[0]
    row = lax.broadcasted_iota(jnp.int32, x.shape, 0)
    s = 1
    while s < n:
        x = x + jnp.where(row >= s, pltpu.roll(x, s, axis=0), 0.0)
        s *= 2
    return x


def _mm(a, b):
    return jnp.dot(a.astype(BF16), b.astype(BF16), preferred_element_type=F32)


def _mm_nt(a, b):
    return lax.dot_general(a.astype(BF16), b.astype(BF16), (((1,), (1,)), ((), ())),
                           preferred_element_type=F32)


def _mm_tn(a, b):
    return lax.dot_general(a.astype(BF16), b.astype(BF16), (((0,), (0,)), ((), ())),
                           preferred_element_type=F32)


def _unit_lower_inverses(l_mats, masks_ref):
    eye = masks_ref[0]
    m0 = [l * masks_ref[1] for l in l_mats]
    x = [eye - m for m in m0]
    p = [_mm(m, m) for m in m0]
    x = [xi + _mm(xi, pi) for xi, pi in zip(x, p)]
    p = [_mm(pi, pi) for pi in p]
    x = [xi + _mm(xi, pi) for xi, pi in zip(x, p)]
    for lvl in range(2, masks_ref.shape[0]):
        nx = [_mm(l * masks_ref[lvl], xi) for l, xi in zip(l_mats, x)]
        x = [xi - _mm(xi, ni) for xi, ni in zip(x, nx)]
    return x


def _gdn_kernel(cur_ref, prev_ref, z_ref, ba_ref, convw_ref, band_ref, alog_ref, dtb_ref, gain_ref,
                masks_ref, o_ref, s_ref, q_sc, k_sc, kb_sc, vb_sc, kbg_sc, qd_sc, kd_sc, dec_sc, el_sc):
    s = pl.program_id(0)
    c = GDN_BLOCK
    heads = range(GDN_HEADS)
    blocks = range(GDN_STEP_BLOCKS)
    items = [(b, h) for b in blocks for h in heads]
    at = lambda b, h: b * GDN_HEADS + h
    lane_of = lambda a, h: a[:, GDN_HEADS + h:GDN_HEADS + h + 1]

    @pl.when(s == 0)
    def _():
        for ref in (s_ref, q_sc, k_sc, kb_sc, vb_sc, kbg_sc, qd_sc, kd_sc, dec_sc, el_sc):
            ref[...] = jnp.zeros_like(ref)

    pad = prev_ref.shape[0]
    conv_w = convw_ref[...].astype(BF16)
    conv = []
    for b in blocks:
        if b == 0:
            ctx = jnp.where(s > 0, prev_ref[...], jnp.zeros_like(prev_ref))
        else:
            ctx = cur_ref[b * c - pad:b * c, :]
        xcat = jnp.concatenate([ctx, cur_ref[b * c:(b + 1) * c, :]], axis=0)
        taps = jnp.concatenate([xcat * conv_w[i:i + 1, :] for i in range(GDN_CONV)], axis=0)
        conv.append(jnp.dot(band_ref[...], taps, preferred_element_type=F32))

    row = lax.broadcasted_iota(jnp.int32, (c, c), 0)
    col = lax.broadcasted_iota(jnp.int32, (c, c), 1)
    strict = row > col
    kk = [_mm_nt(kb_sc[at(b, h)], k_sc[at(b, h)]) for b, h in items]
    qk = [_mm_nt(q_sc[at(b, h)], k_sc[at(b, h)]) for b, h in items]
    l_mat = [jnp.where(strict, kk[at(b, h)] * dec_sc[at(b, h)], 0.0) for b, h in items]
    attn = [(qk[at(b, h)] * dec_sc[at(b, h)]).astype(BF16) for b, h in items]
    t_inv = _unit_lower_inverses(l_mat, masks_ref)
    state = [s_ref[h] for h in heads]
    for b in blocks:
        el = el_sc[b]
        s_bf = [state[h].astype(BF16) for h in heads]
        resid = [vb_sc[at(b, h)] - _mm(kbg_sc[at(b, h)], s_bf[h]) for h in heads]
        v_new = [_mm(t_inv[at(b, h)], resid[h]).astype(BF16) for h in heads]
        o = [_mm(jnp.concatenate([qd_sc[at(b, h)], attn[at(b, h)]], axis=1),
                 jnp.concatenate([s_bf[h], v_new[h]], axis=0)) for h in heads]
        state = [state[h] * lane_of(el, h) + _mm_tn(kd_sc[at(b, h)], v_new[h]) for h in heads]
        for h in heads:
            on = o[h] * lax.rsqrt(jnp.mean(o[h] * o[h], axis=-1, keepdims=True) + EPS) * gain_ref[...]
            z = z_ref[b * c:(b + 1) * c, h * GDN_DV:(h + 1) * GDN_DV].astype(F32)
            o_ref[b * c:(b + 1) * c, h * GDN_DV:(h + 1) * GDN_DV] = (on * _silu(z)).astype(o_ref.dtype)
    for h in heads:
        s_ref[h] = state[h]

    causal = row >= col

    def l2n(a):
        return a * lax.rsqrt(jnp.sum(a * a, axis=-1, keepdims=True) + EPS)

    for b in blocks:
        ba = ba_ref[b * c:(b + 1) * c, :]
        beta_all = _sigmoid(ba)
        xg = ba + dtb_ref[...]
        softplus = jnp.maximum(xg, 0.0) + jnp.log(1.0 + jnp.exp(-jnp.abs(xg)))
        g_all = -jnp.exp(alog_ref[...]) * softplus
        gc = _chunk_cumsum_rows(g_all)
        gc_t = gc.T
        g_last = gc[c - 1:c, :]
        eg_all = jnp.exp(gc)
        ekd_all = jnp.exp(g_last - gc)
        conv_tile = lambda col0: _silu(conv[b][:, col0:col0 + LANES])
        for h in heads:
            q = l2n(conv_tile(h * GDN_DK)) * (GDN_DK ** -0.5)
            k = l2n(conv_tile(GDN_QK + h * GDN_DK))
            v = conv_tile(2 * GDN_QK + h * GDN_DV)
            beta = beta_all[:, h:h + 1]
            eg = lane_of(eg_all, h)
            kb = k * beta
            gdiff = lane_of(gc, h) - gc_t[GDN_HEADS + h:GDN_HEADS + h + 1, :]
            q_sc[at(b, h)] = q.astype(BF16)
            k_sc[at(b, h)] = k.astype(BF16)
            kb_sc[at(b, h)] = kb.astype(BF16)
            vb_sc[at(b, h)] = v * beta
            kbg_sc[at(b, h)] = (kb * eg).astype(BF16)
            qd_sc[at(b, h)] = (q * eg).astype(BF16)
            kd_sc[at(b, h)] = (k * lane_of(ekd_all, h)).astype(BF16)
            dec_sc[at(b, h)] = jnp.where(causal, jnp.exp(jnp.where(causal, gdiff, 0.0)), 0.0)
        el_sc[b] = jnp.exp(g_last)


def _gdn_masks():
    c = GDN_BLOCK
    r = np.arange(c)[:, None]
    k = np.arange(c)[None, :]
    mats = [np.eye(c), (r // 8 == k // 8) & (r > k)]
    b = 8
    while b < c:
        mats.append((r // (2 * b) == k // (2 * b)) & ((r // b) % 2 == 1) & ((k // b) % 2 == 0))
        b *= 2
    return jnp.asarray(np.stack([np.asarray(m, np.float32) for m in mats]))


def _conv_band(c, pad):
    band = np.zeros((c, GDN_CONV * (pad + c)), np.float32)
    for i in range(GDN_CONV):
        band[np.arange(c), i * (pad + c) + pad + np.arange(c) - (GDN_CONV - 1) + i] = 1.0
    return jnp.asarray(band, BF16)


def _gdn(proj, ba, conv_w, alog_lane, dtb_lane, gain):
    t = proj.shape[0]
    c = GDN_BLOCK
    rows = GDN_STEP_BLOCKS * c
    pad = 16
    masks = _gdn_masks()
    nm = masks.shape[0]
    band = _conv_band(c, pad)
    n_steps = t // rows
    staged = lambda i: jnp.minimum(i, n_steps - 1)
    chained = lambda i: jnp.maximum(i - 1, 0)
    item_sq = lambda w, dt: pltpu.VMEM((GDN_STEP_BLOCKS * GDN_HEADS, c, w), dt)
    return pl.pallas_call(
        _gdn_kernel,
        out_shape=jax.ShapeDtypeStruct((t, GDN_V), BF16),
        grid=(n_steps + 1,),
        in_specs=[pl.BlockSpec((rows, GDN_CONV_CH), lambda i: (staged(i), 0)),
                  pl.BlockSpec((pad, GDN_CONV_CH),
                               lambda i: (jnp.maximum(staged(i) * (rows // pad) - 1, 0), 0)),
                  pl.BlockSpec((rows, GDN_V), lambda i: (chained(i), COL_ZA // GDN_V)),
                  pl.BlockSpec((rows, LANES), lambda i: (staged(i), 0)),
                  pl.BlockSpec((GDN_CONV, GDN_CONV_CH), lambda i: (0, 0)),
                  pl.BlockSpec(band.shape, lambda i: (0, 0)),
                  pl.BlockSpec((1, LANES), lambda i: (0, 0)),
                  pl.BlockSpec((1, LANES), lambda i: (0, 0)),
                  pl.BlockSpec((1, GDN_DV), lambda i: (0, 0)),
                  pl.BlockSpec((nm, c, c), lambda i: (0, 0, 0))],
        out_specs=pl.BlockSpec((rows, GDN_V), lambda i: (chained(i), 0)),
        scratch_shapes=[pltpu.VMEM((GDN_HEADS, GDN_DK, GDN_DV), F32),
                        item_sq(GDN_DK, BF16), item_sq(GDN_DK, BF16), item_sq(GDN_DK, BF16),
                        item_sq(GDN_DV, F32), item_sq(GDN_DK, BF16),
                        item_sq(GDN_DK, BF16), item_sq(GDN_DK, BF16),
                        item_sq(c, F32),
                        pltpu.VMEM((GDN_STEP_BLOCKS, 1, LANES), F32)],
        compiler_params=_params(("arbitrary",)),
        name="gdn",
    )(proj, proj, proj, ba, conv_w, band, alog_lane, dtb_lane, gain, masks)


def _swa_kernel(q_ref, kc_ref, kp_ref, vc_ref, vp_ref, z_ref, bias_ref, qg_ref, kg_ref, seg_ref, o_ref):
    n = pl.program_id(0)
    qb = SWA_BLOCK
    hd = SWA_HEAD_DIM
    n_tiles = SWA_Q // LANES
    lane = lax.broadcasted_iota(jnp.int32, (1, LANES), 1)
    lo = lane < hd
    seg = seg_ref[...].astype(BF16)

    def qk_norm(a, gain):
        ms = jnp.dot((a * a).astype(BF16), seg, preferred_element_type=F32)
        return a * (lax.rsqrt(ms + EPS) * gain)

    krow = lax.broadcasted_iota(jnp.int32, (2 * qb, 1), 0)
    is_sink = krow == 0
    key_live = (krow >= qb) | (n > 0)
    ones_col = jnp.where(key_live | is_sink, 1.0, 0.0)
    ones2 = jnp.concatenate([jnp.where(lo, ones_col, 0.0), jnp.where(lo, 0.0, ones_col)], axis=0)

    kcat = jnp.concatenate([kp_ref[...], kc_ref[...]], axis=0).astype(F32)
    vcat = jnp.concatenate([vp_ref[...], vc_ref[...]], axis=0).astype(F32)
    k2s, rhs = [], []
    for u in range(SWA_KV // LANES):
        kt = jnp.where(is_sink, 0.0, qk_norm(kcat[:, u * LANES:(u + 1) * LANES], kg_ref[...]))
        vt = jnp.where(key_live & ~is_sink, vcat[:, u * LANES:(u + 1) * LANES], 0.0)
        kt_r = pltpu.roll(kt, hd, axis=1)
        vt_r = pltpu.roll(vt, hd, axis=1)
        for half in range(2):
            k_lo, k_hi, v_lo, v_hi = (kt, kt_r, vt, vt_r) if half == 0 else (kt_r, kt, vt_r, vt)
            k2s.append(jnp.concatenate([jnp.where(lo, k_lo, 0.0), jnp.where(lo, 0.0, k_hi)],
                                       axis=0).astype(BF16))
            v2 = jnp.concatenate([jnp.where(lo, v_lo, 0.0), jnp.where(lo, 0.0, v_hi)], axis=0)
            rhs.append(jnp.concatenate([v2, ones2], axis=1).astype(BF16))

    q_gain = qg_ref[...] * (hd ** -0.5 * LOG2E)
    kv_of = lambda tq: (2 * tq) // (SWA_HEADS // SWA_KV_HEADS)
    qt = [qk_norm(q_ref[:, tq * LANES:(tq + 1) * LANES].astype(F32), q_gain) for tq in range(n_tiles)]
    logits = [_mm_nt(qt[tq], k2s[kv_of(tq)]) for tq in range(n_tiles)]
    probs = []
    for tq in range(n_tiles):
        ps = []
        for e in range(2):
            s = logits[tq][:, e * 2 * qb:(e + 1) * 2 * qb] + bias_ref[2 * tq + e]
            ps.append(jnp.exp2(s - jnp.max(s, axis=-1, keepdims=True)).astype(BF16))
        probs.append(jnp.concatenate(ps, axis=1))
    pv = [jnp.dot(probs[tq], rhs[kv_of(tq)], preferred_element_type=F32) for tq in range(n_tiles)]
    for tq in range(n_tiles):
        out = pv[tq][:, :LANES] * (1.0 / pv[tq][:, LANES:])
        z = z_ref[:, tq * LANES:(tq + 1) * LANES].astype(F32)
        o_ref[:, tq * LANES:(tq + 1) * LANES] = (out * _silu(z)).astype(o_ref.dtype)


def _swa(proj, bias, qg2, kg2):
    t = proj.shape[0]
    qb = SWA_BLOCK
    seg = np.kron(np.eye(LANES // SWA_HEAD_DIM), np.ones((SWA_HEAD_DIM, SWA_HEAD_DIM))) / SWA_HEAD_DIM
    seg = jnp.asarray(seg, F32)
    kcol = COL_KB // SWA_KV
    vcol = COL_VB // SWA_KV
    prev = lambda i: jnp.maximum(i - 1, 0)
    return pl.pallas_call(
        _swa_kernel,
        out_shape=jax.ShapeDtypeStruct((t, SWA_Q), BF16),
        grid=(t // qb,),
        in_specs=[pl.BlockSpec((qb, SWA_Q), lambda i: (i, COL_QB // SWA_Q)),
                  pl.BlockSpec((qb, SWA_KV), lambda i: (i, kcol)),
                  pl.BlockSpec((qb, SWA_KV), lambda i: (prev(i), kcol)),
                  pl.BlockSpec((qb, SWA_KV), lambda i: (i, vcol)),
                  pl.BlockSpec((qb, SWA_KV), lambda i: (prev(i), vcol)),
                  pl.BlockSpec((qb, SWA_Q), lambda i: (i, COL_ZB // SWA_Q)),
                  pl.BlockSpec((SWA_HEADS, qb, 2 * qb), lambda i: (0, 0, 0)),
                  pl.BlockSpec((1, LANES), lambda i: (0, 0)),
                  pl.BlockSpec((1, LANES), lambda i: (0, 0)),
                  pl.BlockSpec((LANES, LANES), lambda i: (0, 0))],
        out_specs=pl.BlockSpec((qb, SWA_Q), lambda i: (i, 0)),
        compiler_params=_params(("arbitrary",)),
        name="swa",
    )(proj, proj, proj, proj, proj, proj, bias, qg2, kg2, seg)


def _merge_out_kernel(x_ref, oa_ref, ob_ref, ga_ref, gb_ref, gate_ref, wa_ref, wb_ref, wo_ref, o_ref):
    ya = jnp.dot(oa_ref[...], wa_ref[...], preferred_element_type=F32)
    yb = jnp.dot(ob_ref[...], wb_ref[...], preferred_element_type=F32)
    mixed = _sigmoid(ga_ref[...].astype(F32)) * ya + _sigmoid(gb_ref[...].astype(F32)) * yb
    y = jnp.dot(mixed.astype(BF16), wo_ref[...], preferred_element_type=F32)
    o_ref[...] = x_ref[...] + gate_ref[...] * y


def _merge_out(x2d, o_a, o_b, proj, gate, w_a, w_b, w_o, *, tm):
    t, d = x2d.shape
    const = lambda shape: pl.BlockSpec(shape, lambda i: (0, 0), pipeline_mode=pl.Buffered(1))
    return pl.pallas_call(
        _merge_out_kernel,
        out_shape=jax.ShapeDtypeStruct((t, d), F32),
        grid=(t // tm,),
        in_specs=[pl.BlockSpec((tm, d), lambda i: (i, 0)),
                  pl.BlockSpec((tm, GDN_V), lambda i: (i, 0)),
                  pl.BlockSpec((tm, SWA_Q), lambda i: (i, 0)),
                  pl.BlockSpec((tm, d), lambda i: (i, COL_GA // D_MODEL)),
                  pl.BlockSpec((tm, d), lambda i: (i, COL_GB // D_MODEL)),
                  pl.BlockSpec((1, d), lambda i: (0, 0)),
                  const((GDN_V, d)), const((SWA_Q, d)), const((d, d))],
        out_specs=pl.BlockSpec((tm, d), lambda i: (i, 0)),
        compiler_params=_params(("arbitrary",)),
        name="merge_out",
    )(x2d, o_a, o_b, proj, proj, gate, w_a, w_b, w_o)


REPACK_TILE = 512
N_GATE_COLS = 2 * GDN_HEADS


def _repack_plan():
    src_of = {COL_QKV: 0, COL_ZA: GDN_CONV_CH}
    after_gates = COL_QB + N_GATE_COLS
    src_of.update({COL_QB: after_gates, COL_KB: after_gates + SWA_Q, COL_VB: after_gates + SWA_Q + SWA_KV,
                   COL_ZB: after_gates + SWA_Q + 2 * SWA_KV, COL_GA: after_gates + 2 * SWA_Q + 2 * SWA_KV,
                   COL_GB: after_gates + 2 * SWA_Q + 2 * SWA_KV + D_MODEL})
    starts = sorted(src_of)
    src = []
    for dst in range(0, PROJ_COLS, REPACK_TILE):
        grp = max(s for s in starts if s <= dst)
        src.append(src_of[grp] + dst - grp)
    return np.asarray(src, np.int32)


def _repack_kernel(tab_ref, w_ref, g_ref, big_ref, small_ref):
    del tab_ref
    big_ref[...] = w_ref[...].astype(BF16)

    @pl.when(pl.program_id(0) == 0)
    def _():
        row = lax.broadcasted_iota(jnp.int32, (LANES, 1), 0)
        small_ref[...] = jnp.where(row < N_GATE_COLS, g_ref[...], 0.0).astype(BF16)


def _repack_w_in(w_t):
    d = w_t.shape[1]
    src = _repack_plan()
    gate_row = COL_QB
    assert gate_row % LANES == 0 and np.all(src % N_GATE_COLS == 0)
    return pl.pallas_call(
        _repack_kernel,
        out_shape=(jax.ShapeDtypeStruct((PROJ_COLS, d), BF16), jax.ShapeDtypeStruct((LANES, d), BF16)),
        grid_spec=pltpu.PrefetchScalarGridSpec(
            num_scalar_prefetch=1,
            grid=(PROJ_COLS // REPACK_TILE,),
            in_specs=[pl.BlockSpec((pl.Element(REPACK_TILE), pl.Element(d)),
                                   lambda o, tab: (tab[o] * N_GATE_COLS, 0)),
                      pl.BlockSpec((LANES, d), lambda o, tab: (gate_row // LANES, 0))],
            out_specs=(pl.BlockSpec((REPACK_TILE, d), lambda o, tab: (o, 0)),
                       pl.BlockSpec((LANES, d), lambda o, tab: (0, 0)))),
        compiler_params=_params(("arbitrary",)),
        name="repack_w_in",
    )(jnp.asarray(src // N_GATE_COLS), w_t, w_t)


def _lane_row(vec, offset):
    return jnp.pad(vec.astype(F32), (offset, LANES - offset - vec.shape[0]))[None, :]


def kernel(x, c, w_ada, b_ada, norm_gain, w_in, conv_w, a_log, dt_bias, gdn_norm_gain, q_norm_gain,
           k_norm_gain, sinks, rel_bias, w_branch_gdn, w_branch_swa, w_out):
    bsz, t, d = x.shape
    depth = w_in.shape[0]
    outs = []
    for b in range(bsz):
        xb = x[b]
        c_col = c[b].astype(F32)[:, None]
        for l in range(depth):
            mod = _ada_mod(c_col, w_ada[l], b_ada[l][None, :])
            shift, scale, gate = mod[:, :d], mod[:, d:2 * d], mod[:, 2 * d:]
            w_big, w_small = _repack_w_in(w_in[l].T)
            proj, ba = _in_proj(xb, norm_gain[l][None, :], scale, shift, w_big, w_small,
                                tm=min(1024, t), tn=1792)
            o_a = _gdn(proj, ba, conv_w[l], _lane_row(a_log[l], GDN_HEADS), _lane_row(dt_bias[l], GDN_HEADS),
                       gdn_norm_gain[l][None, :])
            bias = _swa_bias(rel_bias.T.astype(F32), sinks[l].astype(F32))
            o_b = _swa(proj, bias,
                       jnp.tile(q_norm_gain[l], LANES // SWA_HEAD_DIM)[None, :],
                       jnp.tile(k_norm_gain[l], LANES // SWA_HEAD_DIM)[None, :])
            xb = _merge_out(xb, o_a, o_b, proj, gate, w_branch_gdn[l].astype(BF16),
                            w_branch_swa[l].astype(BF16), w_out[l].astype(BF16), tm=min(512, t))
        outs.append(xb)
    return jnp.stack(outs, axis=0)
```

```python
import functools
import math

import jax
import jax.numpy as jnp
import numpy as np
from jax import lax
from jax.experimental import pallas as pl
from jax.experimental.pallas import tpu as pltpu

F32 = jnp.float32
BF16 = jnp.bfloat16

LANES = 128
D_MODEL = 2048
GDN_HEADS = 8
GDN_DK = 128
GDN_DV = 128
GDN_CONV = 4
GDN_QK = GDN_HEADS * GDN_DK
GDN_V = GDN_HEADS * GDN_DV
GDN_CONV_CH = 2 * GDN_QK + GDN_V
GDN_BLOCK = 128
GDN_STEP_BLOCKS = 2
GDN_CONV_PIECE = 512
SWA_HEADS = 16
SWA_KV_HEADS = 4
SWA_HEAD_DIM = 64
SWA_WINDOW = 128
SWA_BLOCK = 128
SWA_STEP_BLOCKS = 2
SWA_Q = SWA_HEADS * SWA_HEAD_DIM
SWA_KV = SWA_KV_HEADS * SWA_HEAD_DIM
REL_BUCKETS = 32
REL_MAX_DIST = 128
EPS = 1e-6
NEG_BIG = -1e30
LOG2E = math.log2(math.e)

COL_QKV = 0
COL_ZA = GDN_CONV_CH
COL_QB = COL_ZA + GDN_V
COL_ZB = COL_QB + SWA_Q
COL_GA = COL_ZB + SWA_Q
COL_GB = COL_GA + D_MODEL
COL_KB = COL_GB + D_MODEL
COL_VB = COL_KB + SWA_KV
PROJ_COLS = COL_VB + SWA_KV
for _col, _width in ((COL_ZA, GDN_V), (COL_QB, SWA_Q), (COL_ZB, SWA_Q), (COL_GA, D_MODEL),
                     (COL_GB, D_MODEL), (COL_KB, SWA_KV), (COL_VB, SWA_KV)):
    assert _col % _width == 0

VMEM_LIMIT = 56 * 1024 * 1024


def _sigmoid(x):
    return 0.5 + 0.5 * jnp.tanh(0.5 * x)


def _silu(x):
    half = 0.5 * x
    return half + half * jnp.tanh(half)


def _params(sem):
    return pltpu.CompilerParams(dimension_semantics=sem, vmem_limit_bytes=VMEM_LIMIT)


def _ada_mod_kernel(c_ref, w_ref, b_ref, o_ref):
    c = c_ref[...]
    o_ref[...] = jnp.sum(_silu(c) * w_ref[...], axis=0, keepdims=True) + b_ref[...]


def _ada_mod(c_col, w_ada, b_ada):
    d, n = w_ada.shape
    tn = 512
    return pl.pallas_call(
        _ada_mod_kernel,
        out_shape=jax.ShapeDtypeStruct((1, n), F32),
        grid=(n // tn,),
        in_specs=[pl.BlockSpec((d, 1), lambda j: (0, 0)),
                  pl.BlockSpec((d, tn), lambda j: (0, j)),
                  pl.BlockSpec((1, tn), lambda j: (0, j))],
        out_specs=pl.BlockSpec((1, tn), lambda j: (0, j)),
        compiler_params=_params(("arbitrary",)),
        name="ada_mod",
    )(c_col, w_ada, b_ada)


def _in_proj_kernel(x_ref, gain_ref, scale_ref, shift_ref, w_ref, ws_ref, o_ref, ba_ref, h_ref,
                    *, row_chunk):
    j = pl.program_id(1)

    @pl.when(j == 0)
    def _():
        gs = gain_ref[...] * (1.0 + scale_ref[...])
        sh = shift_ref[...]
        tm = x_ref.shape[0]

        def body(r, carry):
            rows = pl.ds(pl.multiple_of(r * row_chunk, row_chunk), row_chunk)
            x = x_ref[rows, :]
            ms = jnp.mean(x * x, axis=-1, keepdims=True)
            h = (x * lax.rsqrt(ms + EPS)) * gs + sh
            h_ref[rows, :] = h.astype(BF16)
            return carry

        lax.fori_loop(0, tm // row_chunk, body, 0)
        ba_ref[...] = _mm_nt(h_ref[...], ws_ref[...])

    o_ref[...] = _mm_nt(h_ref[...], w_ref[...]).astype(o_ref.dtype)


def _in_proj(x2d, gain, scale, shift, w_big, w_small, *, tm, tn):
    t, d = x2d.shape
    n = w_big.shape[0]
    return pl.pallas_call(
        functools.partial(_in_proj_kernel, row_chunk=128),
        out_shape=(jax.ShapeDtypeStruct((t, n), BF16), jax.ShapeDtypeStruct((t, LANES), F32)),
        grid=(t // tm, n // tn),
        in_specs=[pl.BlockSpec((tm, d), lambda i, j: (i, 0)),
                  pl.BlockSpec((1, d), lambda i, j: (0, 0)),
                  pl.BlockSpec((1, d), lambda i, j: (0, 0)),
                  pl.BlockSpec((1, d), lambda i, j: (0, 0)),
                  pl.BlockSpec((tn, d), lambda i, j: (j, 0)),
                  pl.BlockSpec((LANES, d), lambda i, j: (0, 0))],
        out_specs=(pl.BlockSpec((tm, tn), lambda i, j: (i, j)),
                   pl.BlockSpec((tm, LANES), lambda i, j: (i, 0))),
        scratch_shapes=[pltpu.VMEM((tm, d), BF16)],
        compiler_params=_params(("arbitrary", "arbitrary")),
        name="in_proj",
    )(x2d, gain, scale, shift, w_big, w_small)


def _swa_bias_kernel(tab_ref, sink_ref, o_ref):
    h = pl.program_id(0)
    q = SWA_BLOCK
    qpos = lax.broadcasted_iota(jnp.int32, (q, 2 * q), 0) + q
    kpos = lax.broadcasted_iota(jnp.int32, (q, 2 * q), 1)
    dist = qpos - kpos
    in_window = (dist >= 0) & (dist < SWA_WINDOW)
    d = jnp.maximum(dist, 0)
    max_exact = REL_BUCKETS // 2
    df = jnp.maximum(d, 1).astype(F32)
    large = max_exact + (jnp.log(df / max_exact) / math.log(REL_MAX_DIST / max_exact)
                         * (REL_BUCKETS - max_exact)).astype(jnp.int32)
    large = jnp.minimum(large, REL_BUCKETS - 1)
    bucket = jnp.where(d < max_exact, d, large)
    acc = jnp.zeros((q, 2 * q), F32)
    for b in range(REL_BUCKETS):
        acc = jnp.where(bucket == b, tab_ref[h, b], acc)
    table = jnp.where(in_window, acc, NEG_BIG)
    o_ref[0] = jnp.where(kpos == 0, sink_ref[h], table) * LOG2E


def _swa_bias(rel_bias_t, sinks):
    q = SWA_BLOCK
    return pl.pallas_call(
        _swa_bias_kernel,
        out_shape=jax.ShapeDtypeStruct((SWA_HEADS, q, 2 * q), F32),
        grid=(SWA_HEADS,),
        in_specs=[pl.BlockSpec(memory_space=pltpu.SMEM), pl.BlockSpec(memory_space=pltpu.SMEM)],
        out_specs=pl.BlockSpec((1, q, 2 * q), lambda h: (h, 0, 0)),
        compiler_params=_params(("arbitrary",)),
        name="swa_bias",
    )(rel_bias_t, sinks)


def _chunk_cumsum_rows(x):
    n = x.shape[0]
    row = lax.broadcasted_iota(jnp.int32, x.shape, 0)
    s = 1
    while s < n:
        x = x + jnp.where(row >= s, pltpu.roll(x, s, axis=0), 0.0)
        s *= 2
    return x


def _mm(a, b):
    return jnp.dot(a.astype(BF16), b.astype(BF16), preferred_element_type=F32)


def _mm_nt(a, b):
    return lax.dot_general(a.astype(BF16), b.astype(BF16), (((1,), (1,)), ((), ())),
                           preferred_element_type=F32)


def _mm_tn(a, b):
    return lax.dot_general(a.astype(BF16), b.astype(BF16), (((0,), (0,)), ((), ())),
                           preferred_element_type=F32)


def _unit_lower_inverses(l_mats, masks_ref, between_stages):
    eye = masks_ref[0].astype(F32)
    l_bf = [l.astype(BF16) for l in l_mats]
    m0 = [l * masks_ref[1] for l in l_bf]
    x = [eye - m for m in m0]
    p = [_mm(m, m) for m in m0]
    between_stages()
    x = [xi + _mm(xi, pi) for xi, pi in zip(x, p)]
    between_stages()
    p = [_mm(pi, pi) for pi in p]
    between_stages()
    x = [xi + _mm(xi, pi) for xi, pi in zip(x, p)]
    between_stages()
    for lvl in range(2, masks_ref.shape[0]):
        nx = [_mm(l * masks_ref[lvl], xi) for l, xi in zip(l_bf, x)]
        between_stages()
        x = [xi - _mm(xi, ni) for xi, ni in zip(x, nx)]
        between_stages()
    return x


def _gdn_kernel(cur_ref, prev_ref, z_ref, ba_ref, convw_ref, band_ref, alog_ref, dtb_ref, gain_ref,
                masks_ref, o_ref, s_ref, q_sc, k_sc, kb_sc, vb_sc, kbg_sc, qd_sc, kd_sc, dec_sc, el_sc):
    s = pl.program_id(0)
    c = GDN_BLOCK
    heads = range(GDN_HEADS)
    blocks = range(GDN_STEP_BLOCKS)
    items = [(b, h) for b in blocks for h in heads]
    at = lambda b, h: b * GDN_HEADS + h
    lane_of = lambda a, h: a[:, GDN_HEADS + h:GDN_HEADS + h + 1]

    @pl.when(s == 0)
    def _():
        for ref in (s_ref, q_sc, k_sc, kb_sc, vb_sc, kbg_sc, qd_sc, kd_sc, dec_sc, el_sc):
            ref[...] = jnp.zeros_like(ref)

    pad = prev_ref.shape[0]
    conv_w = convw_ref[...].astype(BF16)
    taps = []
    for b in blocks:
        if b == 0:
            ctx = jnp.where(s > 0, prev_ref[...], jnp.zeros_like(prev_ref))
        else:
            ctx = cur_ref[b * c - pad:b * c, :]
        xcat = jnp.concatenate([ctx, cur_ref[b * c:(b + 1) * c, :]], axis=0)
        taps.append(jnp.concatenate([xcat * conv_w[i:i + 1, :] for i in range(GDN_CONV)], axis=0))
    conv_pieces = {}
    todo = [(b, p) for p in range(GDN_CONV_CH // GDN_CONV_PIECE) for b in blocks]

    def next_conv_piece():
        if todo:
            b, p = todo.pop(0)
            cols = slice(p * GDN_CONV_PIECE, (p + 1) * GDN_CONV_PIECE)
            conv_pieces[b, p] = jnp.dot(band_ref[...], taps[b][:, cols], preferred_element_type=F32)

    row = lax.broadcasted_iota(jnp.int32, (c, c), 0)
    col = lax.broadcasted_iota(jnp.int32, (c, c), 1)
    strict = row > col
    kk = [_mm_nt(kb_sc[at(b, h)], k_sc[at(b, h)]) for b, h in items]
    qk = [_mm_nt(q_sc[at(b, h)], k_sc[at(b, h)]) for b, h in items]
    while todo:
        next_conv_piece()
    l_mat = [jnp.where(strict, kk[at(b, h)] * dec_sc[at(b, h)], 0.0) for b, h in items]
    attn = [(qk[at(b, h)] * dec_sc[at(b, h)]).astype(BF16) for b, h in items]
    t_inv = _unit_lower_inverses(l_mat, masks_ref, lambda: None)
    state = [s_ref[h] for h in heads]
    for b in blocks:
        el = el_sc[b]
        s_bf = [state[h].astype(BF16) for h in heads]
        resid = [vb_sc[at(b, h)] - _mm(kbg_sc[at(b, h)], s_bf[h]) for h in heads]
        v_new = [_mm(t_inv[at(b, h)], resid[h]).astype(BF16) for h in heads]
        o = [_mm(jnp.concatenate([qd_sc[at(b, h)], attn[at(b, h)]], axis=1),
                 jnp.concatenate([s_bf[h], v_new[h]], axis=0)) for h in heads]
        state = [state[h] * lane_of(el, h) + _mm_tn(kd_sc[at(b, h)], v_new[h]) for h in heads]
        for h in heads:
            on = o[h] * lax.rsqrt(jnp.mean(o[h] * o[h], axis=-1, keepdims=True) + EPS) * gain_ref[...]
            z = z_ref[b * c:(b + 1) * c, h * GDN_DV:(h + 1) * GDN_DV].astype(F32)
            o_ref[b * c:(b + 1) * c, h * GDN_DV:(h + 1) * GDN_DV] = (on * _silu(z)).astype(o_ref.dtype)
    for h in heads:
        s_ref[h] = state[h]

    causal = row >= col

    def l2n(a):
        return a * lax.rsqrt(jnp.sum(a * a, axis=-1, keepdims=True) + EPS)

    for b in blocks:
        ba = ba_ref[b * c:(b + 1) * c, :]
        beta_all = _sigmoid(ba)
        xg = ba + dtb_ref[...]
        softplus = jnp.maximum(xg, 0.0) + jnp.log(1.0 + jnp.exp(-jnp.abs(xg)))
        g_all = -jnp.exp(alog_ref[...]) * softplus
        gc = _chunk_cumsum_rows(g_all)
        gc_t = gc.T
        g_last = gc[c - 1:c, :]
        eg_all = jnp.exp(gc)
        ekd_all = jnp.exp(g_last - gc)
        conv_tile = lambda col0: _silu(conv_pieces[b, col0 // GDN_CONV_PIECE][
            :, col0 % GDN_CONV_PIECE:col0 % GDN_CONV_PIECE + LANES])
        for h in heads:
            q = l2n(conv_tile(h * GDN_DK)) * (GDN_DK ** -0.5)
            k = l2n(conv_tile(GDN_QK + h * GDN_DK))
            v = conv_tile(2 * GDN_QK + h * GDN_DV)
            beta = beta_all[:, h:h + 1]
            eg = lane_of(eg_all, h)
            kb = k * beta
            gdiff = lane_of(gc, h) - gc_t[GDN_HEADS + h:GDN_HEADS + h + 1, :]
            q_sc[at(b, h)] = q.astype(BF16)
            k_sc[at(b, h)] = k.astype(BF16)
            kb_sc[at(b, h)] = kb.astype(BF16)
            vb_sc[at(b, h)] = v * beta
            kbg_sc[at(b, h)] = (kb * eg).astype(BF16)
            qd_sc[at(b, h)] = (q * eg).astype(BF16)
            kd_sc[at(b, h)] = (k * lane_of(ekd_all, h)).astype(BF16)
            dec_sc[at(b, h)] = jnp.where(causal, jnp.exp(jnp.where(causal, gdiff, 0.0)), 0.0)
        el_sc[b] = jnp.exp(g_last)


def _gdn_masks():
    c = GDN_BLOCK
    r = np.arange(c)[:, None]
    k = np.arange(c)[None, :]
    mats = [np.eye(c), (r // 8 == k // 8) & (r > k)]
    b = 8
    while b < c:
        mats.append((r // (2 * b) == k // (2 * b)) & ((r // b) % 2 == 1) & ((k // b) % 2 == 0))
        b *= 2
    return jnp.asarray(np.stack([np.asarray(m, np.float32) for m in mats]), BF16)


def _conv_band(c, pad):
    band = np.zeros((c, GDN_CONV * (pad + c)), np.float32)
    for i in range(GDN_CONV):
        band[np.arange(c), i * (pad + c) + pad + np.arange(c) - (GDN_CONV - 1) + i] = 1.0
    return jnp.asarray(band, BF16)


def _gdn(proj, ba, conv_w, alog_lane, dtb_lane, gain):
    t = proj.shape[0]
    c = GDN_BLOCK
    rows = GDN_STEP_BLOCKS * c
    pad = 16
    masks = _gdn_masks()
    nm = masks.shape[0]
    band = _conv_band(c, pad)
    n_steps = t // rows
    staged = lambda i: jnp.minimum(i, n_steps - 1)
    chained = lambda i: jnp.maximum(i - 1, 0)
    item_sq = lambda w, dt: pltpu.VMEM((GDN_STEP_BLOCKS * GDN_HEADS, c, w), dt)
    return pl.pallas_call(
        _gdn_kernel,
        out_shape=jax.ShapeDtypeStruct((t, GDN_V), BF16),
        grid=(n_steps + 1,),
        in_specs=[pl.BlockSpec((rows, GDN_CONV_CH), lambda i: (staged(i), 0)),
                  pl.BlockSpec((pad, GDN_CONV_CH),
                               lambda i: (jnp.maximum(staged(i) * (rows // pad) - 1, 0), 0)),
                  pl.BlockSpec((rows, GDN_V), lambda i: (chained(i), COL_ZA // GDN_V)),
                  pl.BlockSpec((rows, LANES), lambda i: (staged(i), 0)),
                  pl.BlockSpec((GDN_CONV, GDN_CONV_CH), lambda i: (0, 0)),
                  pl.BlockSpec(band.shape, lambda i: (0, 0)),
                  pl.BlockSpec((1, LANES), lambda i: (0, 0)),
                  pl.BlockSpec((1, LANES), lambda i: (0, 0)),
                  pl.BlockSpec((1, GDN_DV), lambda i: (0, 0)),
                  pl.BlockSpec((nm, c, c), lambda i: (0, 0, 0))],
        out_specs=pl.BlockSpec((rows, GDN_V), lambda i: (chained(i), 0)),
        scratch_shapes=[pltpu.VMEM((GDN_HEADS, GDN_DK, GDN_DV), F32),
                        item_sq(GDN_DK, BF16), item_sq(GDN_DK, BF16), item_sq(GDN_DK, BF16),
                        item_sq(GDN_DV, F32), item_sq(GDN_DK, BF16),
                        item_sq(GDN_DK, BF16), item_sq(GDN_DK, BF16),
                        item_sq(c, F32),
                        pltpu.VMEM((GDN_STEP_BLOCKS, 1, LANES), F32)],
        compiler_params=_params(("arbitrary",)),
        name="gdn",
    )(proj, proj, proj, ba, conv_w, band, alog_lane, dtb_lane, gain, masks)


def _swa_kernel(q_ref, kc_ref, kp_ref, vc_ref, vp_ref, z_ref, bias_ref, qg_ref, kg_ref, seg_ref, o_ref):
    n = pl.program_id(0)
    qb = SWA_BLOCK
    hd = SWA_HEAD_DIM
    n_tiles = SWA_Q // LANES
    blocks = range(SWA_STEP_BLOCKS)
    items = [(j, tq) for j in blocks for tq in range(n_tiles)]
    lane = lax.broadcasted_iota(jnp.int32, (1, LANES), 1)
    lo = lane < hd
    seg = seg_ref[...].astype(BF16)

    def qk_norm(a, gain):
        ms = jnp.dot((a * a).astype(BF16), seg, preferred_element_type=F32)
        return a * (lax.rsqrt(ms + EPS) * gain)

    def two_heads(lo_part, hi_part):
        return jnp.concatenate([jnp.where(lo, lo_part, 0.0), jnp.where(lo, 0.0, hi_part)], axis=0)

    kall = jnp.concatenate([kp_ref[...], kc_ref[...]], axis=0).astype(F32)
    vall = jnp.concatenate([vp_ref[...], vc_ref[...]], axis=0).astype(F32)
    kn, vv = {}, {}
    for u in range(SWA_KV // LANES):
        knorm = qk_norm(kall[:, u * LANES:(u + 1) * LANES], kg_ref[...])
        vtile = vall[:, u * LANES:(u + 1) * LANES]
        for ci in range(SWA_STEP_BLOCKS + 1):
            kc = knorm[ci * qb:(ci + 1) * qb]
            vc = vtile[ci * qb:(ci + 1) * qb]
            kn[u, ci] = (kc, pltpu.roll(kc, hd, axis=1))
            vv[u, ci] = (vc, pltpu.roll(vc, hd, axis=1))

    row = lax.broadcasted_iota(jnp.int32, (qb, 1), 0)
    is_sink = row == 0
    k2s, rhs = {}, {}
    for j in blocks:
        prev_live = (n * SWA_STEP_BLOCKS + j) > 0
        ones_prev = jnp.where(prev_live | is_sink, 1.0, 0.0)
        ones_band = jnp.concatenate([ones_prev, jnp.ones_like(ones_prev)], axis=0)
        ones2 = two_heads(ones_band, ones_band)
        for u in range(SWA_KV // LANES):
            kband = [jnp.concatenate([jnp.where(is_sink, 0.0, kn[u, j][r]), kn[u, j + 1][r]], axis=0)
                     for r in range(2)]
            vband = [jnp.concatenate([jnp.where(prev_live & ~is_sink, vv[u, j][r], 0.0), vv[u, j + 1][r]],
                                     axis=0) for r in range(2)]
            for half in range(2):
                g = 2 * u + half
                k2s[j, g] = two_heads(kband[half], kband[1 - half]).astype(BF16)
                v2 = two_heads(vband[half], vband[1 - half])
                rhs[j, g] = jnp.concatenate([v2, ones2], axis=1).astype(BF16)

    q_gain = qg_ref[...] * (hd ** -0.5 * LOG2E)
    kv_of = lambda tq: (2 * tq) // (SWA_HEADS // SWA_KV_HEADS)
    qt = {(j, tq): qk_norm(q_ref[j * qb:(j + 1) * qb, tq * LANES:(tq + 1) * LANES].astype(F32), q_gain)
          for j, tq in items}
    logits = {(j, tq): _mm_nt(qt[j, tq], k2s[j, kv_of(tq)]) for j, tq in items}
    probs = {}
    for j, tq in items:
        ps = []
        for e in range(2):
            s = logits[j, tq][:, e * 2 * qb:(e + 1) * 2 * qb] + bias_ref[2 * tq + e]
            ps.append(jnp.exp2(s - jnp.max(s, axis=-1, keepdims=True)).astype(BF16))
        probs[j, tq] = jnp.concatenate(ps, axis=1)
    pv = {(j, tq): jnp.dot(probs[j, tq], rhs[j, kv_of(tq)], preferred_element_type=F32) for j, tq in items}
    for j, tq in items:
        out = pv[j, tq][:, :LANES] * (1.0 / pv[j, tq][:, LANES:])
        z = z_ref[j * qb:(j + 1) * qb, tq * LANES:(tq + 1) * LANES].astype(F32)
        o_ref[j * qb:(j + 1) * qb, tq * LANES:(tq + 1) * LANES] = (out * _silu(z)).astype(o_ref.dtype)


def _swa(proj, bias, qg2, kg2):
    t = proj.shape[0]
    qb = SWA_BLOCK
    rows = SWA_STEP_BLOCKS * qb
    seg = np.kron(np.eye(LANES // SWA_HEAD_DIM), np.ones((SWA_HEAD_DIM, SWA_HEAD_DIM))) / SWA_HEAD_DIM
    seg = jnp.asarray(seg, F32)
    kcol = COL_KB // SWA_KV
    vcol = COL_VB // SWA_KV
    prev = lambda i: jnp.maximum(i * SWA_STEP_BLOCKS - 1, 0)
    return pl.pallas_call(
        _swa_kernel,
        out_shape=jax.ShapeDtypeStruct((t, SWA_Q), BF16),
        grid=(t // rows,),
        in_specs=[pl.BlockSpec((rows, SWA_Q), lambda i: (i, COL_QB // SWA_Q)),
                  pl.BlockSpec((rows, SWA_KV), lambda i: (i, kcol)),
                  pl.BlockSpec((qb, SWA_KV), lambda i: (prev(i), kcol)),
                  pl.BlockSpec((rows, SWA_KV), lambda i: (i, vcol)),
                  pl.BlockSpec((qb, SWA_KV), lambda i: (prev(i), vcol)),
                  pl.BlockSpec((rows, SWA_Q), lambda i: (i, COL_ZB // SWA_Q)),
                  pl.BlockSpec((SWA_HEADS, qb, 2 * qb), lambda i: (0, 0, 0)),
                  pl.BlockSpec((1, LANES), lambda i: (0, 0)),
                  pl.BlockSpec((1, LANES), lambda i: (0, 0)),
                  pl.BlockSpec((LANES, LANES), lambda i: (0, 0))],
        out_specs=pl.BlockSpec((rows, SWA_Q), lambda i: (i, 0)),
        compiler_params=_params(("arbitrary",)),
        name="swa",
    )(proj, proj, proj, proj, proj, proj, bias, qg2, kg2, seg)


def _merge_out_kernel(x_ref, oa_ref, ob_ref, ga_ref, gb_ref, gate_ref, wa_ref, wb_ref, wo_ref, o_ref):
    ya = jnp.dot(oa_ref[...], wa_ref[...], preferred_element_type=F32)
    yb = jnp.dot(ob_ref[...], wb_ref[...], preferred_element_type=F32)
    mixed = _sigmoid(ga_ref[...].astype(F32)) * ya + _sigmoid(gb_ref[...].astype(F32)) * yb
    y = jnp.dot(mixed.astype(BF16), wo_ref[...], preferred_element_type=F32)
    o_ref[...] = x_ref[...] + gate_ref[...] * y


def _merge_out(x2d, o_a, o_b, proj, gate, w_a, w_b, w_o, *, tm):
    t, d = x2d.shape
    const = lambda shape: pl.BlockSpec(shape, lambda i: (0, 0), pipeline_mode=pl.Buffered(1))
    return pl.pallas_call(
        _merge_out_kernel,
        out_shape=jax.ShapeDtypeStruct((t, d), F32),
        grid=(t // tm,),
        in_specs=[pl.BlockSpec((tm, d), lambda i: (i, 0)),
                  pl.BlockSpec((tm, GDN_V), lambda i: (i, 0)),
                  pl.BlockSpec((tm, SWA_Q), lambda i: (i, 0)),
                  pl.BlockSpec((tm, d), lambda i: (i, COL_GA // D_MODEL)),
                  pl.BlockSpec((tm, d), lambda i: (i, COL_GB // D_MODEL)),
                  pl.BlockSpec((1, d), lambda i: (0, 0)),
                  const((GDN_V, d)), const((SWA_Q, d)), const((d, d))],
        out_specs=pl.BlockSpec((tm, d), lambda i: (i, 0)),
        compiler_params=_params(("arbitrary",)),
        name="merge_out",
    )(x2d, o_a, o_b, proj, proj, gate, w_a, w_b, w_o)


REPACK_TILE = 512
N_GATE_COLS = 2 * GDN_HEADS


def _repack_plan():
    src_of = {COL_QKV: 0, COL_ZA: GDN_CONV_CH}
    after_gates = COL_QB + N_GATE_COLS
    src_of.update({COL_QB: after_gates, COL_KB: after_gates + SWA_Q, COL_VB: after_gates + SWA_Q + SWA_KV,
                   COL_ZB: after_gates + SWA_Q + 2 * SWA_KV, COL_GA: after_gates + 2 * SWA_Q + 2 * SWA_KV,
                   COL_GB: after_gates + 2 * SWA_Q + 2 * SWA_KV + D_MODEL})
    starts = sorted(src_of)
    src = []
    for dst in range(0, PROJ_COLS, REPACK_TILE):
        grp = max(s for s in starts if s <= dst)
        src.append(src_of[grp] + dst - grp)
    return np.asarray(src, np.int32)


def _repack_kernel(tab_ref, w_ref, g_ref, big_ref, small_ref):
    del tab_ref
    big_ref[...] = w_ref[...].astype(BF16)

    @pl.when(pl.program_id(0) == 0)
    def _():
        row = lax.broadcasted_iota(jnp.int32, (LANES, 1), 0)
        small_ref[...] = jnp.where(row < N_GATE_COLS, g_ref[...], 0.0).astype(BF16)


def _repack_w_in(w_t):
    d = w_t.shape[1]
    src = _repack_plan()
    gate_row = COL_QB
    assert gate_row % LANES == 0 and np.all(src % N_GATE_COLS == 0)
    return pl.pallas_call(
        _repack_kernel,
        out_shape=(jax.ShapeDtypeStruct((PROJ_COLS, d), BF16), jax.ShapeDtypeStruct((LANES, d), BF16)),
        grid_spec=pltpu.PrefetchScalarGridSpec(
            num_scalar_prefetch=1,
            grid=(PROJ_COLS // REPACK_TILE,),
            in_specs=[pl.BlockSpec((pl.Element(REPACK_TILE), pl.Element(d)),
                                   lambda o, tab: (tab[o] * N_GATE_COLS, 0)),
                      pl.BlockSpec((LANES, d), lambda o, tab: (gate_row // LANES, 0))],
            out_specs=(pl.BlockSpec((REPACK_TILE, d), lambda o, tab: (o, 0)),
                       pl.BlockSpec((LANES, d), lambda o, tab: (0, 0)))),
        compiler_params=_params(("arbitrary",)),
        name="repack_w_in",
    )(jnp.asarray(src // N_GATE_COLS), w_t, w_t)


def _lane_row(vec, offset):
    return jnp.pad(vec.astype(F32), (offset, LANES - offset - vec.shape[0]))[None, :]


def kernel(x, c, w_ada, b_ada, norm_gain, w_in, conv_w, a_log, dt_bias, gdn_norm_gain, q_norm_gain,
           k_norm_gain, sinks, rel_bias, w_branch_gdn, w_branch_swa, w_out):
    bsz, t, d = x.shape
    depth = w_in.shape[0]
    outs = []
    for b in range(bsz):
        xb = x[b]
        c_col = c[b].astype(F32)[:, None]
        for l in range(depth):
            mod = _ada_mod(c_col, w_ada[l], b_ada[l][None, :])
            shift, scale, gate = mod[:, :d], mod[:, d:2 * d], mod[:, 2 * d:]
            w_big, w_small = _repack_w_in(w_in[l].T)
            proj, ba = _in_proj(xb, norm_gain[l][None, :], scale, shift, w_big, w_small,
                                tm=min(1024, t), tn=1792)
            o_a = _gdn(proj, ba, conv_w[l], _lane_row(a_log[l], GDN_HEADS), _lane_row(dt_bias[l], GDN_HEADS),
                       gdn_norm_gain[l][None, :])
            bias = _swa_bias(rel_bias.T.astype(F32), sinks[l].astype(F32))
            o_b = _swa(proj, bias,
                       jnp.tile(q_norm_gain[l], LANES // SWA_HEAD_DIM)[None, :],
                       jnp.tile(k_norm_gain[l], LANES // SWA_HEAD_DIM)[None, :])
            xb = _merge_out(xb, o_a, o_b, proj, gate, w_branch_gdn[l].astype(BF16),
                            w_branch_swa[l].astype(BF16), w_out[l].astype(BF16), tm=min(512, t))
        outs.append(xb)
    return jnp.stack(outs, axis=0)
```

```python
import functools
import math

import jax
import jax.numpy as jnp
import numpy as np
from jax import lax
from jax.experimental import pallas as pl
from jax.experimental.pallas import tpu as pltpu

F32 = jnp.float32
BF16 = jnp.bfloat16

LANES = 128
D_MODEL = 2048
GDN_HEADS = 8
GDN_DK = 128
GDN_DV = 128
GDN_CONV = 4
GDN_QK = GDN_HEADS * GDN_DK
GDN_V = GDN_HEADS * GDN_DV
GDN_CONV_CH = 2 * GDN_QK + GDN_V
GDN_BLOCK = 128
GDN_STEP_BLOCKS = 2
GDN_CONV_PIECE = 512
SWA_HEADS = 16
SWA_KV_HEADS = 4
SWA_HEAD_DIM = 64
SWA_WINDOW = 128
SWA_BLOCK = 128
SWA_STEP_BLOCKS = 2
SWA_Q = SWA_HEADS * SWA_HEAD_DIM
SWA_KV = SWA_KV_HEADS * SWA_HEAD_DIM
REL_BUCKETS = 32
REL_MAX_DIST = 128
EPS = 1e-6
NEG_BIG = -1e30
LOG2E = math.log2(math.e)

COL_QKV = 0
COL_ZA = GDN_CONV_CH
COL_QB = COL_ZA + GDN_V
COL_ZB = COL_QB + SWA_Q
COL_GA = COL_ZB + SWA_Q
COL_GB = COL_GA + D_MODEL
COL_KB = COL_GB + D_MODEL
COL_VB = COL_KB + SWA_KV
PROJ_COLS = COL_VB + SWA_KV
for _col, _width in ((COL_ZA, GDN_V), (COL_QB, SWA_Q), (COL_ZB, SWA_Q), (COL_GA, D_MODEL),
                     (COL_GB, D_MODEL), (COL_KB, SWA_KV), (COL_VB, SWA_KV)):
    assert _col % _width == 0

VMEM_LIMIT = 56 * 1024 * 1024


def _sigmoid(x):
    return 0.5 + 0.5 * jnp.tanh(0.5 * x)


def _silu(x):
    half = 0.5 * x
    return half + half * jnp.tanh(half)


def _params(sem):
    return pltpu.CompilerParams(dimension_semantics=sem, vmem_limit_bytes=VMEM_LIMIT)


def _ada_mod_kernel(c_ref, w_ref, b_ref, o_ref):
    c = c_ref[...]
    o_ref[...] = jnp.sum(_silu(c) * w_ref[...], axis=0, keepdims=True) + b_ref[...]


def _ada_mod(c_col, w_ada, b_ada):
    d, n = w_ada.shape
    tn = 512
    return pl.pallas_call(
        _ada_mod_kernel,
        out_shape=jax.ShapeDtypeStruct((1, n), F32),
        grid=(n // tn,),
        in_specs=[pl.BlockSpec((d, 1), lambda j: (0, 0)),
                  pl.BlockSpec((d, tn), lambda j: (0, j)),
                  pl.BlockSpec((1, tn), lambda j: (0, j))],
        out_specs=pl.BlockSpec((1, tn), lambda j: (0, j)),
        compiler_params=_params(("arbitrary",)),
        name="ada_mod",
    )(c_col, w_ada, b_ada)


def _in_proj_kernel(x_ref, gain_ref, scale_ref, shift_ref, w_ref, ws_ref, o_ref, ba_ref, h_ref,
                    *, row_chunk):
    j = pl.program_id(1)

    @pl.when(j == 0)
    def _():
        gs = gain_ref[...] * (1.0 + scale_ref[...])
        sh = shift_ref[...]
        tm = x_ref.shape[0]

        def body(r, carry):
            rows = pl.ds(pl.multiple_of(r * row_chunk, row_chunk), row_chunk)
            x = x_ref[rows, :]
            ms = jnp.mean(x * x, axis=-1, keepdims=True)
            h = (x * lax.rsqrt(ms + EPS)) * gs + sh
            h_ref[rows, :] = h.astype(BF16)
            return carry

        lax.fori_loop(0, tm // row_chunk, body, 0)
        ba_ref[...] = _mm_nt(h_ref[...], ws_ref[...])

    o_ref[...] = _mm_nt(h_ref[...], w_ref[...]).astype(o_ref.dtype)


def _in_proj(x2d, gain, scale, shift, w_big, w_small, *, tm, tn):
    t, d = x2d.shape
    n = w_big.shape[0]
    return pl.pallas_call(
        functools.partial(_in_proj_kernel, row_chunk=128),
        out_shape=(jax.ShapeDtypeStruct((t, n), BF16), jax.ShapeDtypeStruct((t, LANES), F32)),
        grid=(t // tm, n // tn),
        in_specs=[pl.BlockSpec((tm, d), lambda i, j: (i, 0)),
                  pl.BlockSpec((1, d), lambda i, j: (0, 0)),
                  pl.BlockSpec((1, d), lambda i, j: (0, 0)),
                  pl.BlockSpec((1, d), lambda i, j: (0, 0)),
                  pl.BlockSpec((tn, d), lambda i, j: (j, 0)),
                  pl.BlockSpec((LANES, d), lambda i, j: (0, 0))],
        out_specs=(pl.BlockSpec((tm, tn), lambda i, j: (i, j)),
                   pl.BlockSpec((tm, LANES), lambda i, j: (i, 0))),
        scratch_shapes=[pltpu.VMEM((tm, d), BF16)],
        compiler_params=_params(("arbitrary", "arbitrary")),
        name="in_proj",
    )(x2d, gain, scale, shift, w_big, w_small)


def _swa_bias_kernel(tab_ref, sink_ref, o_ref):
    h = pl.program_id(0)
    q = SWA_BLOCK
    qpos = lax.broadcasted_iota(jnp.int32, (q, 2 * q), 0) + q
    kpos = lax.broadcasted_iota(jnp.int32, (q, 2 * q), 1)
    dist = qpos - kpos
    in_window = (dist >= 0) & (dist < SWA_WINDOW)
    d = jnp.maximum(dist, 0)
    max_exact = REL_BUCKETS // 2
    df = jnp.maximum(d, 1).astype(F32)
    large = max_exact + (jnp.log(df / max_exact) / math.log(REL_MAX_DIST / max_exact)
                         * (REL_BUCKETS - max_exact)).astype(jnp.int32)
    large = jnp.minimum(large, REL_BUCKETS - 1)
    bucket = jnp.where(d < max_exact, d, large)
    acc = jnp.zeros((q, 2 * q), F32)
    for b in range(REL_BUCKETS):
        acc = jnp.where(bucket == b, tab_ref[h, b], acc)
    table = jnp.where(in_window, acc, NEG_BIG)
    o_ref[0] = jnp.where(kpos == 0, sink_ref[h], table) * LOG2E


def _swa_bias(rel_bias_t, sinks):
    q = SWA_BLOCK
    return pl.pallas_call(
        _swa_bias_kernel,
        out_shape=jax.ShapeDtypeStruct((SWA_HEADS, q, 2 * q), F32),
        grid=(SWA_HEADS,),
        in_specs=[pl.BlockSpec(memory_space=pltpu.SMEM), pl.BlockSpec(memory_space=pltpu.SMEM)],
        out_specs=pl.BlockSpec((1, q, 2 * q), lambda h: (h, 0, 0)),
        compiler_params=_params(("arbitrary",)),
        name="swa_bias",
    )(rel_bias_t, sinks)


def _chunk_cumsum_rows(x):
    n = x.shape[0]
    row = lax.broadcasted_iota(jnp.int32, x.shape, 0)
    s = 1
    while s < n:
        x = x + jnp.where(row >= s, pltpu.roll(x, s, axis=0), 0.0)
        s *= 2
    return x


def _mm(a, b):
    return jnp.dot(a.astype(BF16), b.astype(BF16), preferred_element_type=F32)


def _mm_nt(a, b):
    return lax.dot_general(a.astype(BF16), b.astype(BF16), (((1,), (1,)), ((), ())),
                           preferred_element_type=F32)


def _mm_tn(a, b):
    return lax.dot_general(a.astype(BF16), b.astype(BF16), (((0,), (0,)), ((), ())),
                           preferred_element_type=F32)


def _gdn_kernel(cur_ref, prev_ref, z_ref, ba_ref, convw_ref, band_ref, alog_ref, dtb_ref, gain_ref,
                masks_ref, o_ref, s_ref, q_sc, k_sc, kb_sc, dec_sc, t_sc, attn_sc,
                vb_sc, kbg_sc, qd_sc, kd_sc, el_sc):
    s = pl.program_id(0)
    slot = lax.rem(s, 2)
    c = GDN_BLOCK
    heads = range(GDN_HEADS)
    blocks = range(GDN_STEP_BLOCKS)
    items = [(b, h) for b in blocks for h in heads]
    at = lambda b, h: b * GDN_HEADS + h
    lane_of = lambda a, h: a[:, GDN_HEADS + h:GDN_HEADS + h + 1]

    @pl.when(s == 0)
    def _():
        for ref in (s_ref, q_sc, k_sc, kb_sc, dec_sc, t_sc, attn_sc, vb_sc, kbg_sc, qd_sc, kd_sc, el_sc):
            ref[...] = jnp.zeros_like(ref)

    pad = prev_ref.shape[0]
    conv_w = convw_ref[...].astype(BF16)
    taps = []
    for b in blocks:
        if b == 0:
            ctx = jnp.where(s > 0, prev_ref[...], jnp.zeros_like(prev_ref))
        else:
            ctx = cur_ref[b * c - pad:b * c, :]
        xcat = jnp.concatenate([ctx, cur_ref[b * c:(b + 1) * c, :]], axis=0)
        taps.append(jnp.concatenate([xcat * conv_w[i:i + 1, :] for i in range(GDN_CONV)], axis=0))
    conv_pieces = {}
    todo = [(b, p) for p in range(GDN_CONV_CH // GDN_CONV_PIECE) for b in blocks]

    def next_conv_piece():
        if todo:
            b, p = todo.pop(0)
            cols = slice(p * GDN_CONV_PIECE, (p + 1) * GDN_CONV_PIECE)
            conv_pieces[b, p] = jnp.dot(band_ref[...], taps[b][:, cols], preferred_element_type=F32)

    row = lax.broadcasted_iota(jnp.int32, (c, c), 0)
    col = lax.broadcasted_iota(jnp.int32, (c, c), 1)

    def local_chain():
        strict = row > col
        kk = [_mm_nt(kb_sc[i], k_sc[i]) for i in range(len(items))]
        yield
        qk = [_mm_nt(q_sc[i], k_sc[i]) for i in range(len(items))]
        yield
        l_bf = [jnp.where(strict, kk[i] * dec_sc[i], 0.0).astype(BF16) for i in range(len(items))]
        attn = [(qk[i] * dec_sc[i]).astype(BF16) for i in range(len(items))]
        eye = masks_ref[0].astype(F32)
        m0 = [l * masks_ref[1] for l in l_bf]
        x = [eye - m for m in m0]
        p = [_mm(m, m) for m in m0]
        yield
        x = [xi + _mm(xi, pi) for xi, pi in zip(x, p)]
        yield
        p = [_mm(pi, pi) for pi in p]
        yield
        x = [xi + _mm(xi, pi) for xi, pi in zip(x, p)]
        yield
        for lvl in range(2, masks_ref.shape[0]):
            nx = [_mm(l * masks_ref[lvl], xi) for l, xi in zip(l_bf, x)]
            yield
            x = [xi - _mm(xi, ni) for xi, ni in zip(x, nx)]
            yield
        for i in range(len(items)):
            t_sc[i] = x[i].astype(BF16)
            attn_sc[i] = attn[i]

    def recurrence():
        state = [s_ref[h] for h in heads]
        for b in blocks:
            el = el_sc[slot, b]
            s_bf = [state[h].astype(BF16) for h in heads]
            resid = [vb_sc[slot, at(b, h)] - _mm(kbg_sc[slot, at(b, h)], s_bf[h]) for h in heads]
            yield
            v_new = [_mm(t_sc[at(b, h)], resid[h]).astype(BF16) for h in heads]
            yield
            o = [_mm(jnp.concatenate([qd_sc[slot, at(b, h)], attn_sc[at(b, h)]], axis=1),
                     jnp.concatenate([s_bf[h], v_new[h]], axis=0)) for h in heads]
            state = [state[h] * lane_of(el, h) + _mm_tn(kd_sc[slot, at(b, h)], v_new[h]) for h in heads]
            yield
            for h in heads:
                on = o[h] * lax.rsqrt(jnp.mean(o[h] * o[h], axis=-1, keepdims=True) + EPS) * gain_ref[...]
                z = z_ref[b * c:(b + 1) * c, h * GDN_DV:(h + 1) * GDN_DV].astype(F32)
                o_ref[b * c:(b + 1) * c, h * GDN_DV:(h + 1) * GDN_DV] = (on * _silu(z)).astype(o_ref.dtype)
        for h in heads:
            s_ref[h] = state[h]

    running = [local_chain(), recurrence()]
    while running:
        for gen in list(running):
            if next(gen, StopIteration) is StopIteration:
                running.remove(gen)
        next_conv_piece()
        next_conv_piece()
    while todo:
        next_conv_piece()

    causal = row >= col

    def l2n(a, scale=1.0):
        return a * (lax.rsqrt(jnp.sum(a * a, axis=-1, keepdims=True) + EPS) * scale)

    for b in blocks:
        ba = ba_ref[b * c:(b + 1) * c, :]
        beta_all = _sigmoid(ba)
        xg = ba + dtb_ref[...]
        softplus = jnp.maximum(xg, 0.0) + jnp.log(1.0 + jnp.exp(-jnp.abs(xg)))
        g_all = -jnp.exp(alog_ref[...]) * softplus
        gc = _chunk_cumsum_rows(g_all)
        gc_t = gc.T
        g_last = gc[c - 1:c, :]
        eg_all = jnp.exp(gc)
        ekd_all = jnp.exp(g_last - gc)
        conv_tile = lambda col0: _silu(conv_pieces[b, col0 // GDN_CONV_PIECE][
            :, col0 % GDN_CONV_PIECE:col0 % GDN_CONV_PIECE + LANES])
        for h in heads:
            q = l2n(conv_tile(h * GDN_DK), GDN_DK ** -0.5)
            k = l2n(conv_tile(GDN_QK + h * GDN_DK))
            v = conv_tile(2 * GDN_QK + h * GDN_DV)
            beta = beta_all[:, h:h + 1]
            eg = lane_of(eg_all, h)
            kb = k * beta
            gdiff = lane_of(gc, h) - gc_t[GDN_HEADS + h:GDN_HEADS + h + 1, :]
            q_sc[at(b, h)] = q.astype(BF16)
            k_sc[at(b, h)] = k.astype(BF16)
            kb_sc[at(b, h)] = kb.astype(BF16)
            dec_sc[at(b, h)] = jnp.where(causal, jnp.exp(jnp.where(causal, gdiff, 0.0)), 0.0)
            vb_sc[slot, at(b, h)] = v * beta
            kbg_sc[slot, at(b, h)] = (kb * eg).astype(BF16)
            qd_sc[slot, at(b, h)] = (q * eg).astype(BF16)
            kd_sc[slot, at(b, h)] = (k * lane_of(ekd_all, h)).astype(BF16)
        el_sc[slot, b] = jnp.exp(g_last)


def _gdn_masks():
    c = GDN_BLOCK
    r = np.arange(c)[:, None]
    k = np.arange(c)[None, :]
    mats = [np.eye(c), (r // 8 == k // 8) & (r > k)]
    b = 8
    while b < c:
        mats.append((r // (2 * b) == k // (2 * b)) & ((r // b) % 2 == 1) & ((k // b) % 2 == 0))
        b *= 2
    return jnp.asarray(np.stack([np.asarray(m, np.float32) for m in mats]), BF16)


def _conv_band(c, pad):
    band = np.zeros((c, GDN_CONV * (pad + c)), np.float32)
    for i in range(GDN_CONV):
        band[np.arange(c), i * (pad + c) + pad + np.arange(c) - (GDN_CONV - 1) + i] = 1.0
    return jnp.asarray(band, BF16)


def _gdn(proj, ba, conv_w, alog_lane, dtb_lane, gain):
    t = proj.shape[0]
    c = GDN_BLOCK
    rows = GDN_STEP_BLOCKS * c
    pad = 16
    masks = _gdn_masks()
    nm = masks.shape[0]
    band = _conv_band(c, pad)
    n_groups = t // rows
    staged = lambda i: jnp.minimum(i, n_groups - 1)
    finished = lambda i: jnp.maximum(i - 2, 0)
    n_items = GDN_STEP_BLOCKS * GDN_HEADS
    once = lambda w, dt: pltpu.VMEM((n_items, c, w), dt)
    twice = lambda w, dt: pltpu.VMEM((2, n_items, c, w), dt)
    return pl.pallas_call(
        _gdn_kernel,
        out_shape=jax.ShapeDtypeStruct((t, GDN_V), BF16),
        grid=(n_groups + 2,),
        in_specs=[pl.BlockSpec((rows, GDN_CONV_CH), lambda i: (staged(i), 0)),
                  pl.BlockSpec((pad, GDN_CONV_CH),
                               lambda i: (jnp.maximum(staged(i) * (rows // pad) - 1, 0), 0)),
                  pl.BlockSpec((rows, GDN_V), lambda i: (finished(i), COL_ZA // GDN_V)),
                  pl.BlockSpec((rows, LANES), lambda i: (staged(i), 0)),
                  pl.BlockSpec((GDN_CONV, GDN_CONV_CH), lambda i: (0, 0)),
                  pl.BlockSpec(band.shape, lambda i: (0, 0)),
                  pl.BlockSpec((1, LANES), lambda i: (0, 0)),
                  pl.BlockSpec((1, LANES), lambda i: (0, 0)),
                  pl.BlockSpec((1, GDN_DV), lambda i: (0, 0)),
                  pl.BlockSpec((nm, c, c), lambda i: (0, 0, 0))],
        out_specs=pl.BlockSpec((rows, GDN_V), lambda i: (finished(i), 0)),
        scratch_shapes=[pltpu.VMEM((GDN_HEADS, GDN_DK, GDN_DV), F32),
                        once(GDN_DK, BF16), once(GDN_DK, BF16), once(GDN_DK, BF16),
                        once(c, F32),
                        once(c, BF16), once(c, BF16),
                        twice(GDN_DV, F32), twice(GDN_DK, BF16),
                        twice(GDN_DK, BF16), twice(GDN_DK, BF16),
                        pltpu.VMEM((2, GDN_STEP_BLOCKS, 1, LANES), F32)],
        compiler_params=_params(("arbitrary",)),
        name="gdn",
    )(proj, proj, proj, ba, conv_w, band, alog_lane, dtb_lane, gain, masks)


def _swa_kernel(q_ref, kc_ref, kp_ref, vc_ref, vp_ref, z_ref, bias_ref, qg_ref, kg_ref, seg_ref, o_ref):
    n = pl.program_id(0)
    qb = SWA_BLOCK
    hd = SWA_HEAD_DIM
    n_tiles = SWA_Q // LANES
    blocks = range(SWA_STEP_BLOCKS)
    items = [(j, tq) for j in blocks for tq in range(n_tiles)]
    lane = lax.broadcasted_iota(jnp.int32, (1, LANES), 1)
    lo = lane < hd
    seg = seg_ref[...].astype(BF16)

    def qk_norm(a, gain):
        ms = jnp.dot((a * a).astype(BF16), seg, preferred_element_type=F32)
        return a * (lax.rsqrt(ms + EPS) * gain)

    def two_heads(lo_part, hi_part):
        return jnp.concatenate([jnp.where(lo, lo_part, 0.0), jnp.where(lo, 0.0, hi_part)], axis=0)

    kall = jnp.concatenate([kp_ref[...], kc_ref[...]], axis=0).astype(F32)
    vall = jnp.concatenate([vp_ref[...], vc_ref[...]], axis=0).astype(F32)
    kn, vv = {}, {}
    for u in range(SWA_KV // LANES):
        knorm = qk_norm(kall[:, u * LANES:(u + 1) * LANES], kg_ref[...])
        vtile = vall[:, u * LANES:(u + 1) * LANES]
        for ci in range(SWA_STEP_BLOCKS + 1):
            kc = knorm[ci * qb:(ci + 1) * qb]
            vc = vtile[ci * qb:(ci + 1) * qb]
            kn[u, ci] = (kc, pltpu.roll(kc, hd, axis=1))
            vv[u, ci] = (vc, pltpu.roll(vc, hd, axis=1))

    row = lax.broadcasted_iota(jnp.int32, (qb, 1), 0)
    is_sink = row == 0
    k2s, rhs = {}, {}
    for j in blocks:
        prev_live = (n * SWA_STEP_BLOCKS + j) > 0
        ones_prev = jnp.where(prev_live | is_sink, 1.0, 0.0)
        ones_band = jnp.concatenate([ones_prev, jnp.ones_like(ones_prev)], axis=0)
        ones2 = two_heads(ones_band, ones_band)
        for u in range(SWA_KV // LANES):
            kband = [jnp.concatenate([jnp.where(is_sink, 0.0, kn[u, j][r]), kn[u, j + 1][r]], axis=0)
                     for r in range(2)]
            vband = [jnp.concatenate([jnp.where(prev_live & ~is_sink, vv[u, j][r], 0.0), vv[u, j + 1][r]],
                                     axis=0) for r in range(2)]
            for half in range(2):
                g = 2 * u + half
                k2s[j, g] = two_heads(kband[half], kband[1 - half]).astype(BF16)
                v2 = two_heads(vband[half], vband[1 - half])
                rhs[j, g] = jnp.concatenate([v2, ones2], axis=1).astype(BF16)

    q_gain = qg_ref[...] * (hd ** -0.5 * LOG2E)
    kv_of = lambda tq: (2 * tq) // (SWA_HEADS // SWA_KV_HEADS)
    qt = {(j, tq): qk_norm(q_ref[j * qb:(j + 1) * qb, tq * LANES:(tq + 1) * LANES].astype(F32), q_gain)
          for j, tq in items}
    logits = {(j, tq): _mm_nt(qt[j, tq], k2s[j, kv_of(tq)]) for j, tq in items}
    probs = {}
    for j, tq in items:
        ps = []
        for e in range(2):
            s = logits[j, tq][:, e * 2 * qb:(e + 1) * 2 * qb] + bias_ref[2 * tq + e]
            ps.append(jnp.exp2(s - jnp.max(s, axis=-1, keepdims=True)).astype(BF16))
        probs[j, tq] = jnp.concatenate(ps, axis=1)
    pv = {(j, tq): jnp.dot(probs[j, tq], rhs[j, kv_of(tq)], preferred_element_type=F32) for j, tq in items}
    for j, tq in items:
        out = pv[j, tq][:, :LANES] * (1.0 / pv[j, tq][:, LANES:])
        z = z_ref[j * qb:(j + 1) * qb, tq * LANES:(tq + 1) * LANES].astype(F32)
        o_ref[j * qb:(j + 1) * qb, tq * LANES:(tq + 1) * LANES] = (out * _silu(z)).astype(o_ref.dtype)


def _swa(proj, bias, qg2, kg2):
    t = proj.shape[0]
    qb = SWA_BLOCK
    rows = SWA_STEP_BLOCKS * qb
    seg = np.kron(np.eye(LANES // SWA_HEAD_DIM), np.ones((SWA_HEAD_DIM, SWA_HEAD_DIM))) / SWA_HEAD_DIM
    seg = jnp.asarray(seg, F32)
    kcol = COL_KB // SWA_KV
    vcol = COL_VB // SWA_KV
    prev = lambda i: jnp.maximum(i * SWA_STEP_BLOCKS - 1, 0)
    return pl.pallas_call(
        _swa_kernel,
        out_shape=jax.ShapeDtypeStruct((t, SWA_Q), BF16),
        grid=(t // rows,),
        in_specs=[pl.BlockSpec((rows, SWA_Q), lambda i: (i, COL_QB // SWA_Q)),
                  pl.BlockSpec((rows, SWA_KV), lambda i: (i, kcol)),
                  pl.BlockSpec((qb, SWA_KV), lambda i: (prev(i), kcol)),
                  pl.BlockSpec((rows, SWA_KV), lambda i: (i, vcol)),
                  pl.BlockSpec((qb, SWA_KV), lambda i: (prev(i), vcol)),
                  pl.BlockSpec((rows, SWA_Q), lambda i: (i, COL_ZB // SWA_Q)),
                  pl.BlockSpec((SWA_HEADS, qb, 2 * qb), lambda i: (0, 0, 0)),
                  pl.BlockSpec((1, LANES), lambda i: (0, 0)),
                  pl.BlockSpec((1, LANES), lambda i: (0, 0)),
                  pl.BlockSpec((LANES, LANES), lambda i: (0, 0))],
        out_specs=pl.BlockSpec((rows, SWA_Q), lambda i: (i, 0)),
        compiler_params=_params(("arbitrary",)),
        name="swa",
    )(proj, proj, proj, proj, proj, proj, bias, qg2, kg2, seg)


def _merge_out_kernel(x_ref, oa_ref, ob_ref, ga_ref, gb_ref, gate_ref, wa_ref, wb_ref, wo_ref, o_ref):
    ya = jnp.dot(oa_ref[...], wa_ref[...], preferred_element_type=F32)
    yb = jnp.dot(ob_ref[...], wb_ref[...], preferred_element_type=F32)
    mixed = _sigmoid(ga_ref[...].astype(F32)) * ya + _sigmoid(gb_ref[...].astype(F32)) * yb
    y = jnp.dot(mixed.astype(BF16), wo_ref[...], preferred_element_type=F32)
    o_ref[...] = x_ref[...] + gate_ref[...] * y


def _merge_out(x2d, o_a, o_b, proj, gate, w_a, w_b, w_o, *, tm):
    t, d = x2d.shape
    const = lambda shape: pl.BlockSpec(shape, lambda i: (0, 0), pipeline_mode=pl.Buffered(1))
    return pl.pallas_call(
        _merge_out_kernel,
        out_shape=jax.ShapeDtypeStruct((t, d), F32),
        grid=(t // tm,),
        in_specs=[pl.BlockSpec((tm, d), lambda i: (i, 0)),
                  pl.BlockSpec((tm, GDN_V), lambda i: (i, 0)),
                  pl.BlockSpec((tm, SWA_Q), lambda i: (i, 0)),
                  pl.BlockSpec((tm, d), lambda i: (i, COL_GA // D_MODEL)),
                  pl.BlockSpec((tm, d), lambda i: (i, COL_GB // D_MODEL)),
                  pl.BlockSpec((1, d), lambda i: (0, 0)),
                  const((GDN_V, d)), const((SWA_Q, d)), const((d, d))],
        out_specs=pl.BlockSpec((tm, d), lambda i: (i, 0)),
        compiler_params=_params(("arbitrary",)),
        name="merge_out",
    )(x2d, o_a, o_b, proj, proj, gate, w_a, w_b, w_o)


REPACK_TILE = 512
N_GATE_COLS = 2 * GDN_HEADS


def _repack_plan():
    src_of = {COL_QKV: 0, COL_ZA: GDN_CONV_CH}
    after_gates = COL_QB + N_GATE_COLS
    src_of.update({COL_QB: after_gates, COL_KB: after_gates + SWA_Q, COL_VB: after_gates + SWA_Q + SWA_KV,
                   COL_ZB: after_gates + SWA_Q + 2 * SWA_KV, COL_GA: after_gates + 2 * SWA_Q + 2 * SWA_KV,
                   COL_GB: after_gates + 2 * SWA_Q + 2 * SWA_KV + D_MODEL})
    starts = sorted(src_of)
    src = []
    for dst in range(0, PROJ_COLS, REPACK_TILE):
        grp = max(s for s in starts if s <= dst)
        src.append(src_of[grp] + dst - grp)
    return np.asarray(src, np.int32)


def _repack_kernel(tab_ref, w_ref, g_ref, big_ref, small_ref):
    del tab_ref
    big_ref[...] = w_ref[...].astype(BF16)

    @pl.when(pl.program_id(0) == 0)
    def _():
        row = lax.broadcasted_iota(jnp.int32, (LANES, 1), 0)
        small_ref[...] = jnp.where(row < N_GATE_COLS, g_ref[...], 0.0).astype(BF16)


def _repack_w_in(w_t):
    d = w_t.shape[1]
    src = _repack_plan()
    gate_row = COL_QB
    assert gate_row % LANES == 0 and np.all(src % N_GATE_COLS == 0)
    return pl.pallas_call(
        _repack_kernel,
        out_shape=(jax.ShapeDtypeStruct((PROJ_COLS, d), BF16), jax.ShapeDtypeStruct((LANES, d), BF16)),
        grid_spec=pltpu.PrefetchScalarGridSpec(
            num_scalar_prefetch=1,
            grid=(PROJ_COLS // REPACK_TILE,),
            in_specs=[pl.BlockSpec((pl.Element(REPACK_TILE), pl.Element(d)),
                                   lambda o, tab: (tab[o] * N_GATE_COLS, 0)),
                      pl.BlockSpec((LANES, d), lambda o, tab: (gate_row // LANES, 0))],
            out_specs=(pl.BlockSpec((REPACK_TILE, d), lambda o, tab: (o, 0)),
                       pl.BlockSpec((LANES, d), lambda o, tab: (0, 0)))),
        compiler_params=_params(("arbitrary",)),
        name="repack_w_in",
    )(jnp.asarray(src // N_GATE_COLS), w_t, w_t)


def _lane_row(vec, offset):
    return jnp.pad(vec.astype(F32), (offset, LANES - offset - vec.shape[0]))[None, :]


def kernel(x, c, w_ada, b_ada, norm_gain, w_in, conv_w, a_log, dt_bias, gdn_norm_gain, q_norm_gain,
           k_norm_gain, sinks, rel_bias, w_branch_gdn, w_branch_swa, w_out):
    bsz, t, d = x.shape
    depth = w_in.shape[0]
    outs = []
    for b in range(bsz):
        xb = x[b]
        c_col = c[b].astype(F32)[:, None]
        for l in range(depth):
            mod = _ada_mod(c_col, w_ada[l], b_ada[l][None, :])
            shift, scale, gate = mod[:, :d], mod[:, d:2 * d], mod[:, 2 * d:]
            w_big, w_small = _repack_w_in(w_in[l].T)
            proj, ba = _in_proj(xb, norm_gain[l][None, :], scale, shift, w_big, w_small,
                                tm=min(1024, t), tn=1792)
            o_a = _gdn(proj, ba, conv_w[l], _lane_row(a_log[l], GDN_HEADS), _lane_row(dt_bias[l], GDN_HEADS),
                       gdn_norm_gain[l][None, :])
            bias = _swa_bias(rel_bias.T.astype(F32), sinks[l].astype(F32))
            o_b = _swa(proj, bias,
                       jnp.tile(q_norm_gain[l], LANES // SWA_HEAD_DIM)[None, :],
                       jnp.tile(k_norm_gain[l], LANES // SWA_HEAD_DIM)[None, :])
            xb = _merge_out(xb, o_a, o_b, proj, gate, w_branch_gdn[l].astype(BF16),
                            w_branch_swa[l].astype(BF16), w_out[l].astype(BF16), tm=min(512, t))
        outs.append(xb)
    return jnp.stack(outs, axis=0)
```

```python
import functools
import math

import jax
import jax.numpy as jnp
import numpy as np
from jax import lax
from jax.experimental import pallas as pl
from jax.experimental.pallas import tpu as pltpu

F32 = jnp.float32
BF16 = jnp.bfloat16

LANES = 128
D_MODEL = 2048
GDN_HEADS = 8
GDN_DK = 128
GDN_DV = 128
GDN_CONV = 4
GDN_QK = GDN_HEADS * GDN_DK
GDN_V = GDN_HEADS * GDN_DV
GDN_CONV_CH = 2 * GDN_QK + GDN_V
GDN_BLOCK = 128
GDN_STEP_BLOCKS = 2
GDN_CONV_PIECE = 512
SWA_HEADS = 16
SWA_KV_HEADS = 4
SWA_HEAD_DIM = 64
SWA_WINDOW = 128
SWA_BLOCK = 128
SWA_STEP_BLOCKS = 2
SWA_Q = SWA_HEADS * SWA_HEAD_DIM
SWA_KV = SWA_KV_HEADS * SWA_HEAD_DIM
REL_BUCKETS = 32
REL_MAX_DIST = 128
EPS = 1e-6
NEG_BIG = -1e30
LOG2E = math.log2(math.e)

COL_QKV = 0
COL_ZA = GDN_CONV_CH
COL_QB = COL_ZA + GDN_V
COL_ZB = COL_QB + SWA_Q
COL_GA = COL_ZB + SWA_Q
COL_GB = COL_GA + D_MODEL
COL_KB = COL_GB + D_MODEL
COL_VB = COL_KB + SWA_KV
PROJ_COLS = COL_VB + SWA_KV
for _col, _width in ((COL_ZA, GDN_V), (COL_QB, SWA_Q), (COL_ZB, SWA_Q), (COL_GA, D_MODEL),
                     (COL_GB, D_MODEL), (COL_KB, SWA_KV), (COL_VB, SWA_KV)):
    assert _col % _width == 0

VMEM_LIMIT = 56 * 1024 * 1024


def _sigmoid(x):
    return 0.5 + 0.5 * jnp.tanh(0.5 * x)


def _silu(x):
    half = 0.5 * x
    return half + half * jnp.tanh(half)


def _params(sem):
    return pltpu.CompilerParams(dimension_semantics=sem, vmem_limit_bytes=VMEM_LIMIT)


def _mm(a, b):
    return jnp.dot(a.astype(BF16), b.astype(BF16), preferred_element_type=F32)


def _mm_nt(a, b):
    return lax.dot_general(a.astype(BF16), b.astype(BF16), (((1,), (1,)), ((), ())),
                           preferred_element_type=F32)


def _mm_tn(a, b):
    return lax.dot_general(a.astype(BF16), b.astype(BF16), (((0,), (0,)), ((), ())),
                           preferred_element_type=F32)


def _ada_mod_kernel(c_ref, w_ref, b_ref, o_ref):
    c = c_ref[...]
    o_ref[...] = jnp.sum(_silu(c) * w_ref[...], axis=0, keepdims=True) + b_ref[...]


def _ada_mod(c_col, w_ada, b_ada):
    d, n = w_ada.shape
    tn = 1024
    return pl.pallas_call(
        _ada_mod_kernel,
        out_shape=jax.ShapeDtypeStruct((1, n), F32),
        grid=(n // tn,),
        in_specs=[pl.BlockSpec((d, 1), lambda j: (0, 0)),
                  pl.BlockSpec((d, tn), lambda j: (0, j)),
                  pl.BlockSpec((1, tn), lambda j: (0, j))],
        out_specs=pl.BlockSpec((1, tn), lambda j: (0, j)),
        compiler_params=_params(("arbitrary",)),
        name="ada_mod",
    )(c_col, w_ada, b_ada)


def _in_proj_kernel(x_ref, gain_ref, scale_ref, shift_ref, w_ref, ws_ref, o_ref, ba_ref, h_ref,
                    *, row_chunk):
    j = pl.program_id(1)

    @pl.when(j == 0)
    def _():
        gs = gain_ref[...] * (1.0 + scale_ref[...])
        sh = shift_ref[...]
        tm = x_ref.shape[0]

        def body(r, carry):
            rows = pl.ds(pl.multiple_of(r * row_chunk, row_chunk), row_chunk)
            x = x_ref[rows, :]
            ms = jnp.mean(x * x, axis=-1, keepdims=True)
            h = (x * lax.rsqrt(ms + EPS)) * gs + sh
            h_ref[rows, :] = h.astype(BF16)
            return carry

        lax.fori_loop(0, tm // row_chunk, body, 0)
        ba_ref[...] = _mm_nt(h_ref[...], ws_ref[...])

    o_ref[...] = _mm_nt(h_ref[...], w_ref[...]).astype(o_ref.dtype)


def _in_proj(x2d, gain, scale, shift, w_big, w_small, *, tm, tn):
    t, d = x2d.shape
    n = w_big.shape[0]
    return pl.pallas_call(
        functools.partial(_in_proj_kernel, row_chunk=128),
        out_shape=(jax.ShapeDtypeStruct((t, n), BF16), jax.ShapeDtypeStruct((t, LANES), F32)),
        grid=(t // tm, n // tn),
        in_specs=[pl.BlockSpec((tm, d), lambda i, j: (i, 0)),
                  pl.BlockSpec((1, d), lambda i, j: (0, 0)),
                  pl.BlockSpec((1, d), lambda i, j: (0, 0)),
                  pl.BlockSpec((1, d), lambda i, j: (0, 0)),
                  pl.BlockSpec((tn, d), lambda i, j: (j, 0)),
                  pl.BlockSpec((LANES, d), lambda i, j: (0, 0))],
        out_specs=(pl.BlockSpec((tm, tn), lambda i, j: (i, j)),
                   pl.BlockSpec((tm, LANES), lambda i, j: (i, 0))),
        scratch_shapes=[pltpu.VMEM((tm, d), BF16)],
        compiler_params=_params(("arbitrary", "arbitrary")),
        name="in_proj",
    )(x2d, gain, scale, shift, w_big, w_small)


def _swa_bias_kernel(tab_ref, sink_ref, o_ref):
    q = SWA_BLOCK
    qpos = lax.broadcasted_iota(jnp.int32, (q, 2 * q), 0) + q
    kpos = lax.broadcasted_iota(jnp.int32, (q, 2 * q), 1)
    dist = qpos - kpos
    in_window = (dist >= 0) & (dist < SWA_WINDOW)
    d = jnp.maximum(dist, 0)
    max_exact = REL_BUCKETS // 2
    df = jnp.maximum(d, 1).astype(F32)
    large = max_exact + (jnp.log(df / max_exact) / math.log(REL_MAX_DIST / max_exact)
                         * (REL_BUCKETS - max_exact)).astype(jnp.int32)
    large = jnp.minimum(large, REL_BUCKETS - 1)
    bucket = jnp.where(in_window, jnp.where(d < max_exact, d, large), REL_BUCKETS)
    for h in range(SWA_HEADS):
        acc = jnp.full((q, 2 * q), NEG_BIG, F32)
        for b in range(REL_BUCKETS):
            acc = jnp.where(bucket == b, tab_ref[h, b], acc)
        o_ref[h] = (jnp.where(kpos == 0, sink_ref[h], acc) * LOG2E).astype(o_ref.dtype)


def _swa_bias(rel_bias_t, sinks):
    q = SWA_BLOCK
    return pl.pallas_call(
        _swa_bias_kernel,
        out_shape=jax.ShapeDtypeStruct((SWA_HEADS, q, 2 * q), BF16),
        in_specs=[pl.BlockSpec(memory_space=pltpu.SMEM), pl.BlockSpec(memory_space=pltpu.SMEM)],
        out_specs=pl.BlockSpec(memory_space=pltpu.VMEM),
        compiler_params=pltpu.CompilerParams(vmem_limit_bytes=VMEM_LIMIT),
        name="swa_bias",
    )(rel_bias_t, sinks)


def _chunk_cumsum_rows(x):
    n = x.shape[0]
    row = lax.broadcasted_iota(jnp.int32, x.shape, 0)
    s = 1
    while s < n:
        x = x + jnp.where(row >= s, pltpu.roll(x, s, axis=0), 0.0)
        s *= 2
    return x


def _unit_lower_inverses(l_mats, masks_ref):
    eye = masks_ref[0].astype(F32)
    l_bf = [l.astype(BF16) for l in l_mats]
    m0 = [l * masks_ref[1] for l in l_bf]
    x = [eye - m for m in m0]
    p = [_mm(m, m) for m in m0]
    x = [xi + _mm(xi, pi) for xi, pi in zip(x, p)]
    p = [_mm(pi, pi) for pi in p]
    x = [xi + _mm(xi, pi) for xi, pi in zip(x, p)]
    for lvl in range(2, masks_ref.shape[0]):
        nx = [_mm(l * masks_ref[lvl], xi) for l, xi in zip(l_bf, x)]
        x = [xi - _mm(xi, ni) for xi, ni in zip(x, nx)]
    return x


def _gdn_kernel(cur_ref, prev_ref, z_ref, ba_ref, convw_ref, band_ref, alog_ref, dtb_ref, gain_ref,
                masks_ref, o_ref, s_ref, q_sc, k_sc, kb_sc, vb_sc, kbg_sc, qd_sc, kd_sc, dec_sc, el_sc):
    s = pl.program_id(0)
    c = GDN_BLOCK
    heads = range(GDN_HEADS)
    blocks = range(GDN_STEP_BLOCKS)
    items = [(b, h) for b in blocks for h in heads]
    at = lambda b, h: b * GDN_HEADS + h
    lane_of = lambda a, h: a[:, GDN_HEADS + h:GDN_HEADS + h + 1]

    @pl.when(s == 0)
    def _():
        for ref in (s_ref, q_sc, k_sc, kb_sc, vb_sc, kbg_sc, qd_sc, kd_sc, dec_sc, el_sc):
            ref[...] = jnp.zeros_like(ref)

    pad = prev_ref.shape[0]
    conv_w = convw_ref[...].astype(BF16)
    taps = []
    for b in blocks:
        if b == 0:
            ctx = jnp.where(s > 0, prev_ref[...], jnp.zeros_like(prev_ref))
        else:
            ctx = cur_ref[b * c - pad:b * c, :]
        xcat = jnp.concatenate([ctx, cur_ref[b * c:(b + 1) * c, :]], axis=0)
        taps.append(jnp.concatenate([xcat * conv_w[i:i + 1, :] for i in range(GDN_CONV)], axis=0))

    row = lax.broadcasted_iota(jnp.int32, (c, c), 0)
    col = lax.broadcasted_iota(jnp.int32, (c, c), 1)
    strict = row > col
    kk = [_mm_nt(kb_sc[at(b, h)], k_sc[at(b, h)]) for b, h in items]
    qk = [_mm_nt(q_sc[at(b, h)], k_sc[at(b, h)]) for b, h in items]
    conv_pieces = {}
    for p in range(GDN_CONV_CH // GDN_CONV_PIECE):
        for b in blocks:
            cols = slice(p * GDN_CONV_PIECE, (p + 1) * GDN_CONV_PIECE)
            conv_pieces[b, p] = jnp.dot(band_ref[...], taps[b][:, cols], preferred_element_type=F32)
    l_mat = [jnp.where(strict, kk[at(b, h)] * dec_sc[at(b, h)], 0.0) for b, h in items]
    attn = [(qk[at(b, h)] * dec_sc[at(b, h)]).astype(BF16) for b, h in items]
    t_inv = _unit_lower_inverses(l_mat, masks_ref)
    state = [s_ref[h] for h in heads]
    for b in blocks:
        el = el_sc[b]
        s_bf = [state[h].astype(BF16) for h in heads]
        resid = [vb_sc[at(b, h)] - _mm(kbg_sc[at(b, h)], s_bf[h]) for h in heads]
        v_new = [_mm(t_inv[at(b, h)], resid[h]).astype(BF16) for h in heads]
        o = [_mm(jnp.concatenate([qd_sc[at(b, h)], attn[at(b, h)]], axis=1),
                 jnp.concatenate([s_bf[h], v_new[h]], axis=0)) for h in heads]
        state = [state[h] * lane_of(el, h) + _mm_tn(kd_sc[at(b, h)], v_new[h]) for h in heads]
        for h in heads:
            on = o[h] * lax.rsqrt(jnp.mean(o[h] * o[h], axis=-1, keepdims=True) + EPS) * gain_ref[...]
            z = z_ref[b * c:(b + 1) * c, h * GDN_DV:(h + 1) * GDN_DV].astype(F32)
            o_ref[b * c:(b + 1) * c, h * GDN_DV:(h + 1) * GDN_DV] = (on * _silu(z)).astype(o_ref.dtype)
    for h in heads:
        s_ref[h] = state[h]

    causal = row >= col

    def l2n(a, scale=1.0):
        return a * (lax.rsqrt(jnp.sum(a * a, axis=-1, keepdims=True) + EPS) * scale)

    for b in blocks:
        ba = ba_ref[b * c:(b + 1) * c, :]
        beta_all = _sigmoid(ba)
        xg = ba + dtb_ref[...]
        softplus = jnp.maximum(xg, 0.0) + jnp.log(1.0 + jnp.exp(-jnp.abs(xg)))
        g_all = -jnp.exp(alog_ref[...]) * softplus
        gc = _chunk_cumsum_rows(g_all)
        gc_t = gc.T
        g_last = gc[c - 1:c, :]
        eg_all = jnp.exp(gc)
        ekd_all = jnp.exp(g_last - gc)
        conv_tile = lambda col0: _silu(conv_pieces[b, col0 // GDN_CONV_PIECE][
            :, col0 % GDN_CONV_PIECE:col0 % GDN_CONV_PIECE + LANES])
        for h in heads:
            q = l2n(conv_tile(h * GDN_DK), GDN_DK ** -0.5)
            k = l2n(conv_tile(GDN_QK + h * GDN_DK))
            v = conv_tile(2 * GDN_QK + h * GDN_DV)
            beta = beta_all[:, h:h + 1]
            eg = lane_of(eg_all, h)
            kb = k * beta
            gdiff = lane_of(gc, h) - gc_t[GDN_HEADS + h:GDN_HEADS + h + 1, :]
            q_sc[at(b, h)] = q.astype(BF16)
            k_sc[at(b, h)] = k.astype(BF16)
            kb_sc[at(b, h)] = kb.astype(BF16)
            vb_sc[at(b, h)] = v * beta
            kbg_sc[at(b, h)] = (kb * eg).astype(BF16)
            qd_sc[at(b, h)] = (q * eg).astype(BF16)
            kd_sc[at(b, h)] = (k * lane_of(ekd_all, h)).astype(BF16)
            dec_sc[at(b, h)] = jnp.where(causal, jnp.exp(jnp.where(causal, gdiff, 0.0)), 0.0)
        el_sc[b] = jnp.exp(g_last)


def _gdn_masks():
    c = GDN_BLOCK
    r = np.arange(c)[:, None]
    k = np.arange(c)[None, :]
    mats = [np.eye(c), (r // 8 == k // 8) & (r > k)]
    b = 8
    while b < c:
        mats.append((r // (2 * b) == k // (2 * b)) & ((r // b) % 2 == 1) & ((k // b) % 2 == 0))
        b *= 2
    return jnp.asarray(np.stack([np.asarray(m, np.float32) for m in mats]), BF16)


def _conv_band(c, pad):
    band = np.zeros((c, GDN_CONV * (pad + c)), np.float32)
    for i in range(GDN_CONV):
        band[np.arange(c), i * (pad + c) + pad + np.arange(c) - (GDN_CONV - 1) + i] = 1.0
    return jnp.asarray(band, BF16)


def _gdn(proj, ba, conv_w, alog_lane, dtb_lane, gain):
    t = proj.shape[0]
    c = GDN_BLOCK
    rows = GDN_STEP_BLOCKS * c
    pad = 16
    masks = _gdn_masks()
    nm = masks.shape[0]
    band = _conv_band(c, pad)
    n_steps = t // rows
    staged = lambda i: jnp.minimum(i, n_steps - 1)
    chained = lambda i: jnp.maximum(i - 1, 0)
    item_sq = lambda w, dt: pltpu.VMEM((GDN_STEP_BLOCKS * GDN_HEADS, c, w), dt)
    return pl.pallas_call(
        _gdn_kernel,
        out_shape=jax.ShapeDtypeStruct((t, GDN_V), BF16),
        grid=(n_steps + 1,),
        in_specs=[pl.BlockSpec((rows, GDN_CONV_CH), lambda i: (staged(i), 0)),
                  pl.BlockSpec((pad, GDN_CONV_CH),
                               lambda i: (jnp.maximum(staged(i) * (rows // pad) - 1, 0), 0)),
                  pl.BlockSpec((rows, GDN_V), lambda i: (chained(i), COL_ZA // GDN_V)),
                  pl.BlockSpec((rows, LANES), lambda i: (staged(i), 0)),
                  pl.BlockSpec((GDN_CONV, GDN_CONV_CH), lambda i: (0, 0)),
                  pl.BlockSpec(band.shape, lambda i: (0, 0)),
                  pl.BlockSpec((1, LANES), lambda i: (0, 0)),
                  pl.BlockSpec((1, LANES), lambda i: (0, 0)),
                  pl.BlockSpec((1, GDN_DV), lambda i: (0, 0)),
                  pl.BlockSpec((nm, c, c), lambda i: (0, 0, 0))],
        out_specs=pl.BlockSpec((rows, GDN_V), lambda i: (chained(i), 0)),
        scratch_shapes=[pltpu.VMEM((GDN_HEADS, GDN_DK, GDN_DV), F32),
                        item_sq(GDN_DK, BF16), item_sq(GDN_DK, BF16), item_sq(GDN_DK, BF16),
                        item_sq(GDN_DV, F32), item_sq(GDN_DK, BF16),
                        item_sq(GDN_DK, BF16), item_sq(GDN_DK, BF16),
                        item_sq(c, F32),
                        pltpu.VMEM((GDN_STEP_BLOCKS, 1, LANES), F32)],
        compiler_params=_params(("arbitrary",)),
        name="gdn",
    )(proj, proj, proj, ba, conv_w, band, alog_lane, dtb_lane, gain, masks)


def _swa_kernel(q_ref, kc_ref, kp_ref, vc_ref, vp_ref, z_ref, bias_ref, qg_ref, kg_ref, seg_ref, o_ref):
    n = pl.program_id(0)
    qb = SWA_BLOCK
    hd = SWA_HEAD_DIM
    n_tiles = SWA_Q // LANES
    blocks = range(SWA_STEP_BLOCKS)
    items = [(j, tq) for j in blocks for tq in range(n_tiles)]
    lane = lax.broadcasted_iota(jnp.int32, (1, LANES), 1)
    lo = lane < hd
    seg = seg_ref[...].astype(BF16)

    def qk_norm(a, gain):
        ms = jnp.dot((a * a).astype(BF16), seg, preferred_element_type=F32)
        return a * (lax.rsqrt(ms + EPS) * gain)

    def two_heads(lo_part, hi_part):
        return jnp.concatenate([jnp.where(lo, lo_part, 0.0), jnp.where(lo, 0.0, hi_part)], axis=0)

    kall = jnp.concatenate([kp_ref[...], kc_ref[...]], axis=0).astype(F32)
    vall = jnp.concatenate([vp_ref[...], vc_ref[...]], axis=0).astype(F32)
    kn, vv = {}, {}
    for u in range(SWA_KV // LANES):
        knorm = qk_norm(kall[:, u * LANES:(u + 1) * LANES], kg_ref[...])
        vtile = vall[:, u * LANES:(u + 1) * LANES]
        for ci in range(SWA_STEP_BLOCKS + 1):
            kc = knorm[ci * qb:(ci + 1) * qb]
            vc = vtile[ci * qb:(ci + 1) * qb]
            kn[u, ci] = (kc, pltpu.roll(kc, hd, axis=1))
            vv[u, ci] = (vc, pltpu.roll(vc, hd, axis=1))

    row = lax.broadcasted_iota(jnp.int32, (qb, 1), 0)
    is_sink = row == 0
    k2s, rhs = {}, {}
    for j in blocks:
        prev_live = (n * SWA_STEP_BLOCKS + j) > 0
        ones_prev = jnp.where(prev_live | is_sink, 1.0, 0.0)
        ones_band = jnp.concatenate([ones_prev, jnp.ones_like(ones_prev)], axis=0)
        ones2 = two_heads(ones_band, ones_band)
        for u in range(SWA_KV // LANES):
            kband = [jnp.concatenate([jnp.where(is_sink, 0.0, kn[u, j][r]), kn[u, j + 1][r]], axis=0)
                     for r in range(2)]
            vband = [jnp.concatenate([jnp.where(prev_live & ~is_sink, vv[u, j][r], 0.0), vv[u, j + 1][r]],
                                     axis=0) for r in range(2)]
            for half in range(2):
                g = 2 * u + half
                k2s[j, g] = two_heads(kband[half], kband[1 - half]).astype(BF16)
                v2 = two_heads(vband[half], vband[1 - half])
                rhs[j, g] = jnp.concatenate([v2, ones2], axis=1).astype(BF16)

    q_gain = qg_ref[...] * (hd ** -0.5 * LOG2E)
    kv_of = lambda tq: (2 * tq) // (SWA_HEADS // SWA_KV_HEADS)
    qt = {(j, tq): qk_norm(q_ref[j * qb:(j + 1) * qb, tq * LANES:(tq + 1) * LANES].astype(F32), q_gain)
          for j, tq in items}
    logits = {(j, tq): _mm_nt(qt[j, tq], k2s[j, kv_of(tq)]) for j, tq in items}
    probs = {}
    for j, tq in items:
        ps = []
        for e in range(2):
            s = logits[j, tq][:, e * 2 * qb:(e + 1) * 2 * qb].astype(BF16) + bias_ref[2 * tq + e]
            ps.append(jnp.exp2(s - jnp.max(s, axis=-1, keepdims=True)))
        probs[j, tq] = jnp.concatenate(ps, axis=1)
    pv = {(j, tq): jnp.dot(probs[j, tq], rhs[j, kv_of(tq)], preferred_element_type=F32) for j, tq in items}
    for j, tq in items:
        out = pv[j, tq][:, :LANES] * (1.0 / pv[j, tq][:, LANES:])
        z = z_ref[j * qb:(j + 1) * qb, tq * LANES:(tq + 1) * LANES].astype(F32)
        o_ref[j * qb:(j + 1) * qb, tq * LANES:(tq + 1) * LANES] = (out * _silu(z)).astype(o_ref.dtype)


def _swa(proj, bias, qg2, kg2):
    t = proj.shape[0]
    qb = SWA_BLOCK
    rows = SWA_STEP_BLOCKS * qb
    seg = np.kron(np.eye(LANES // SWA_HEAD_DIM), np.ones((SWA_HEAD_DIM, SWA_HEAD_DIM))) / SWA_HEAD_DIM
    seg = jnp.asarray(seg, F32)
    kcol = COL_KB // SWA_KV
    vcol = COL_VB // SWA_KV
    prev = lambda i: jnp.maximum(i * SWA_STEP_BLOCKS - 1, 0)
    return pl.pallas_call(
        _swa_kernel,
        out_shape=jax.ShapeDtypeStruct((t, SWA_Q), BF16),
        grid=(t // rows,),
        in_specs=[pl.BlockSpec((rows, SWA_Q), lambda i: (i, COL_QB // SWA_Q)),
                  pl.BlockSpec((rows, SWA_KV), lambda i: (i, kcol)),
                  pl.BlockSpec((qb, SWA_KV), lambda i: (prev(i), kcol)),
                  pl.BlockSpec((rows, SWA_KV), lambda i: (i, vcol)),
                  pl.BlockSpec((qb, SWA_KV), lambda i: (prev(i), vcol)),
                  pl.BlockSpec((rows, SWA_Q), lambda i: (i, COL_ZB // SWA_Q)),
                  pl.BlockSpec((SWA_HEADS, qb, 2 * qb), lambda i: (0, 0, 0)),
                  pl.BlockSpec((1, LANES), lambda i: (0, 0)),
                  pl.BlockSpec((1, LANES), lambda i: (0, 0)),
                  pl.BlockSpec((LANES, LANES), lambda i: (0, 0))],
        out_specs=pl.BlockSpec((rows, SWA_Q), lambda i: (i, 0)),
        compiler_params=_params(("arbitrary",)),
        name="swa",
    )(proj, proj, proj, proj, proj, proj, bias, qg2, kg2, seg)


def _merge_out_kernel(x_ref, oa_ref, ob_ref, ga_ref, gb_ref, gate_ref, wa_ref, wb_ref, wo_ref, o_ref):
    ya = jnp.dot(oa_ref[...], wa_ref[...], preferred_element_type=F32)
    yb = jnp.dot(ob_ref[...], wb_ref[...], preferred_element_type=F32)
    mixed = _sigmoid(ga_ref[...].astype(F32)) * ya + _sigmoid(gb_ref[...].astype(F32)) * yb
    y = jnp.dot(mixed.astype(BF16), wo_ref[...], preferred_element_type=F32)
    o_ref[...] = x_ref[...] + gate_ref[...] * y


def _merge_out(x2d, o_a, o_b, proj, gate, w_a, w_b, w_o, *, tm):
    t, d = x2d.shape
    const = lambda shape: pl.BlockSpec(shape, lambda i: (0, 0), pipeline_mode=pl.Buffered(1))
    return pl.pallas_call(
        _merge_out_kernel,
        out_shape=jax.ShapeDtypeStruct((t, d), F32),
        grid=(t // tm,),
        in_specs=[pl.BlockSpec((tm, d), lambda i: (i, 0)),
                  pl.BlockSpec((tm, GDN_V), lambda i: (i, 0)),
                  pl.BlockSpec((tm, SWA_Q), lambda i: (i, 0)),
                  pl.BlockSpec((tm, d), lambda i: (i, COL_GA // D_MODEL)),
                  pl.BlockSpec((tm, d), lambda i: (i, COL_GB // D_MODEL)),
                  pl.BlockSpec((1, d), lambda i: (0, 0)),
                  const((GDN_V, d)), const((SWA_Q, d)), const((d, d))],
        out_specs=pl.BlockSpec((tm, d), lambda i: (i, 0)),
        compiler_params=_params(("arbitrary",)),
        name="merge_out",
    )(x2d, o_a, o_b, proj, proj, gate, w_a, w_b, w_o)


REPACK_TILE = 512
N_GATE_COLS = 2 * GDN_HEADS


def _repack_plan():
    src_of = {COL_QKV: 0, COL_ZA: GDN_CONV_CH}
    after_gates = COL_QB + N_GATE_COLS
    src_of.update({COL_QB: after_gates, COL_KB: after_gates + SWA_Q, COL_VB: after_gates + SWA_Q + SWA_KV,
                   COL_ZB: after_gates + SWA_Q + 2 * SWA_KV, COL_GA: after_gates + 2 * SWA_Q + 2 * SWA_KV,
                   COL_GB: after_gates + 2 * SWA_Q + 2 * SWA_KV + D_MODEL})
    starts = sorted(src_of)
    src = []
    for dst in range(0, PROJ_COLS, REPACK_TILE):
        grp = max(s for s in starts if s <= dst)
        src.append(src_of[grp] + dst - grp)
    return np.asarray(src, np.int32)


def _repack_kernel(tab_ref, w_ref, g_ref, big_ref, small_ref):
    del tab_ref
    big_ref[...] = w_ref[...].astype(BF16)

    @pl.when(pl.program_id(0) == 0)
    def _():
        row = lax.broadcasted_iota(jnp.int32, (LANES, 1), 0)
        small_ref[...] = jnp.where(row < N_GATE_COLS, g_ref[...], 0.0).astype(BF16)


def _repack_w_in(w_t):
    d = w_t.shape[1]
    src = _repack_plan()
    gate_row = COL_QB
    assert gate_row % LANES == 0 and np.all(src % N_GATE_COLS == 0)
    return pl.pallas_call(
        _repack_kernel,
        out_shape=(jax.ShapeDtypeStruct((PROJ_COLS, d), BF16), jax.ShapeDtypeStruct((LANES, d), BF16)),
        grid_spec=pltpu.PrefetchScalarGridSpec(
            num_scalar_prefetch=1,
            grid=(PROJ_COLS // REPACK_TILE,),
            in_specs=[pl.BlockSpec((pl.Element(REPACK_TILE), pl.Element(d)),
                                   lambda o, tab: (tab[o] * N_GATE_COLS, 0)),
                      pl.BlockSpec((LANES, d), lambda o, tab: (gate_row // LANES, 0))],
            out_specs=(pl.BlockSpec((REPACK_TILE, d), lambda o, tab: (o, 0)),
                       pl.BlockSpec((LANES, d), lambda o, tab: (0, 0)))),
        compiler_params=_params(("arbitrary",)),
        name="repack_w_in",
    )(jnp.asarray(src // N_GATE_COLS), w_t, w_t)


def _lane_row(vec, offset):
    return jnp.pad(vec.astype(F32), (offset, LANES - offset - vec.shape[0]))[None, :]


def kernel(x, c, w_ada, b_ada, norm_gain, w_in, conv_w, a_log, dt_bias, gdn_norm_gain, q_norm_gain,
           k_norm_gain, sinks, rel_bias, w_branch_gdn, w_branch_swa, w_out):
    bsz, t, d = x.shape
    depth = w_in.shape[0]
    outs = []
    for b in range(bsz):
        xb = x[b]
        c_col = c[b].astype(F32)[:, None]
        for l in range(depth):
            mod = _ada_mod(c_col, w_ada[l], b_ada[l][None, :])
            shift, scale, gate = mod[:, :d], mod[:, d:2 * d], mod[:, 2 * d:]
            w_big, w_small = _repack_w_in(w_in[l].T)
            proj, ba = _in_proj(xb, norm_gain[l][None, :], scale, shift, w_big, w_small,
                                tm=min(1024, t), tn=1792)
            o_a = _gdn(proj, ba, conv_w[l], _lane_row(a_log[l], GDN_HEADS), _lane_row(dt_bias[l], GDN_HEADS),
                       gdn_norm_gain[l][None, :])
            bias = _swa_bias(rel_bias.T.astype(F32), sinks[l].astype(F32))
            o_b = _swa(proj, bias,
                       jnp.tile(q_norm_gain[l], LANES // SWA_HEAD_DIM)[None, :],
                       jnp.tile(k_norm_gain[l], LANES // SWA_HEAD_DIM)[None, :])
            xb = _merge_out(xb, o_a, o_b, proj, gate, w_branch_gdn[l].astype(BF16),
                            w_branch_swa[l].astype(BF16), w_out[l].astype(BF16), tm=min(512, t))
        outs.append(xb)
    return jnp.stack(outs, axis=0)
```

```python
import functools
import math

import jax
import jax.numpy as jnp
import numpy as np
from jax import lax
from jax.experimental import pallas as pl
from jax.experimental.pallas import tpu as pltpu

F32 = jnp.float32
BF16 = jnp.bfloat16

LANES = 128
D_MODEL = 2048
GDN_HEADS = 8
GDN_DK = 128
GDN_DV = 128
GDN_CONV = 4
GDN_QK = GDN_HEADS * GDN_DK
GDN_V = GDN_HEADS * GDN_DV
GDN_CONV_CH = 2 * GDN_QK + GDN_V
GDN_BLOCK = 128
GDN_STEP_BLOCKS = 2
GDN_CONV_PIECE = 512
SWA_HEADS = 16
SWA_KV_HEADS = 4
SWA_HEAD_DIM = 64
SWA_WINDOW = 128
SWA_BLOCK = 128
SWA_STEP_BLOCKS = 4
SWA_Q = SWA_HEADS * SWA_HEAD_DIM
SWA_KV = SWA_KV_HEADS * SWA_HEAD_DIM
REL_BUCKETS = 32
REL_MAX_DIST = 128
EPS = 1e-6
NEG_BIG = -1e30
LOG2E = math.log2(math.e)

COL_QKV = 0
COL_ZA = GDN_CONV_CH
COL_QB = COL_ZA + GDN_V
COL_ZB = COL_QB + SWA_Q
COL_GA = COL_ZB + SWA_Q
COL_GB = COL_GA + D_MODEL
COL_KB = COL_GB + D_MODEL
COL_VB = COL_KB + SWA_KV
COL_GATES = COL_VB + SWA_KV
PROJ_COLS = COL_GATES + LANES
for _col, _width in ((COL_ZA, GDN_V), (COL_QB, SWA_Q), (COL_ZB, SWA_Q), (COL_GA, D_MODEL),
                     (COL_GB, D_MODEL), (COL_KB, SWA_KV), (COL_VB, SWA_KV)):
    assert _col % _width == 0

VMEM_LIMIT = 56 * 1024 * 1024


def _sigmoid(x):
    return 0.5 + 0.5 * jnp.tanh(0.5 * x)


def _silu(x):
    half = 0.5 * x
    return half + half * jnp.tanh(half)


def _params(sem):
    return pltpu.CompilerParams(dimension_semantics=sem, vmem_limit_bytes=VMEM_LIMIT)


def _mm(a, b):
    return jnp.dot(a.astype(BF16), b.astype(BF16), preferred_element_type=F32)


def _mm_nt(a, b):
    return lax.dot_general(a.astype(BF16), b.astype(BF16), (((1,), (1,)), ((), ())),
                           preferred_element_type=F32)


def _mm_tn(a, b):
    return lax.dot_general(a.astype(BF16), b.astype(BF16), (((0,), (0,)), ((), ())),
                           preferred_element_type=F32)


def _ada_mod_kernel(c_ref, w_ref, b_ref, o_ref):
    c = c_ref[...]
    o_ref[...] = jnp.sum(_silu(c) * w_ref[...], axis=0, keepdims=True) + b_ref[...]


def _ada_mod(c_col, w_ada, b_ada):
    d, n = w_ada.shape
    tn = 1024
    return pl.pallas_call(
        _ada_mod_kernel,
        out_shape=jax.ShapeDtypeStruct((1, n), F32),
        grid=(n // tn,),
        in_specs=[pl.BlockSpec((d, 1), lambda j: (0, 0)),
                  pl.BlockSpec((d, tn), lambda j: (0, j)),
                  pl.BlockSpec((1, tn), lambda j: (0, j))],
        out_specs=pl.BlockSpec((1, tn), lambda j: (0, j)),
        compiler_params=_params(("arbitrary",)),
        name="ada_mod",
    )(c_col, w_ada, b_ada)


def _in_proj_kernel(x_ref, gain_ref, scale_ref, shift_ref, w_ref, o_ref, ba_ref, h_ref, *, row_chunk):
    j = pl.program_id(1)

    @pl.when(j == 0)
    def _():
        gs = gain_ref[...] * (1.0 + scale_ref[...])
        sh = shift_ref[...]
        tm = x_ref.shape[0]

        def body(r, carry):
            rows = pl.ds(pl.multiple_of(r * row_chunk, row_chunk), row_chunk)
            x = x_ref[rows, :]
            ms = jnp.mean(x * x, axis=-1, keepdims=True)
            h = (x * lax.rsqrt(ms + EPS)) * gs + sh
            h_ref[rows, :] = h.astype(BF16)
            return carry

        lax.fori_loop(0, tm // row_chunk, body, 0)

    res = _mm_nt(h_ref[...], w_ref[...])
    o_ref[...] = res.astype(o_ref.dtype)

    @pl.when(j == pl.num_programs(1) - 1)
    def _():
        ba_ref[...] = res[:, res.shape[1] - LANES:]


def _in_proj(x2d, gain, scale, shift, w_big, *, tm, tn):
    t, d = x2d.shape
    n = w_big.shape[0]
    assert n % tn == 0 and COL_GATES % tn == tn - LANES
    return pl.pallas_call(
        functools.partial(_in_proj_kernel, row_chunk=128),
        out_shape=(jax.ShapeDtypeStruct((t, n), BF16), jax.ShapeDtypeStruct((t, LANES), F32)),
        grid=(t // tm, n // tn),
        in_specs=[pl.BlockSpec((tm, d), lambda i, j: (i, 0)),
                  pl.BlockSpec((1, d), lambda i, j: (0, 0)),
                  pl.BlockSpec((1, d), lambda i, j: (0, 0)),
                  pl.BlockSpec((1, d), lambda i, j: (0, 0)),
                  pl.BlockSpec((tn, d), lambda i, j: (j, 0))],
        out_specs=(pl.BlockSpec((tm, tn), lambda i, j: (i, j)),
                   pl.BlockSpec((tm, LANES), lambda i, j: (i, 0))),
        scratch_shapes=[pltpu.VMEM((tm, d), BF16)],
        compiler_params=_params(("arbitrary", "arbitrary")),
        name="in_proj",
    )(x2d, gain, scale, shift, w_big)


def _swa_bias_kernel(tab_ref, sink_ref, o_ref):
    q = SWA_BLOCK
    qpos = lax.broadcasted_iota(jnp.int32, (q, 2 * q), 0) + q
    kpos = lax.broadcasted_iota(jnp.int32, (q, 2 * q), 1)
    dist = qpos - kpos
    in_window = (dist >= 0) & (dist < SWA_WINDOW)
    d = jnp.maximum(dist, 0)
    max_exact = REL_BUCKETS // 2
    df = jnp.maximum(d, 1).astype(F32)
    large = max_exact + (jnp.log(df / max_exact) / math.log(REL_MAX_DIST / max_exact)
                         * (REL_BUCKETS - max_exact)).astype(jnp.int32)
    large = jnp.minimum(large, REL_BUCKETS - 1)
    bucket = jnp.where(in_window, jnp.where(d < max_exact, d, large), REL_BUCKETS)
    for h in range(SWA_HEADS):
        acc = jnp.full((q, 2 * q), NEG_BIG, F32)
        for b in range(REL_BUCKETS):
            acc = jnp.where(bucket == b, tab_ref[h, b], acc)
        o_ref[h] = (jnp.where(kpos == 0, sink_ref[h], acc) * LOG2E).astype(o_ref.dtype)


def _swa_bias(rel_bias_t, sinks):
    q = SWA_BLOCK
    return pl.pallas_call(
        _swa_bias_kernel,
        out_shape=jax.ShapeDtypeStruct((SWA_HEADS, q, 2 * q), BF16),
        in_specs=[pl.BlockSpec(memory_space=pltpu.SMEM), pl.BlockSpec(memory_space=pltpu.SMEM)],
        out_specs=pl.BlockSpec(memory_space=pltpu.VMEM),
        compiler_params=pltpu.CompilerParams(vmem_limit_bytes=VMEM_LIMIT),
        name="swa_bias",
    )(rel_bias_t, sinks)


def _chunk_cumsum_rows(x):
    n = x.shape[0]
    row = lax.broadcasted_iota(jnp.int32, x.shape, 0)
    s = 1
    while s < n:
        x = x + jnp.where(row >= s, pltpu.roll(x, s, axis=0), 0.0)
        s *= 2
    return x


def _unit_lower_inverses(l_mats, masks_ref):
    eye = masks_ref[0].astype(F32)
    l_bf = [l.astype(BF16) for l in l_mats]
    m0 = [l * masks_ref[1] for l in l_bf]
    x = [eye - m for m in m0]
    p = [_mm(m, m) for m in m0]
    x = [xi + _mm(xi, pi) for xi, pi in zip(x, p)]
    p = [_mm(pi, pi) for pi in p]
    x = [xi + _mm(xi, pi) for xi, pi in zip(x, p)]
    for lvl in range(2, masks_ref.shape[0]):
        nx = [_mm(l * masks_ref[lvl], xi) for l, xi in zip(l_bf, x)]
        x = [xi - _mm(xi, ni) for xi, ni in zip(x, nx)]
    return x


def _gdn_kernel(cur_ref, prev_ref, z_ref, ba_ref, convw_ref, band_ref, alog_ref, dtb_ref, gain_ref,
                masks_ref, o_ref, s_ref, q_sc, k_sc, kb_sc, vb_sc, kbg_sc, qd_sc, kd_sc, dec_sc, el_sc):
    s = pl.program_id(0)
    c = GDN_BLOCK
    heads = range(GDN_HEADS)
    blocks = range(GDN_STEP_BLOCKS)
    items = [(b, h) for b in blocks for h in heads]
    at = lambda b, h: b * GDN_HEADS + h
    lane_of = lambda a, h: a[:, GDN_HEADS + h:GDN_HEADS + h + 1]

    @pl.when(s == 0)
    def _():
        for ref in (s_ref, q_sc, k_sc, kb_sc, vb_sc, kbg_sc, qd_sc, kd_sc, dec_sc, el_sc):
            ref[...] = jnp.zeros_like(ref)

    pad = prev_ref.shape[0]
    conv_w = convw_ref[...].astype(BF16)
    taps = []
    for b in blocks:
        if b == 0:
            ctx = jnp.where(s > 0, prev_ref[...], jnp.zeros_like(prev_ref))
        else:
            ctx = cur_ref[b * c - pad:b * c, :]
        xcat = jnp.concatenate([ctx, cur_ref[b * c:(b + 1) * c, :]], axis=0)
        taps.append(jnp.concatenate([xcat * conv_w[i:i + 1, :] for i in range(GDN_CONV)], axis=0))

    row = lax.broadcasted_iota(jnp.int32, (c, c), 0)
    col = lax.broadcasted_iota(jnp.int32, (c, c), 1)
    strict = row > col
    kk = [_mm_nt(kb_sc[at(b, h)], k_sc[at(b, h)]) for b, h in items]
    qk = [_mm_nt(q_sc[at(b, h)], k_sc[at(b, h)]) for b, h in items]
    conv_pieces = {}
    for p in range(GDN_CONV_CH // GDN_CONV_PIECE):
        for b in blocks:
            cols = slice(p * GDN_CONV_PIECE, (p + 1) * GDN_CONV_PIECE)
            conv_pieces[b, p] = jnp.dot(band_ref[...], taps[b][:, cols], preferred_element_type=F32)
    l_mat = [jnp.where(strict, kk[at(b, h)] * dec_sc[at(b, h)], 0.0) for b, h in items]
    attn = [(qk[at(b, h)] * dec_sc[at(b, h)]).astype(BF16) for b, h in items]
    t_inv = _unit_lower_inverses(l_mat, masks_ref)
    state = [s_ref[h] for h in heads]
    for b in blocks:
        el = el_sc[b]
        s_bf = [state[h].astype(BF16) for h in heads]
        resid = [vb_sc[at(b, h)] - _mm(kbg_sc[at(b, h)], s_bf[h]) for h in heads]
        v_new = [_mm(t_inv[at(b, h)], resid[h]).astype(BF16) for h in heads]
        o = [_mm(jnp.concatenate([qd_sc[at(b, h)], attn[at(b, h)]], axis=1),
                 jnp.concatenate([s_bf[h], v_new[h]], axis=0)) for h in heads]
        state = [state[h] * lane_of(el, h) + _mm_tn(kd_sc[at(b, h)], v_new[h]) for h in heads]
        for h in heads:
            on = o[h] * lax.rsqrt(jnp.mean(o[h] * o[h], axis=-1, keepdims=True) + EPS) * gain_ref[...]
            z = z_ref[b * c:(b + 1) * c, h * GDN_DV:(h + 1) * GDN_DV].astype(F32)
            o_ref[b * c:(b + 1) * c, h * GDN_DV:(h + 1) * GDN_DV] = (on * _silu(z)).astype(o_ref.dtype)
    for h in heads:
        s_ref[h] = state[h]

    causal = row >= col

    def l2n(a, scale=1.0):
        return a * (lax.rsqrt(jnp.sum(a * a, axis=-1, keepdims=True) + EPS) * scale)

    for b in blocks:
        ba = ba_ref[b * c:(b + 1) * c, :]
        beta_all = _sigmoid(ba)
        xg = ba + dtb_ref[...]
        softplus = jnp.maximum(xg, 0.0) + jnp.log(1.0 + jnp.exp(-jnp.abs(xg)))
        g_all = -jnp.exp(alog_ref[...]) * softplus
        gc = _chunk_cumsum_rows(g_all)
        gc_t = gc.T
        g_last = gc[c - 1:c, :]
        eg_all = jnp.exp(gc)
        ekd_all = jnp.exp(g_last - gc)
        conv_tile = lambda col0: _silu(conv_pieces[b, col0 // GDN_CONV_PIECE][
            :, col0 % GDN_CONV_PIECE:col0 % GDN_CONV_PIECE + LANES])
        for h in heads:
            q = l2n(conv_tile(h * GDN_DK), GDN_DK ** -0.5)
            k = l2n(conv_tile(GDN_QK + h * GDN_DK))
            v = conv_tile(2 * GDN_QK + h * GDN_DV)
            beta = beta_all[:, h:h + 1]
            eg = lane_of(eg_all, h)
            kb = k * beta
            gdiff = lane_of(gc, h) - gc_t[GDN_HEADS + h:GDN_HEADS + h + 1, :]
            q_sc[at(b, h)] = q.astype(BF16)
            k_sc[at(b, h)] = k.astype(BF16)
            kb_sc[at(b, h)] = kb.astype(BF16)
            vb_sc[at(b, h)] = v * beta
            kbg_sc[at(b, h)] = (kb * eg).astype(BF16)
            qd_sc[at(b, h)] = (q * eg).astype(BF16)
            kd_sc[at(b, h)] = (k * lane_of(ekd_all, h)).astype(BF16)
            dec_sc[at(b, h)] = jnp.where(causal, jnp.exp(jnp.where(causal, gdiff, 0.0)), 0.0)
        el_sc[b] = jnp.exp(g_last)


def _gdn_masks():
    c = GDN_BLOCK
    r = np.arange(c)[:, None]
    k = np.arange(c)[None, :]
    mats = [np.eye(c), (r // 8 == k // 8) & (r > k)]
    b = 8
    while b < c:
        mats.append((r // (2 * b) == k // (2 * b)) & ((r // b) % 2 == 1) & ((k // b) % 2 == 0))
        b *= 2
    return jnp.asarray(np.stack([np.asarray(m, np.float32) for m in mats]), BF16)


def _conv_band(c, pad):
    band = np.zeros((c, GDN_CONV * (pad + c)), np.float32)
    for i in range(GDN_CONV):
        band[np.arange(c), i * (pad + c) + pad + np.arange(c) - (GDN_CONV - 1) + i] = 1.0
    return jnp.asarray(band, BF16)


def _gdn(proj, ba, conv_w, alog_lane, dtb_lane, gain):
    t = proj.shape[0]
    c = GDN_BLOCK
    rows = GDN_STEP_BLOCKS * c
    pad = 16
    masks = _gdn_masks()
    nm = masks.shape[0]
    band = _conv_band(c, pad)
    n_steps = t // rows
    staged = lambda i: jnp.minimum(i, n_steps - 1)
    chained = lambda i: jnp.maximum(i - 1, 0)
    item_sq = lambda w, dt: pltpu.VMEM((GDN_STEP_BLOCKS * GDN_HEADS, c, w), dt)
    return pl.pallas_call(
        _gdn_kernel,
        out_shape=jax.ShapeDtypeStruct((t, GDN_V), BF16),
        grid=(n_steps + 1,),
        in_specs=[pl.BlockSpec((rows, GDN_CONV_CH), lambda i: (staged(i), 0)),
                  pl.BlockSpec((pad, GDN_CONV_CH),
                               lambda i: (jnp.maximum(staged(i) * (rows // pad) - 1, 0), 0)),
                  pl.BlockSpec((rows, GDN_V), lambda i: (chained(i), COL_ZA // GDN_V)),
                  pl.BlockSpec((rows, LANES), lambda i: (staged(i), 0)),
                  pl.BlockSpec((GDN_CONV, GDN_CONV_CH), lambda i: (0, 0)),
                  pl.BlockSpec(band.shape, lambda i: (0, 0)),
                  pl.BlockSpec((1, LANES), lambda i: (0, 0)),
                  pl.BlockSpec((1, LANES), lambda i: (0, 0)),
                  pl.BlockSpec((1, GDN_DV), lambda i: (0, 0)),
                  pl.BlockSpec((nm, c, c), lambda i: (0, 0, 0))],
        out_specs=pl.BlockSpec((rows, GDN_V), lambda i: (chained(i), 0)),
        scratch_shapes=[pltpu.VMEM((GDN_HEADS, GDN_DK, GDN_DV), F32),
                        item_sq(GDN_DK, BF16), item_sq(GDN_DK, BF16), item_sq(GDN_DK, BF16),
                        item_sq(GDN_DV, F32), item_sq(GDN_DK, BF16),
                        item_sq(GDN_DK, BF16), item_sq(GDN_DK, BF16),
                        item_sq(c, F32),
                        pltpu.VMEM((GDN_STEP_BLOCKS, 1, LANES), F32)],
        compiler_params=_params(("arbitrary",)),
        name="gdn",
    )(proj, proj, proj, ba, conv_w, band, alog_lane, dtb_lane, gain, masks)


def _swa_kernel(q_ref, kc_ref, kp_ref, vc_ref, vp_ref, z_ref, bias_ref, qg_ref, kg_ref, seg_ref, o_ref):
    n = pl.program_id(0)
    qb = SWA_BLOCK
    hd = SWA_HEAD_DIM
    n_tiles = SWA_Q // LANES
    blocks = range(SWA_STEP_BLOCKS)
    items = [(j, tq) for j in blocks for tq in range(n_tiles)]
    lane = lax.broadcasted_iota(jnp.int32, (1, LANES), 1)
    lo = lane < hd
    seg = seg_ref[...].astype(BF16)

    def qk_norm(a, gain):
        ms = jnp.dot((a * a).astype(BF16), seg, preferred_element_type=F32)
        return a * (lax.rsqrt(ms + EPS) * gain)

    def two_heads(lo_part, hi_part):
        return jnp.concatenate([jnp.where(lo, lo_part, 0.0), jnp.where(lo, 0.0, hi_part)], axis=0)

    kall = jnp.concatenate([kp_ref[...], kc_ref[...]], axis=0).astype(F32)
    vall = jnp.concatenate([vp_ref[...], vc_ref[...]], axis=0).astype(F32)
    kn, vv = {}, {}
    for u in range(SWA_KV // LANES):
        knorm = qk_norm(kall[:, u * LANES:(u + 1) * LANES], kg_ref[...])
        vtile = vall[:, u * LANES:(u + 1) * LANES]
        for ci in range(SWA_STEP_BLOCKS + 1):
            kc = knorm[ci * qb:(ci + 1) * qb]
            vc = vtile[ci * qb:(ci + 1) * qb]
            kn[u, ci] = (kc, pltpu.roll(kc, hd, axis=1))
            vv[u, ci] = (vc, pltpu.roll(vc, hd, axis=1))

    row = lax.broadcasted_iota(jnp.int32, (qb, 1), 0)
    is_sink = row == 0
    k2s, rhs = {}, {}
    for j in blocks:
        prev_live = (n * SWA_STEP_BLOCKS + j) > 0
        ones_prev = jnp.where(prev_live | is_sink, 1.0, 0.0)
        ones_band = jnp.concatenate([ones_prev, jnp.ones_like(ones_prev)], axis=0)
        ones2 = two_heads(ones_band, ones_band)
        for u in range(SWA_KV // LANES):
            kband = [jnp.concatenate([jnp.where(is_sink, 0.0, kn[u, j][r]), kn[u, j + 1][r]], axis=0)
                     for r in range(2)]
            vband = [jnp.concatenate([jnp.where(prev_live & ~is_sink, vv[u, j][r], 0.0), vv[u, j + 1][r]],
                                     axis=0) for r in range(2)]
            for half in range(2):
                g = 2 * u + half
                k2s[j, g] = two_heads(kband[half], kband[1 - half]).astype(BF16)
                v2 = two_heads(vband[half], vband[1 - half])
                rhs[j, g] = jnp.concatenate([v2, ones2], axis=1).astype(BF16)

    q_gain = qg_ref[...] * (hd ** -0.5 * LOG2E)
    kv_of = lambda tq: (2 * tq) // (SWA_HEADS // SWA_KV_HEADS)
    qt = {(j, tq): qk_norm(q_ref[j * qb:(j + 1) * qb, tq * LANES:(tq + 1) * LANES].astype(F32), q_gain)
          for j, tq in items}
    logits = {(j, tq): _mm_nt(qt[j, tq], k2s[j, kv_of(tq)]) for j, tq in items}
    probs = {}
    for j, tq in items:
        ps = []
        for e in range(2):
            s = logits[j, tq][:, e * 2 * qb:(e + 1) * 2 * qb].astype(BF16) + bias_ref[2 * tq + e]
            ps.append(jnp.exp2(s - jnp.max(s, axis=-1, keepdims=True)))
        probs[j, tq] = jnp.concatenate(ps, axis=1)
    pv = {(j, tq): jnp.dot(probs[j, tq], rhs[j, kv_of(tq)], preferred_element_type=F32) for j, tq in items}
    for j, tq in items:
        out = pv[j, tq][:, :LANES] * (1.0 / pv[j, tq][:, LANES:])
        z = z_ref[j * qb:(j + 1) * qb, tq * LANES:(tq + 1) * LANES].astype(F32)
        o_ref[j * qb:(j + 1) * qb, tq * LANES:(tq + 1) * LANES] = (out * _silu(z)).astype(o_ref.dtype)


def _swa(proj, bias, qg2, kg2):
    t = proj.shape[0]
    qb = SWA_BLOCK
    rows = SWA_STEP_BLOCKS * qb
    seg = np.kron(np.eye(LANES // SWA_HEAD_DIM), np.ones((SWA_HEAD_DIM, SWA_HEAD_DIM))) / SWA_HEAD_DIM
    seg = jnp.asarray(seg, F32)
    kcol = COL_KB // SWA_KV
    vcol = COL_VB // SWA_KV
    prev = lambda i: jnp.maximum(i * SWA_STEP_BLOCKS - 1, 0)
    return pl.pallas_call(
        _swa_kernel,
        out_shape=jax.ShapeDtypeStruct((t, SWA_Q), BF16),
        grid=(t // rows,),
        in_specs=[pl.BlockSpec((rows, SWA_Q), lambda i: (i, COL_QB // SWA_Q)),
                  pl.BlockSpec((rows, SWA_KV), lambda i: (i, kcol)),
                  pl.BlockSpec((qb, SWA_KV), lambda i: (prev(i), kcol)),
                  pl.BlockSpec((rows, SWA_KV), lambda i: (i, vcol)),
                  pl.BlockSpec((qb, SWA_KV), lambda i: (prev(i), vcol)),
                  pl.BlockSpec((rows, SWA_Q), lambda i: (i, COL_ZB // SWA_Q)),
                  pl.BlockSpec((SWA_HEADS, qb, 2 * qb), lambda i: (0, 0, 0)),
                  pl.BlockSpec((1, LANES), lambda i: (0, 0)),
                  pl.BlockSpec((1, LANES), lambda i: (0, 0)),
                  pl.BlockSpec((LANES, LANES), lambda i: (0, 0))],
        out_specs=pl.BlockSpec((rows, SWA_Q), lambda i: (i, 0)),
        compiler_params=_params(("arbitrary",)),
        name="swa",
    )(proj, proj, proj, proj, proj, proj, bias, qg2, kg2, seg)


def _merge_out_kernel(x_ref, oa_ref, ob_ref, ga_ref, gb_ref, gate_ref, wa_ref, wb_ref, wo_ref, o_ref):
    ya = jnp.dot(oa_ref[...], wa_ref[...], preferred_element_type=F32)
    yb = jnp.dot(ob_ref[...], wb_ref[...], preferred_element_type=F32)
    mixed = _sigmoid(ga_ref[...].astype(F32)) * ya + _sigmoid(gb_ref[...].astype(F32)) * yb
    y = jnp.dot(mixed.astype(BF16), wo_ref[...], preferred_element_type=F32)
    o_ref[...] = x_ref[...] + gate_ref[...] * y


def _merge_out(x2d, o_a, o_b, proj, gate, w_a, w_b, w_o, *, tm):
    t, d = x2d.shape
    const = lambda shape: pl.BlockSpec(shape, lambda i: (0, 0), pipeline_mode=pl.Buffered(1))
    return pl.pallas_call(
        _merge_out_kernel,
        out_shape=jax.ShapeDtypeStruct((t, d), F32),
        grid=(t // tm,),
        in_specs=[pl.BlockSpec((tm, d), lambda i: (i, 0)),
                  pl.BlockSpec((tm, GDN_V), lambda i: (i, 0)),
                  pl.BlockSpec((tm, SWA_Q), lambda i: (i, 0)),
                  pl.BlockSpec((tm, d), lambda i: (i, COL_GA // D_MODEL)),
                  pl.BlockSpec((tm, d), lambda i: (i, COL_GB // D_MODEL)),
                  pl.BlockSpec((1, d), lambda i: (0, 0)),
                  const((GDN_V, d)), const((SWA_Q, d)), const((d, d))],
        out_specs=pl.BlockSpec((tm, d), lambda i: (i, 0)),
        compiler_params=_params(("arbitrary",)),
        name="merge_out",
    )(x2d, o_a, o_b, proj, proj, gate, w_a, w_b, w_o)


REPACK_TILE = 512
N_GATE_COLS = 2 * GDN_HEADS


def _repack_plan():
    src_of = {COL_QKV: 0, COL_ZA: GDN_CONV_CH}
    after_gates = COL_QB + N_GATE_COLS
    src_of.update({COL_QB: after_gates, COL_KB: after_gates + SWA_Q, COL_VB: after_gates + SWA_Q + SWA_KV,
                   COL_ZB: after_gates + SWA_Q + 2 * SWA_KV, COL_GA: after_gates + 2 * SWA_Q + 2 * SWA_KV,
                   COL_GB: after_gates + 2 * SWA_Q + 2 * SWA_KV + D_MODEL})
    starts = sorted(src_of)
    src = []
    for dst in range(0, COL_GATES, REPACK_TILE):
        grp = max(s for s in starts if s <= dst)
        src.append(src_of[grp] + dst - grp)
    src.append(COL_QB)
    return np.asarray(src, np.int32)


def _repack_kernel(tab_ref, w_ref, big_ref):
    del tab_ref
    is_gate_tile = pl.program_id(0) == pl.num_programs(0) - 1

    @pl.when(jnp.logical_not(is_gate_tile))
    def _():
        big_ref[...] = w_ref[...].astype(BF16)

    @pl.when(is_gate_tile)
    def _():
        row = lax.broadcasted_iota(jnp.int32, (REPACK_TILE, 1), 0)
        big_ref[...] = jnp.where(row < N_GATE_COLS, w_ref[...], 0.0).astype(BF16)


def _repack_w_in(w_t):
    d = w_t.shape[1]
    src = _repack_plan()
    assert COL_GATES % REPACK_TILE == 0 and np.all(src % N_GATE_COLS == 0)
    return pl.pallas_call(
        _repack_kernel,
        out_shape=jax.ShapeDtypeStruct((PROJ_COLS, d), BF16),
        grid_spec=pltpu.PrefetchScalarGridSpec(
            num_scalar_prefetch=1,
            grid=(len(src),),
            in_specs=[pl.BlockSpec((pl.Element(REPACK_TILE), pl.Element(d)),
                                   lambda o, tab: (tab[o] * N_GATE_COLS, 0))],
            out_specs=pl.BlockSpec((REPACK_TILE, d), lambda o, tab: (o, 0))),
        compiler_params=_params(("arbitrary",)),
        name="repack_w_in",
    )(jnp.asarray(src // N_GATE_COLS), w_t)


def _lane_row(vec, offset):
    return jnp.pad(vec.astype(F32), (offset, LANES - offset - vec.shape[0]))[None, :]


def kernel(x, c, w_ada, b_ada, norm_gain, w_in, conv_w, a_log, dt_bias, gdn_norm_gain, q_norm_gain,
           k_norm_gain, sinks, rel_bias, w_branch_gdn, w_branch_swa, w_out):
    bsz, t, d = x.shape
    depth = w_in.shape[0]
    outs = []
    for b in range(bsz):
        xb = x[b]
        c_col = c[b].astype(F32)[:, None]
        for l in range(depth):
            mod = _ada_mod(c_col, w_ada[l], b_ada[l][None, :])
            shift, scale, gate = mod[:, :d], mod[:, d:2 * d], mod[:, 2 * d:]
            w_big = _repack_w_in(w_in[l].T)
            proj, ba = _in_proj(xb, norm_gain[l][None, :], scale, shift, w_big, tm=min(1024, t), tn=2176)
            o_a = _gdn(proj, ba, conv_w[l], _lane_row(a_log[l], GDN_HEADS), _lane_row(dt_bias[l], GDN_HEADS),
                       gdn_norm_gain[l][None, :])
            bias = _swa_bias(rel_bias.T.astype(F32), sinks[l].astype(F32))
            o_b = _swa(proj, bias,
                       jnp.tile(q_norm_gain[l], LANES // SWA_HEAD_DIM)[None, :],
                       jnp.tile(k_norm_gain[l], LANES // SWA_HEAD_DIM)[None, :])
            xb = _merge_out(xb, o_a, o_b, proj, gate, w_branch_gdn[l].astype(BF16),
                            w_branch_swa[l].astype(BF16), w_out[l].astype(BF16), tm=min(512, t))
        outs.append(xb)
    return jnp.stack(outs, axis=0)
```

```python
import functools
import math

import jax
import jax.numpy as jnp
import numpy as np
from jax import lax
from jax.experimental import pallas as pl
from jax.experimental.pallas import tpu as pltpu

F32 = jnp.float32
BF16 = jnp.bfloat16

LANES = 128
D_MODEL = 2048
GDN_HEADS = 8
GDN_DK = 128
GDN_DV = 128
GDN_CONV = 4
GDN_QK = GDN_HEADS * GDN_DK
GDN_V = GDN_HEADS * GDN_DV
GDN_CONV_CH = 2 * GDN_QK + GDN_V
GDN_BLOCK = 128
GDN_STEP_BLOCKS = 2
GDN_CONV_PIECE = 512
SWA_HEADS = 16
SWA_KV_HEADS = 4
SWA_HEAD_DIM = 64
SWA_WINDOW = 128
SWA_BLOCK = 128
SWA_STEP_BLOCKS = 4
SWA_Q = SWA_HEADS * SWA_HEAD_DIM
SWA_KV = SWA_KV_HEADS * SWA_HEAD_DIM
REL_BUCKETS = 32
REL_MAX_DIST = 128
EPS = 1e-6
NEG_BIG = -1e30
LOG2E = math.log2(math.e)

COL_QKV = 0
COL_ZA = GDN_CONV_CH
COL_QB = COL_ZA + GDN_V
COL_ZB = COL_QB + SWA_Q
COL_GA = COL_ZB + SWA_Q
COL_GB = COL_GA + D_MODEL
COL_KB = COL_GB + D_MODEL
COL_VB = COL_KB + SWA_KV
PROJ_COLS = COL_VB + SWA_KV
for _col, _width in ((COL_ZA, GDN_V), (COL_QB, SWA_Q), (COL_ZB, SWA_Q), (COL_GA, D_MODEL),
                     (COL_GB, D_MODEL), (COL_KB, SWA_KV), (COL_VB, SWA_KV)):
    assert _col % _width == 0

VMEM_LIMIT = 56 * 1024 * 1024


def _sigmoid(x):
    return 0.5 + 0.5 * jnp.tanh(0.5 * x)


def _silu(x):
    half = 0.5 * x
    return half + half * jnp.tanh(half)


def _params(sem):
    return pltpu.CompilerParams(dimension_semantics=sem, vmem_limit_bytes=VMEM_LIMIT)


def _mm(a, b):
    return jnp.dot(a.astype(BF16), b.astype(BF16), preferred_element_type=F32)


def _mm_nt(a, b):
    return lax.dot_general(a.astype(BF16), b.astype(BF16), (((1,), (1,)), ((), ())),
                           preferred_element_type=F32)


def _mm_tn(a, b):
    return lax.dot_general(a.astype(BF16), b.astype(BF16), (((0,), (0,)), ((), ())),
                           preferred_element_type=F32)


def _ada_mod_kernel(c_ref, w_ref, b_ref, o_ref):
    c = c_ref[...]
    o_ref[...] = jnp.sum(_silu(c) * w_ref[...], axis=0, keepdims=True) + b_ref[...]


def _ada_mod(c_col, w_ada, b_ada):
    d, n = w_ada.shape
    tn = 1024
    return pl.pallas_call(
        _ada_mod_kernel,
        out_shape=jax.ShapeDtypeStruct((1, n), F32),
        grid=(n // tn,),
        in_specs=[pl.BlockSpec((d, 1), lambda j: (0, 0)),
                  pl.BlockSpec((d, tn), lambda j: (0, j)),
                  pl.BlockSpec((1, tn), lambda j: (0, j))],
        out_specs=pl.BlockSpec((1, tn), lambda j: (0, j)),
        compiler_params=_params(("arbitrary",)),
        name="ada_mod",
    )(c_col, w_ada, b_ada)


def _in_proj_kernel(x_ref, gain_ref, scale_ref, shift_ref, w_ref, ws_ref, *rest, row_chunk, n_side):
    side_in, (o_ref, ba_ref), side_out, h_ref = (rest[:n_side], rest[n_side:n_side + 2],
                                                 rest[n_side + 2:-1], rest[-1])
    for src, dst in zip(side_in, side_out):
        dst[...] = src[...].astype(dst.dtype)
    j = pl.program_id(1)

    @pl.when(j == 0)
    def _():
        gs = gain_ref[...] * (1.0 + scale_ref[...])
        sh = shift_ref[...]
        tm = x_ref.shape[0]

        def body(r, carry):
            rows = pl.ds(pl.multiple_of(r * row_chunk, row_chunk), row_chunk)
            x = x_ref[rows, :]
            ms = jnp.mean(x * x, axis=-1, keepdims=True)
            h = (x * lax.rsqrt(ms + EPS)) * gs + sh
            h_ref[rows, :] = h.astype(BF16)
            return carry

        lax.fori_loop(0, tm // row_chunk, body, 0)
        ba_ref[...] = _mm_nt(h_ref[...], ws_ref[...])

    o_ref[...] = _mm_nt(h_ref[...], w_ref[...]).astype(o_ref.dtype)


def _in_proj(x2d, gain, scale, shift, w_big, w_small, side_weights, *, tm, tn):
    t, d = x2d.shape
    n = w_big.shape[0]
    n_i, n_j = t // tm, n // tn
    slabs = 1 << ((n_i * n_j).bit_length() - 1)
    slab_of = lambda i, j: jnp.minimum(i * n_j + j, slabs - 1)
    side_specs = [pl.BlockSpec((w.shape[0] // slabs, w.shape[1]), lambda i, j: (slab_of(i, j), 0))
                  for w in side_weights]
    assert all(w.shape[0] % (16 * slabs) == 0 for w in side_weights)
    out = pl.pallas_call(
        functools.partial(_in_proj_kernel, row_chunk=128, n_side=len(side_weights)),
        out_shape=(jax.ShapeDtypeStruct((t, n), BF16), jax.ShapeDtypeStruct((t, LANES), F32),
                   *[jax.ShapeDtypeStruct(w.shape, BF16) for w in side_weights]),
        grid=(n_i, n_j),
        in_specs=[pl.BlockSpec((tm, d), lambda i, j: (i, 0)),
                  pl.BlockSpec((1, d), lambda i, j: (0, 0)),
                  pl.BlockSpec((1, d), lambda i, j: (0, 0)),
                  pl.BlockSpec((1, d), lambda i, j: (0, 0)),
                  pl.BlockSpec((tn, d), lambda i, j: (j, 0)),
                  pl.BlockSpec((LANES, d), lambda i, j: (0, 0)),
                  *side_specs],
        out_specs=(pl.BlockSpec((tm, tn), lambda i, j: (i, j)),
                   pl.BlockSpec((tm, LANES), lambda i, j: (i, 0)),
                   *side_specs),
        scratch_shapes=[pltpu.VMEM((tm, d), BF16)],
        compiler_params=_params(("arbitrary", "arbitrary")),
        name="in_proj",
    )(x2d, gain, scale, shift, w_big, w_small, *side_weights)
    return out[0], out[1], out[2:]


def _swa_bias_kernel(tab_ref, sink_ref, o_ref):
    q = SWA_BLOCK
    qpos = lax.broadcasted_iota(jnp.int32, (q, 2 * q), 0) + q
    kpos = lax.broadcasted_iota(jnp.int32, (q, 2 * q), 1)
    dist = qpos - kpos
    in_window = (dist >= 0) & (dist < SWA_WINDOW)
    d = jnp.maximum(dist, 0)
    max_exact = REL_BUCKETS // 2
    df = jnp.maximum(d, 1).astype(F32)
    large = max_exact + (jnp.log(df / max_exact) / math.log(REL_MAX_DIST / max_exact)
                         * (REL_BUCKETS - max_exact)).astype(jnp.int32)
    large = jnp.minimum(large, REL_BUCKETS - 1)
    bucket = jnp.where(in_window, jnp.where(d < max_exact, d, large), REL_BUCKETS)
    for h in range(SWA_HEADS):
        acc = jnp.full((q, 2 * q), NEG_BIG, F32)
        for b in range(REL_BUCKETS):
            acc = jnp.where(bucket == b, tab_ref[h, b], acc)
        o_ref[h] = (jnp.where(kpos == 0, sink_ref[h], acc) * LOG2E).astype(o_ref.dtype)


def _swa_bias(rel_bias_t, sinks):
    q = SWA_BLOCK
    return pl.pallas_call(
        _swa_bias_kernel,
        out_shape=jax.ShapeDtypeStruct((SWA_HEADS, q, 2 * q), BF16),
        in_specs=[pl.BlockSpec(memory_space=pltpu.SMEM), pl.BlockSpec(memory_space=pltpu.SMEM)],
        out_specs=pl.BlockSpec(memory_space=pltpu.VMEM),
        compiler_params=pltpu.CompilerParams(vmem_limit_bytes=VMEM_LIMIT),
        name="swa_bias",
    )(rel_bias_t, sinks)


def _chunk_cumsum_rows(x):
    n = x.shape[0]
    row = lax.broadcasted_iota(jnp.int32, x.shape, 0)
    s = 1
    while s < n:
        x = x + jnp.where(row >= s, pltpu.roll(x, s, axis=0), 0.0)
        s *= 2
    return x


def _unit_lower_inverses(l_mats, masks_ref):
    eye = masks_ref[0].astype(F32)
    l_bf = [l.astype(BF16) for l in l_mats]
    m0 = [l * masks_ref[1] for l in l_bf]
    x = [eye - m for m in m0]
    p = [_mm(m, m) for m in m0]
    x = [xi + _mm(xi, pi) for xi, pi in zip(x, p)]
    p = [_mm(pi, pi) for pi in p]
    x = [xi + _mm(xi, pi) for xi, pi in zip(x, p)]
    for lvl in range(2, masks_ref.shape[0]):
        nx = [_mm(l * masks_ref[lvl], xi) for l, xi in zip(l_bf, x)]
        x = [xi - _mm(xi, ni) for xi, ni in zip(x, nx)]
    return x


def _gdn_kernel(cur_ref, prev_ref, z_ref, ba_ref, convw_ref, band_ref, alog_ref, dtb_ref, gain_ref,
                masks_ref, o_ref, s_ref, q_sc, k_sc, kb_sc, vb_sc, kbg_sc, qd_sc, kd_sc, dec_sc, el_sc):
    s = pl.program_id(0)
    c = GDN_BLOCK
    heads = range(GDN_HEADS)
    blocks = range(GDN_STEP_BLOCKS)
    items = [(b, h) for b in blocks for h in heads]
    at = lambda b, h: b * GDN_HEADS + h
    lane_of = lambda a, h: a[:, GDN_HEADS + h:GDN_HEADS + h + 1]

    @pl.when(s == 0)
    def _():
        for ref in (s_ref, q_sc, k_sc, kb_sc, vb_sc, kbg_sc, qd_sc, kd_sc, dec_sc, el_sc):
            ref[...] = jnp.zeros_like(ref)

    pad = prev_ref.shape[0]
    conv_w = convw_ref[...].astype(BF16)
    taps = []
    for b in blocks:
        if b == 0:
            ctx = jnp.where(s > 0, prev_ref[...], jnp.zeros_like(prev_ref))
        else:
            ctx = cur_ref[b * c - pad:b * c, :]
        xcat = jnp.concatenate([ctx, cur_ref[b * c:(b + 1) * c, :]], axis=0)
        taps.append(jnp.concatenate([xcat * conv_w[i:i + 1, :] for i in range(GDN_CONV)], axis=0))

    row = lax.broadcasted_iota(jnp.int32, (c, c), 0)
    col = lax.broadcasted_iota(jnp.int32, (c, c), 1)
    strict = row > col
    kk = [_mm_nt(kb_sc[at(b, h)], k_sc[at(b, h)]) for b, h in items]
    qk = [_mm_nt(q_sc[at(b, h)], k_sc[at(b, h)]) for b, h in items]
    conv_pieces = {}
    for p in range(GDN_CONV_CH // GDN_CONV_PIECE):
        for b in blocks:
            cols = slice(p * GDN_CONV_PIECE, (p + 1) * GDN_CONV_PIECE)
            conv_pieces[b, p] = jnp.dot(band_ref[...], taps[b][:, cols], preferred_element_type=F32)
    l_mat = [jnp.where(strict, kk[at(b, h)] * dec_sc[at(b, h)], 0.0) for b, h in items]
    attn = [(qk[at(b, h)] * dec_sc[at(b, h)]).astype(BF16) for b, h in items]
    t_inv = _unit_lower_inverses(l_mat, masks_ref)
    state = [s_ref[h] for h in heads]
    for b in blocks:
        el = el_sc[b]
        s_bf = [state[h].astype(BF16) for h in heads]
        resid = [vb_sc[at(b, h)] - _mm(kbg_sc[at(b, h)], s_bf[h]) for h in heads]
        v_new = [_mm(t_inv[at(b, h)], resid[h]).astype(BF16) for h in heads]
        o = [_mm(jnp.concatenate([qd_sc[at(b, h)], attn[at(b, h)]], axis=1),
                 jnp.concatenate([s_bf[h], v_new[h]], axis=0)) for h in heads]
        state = [state[h] * lane_of(el, h) + _mm_tn(kd_sc[at(b, h)], v_new[h]) for h in heads]
        for h in heads:
            on = o[h] * lax.rsqrt(jnp.mean(o[h] * o[h], axis=-1, keepdims=True) + EPS) * gain_ref[...]
            z = z_ref[b * c:(b + 1) * c, h * GDN_DV:(h + 1) * GDN_DV].astype(F32)
            o_ref[b * c:(b + 1) * c, h * GDN_DV:(h + 1) * GDN_DV] = (on * _silu(z)).astype(o_ref.dtype)
    for h in heads:
        s_ref[h] = state[h]

    causal = row >= col

    def l2n(a, scale=1.0):
        return a * (lax.rsqrt(jnp.sum(a * a, axis=-1, keepdims=True) + EPS) * scale)

    for b in blocks:
        ba = ba_ref[b * c:(b + 1) * c, :]
        beta_all = _sigmoid(ba)
        xg = ba + dtb_ref[...]
        softplus = jnp.maximum(xg, 0.0) + jnp.log(1.0 + jnp.exp(-jnp.abs(xg)))
        g_all = -jnp.exp(alog_ref[...]) * softplus
        gc = _chunk_cumsum_rows(g_all)
        gc_t = gc.T
        g_last = gc[c - 1:c, :]
        eg_all = jnp.exp(gc)
        ekd_all = jnp.exp(g_last - gc)
        conv_tile = lambda col0: _silu(conv_pieces[b, col0 // GDN_CONV_PIECE][
            :, col0 % GDN_CONV_PIECE:col0 % GDN_CONV_PIECE + LANES])
        for h in heads:
            q = l2n(conv_tile(h * GDN_DK), GDN_DK ** -0.5)
            k = l2n(conv_tile(GDN_QK + h * GDN_DK))
            v = conv_tile(2 * GDN_QK + h * GDN_DV)
            beta = beta_all[:, h:h + 1]
            eg = lane_of(eg_all, h)
            kb = k * beta
            gdiff = lane_of(gc, h) - gc_t[GDN_HEADS + h:GDN_HEADS + h + 1, :]
            q_sc[at(b, h)] = q.astype(BF16)
            k_sc[at(b, h)] = k.astype(BF16)
            kb_sc[at(b, h)] = kb.astype(BF16)
            vb_sc[at(b, h)] = v * beta
            kbg_sc[at(b, h)] = (kb * eg).astype(BF16)
            qd_sc[at(b, h)] = (q * eg).astype(BF16)
            kd_sc[at(b, h)] = (k * lane_of(ekd_all, h)).astype(BF16)
            dec_sc[at(b, h)] = jnp.where(causal, jnp.exp(jnp.where(causal, gdiff, 0.0)), 0.0)
        el_sc[b] = jnp.exp(g_last)


def _gdn_masks():
    c = GDN_BLOCK
    r = np.arange(c)[:, None]
    k = np.arange(c)[None, :]
    mats = [np.eye(c), (r // 8 == k // 8) & (r > k)]
    b = 8
    while b < c:
        mats.append((r // (2 * b) == k // (2 * b)) & ((r // b) % 2 == 1) & ((k // b) % 2 == 0))
        b *= 2
    return jnp.asarray(np.stack([np.asarray(m, np.float32) for m in mats]), BF16)


def _conv_band(c, pad):
    band = np.zeros((c, GDN_CONV * (pad + c)), np.float32)
    for i in range(GDN_CONV):
        band[np.arange(c), i * (pad + c) + pad + np.arange(c) - (GDN_CONV - 1) + i] = 1.0
    return jnp.asarray(band, BF16)


def _gdn(proj, ba, conv_w, alog_lane, dtb_lane, gain):
    t = proj.shape[0]
    c = GDN_BLOCK
    rows = GDN_STEP_BLOCKS * c
    pad = 16
    masks = _gdn_masks()
    nm = masks.shape[0]
    band = _conv_band(c, pad)
    n_steps = t // rows
    staged = lambda i: jnp.minimum(i, n_steps - 1)
    chained = lambda i: jnp.maximum(i - 1, 0)
    item_sq = lambda w, dt: pltpu.VMEM((GDN_STEP_BLOCKS * GDN_HEADS, c, w), dt)
    return pl.pallas_call(
        _gdn_kernel,
        out_shape=jax.ShapeDtypeStruct((t, GDN_V), BF16),
        grid=(n_steps + 1,),
        in_specs=[pl.BlockSpec((rows, GDN_CONV_CH), lambda i: (staged(i), 0)),
                  pl.BlockSpec((pad, GDN_CONV_CH),
                               lambda i: (jnp.maximum(staged(i) * (rows // pad) - 1, 0), 0)),
                  pl.BlockSpec((rows, GDN_V), lambda i: (chained(i), COL_ZA // GDN_V)),
                  pl.BlockSpec((rows, LANES), lambda i: (staged(i), 0)),
                  pl.BlockSpec((GDN_CONV, GDN_CONV_CH), lambda i: (0, 0)),
                  pl.BlockSpec(band.shape, lambda i: (0, 0)),
                  pl.BlockSpec((1, LANES), lambda i: (0, 0)),
                  pl.BlockSpec((1, LANES), lambda i: (0, 0)),
                  pl.BlockSpec((1, GDN_DV), lambda i: (0, 0)),
                  pl.BlockSpec((nm, c, c), lambda i: (0, 0, 0))],
        out_specs=pl.BlockSpec((rows, GDN_V), lambda i: (chained(i), 0)),
        scratch_shapes=[pltpu.VMEM((GDN_HEADS, GDN_DK, GDN_DV), F32),
                        item_sq(GDN_DK, BF16), item_sq(GDN_DK, BF16), item_sq(GDN_DK, BF16),
                        item_sq(GDN_DV, F32), item_sq(GDN_DK, BF16),
                        item_sq(GDN_DK, BF16), item_sq(GDN_DK, BF16),
                        item_sq(c, F32),
                        pltpu.VMEM((GDN_STEP_BLOCKS, 1, LANES), F32)],
        compiler_params=_params(("arbitrary",)),
        name="gdn",
    )(proj, proj, proj, ba, conv_w, band, alog_lane, dtb_lane, gain, masks)


def _swa_kernel(q_ref, kc_ref, kp_ref, vc_ref, vp_ref, z_ref, bias_ref, qg_ref, kg_ref, seg_ref, o_ref):
    n = pl.program_id(0)
    qb = SWA_BLOCK
    hd = SWA_HEAD_DIM
    n_tiles = SWA_Q // LANES
    blocks = range(SWA_STEP_BLOCKS)
    items = [(j, tq) for j in blocks for tq in range(n_tiles)]
    lane = lax.broadcasted_iota(jnp.int32, (1, LANES), 1)
    lo = lane < hd
    seg = seg_ref[...].astype(BF16)

    def qk_norm(a, gain):
        ms = jnp.dot((a * a).astype(BF16), seg, preferred_element_type=F32)
        return a * (lax.rsqrt(ms + EPS) * gain)

    def two_heads(lo_part, hi_part):
        return jnp.concatenate([jnp.where(lo, lo_part, 0.0), jnp.where(lo, 0.0, hi_part)], axis=0)

    kall = jnp.concatenate([kp_ref[...], kc_ref[...]], axis=0).astype(F32)
    vall = jnp.concatenate([vp_ref[...], vc_ref[...]], axis=0).astype(F32)
    kn, vv = {}, {}
    for u in range(SWA_KV // LANES):
        knorm = qk_norm(kall[:, u * LANES:(u + 1) * LANES], kg_ref[...])
        vtile = vall[:, u * LANES:(u + 1) * LANES]
        for ci in range(SWA_STEP_BLOCKS + 1):
            kc = knorm[ci * qb:(ci + 1) * qb]
            vc = vtile[ci * qb:(ci + 1) * qb]
            kn[u, ci] = (kc, pltpu.roll(kc, hd, axis=1))
            vv[u, ci] = (vc, pltpu.roll(vc, hd, axis=1))

    row = lax.broadcasted_iota(jnp.int32, (qb, 1), 0)
    is_sink = row == 0
    k2s, rhs = {}, {}
    for j in blocks:
        prev_live = (n * SWA_STEP_BLOCKS + j) > 0
        ones_prev = jnp.where(prev_live | is_sink, 1.0, 0.0)
        ones_band = jnp.concatenate([ones_prev, jnp.ones_like(ones_prev)], axis=0)
        ones2 = two_heads(ones_band, ones_band)
        for u in range(SWA_KV // LANES):
            kband = [jnp.concatenate([jnp.where(is_sink, 0.0, kn[u, j][r]), kn[u, j + 1][r]], axis=0)
                     for r in range(2)]
            vband = [jnp.concatenate([jnp.where(prev_live & ~is_sink, vv[u, j][r], 0.0), vv[u, j + 1][r]],
                                     axis=0) for r in range(2)]
            for half in range(2):
                g = 2 * u + half
                k2s[j, g] = two_heads(kband[half], kband[1 - half]).astype(BF16)
                v2 = two_heads(vband[half], vband[1 - half])
                rhs[j, g] = jnp.concatenate([v2, ones2], axis=1).astype(BF16)

    q_gain = qg_ref[...] * (hd ** -0.5 * LOG2E)
    kv_of = lambda tq: (2 * tq) // (SWA_HEADS // SWA_KV_HEADS)
    qt = {(j, tq): qk_norm(q_ref[j * qb:(j + 1) * qb, tq * LANES:(tq + 1) * LANES].astype(F32), q_gain)
          for j, tq in items}
    logits = {(j, tq): _mm_nt(qt[j, tq], k2s[j, kv_of(tq)]) for j, tq in items}
    probs = {}
    for j, tq in items:
        ps = []
        for e in range(2):
            s = logits[j, tq][:, e * 2 * qb:(e + 1) * 2 * qb].astype(BF16) + bias_ref[2 * tq + e]
            ps.append(jnp.exp2(s - jnp.max(s, axis=-1, keepdims=True)))
        probs[j, tq] = jnp.concatenate(ps, axis=1)
    pv = {(j, tq): jnp.dot(probs[j, tq], rhs[j, kv_of(tq)], preferred_element_type=F32) for j, tq in items}
    for j, tq in items:
        out = pv[j, tq][:, :LANES] * (1.0 / pv[j, tq][:, LANES:])
        z = z_ref[j * qb:(j + 1) * qb, tq * LANES:(tq + 1) * LANES].astype(F32)
        o_ref[j * qb:(j + 1) * qb, tq * LANES:(tq + 1) * LANES] = (out * _silu(z)).astype(o_ref.dtype)


def _swa(proj, bias, qg2, kg2):
    t = proj.shape[0]
    qb = SWA_BLOCK
    rows = SWA_STEP_BLOCKS * qb
    seg = np.kron(np.eye(LANES // SWA_HEAD_DIM), np.ones((SWA_HEAD_DIM, SWA_HEAD_DIM))) / SWA_HEAD_DIM
    seg = jnp.asarray(seg, F32)
    kcol = COL_KB // SWA_KV
    vcol = COL_VB // SWA_KV
    prev = lambda i: jnp.maximum(i * SWA_STEP_BLOCKS - 1, 0)
    return pl.pallas_call(
        _swa_kernel,
        out_shape=jax.ShapeDtypeStruct((t, SWA_Q), BF16),
        grid=(t // rows,),
        in_specs=[pl.BlockSpec((rows, SWA_Q), lambda i: (i, COL_QB // SWA_Q)),
                  pl.BlockSpec((rows, SWA_KV), lambda i: (i, kcol)),
                  pl.BlockSpec((qb, SWA_KV), lambda i: (prev(i), kcol)),
                  pl.BlockSpec((rows, SWA_KV), lambda i: (i, vcol)),
                  pl.BlockSpec((qb, SWA_KV), lambda i: (prev(i), vcol)),
                  pl.BlockSpec((rows, SWA_Q), lambda i: (i, COL_ZB // SWA_Q)),
                  pl.BlockSpec((SWA_HEADS, qb, 2 * qb), lambda i: (0, 0, 0)),
                  pl.BlockSpec((1, LANES), lambda i: (0, 0)),
                  pl.BlockSpec((1, LANES), lambda i: (0, 0)),
                  pl.BlockSpec((LANES, LANES), lambda i: (0, 0))],
        out_specs=pl.BlockSpec((rows, SWA_Q), lambda i: (i, 0)),
        compiler_params=_params(("arbitrary",)),
        name="swa",
    )(proj, proj, proj, proj, proj, proj, bias, qg2, kg2, seg)


def _merge_out_kernel(x_ref, oa_ref, ob_ref, ga_ref, gb_ref, gate_ref, wa_ref, wb_ref, wo_ref, o_ref):
    ya = jnp.dot(oa_ref[...], wa_ref[...], preferred_element_type=F32)
    yb = jnp.dot(ob_ref[...], wb_ref[...], preferred_element_type=F32)
    mixed = _sigmoid(ga_ref[...].astype(F32)) * ya + _sigmoid(gb_ref[...].astype(F32)) * yb
    y = jnp.dot(mixed.astype(BF16), wo_ref[...], preferred_element_type=F32)
    o_ref[...] = x_ref[...] + gate_ref[...] * y


def _merge_out(x2d, o_a, o_b, proj, gate, w_a, w_b, w_o, *, tm):
    t, d = x2d.shape
    const = lambda shape: pl.BlockSpec(shape, lambda i: (0, 0), pipeline_mode=pl.Buffered(1))
    return pl.pallas_call(
        _merge_out_kernel,
        out_shape=jax.ShapeDtypeStruct((t, d), F32),
        grid=(t // tm,),
        in_specs=[pl.BlockSpec((tm, d), lambda i: (i, 0)),
                  pl.BlockSpec((tm, GDN_V), lambda i: (i, 0)),
                  pl.BlockSpec((tm, SWA_Q), lambda i: (i, 0)),
                  pl.BlockSpec((tm, d), lambda i: (i, COL_GA // D_MODEL)),
                  pl.BlockSpec((tm, d), lambda i: (i, COL_GB // D_MODEL)),
                  pl.BlockSpec((1, d), lambda i: (0, 0)),
                  const((GDN_V, d)), const((SWA_Q, d)), const((d, d))],
        out_specs=pl.BlockSpec((tm, d), lambda i: (i, 0)),
        compiler_params=_params(("arbitrary",)),
        name="merge_out",
    )(x2d, o_a, o_b, proj, proj, gate, w_a, w_b, w_o)


REPACK_TILE = 512
N_GATE_COLS = 2 * GDN_HEADS


def _repack_plan():
    src_of = {COL_QKV: 0, COL_ZA: GDN_CONV_CH}
    after_gates = COL_QB + N_GATE_COLS
    src_of.update({COL_QB: after_gates, COL_KB: after_gates + SWA_Q, COL_VB: after_gates + SWA_Q + SWA_KV,
                   COL_ZB: after_gates + SWA_Q + 2 * SWA_KV, COL_GA: after_gates + 2 * SWA_Q + 2 * SWA_KV,
                   COL_GB: after_gates + 2 * SWA_Q + 2 * SWA_KV + D_MODEL})
    starts = sorted(src_of)
    src = []
    for dst in range(0, PROJ_COLS, REPACK_TILE):
        grp = max(s for s in starts if s <= dst)
        src.append(src_of[grp] + dst - grp)
    return np.asarray(src, np.int32)


def _repack_kernel(tab_ref, w_ref, g_ref, big_ref, small_ref):
    del tab_ref
    big_ref[...] = w_ref[...].astype(BF16)

    @pl.when(pl.program_id(0) == 0)
    def _():
        row = lax.broadcasted_iota(jnp.int32, (LANES, 1), 0)
        small_ref[...] = jnp.where(row < N_GATE_COLS, g_ref[...], 0.0).astype(BF16)


def _repack_w_in(w_t):
    d = w_t.shape[1]
    src = _repack_plan()
    gate_row = COL_QB
    assert gate_row % LANES == 0 and np.all(src % N_GATE_COLS == 0)
    return pl.pallas_call(
        _repack_kernel,
        out_shape=(jax.ShapeDtypeStruct((PROJ_COLS, d), BF16), jax.ShapeDtypeStruct((LANES, d), BF16)),
        grid_spec=pltpu.PrefetchScalarGridSpec(
            num_scalar_prefetch=1,
            grid=(PROJ_COLS // REPACK_TILE,),
            in_specs=[pl.BlockSpec((pl.Element(REPACK_TILE), pl.Element(d)),
                                   lambda o, tab: (tab[o] * N_GATE_COLS, 0)),
                      pl.BlockSpec((LANES, d), lambda o, tab: (gate_row // LANES, 0))],
            out_specs=(pl.BlockSpec((REPACK_TILE, d), lambda o, tab: (o, 0)),
                       pl.BlockSpec((LANES, d), lambda o, tab: (0, 0)))),
        compiler_params=_params(("arbitrary",)),
        name="repack_w_in",
    )(jnp.asarray(src // N_GATE_COLS), w_t, w_t)


def _lane_row(vec, offset):
    return jnp.pad(vec.astype(F32), (offset, LANES - offset - vec.shape[0]))[None, :]


def kernel(x, c, w_ada, b_ada, norm_gain, w_in, conv_w, a_log, dt_bias, gdn_norm_gain, q_norm_gain,
           k_norm_gain, sinks, rel_bias, w_branch_gdn, w_branch_swa, w_out):
    bsz, t, d = x.shape
    depth = w_in.shape[0]
    outs = []
    for b in range(bsz):
        xb = x[b]
        c_col = c[b].astype(F32)[:, None]
        for l in range(depth):
            mod = _ada_mod(c_col, w_ada[l], b_ada[l][None, :])
            shift, scale, gate = mod[:, :d], mod[:, d:2 * d], mod[:, 2 * d:]
            w_big, w_small = _repack_w_in(w_in[l].T)
            proj, ba, (w_a, w_b, w_o) = _in_proj(
                xb, norm_gain[l][None, :], scale, shift, w_big, w_small,
                (w_branch_gdn[l], w_branch_swa[l], w_out[l]), tm=min(1024, t), tn=1792)
            o_a = _gdn(proj, ba, conv_w[l], _lane_row(a_log[l], GDN_HEADS), _lane_row(dt_bias[l], GDN_HEADS),
                       gdn_norm_gain[l][None, :])
            bias = _swa_bias(rel_bias.T.astype(F32), sinks[l].astype(F32))
            o_b = _swa(proj, bias,
                       jnp.tile(q_norm_gain[l], LANES // SWA_HEAD_DIM)[None, :],
                       jnp.tile(k_norm_gain[l], LANES // SWA_HEAD_DIM)[None, :])
            xb = _merge_out(xb, o_a, o_b, proj, gate, w_a, w_b, w_o, tm=min(512, t))
        outs.append(xb)
    return jnp.stack(outs, axis=0)
```

```python
import functools
import math

import jax
import jax.numpy as jnp
import numpy as np
from jax import lax
from jax.experimental import pallas as pl
from jax.experimental.pallas import tpu as pltpu

F32 = jnp.float32
BF16 = jnp.bfloat16

LANES = 128
D_MODEL = 2048
GDN_HEADS = 8
GDN_DK = 128
GDN_DV = 128
GDN_CONV = 4
GDN_QK = GDN_HEADS * GDN_DK
GDN_V = GDN_HEADS * GDN_DV
GDN_CONV_CH = 2 * GDN_QK + GDN_V
GDN_BLOCK = 128
GDN_STEP_BLOCKS = 2
GDN_CONV_PIECE = 512
SWA_HEADS = 16
SWA_KV_HEADS = 4
SWA_HEAD_DIM = 64
SWA_WINDOW = 128
SWA_BLOCK = 128
SWA_STEP_BLOCKS = 4
SWA_Q = SWA_HEADS * SWA_HEAD_DIM
SWA_KV = SWA_KV_HEADS * SWA_HEAD_DIM
REL_BUCKETS = 32
REL_MAX_DIST = 128
EPS = 1e-6
NEG_BIG = -1e30
LOG2E = math.log2(math.e)

COL_QKV = 0
COL_ZA = GDN_CONV_CH
COL_QB = COL_ZA + GDN_V
COL_ZB = COL_QB + SWA_Q
COL_GA = COL_ZB + SWA_Q
COL_GB = COL_GA + D_MODEL
COL_KB = COL_GB + D_MODEL
COL_VB = COL_KB + SWA_KV
PROJ_COLS = COL_VB + SWA_KV
for _col, _width in ((COL_ZA, GDN_V), (COL_QB, SWA_Q), (COL_ZB, SWA_Q), (COL_GA, D_MODEL),
                     (COL_GB, D_MODEL), (COL_KB, SWA_KV), (COL_VB, SWA_KV)):
    assert _col % _width == 0

VMEM_LIMIT = 56 * 1024 * 1024


def _sigmoid(x):
    return 0.5 + 0.5 * jnp.tanh(0.5 * x)


def _silu(x):
    half = 0.5 * x
    return half + half * jnp.tanh(half)


def _params(sem):
    return pltpu.CompilerParams(dimension_semantics=sem, vmem_limit_bytes=VMEM_LIMIT)


def _mm(a, b):
    return jnp.dot(a.astype(BF16), b.astype(BF16), preferred_element_type=F32)


def _mm_nt(a, b):
    return lax.dot_general(a.astype(BF16), b.astype(BF16), (((1,), (1,)), ((), ())),
                           preferred_element_type=F32)


def _mm_tn(a, b):
    return lax.dot_general(a.astype(BF16), b.astype(BF16), (((0,), (0,)), ((), ())),
                           preferred_element_type=F32)


def _ada_mod_kernel(c_ref, w_ref, b_ref, o_ref):
    c = c_ref[...]
    o_ref[...] = jnp.sum(_silu(c) * w_ref[...], axis=0, keepdims=True) + b_ref[...]


def _ada_mod(c_col, w_ada, b_ada):
    d, n = w_ada.shape
    tn = 1024
    return pl.pallas_call(
        _ada_mod_kernel,
        out_shape=jax.ShapeDtypeStruct((1, n), F32),
        grid=(n // tn,),
        in_specs=[pl.BlockSpec((d, 1), lambda j: (0, 0)),
                  pl.BlockSpec((d, tn), lambda j: (0, j)),
                  pl.BlockSpec((1, tn), lambda j: (0, j))],
        out_specs=pl.BlockSpec((1, tn), lambda j: (0, j)),
        compiler_params=_params(("arbitrary",)),
        name="ada_mod",
    )(c_col, w_ada, b_ada)


def _in_proj_kernel(x_ref, gain_ref, scale_ref, shift_ref, w_ref, ws_ref, *rest, row_chunk, n_side):
    side_in, (o_ref, ba_ref), side_out, h_ref = (rest[:n_side], rest[n_side:n_side + 2],
                                                 rest[n_side + 2:-1], rest[-1])
    for src, dst in zip(side_in, side_out):
        dst[...] = src[...].astype(dst.dtype)
    j = pl.program_id(1)

    @pl.when(j == 0)
    def _():
        gs = gain_ref[...] * (1.0 + scale_ref[...])
        sh = shift_ref[...]
        tm = x_ref.shape[0]

        def body(r, carry):
            rows = pl.ds(pl.multiple_of(r * row_chunk, row_chunk), row_chunk)
            x = x_ref[rows, :]
            ms = jnp.mean(x * x, axis=-1, keepdims=True)
            h = (x * lax.rsqrt(ms + EPS)) * gs + sh
            h_ref[rows, :] = h.astype(BF16)
            return carry

        lax.fori_loop(0, tm // row_chunk, body, 0)
        ba_ref[...] = _mm_nt(h_ref[...], ws_ref[...])

    o_ref[...] = _mm_nt(h_ref[...], w_ref[...]).astype(o_ref.dtype)


def _in_proj(x2d, gain, scale, shift, w_big, w_small, side_weights, *, tm, tn):
    t, d = x2d.shape
    n = w_big.shape[0]
    n_i, n_j = t // tm, n // tn
    slabs = 1 << ((n_i * n_j).bit_length() - 1)
    slab_of = lambda i, j: jnp.minimum(i * n_j + j, slabs - 1)
    side_specs = [pl.BlockSpec((w.shape[0] // slabs, w.shape[1]), lambda i, j: (slab_of(i, j), 0))
                  for w in side_weights]
    assert all(w.shape[0] % (16 * slabs) == 0 for w in side_weights)
    out = pl.pallas_call(
        functools.partial(_in_proj_kernel, row_chunk=128, n_side=len(side_weights)),
        out_shape=(jax.ShapeDtypeStruct((t, n), BF16), jax.ShapeDtypeStruct((t, LANES), F32),
                   *[jax.ShapeDtypeStruct(w.shape, BF16) for w in side_weights]),
        grid=(n_i, n_j),
        in_specs=[pl.BlockSpec((tm, d), lambda i, j: (i, 0)),
                  pl.BlockSpec((1, d), lambda i, j: (0, 0)),
                  pl.BlockSpec((1, d), lambda i, j: (0, 0)),
                  pl.BlockSpec((1, d), lambda i, j: (0, 0)),
                  pl.BlockSpec((tn, d), lambda i, j: (j, 0)),
                  pl.BlockSpec((LANES, d), lambda i, j: (0, 0)),
                  *side_specs],
        out_specs=(pl.BlockSpec((tm, tn), lambda i, j: (i, j)),
                   pl.BlockSpec((tm, LANES), lambda i, j: (i, 0)),
                   *side_specs),
        scratch_shapes=[pltpu.VMEM((tm, d), BF16)],
        compiler_params=_params(("arbitrary", "arbitrary")),
        name="in_proj",
    )(x2d, gain, scale, shift, w_big, w_small, *side_weights)
    return out[0], out[1], out[2:]


def _swa_bias_kernel(tab_ref, sink_ref, o_ref):
    q = SWA_BLOCK
    qpos = lax.broadcasted_iota(jnp.int32, (q, 2 * q), 0) + q
    kpos = lax.broadcasted_iota(jnp.int32, (q, 2 * q), 1)
    dist = qpos - kpos
    in_window = (dist >= 0) & (dist < SWA_WINDOW)
    d = jnp.maximum(dist, 0)
    max_exact = REL_BUCKETS // 2
    df = jnp.maximum(d, 1).astype(F32)
    large = max_exact + (jnp.log(df / max_exact) / math.log(REL_MAX_DIST / max_exact)
                         * (REL_BUCKETS - max_exact)).astype(jnp.int32)
    large = jnp.minimum(large, REL_BUCKETS - 1)
    bucket = jnp.where(in_window, jnp.where(d < max_exact, d, large), REL_BUCKETS)
    for h in range(SWA_HEADS):
        acc = jnp.full((q, 2 * q), NEG_BIG, F32)
        for b in range(REL_BUCKETS):
            acc = jnp.where(bucket == b, tab_ref[h, b], acc)
        o_ref[h] = (jnp.where(kpos == 0, sink_ref[h], acc) * LOG2E).astype(o_ref.dtype)


def _swa_bias(rel_bias_t, sinks):
    q = SWA_BLOCK
    return pl.pallas_call(
        _swa_bias_kernel,
        out_shape=jax.ShapeDtypeStruct((SWA_HEADS, q, 2 * q), BF16),
        in_specs=[pl.BlockSpec(memory_space=pltpu.SMEM), pl.BlockSpec(memory_space=pltpu.SMEM)],
        out_specs=pl.BlockSpec(memory_space=pltpu.VMEM),
        compiler_params=pltpu.CompilerParams(vmem_limit_bytes=VMEM_LIMIT),
        name="swa_bias",
    )(rel_bias_t, sinks)


def _chunk_cumsum_rows(x):
    n = x.shape[0]
    row = lax.broadcasted_iota(jnp.int32, x.shape, 0)
    s = 1
    while s < n:
        x = x + jnp.where(row >= s, pltpu.roll(x, s, axis=0), 0.0)
        s *= 2
    return x


def _unit_lower_inverses(l_mats, masks_ref):
    eye = masks_ref[0].astype(F32)
    l_bf = [l.astype(BF16) for l in l_mats]
    m0 = [l * masks_ref[1] for l in l_bf]
    x = [eye - m for m in m0]
    p = [_mm(m, m) for m in m0]
    x = [xi + _mm(xi, pi) for xi, pi in zip(x, p)]
    p = [_mm(pi, pi) for pi in p]
    x = [xi + _mm(xi, pi) for xi, pi in zip(x, p)]
    for lvl in range(2, masks_ref.shape[0]):
        nx = [_mm(l * masks_ref[lvl], xi) for l, xi in zip(l_bf, x)]
        x = [xi - _mm(xi, ni) for xi, ni in zip(x, nx)]
    return x


def _gdn_kernel(cur_ref, prev_ref, z_ref, ba_ref, convw_ref, band_ref, alog_ref, dtb_ref, gain_ref,
                masks_ref, o_ref, s_ref, q_sc, k_sc, kb_sc, vb_sc, kbg_sc, qd_sc, kd_sc, dec_sc, el_sc):
    s = pl.program_id(0)
    c = GDN_BLOCK
    heads = range(GDN_HEADS)
    blocks = range(GDN_STEP_BLOCKS)
    items = [(b, h) for b in blocks for h in heads]
    at = lambda b, h: b * GDN_HEADS + h
    lane_of = lambda a, h: a[:, GDN_HEADS + h:GDN_HEADS + h + 1]

    @pl.when(s == 0)
    def _():
        for ref in (s_ref, q_sc, k_sc, kb_sc, vb_sc, kbg_sc, qd_sc, kd_sc, dec_sc, el_sc):
            ref[...] = jnp.zeros_like(ref)

    pad = prev_ref.shape[0]
    conv_w = convw_ref[...].astype(BF16)
    taps = []
    for b in blocks:
        if b == 0:
            ctx = jnp.where(s > 0, prev_ref[...], jnp.zeros_like(prev_ref))
        else:
            ctx = cur_ref[b * c - pad:b * c, :]
        xcat = jnp.concatenate([ctx, cur_ref[b * c:(b + 1) * c, :]], axis=0)
        taps.append(jnp.concatenate([xcat * conv_w[i:i + 1, :] for i in range(GDN_CONV)], axis=0))

    row = lax.broadcasted_iota(jnp.int32, (c, c), 0)
    col = lax.broadcasted_iota(jnp.int32, (c, c), 1)
    kk = [_mm_nt(kb_sc[at(b, h)], k_sc[at(b, h)]) for b, h in items]
    qk = [_mm_nt(q_sc[at(b, h)], k_sc[at(b, h)]) for b, h in items]
    conv_pieces = {}
    for p in range(GDN_CONV_CH // GDN_CONV_PIECE):
        for b in blocks:
            cols = slice(p * GDN_CONV_PIECE, (p + 1) * GDN_CONV_PIECE)
            conv_pieces[b, p] = jnp.dot(band_ref[...], taps[b][:, cols], preferred_element_type=F32)
    l_mat = [kk[at(b, h)] * dec_sc[at(b, h)] for b, h in items]
    attn = [(qk[at(b, h)] * dec_sc[at(b, h)]).astype(BF16) for b, h in items]
    t_inv = _unit_lower_inverses(l_mat, masks_ref)
    state = [s_ref[h] for h in heads]
    for b in blocks:
        el = el_sc[b]
        s_bf = [state[h].astype(BF16) for h in heads]
        resid = [vb_sc[at(b, h)] - _mm(kbg_sc[at(b, h)], s_bf[h]) for h in heads]
        v_new = [_mm(t_inv[at(b, h)], resid[h]).astype(BF16) for h in heads]
        o = [_mm(jnp.concatenate([qd_sc[at(b, h)], attn[at(b, h)]], axis=1),
                 jnp.concatenate([s_bf[h], v_new[h]], axis=0)) for h in heads]
        state = [state[h] * lane_of(el, h) + _mm_tn(kd_sc[at(b, h)], v_new[h]) for h in heads]
        for h in heads:
            on = o[h] * lax.rsqrt(jnp.mean(o[h] * o[h], axis=-1, keepdims=True) + EPS) * gain_ref[...]
            z = z_ref[b * c:(b + 1) * c, h * GDN_DV:(h + 1) * GDN_DV].astype(F32)
            o_ref[b * c:(b + 1) * c, h * GDN_DV:(h + 1) * GDN_DV] = (on * _silu(z)).astype(o_ref.dtype)
    for h in heads:
        s_ref[h] = state[h]

    causal = row >= col

    def l2n(a, scale=1.0):
        return a * (lax.rsqrt(jnp.sum(a * a, axis=-1, keepdims=True) + EPS) * scale)

    for b in blocks:
        ba = ba_ref[b * c:(b + 1) * c, :]
        beta_all = _sigmoid(ba)
        xg = ba + dtb_ref[...]
        softplus = jnp.maximum(xg, 0.0) + jnp.log(1.0 + jnp.exp(-jnp.abs(xg)))
        g_all = -jnp.exp(alog_ref[...]) * softplus
        gc = _chunk_cumsum_rows(g_all)
        gc_t = gc.T
        g_last = gc[c - 1:c, :]
        eg_all = jnp.exp(gc)
        ekd_all = jnp.exp(g_last - gc)
        conv_tile = lambda col0: _silu(conv_pieces[b, col0 // GDN_CONV_PIECE][
            :, col0 % GDN_CONV_PIECE:col0 % GDN_CONV_PIECE + LANES])
        for h in heads:
            q = l2n(conv_tile(h * GDN_DK), GDN_DK ** -0.5)
            k = l2n(conv_tile(GDN_QK + h * GDN_DK))
            v = conv_tile(2 * GDN_QK + h * GDN_DV)
            beta = beta_all[:, h:h + 1]
            eg = lane_of(eg_all, h)
            kb = k * beta
            gdiff = lane_of(gc, h) - gc_t[GDN_HEADS + h:GDN_HEADS + h + 1, :]
            q_sc[at(b, h)] = q.astype(BF16)
            k_sc[at(b, h)] = k.astype(BF16)
            kb_sc[at(b, h)] = kb.astype(BF16)
            vb_sc[at(b, h)] = v * beta
            kbg_sc[at(b, h)] = (kb * eg).astype(BF16)
            qd_sc[at(b, h)] = (q * eg).astype(BF16)
            kd_sc[at(b, h)] = (k * lane_of(ekd_all, h)).astype(BF16)
            dec_sc[at(b, h)] = jnp.where(causal, jnp.exp(gdiff), 0.0)
        el_sc[b] = jnp.exp(g_last)


def _gdn_masks():
    c = GDN_BLOCK
    r = np.arange(c)[:, None]
    k = np.arange(c)[None, :]
    mats = [np.eye(c), (r // 8 == k // 8) & (r > k)]
    b = 8
    while b < c:
        mats.append((r // (2 * b) == k // (2 * b)) & ((r // b) % 2 == 1) & ((k // b) % 2 == 0))
        b *= 2
    return jnp.asarray(np.stack([np.asarray(m, np.float32) for m in mats]), BF16)


def _conv_band(c, pad):
    band = np.zeros((c, GDN_CONV * (pad + c)), np.float32)
    for i in range(GDN_CONV):
        band[np.arange(c), i * (pad + c) + pad + np.arange(c) - (GDN_CONV - 1) + i] = 1.0
    return jnp.asarray(band, BF16)


def _gdn(proj, ba, conv_w, alog_lane, dtb_lane, gain):
    t = proj.shape[0]
    c = GDN_BLOCK
    rows = GDN_STEP_BLOCKS * c
    pad = 16
    masks = _gdn_masks()
    nm = masks.shape[0]
    band = _conv_band(c, pad)
    n_steps = t // rows
    staged = lambda i: jnp.minimum(i, n_steps - 1)
    chained = lambda i: jnp.maximum(i - 1, 0)
    item_sq = lambda w, dt: pltpu.VMEM((GDN_STEP_BLOCKS * GDN_HEADS, c, w), dt)
    return pl.pallas_call(
        _gdn_kernel,
        out_shape=jax.ShapeDtypeStruct((t, GDN_V), BF16),
        grid=(n_steps + 1,),
        in_specs=[pl.BlockSpec((rows, GDN_CONV_CH), lambda i: (staged(i), 0)),
                  pl.BlockSpec((pad, GDN_CONV_CH),
                               lambda i: (jnp.maximum(staged(i) * (rows // pad) - 1, 0), 0)),
                  pl.BlockSpec((rows, GDN_V), lambda i: (chained(i), COL_ZA // GDN_V)),
                  pl.BlockSpec((rows, LANES), lambda i: (staged(i), 0)),
                  pl.BlockSpec((GDN_CONV, GDN_CONV_CH), lambda i: (0, 0)),
                  pl.BlockSpec(band.shape, lambda i: (0, 0)),
                  pl.BlockSpec((1, LANES), lambda i: (0, 0)),
                  pl.BlockSpec((1, LANES), lambda i: (0, 0)),
                  pl.BlockSpec((1, GDN_DV), lambda i: (0, 0)),
                  pl.BlockSpec((nm, c, c), lambda i: (0, 0, 0))],
        out_specs=pl.BlockSpec((rows, GDN_V), lambda i: (chained(i), 0)),
        scratch_shapes=[pltpu.VMEM((GDN_HEADS, GDN_DK, GDN_DV), F32),
                        item_sq(GDN_DK, BF16), item_sq(GDN_DK, BF16), item_sq(GDN_DK, BF16),
                        item_sq(GDN_DV, F32), item_sq(GDN_DK, BF16),
                        item_sq(GDN_DK, BF16), item_sq(GDN_DK, BF16),
                        item_sq(c, F32),
                        pltpu.VMEM((GDN_STEP_BLOCKS, 1, LANES), F32)],
        compiler_params=_params(("arbitrary",)),
        name="gdn",
    )(proj, proj, proj, ba, conv_w, band, alog_lane, dtb_lane, gain, masks)


def _swa_kernel(q_ref, kc_ref, kp_ref, vc_ref, vp_ref, z_ref, bias_ref, qg_ref, kg_ref, seg_ref, o_ref):
    n = pl.program_id(0)
    qb = SWA_BLOCK
    hd = SWA_HEAD_DIM
    n_tiles = SWA_Q // LANES
    blocks = range(SWA_STEP_BLOCKS)
    items = [(j, tq) for j in blocks for tq in range(n_tiles)]
    lane = lax.broadcasted_iota(jnp.int32, (1, LANES), 1)
    lo = lane < hd
    seg = seg_ref[...].astype(BF16)

    def qk_norm(a, gain):
        ms = jnp.dot((a * a).astype(BF16), seg, preferred_element_type=F32)
        return a * (lax.rsqrt(ms + EPS) * gain)

    def two_heads(lo_part, hi_part):
        return jnp.concatenate([jnp.where(lo, lo_part, 0.0), jnp.where(lo, 0.0, hi_part)], axis=0)

    kall = jnp.concatenate([kp_ref[...], kc_ref[...]], axis=0).astype(F32)
    vall = jnp.concatenate([vp_ref[...], vc_ref[...]], axis=0).astype(F32)
    kn, vv = {}, {}
    for u in range(SWA_KV // LANES):
        knorm = qk_norm(kall[:, u * LANES:(u + 1) * LANES], kg_ref[...])
        vtile = vall[:, u * LANES:(u + 1) * LANES]
        for ci in range(SWA_STEP_BLOCKS + 1):
            kc = knorm[ci * qb:(ci + 1) * qb]
            vc = vtile[ci * qb:(ci + 1) * qb]
            kn[u, ci] = (kc, pltpu.roll(kc, hd, axis=1))
            vv[u, ci] = (vc, pltpu.roll(vc, hd, axis=1))

    row = lax.broadcasted_iota(jnp.int32, (qb, 1), 0)
    is_sink = row == 0
    k2s, rhs = {}, {}
    for j in blocks:
        prev_live = (n * SWA_STEP_BLOCKS + j) > 0
        ones_prev = jnp.where(prev_live | is_sink, 1.0, 0.0)
        ones_band = jnp.concatenate([ones_prev, jnp.ones_like(ones_prev)], axis=0)
        ones2 = two_heads(ones_band, ones_band)
        for u in range(SWA_KV // LANES):
            kband = [jnp.concatenate([jnp.where(is_sink, 0.0, kn[u, j][r]), kn[u, j + 1][r]], axis=0)
                     for r in range(2)]
            vband = [jnp.concatenate([jnp.where(prev_live & ~is_sink, vv[u, j][r], 0.0), vv[u, j + 1][r]],
                                     axis=0) for r in range(2)]
            for half in range(2):
                g = 2 * u + half
                k2s[j, g] = two_heads(kband[half], kband[1 - half]).astype(BF16)
                v2 = two_heads(vband[half], vband[1 - half])
                rhs[j, g] = jnp.concatenate([v2, ones2], axis=1).astype(BF16)

    q_gain = qg_ref[...] * (hd ** -0.5 * LOG2E)
    kv_of = lambda tq: (2 * tq) // (SWA_HEADS // SWA_KV_HEADS)
    qt = {(j, tq): qk_norm(q_ref[j * qb:(j + 1) * qb, tq * LANES:(tq + 1) * LANES].astype(F32), q_gain)
          for j, tq in items}
    logits = {(j, tq): _mm_nt(qt[j, tq], k2s[j, kv_of(tq)]) for j, tq in items}
    probs = {}
    for j, tq in items:
        ps = []
        for e in range(2):
            s = logits[j, tq][:, e * 2 * qb:(e + 1) * 2 * qb].astype(BF16) + bias_ref[2 * tq + e]
            ps.append(jnp.exp2(s - jnp.max(s, axis=-1, keepdims=True)))
        probs[j, tq] = jnp.concatenate(ps, axis=1)
    pv = {(j, tq): jnp.dot(probs[j, tq], rhs[j, kv_of(tq)], preferred_element_type=F32) for j, tq in items}
    for j, tq in items:
        out = pv[j, tq][:, :LANES] * (1.0 / pv[j, tq][:, LANES:])
        z = z_ref[j * qb:(j + 1) * qb, tq * LANES:(tq + 1) * LANES].astype(F32)
        o_ref[j * qb:(j + 1) * qb, tq * LANES:(tq + 1) * LANES] = (out * _silu(z)).astype(o_ref.dtype)


def _swa(proj, bias, qg2, kg2):
    t = proj.shape[0]
    qb = SWA_BLOCK
    rows = SWA_STEP_BLOCKS * qb
    seg = np.kron(np.eye(LANES // SWA_HEAD_DIM), np.ones((SWA_HEAD_DIM, SWA_HEAD_DIM))) / SWA_HEAD_DIM
    seg = jnp.asarray(seg, F32)
    kcol = COL_KB // SWA_KV
    vcol = COL_VB // SWA_KV
    prev = lambda i: jnp.maximum(i * SWA_STEP_BLOCKS - 1, 0)
    return pl.pallas_call(
        _swa_kernel,
        out_shape=jax.ShapeDtypeStruct((t, SWA_Q), BF16),
        grid=(t // rows,),
        in_specs=[pl.BlockSpec((rows, SWA_Q), lambda i: (i, COL_QB // SWA_Q)),
                  pl.BlockSpec((rows, SWA_KV), lambda i: (i, kcol)),
                  pl.BlockSpec((qb, SWA_KV), lambda i: (prev(i), kcol)),
                  pl.BlockSpec((rows, SWA_KV), lambda i: (i, vcol)),
                  pl.BlockSpec((qb, SWA_KV), lambda i: (prev(i), vcol)),
                  pl.BlockSpec((rows, SWA_Q), lambda i: (i, COL_ZB // SWA_Q)),
                  pl.BlockSpec((SWA_HEADS, qb, 2 * qb), lambda i: (0, 0, 0)),
                  pl.BlockSpec((1, LANES), lambda i: (0, 0)),
                  pl.BlockSpec((1, LANES), lambda i: (0, 0)),
                  pl.BlockSpec((LANES, LANES), lambda i: (0, 0))],
        out_specs=pl.BlockSpec((rows, SWA_Q), lambda i: (i, 0)),
        compiler_params=_params(("arbitrary",)),
        name="swa",
    )(proj, proj, proj, proj, proj, proj, bias, qg2, kg2, seg)


def _merge_out_kernel(x_ref, oa_ref, ob_ref, ga_ref, gb_ref, gate_ref, wa_ref, wb_ref, wo_ref, o_ref):
    ya = jnp.dot(oa_ref[...], wa_ref[...], preferred_element_type=F32)
    yb = jnp.dot(ob_ref[...], wb_ref[...], preferred_element_type=F32)
    mixed = _sigmoid(ga_ref[...].astype(F32)) * ya + _sigmoid(gb_ref[...].astype(F32)) * yb
    y = jnp.dot(mixed.astype(BF16), wo_ref[...], preferred_element_type=F32)
    o_ref[...] = x_ref[...] + gate_ref[...] * y


def _merge_out(x2d, o_a, o_b, proj, gate, w_a, w_b, w_o, *, tm):
    t, d = x2d.shape
    const = lambda shape: pl.BlockSpec(shape, lambda i: (0, 0), pipeline_mode=pl.Buffered(1))
    return pl.pallas_call(
        _merge_out_kernel,
        out_shape=jax.ShapeDtypeStruct((t, d), F32),
        grid=(t // tm,),
        in_specs=[pl.BlockSpec((tm, d), lambda i: (i, 0)),
                  pl.BlockSpec((tm, GDN_V), lambda i: (i, 0)),
                  pl.BlockSpec((tm, SWA_Q), lambda i: (i, 0)),
                  pl.BlockSpec((tm, d), lambda i: (i, COL_GA // D_MODEL)),
                  pl.BlockSpec((tm, d), lambda i: (i, COL_GB // D_MODEL)),
                  pl.BlockSpec((1, d), lambda i: (0, 0)),
                  const((GDN_V, d)), const((SWA_Q, d)), const((d, d))],
        out_specs=pl.BlockSpec((tm, d), lambda i: (i, 0)),
        compiler_params=_params(("arbitrary",)),
        name="merge_out",
    )(x2d, o_a, o_b, proj, proj, gate, w_a, w_b, w_o)


REPACK_TILE = 512
N_GATE_COLS = 2 * GDN_HEADS


def _repack_plan():
    src_of = {COL_QKV: 0, COL_ZA: GDN_CONV_CH}
    after_gates = COL_QB + N_GATE_COLS
    src_of.update({COL_QB: after_gates, COL_KB: after_gates + SWA_Q, COL_VB: after_gates + SWA_Q + SWA_KV,
                   COL_ZB: after_gates + SWA_Q + 2 * SWA_KV, COL_GA: after_gates + 2 * SWA_Q + 2 * SWA_KV,
                   COL_GB: after_gates + 2 * SWA_Q + 2 * SWA_KV + D_MODEL})
    starts = sorted(src_of)
    src = []
    for dst in range(0, PROJ_COLS, REPACK_TILE):
        grp = max(s for s in starts if s <= dst)
        src.append(src_of[grp] + dst - grp)
    return np.asarray(src, np.int32)


def _repack_kernel(tab_ref, w_ref, g_ref, big_ref, small_ref):
    del tab_ref
    big_ref[...] = w_ref[...].astype(BF16)

    @pl.when(pl.program_id(0) == 0)
    def _():
        row = lax.broadcasted_iota(jnp.int32, (LANES, 1), 0)
        small_ref[...] = jnp.where(row < N_GATE_COLS, g_ref[...], 0.0).astype(BF16)


def _repack_w_in(w_t):
    d = w_t.shape[1]
    src = _repack_plan()
    gate_row = COL_QB
    assert gate_row % LANES == 0 and np.all(src % N_GATE_COLS == 0)
    return pl.pallas_call(
        _repack_kernel,
        out_shape=(jax.ShapeDtypeStruct((PROJ_COLS, d), BF16), jax.ShapeDtypeStruct((LANES, d), BF16)),
        grid_spec=pltpu.PrefetchScalarGridSpec(
            num_scalar_prefetch=1,
            grid=(PROJ_COLS // REPACK_TILE,),
            in_specs=[pl.BlockSpec((pl.Element(REPACK_TILE), pl.Element(d)),
                                   lambda o, tab: (tab[o] * N_GATE_COLS, 0)),
                      pl.BlockSpec((LANES, d), lambda o, tab: (gate_row // LANES, 0))],
            out_specs=(pl.BlockSpec((REPACK_TILE, d), lambda o, tab: (o, 0)),
                       pl.BlockSpec((LANES, d), lambda o, tab: (0, 0)))),
        compiler_params=_params(("arbitrary",)),
        name="repack_w_in",
    )(jnp.asarray(src // N_GATE_COLS), w_t, w_t)


def _lane_row(vec, offset):
    return jnp.pad(vec.astype(F32), (offset, LANES - offset - vec.shape[0]))[None, :]


def kernel(x, c, w_ada, b_ada, norm_gain, w_in, conv_w, a_log, dt_bias, gdn_norm_gain, q_norm_gain,
           k_norm_gain, sinks, rel_bias, w_branch_gdn, w_branch_swa, w_out):
    bsz, t, d = x.shape
    depth = w_in.shape[0]
    outs = []
    for b in range(bsz):
        xb = x[b]
        c_col = c[b].astype(F32)[:, None]
        for l in range(depth):
            mod = _ada_mod(c_col, w_ada[l], b_ada[l][None, :])
            shift, scale, gate = mod[:, :d], mod[:, d:2 * d], mod[:, 2 * d:]
            w_big, w_small = _repack_w_in(w_in[l].T)
            proj, ba, (w_a, w_b, w_o) = _in_proj(
                xb, norm_gain[l][None, :], scale, shift, w_big, w_small,
                (w_branch_gdn[l], w_branch_swa[l], w_out[l]), tm=min(1024, t), tn=1792)
            o_a = _gdn(proj, ba, conv_w[l], _lane_row(a_log[l], GDN_HEADS), _lane_row(dt_bias[l], GDN_HEADS),
                       gdn_norm_gain[l][None, :])
            bias = _swa_bias(rel_bias.T.astype(F32), sinks[l].astype(F32))
            o_b = _swa(proj, bias,
                       jnp.tile(q_norm_gain[l], LANES // SWA_HEAD_DIM)[None, :],
                       jnp.tile(k_norm_gain[l], LANES // SWA_HEAD_DIM)[None, :])
            xb = _merge_out(xb, o_a, o_b, proj, gate, w_a, w_b, w_o, tm=min(512, t))
        outs.append(xb)
    return jnp.stack(outs, axis=0)
```

```python
import functools
import math

import jax
import jax.numpy as jnp
import numpy as np
from jax import lax
from jax.experimental import pallas as pl
from jax.experimental.pallas import tpu as pltpu

F32 = jnp.float32
BF16 = jnp.bfloat16

LANES = 128
D_MODEL = 2048
GDN_HEADS = 8
GDN_DK = 128
GDN_DV = 128
GDN_CONV = 4
GDN_QK = GDN_HEADS * GDN_DK
GDN_V = GDN_HEADS * GDN_DV
GDN_CONV_CH = 2 * GDN_QK + GDN_V
GDN_BLOCK = 128
GDN_STEP_BLOCKS = 2
GDN_CONV_PIECE = 512
SWA_HEADS = 16
SWA_KV_HEADS = 4
SWA_HEAD_DIM = 64
SWA_WINDOW = 128
SWA_BLOCK = 128
SWA_STEP_BLOCKS = 4
SWA_Q = SWA_HEADS * SWA_HEAD_DIM
SWA_KV = SWA_KV_HEADS * SWA_HEAD_DIM
REL_BUCKETS = 32
REL_MAX_DIST = 128
EPS = 1e-6
NEG_BIG = -1e30
LOG2E = math.log2(math.e)

COL_QKV = 0
COL_ZA = GDN_CONV_CH
COL_QB = COL_ZA + GDN_V
COL_ZB = COL_QB + SWA_Q
COL_GA = COL_ZB + SWA_Q
COL_GB = COL_GA + D_MODEL
COL_KB = COL_GB + D_MODEL
COL_VB = COL_KB + SWA_KV
PROJ_COLS = COL_VB + SWA_KV
for _col, _width in ((COL_ZA, GDN_V), (COL_QB, SWA_Q), (COL_ZB, SWA_Q), (COL_GA, D_MODEL),
                     (COL_GB, D_MODEL), (COL_KB, SWA_KV), (COL_VB, SWA_KV)):
    assert _col % _width == 0

VMEM_LIMIT = 56 * 1024 * 1024


def _sigmoid(x):
    return 0.5 + 0.5 * jnp.tanh(0.5 * x)


def _silu(x):
    half = 0.5 * x
    return half + half * jnp.tanh(half)


def _params(sem):
    return pltpu.CompilerParams(dimension_semantics=sem, vmem_limit_bytes=VMEM_LIMIT)


def _mm(a, b):
    return jnp.dot(a.astype(BF16), b.astype(BF16), preferred_element_type=F32)


def _mm_nt(a, b):
    return lax.dot_general(a.astype(BF16), b.astype(BF16), (((1,), (1,)), ((), ())),
                           preferred_element_type=F32)


def _mm_tn(a, b):
    return lax.dot_general(a.astype(BF16), b.astype(BF16), (((0,), (0,)), ((), ())),
                           preferred_element_type=F32)


def _ada_mod_kernel(c_ref, w_ref, b_ref, o_ref):
    c = c_ref[...]
    o_ref[...] = jnp.sum(_silu(c) * w_ref[...], axis=0, keepdims=True) + b_ref[...]


def _ada_mod(c_col, w_ada, b_ada):
    d, n = w_ada.shape
    tn = 1024
    return pl.pallas_call(
        _ada_mod_kernel,
        out_shape=jax.ShapeDtypeStruct((1, n), F32),
        grid=(n // tn,),
        in_specs=[pl.BlockSpec((d, 1), lambda j: (0, 0)),
                  pl.BlockSpec((d, tn), lambda j: (0, j)),
                  pl.BlockSpec((1, tn), lambda j: (0, j))],
        out_specs=pl.BlockSpec((1, tn), lambda j: (0, j)),
        compiler_params=_params(("arbitrary",)),
        name="ada_mod",
    )(c_col, w_ada, b_ada)


def _in_proj_kernel(x_ref, gain_ref, scale_ref, shift_ref, w_ref, ws_ref, *rest, row_chunk, n_side):
    side_in, (o_ref, ba_ref), side_out, h_ref = (rest[:n_side], rest[n_side:n_side + 2],
                                                 rest[n_side + 2:-1], rest[-1])
    for src, dst in zip(side_in, side_out):
        dst[...] = src[...].astype(dst.dtype)
    j = pl.program_id(1)

    @pl.when(j == 0)
    def _():
        gs = gain_ref[...] * (1.0 + scale_ref[...])
        sh = shift_ref[...]
        tm = x_ref.shape[0]
        for r in range(tm // row_chunk):
            rows = slice(r * row_chunk, (r + 1) * row_chunk)
            x = x_ref[rows, :]
            ms = jnp.mean(x * x, axis=-1, keepdims=True)
            h = ((x * lax.rsqrt(ms + EPS)) * gs + sh).astype(BF16)
            h_ref[rows, :] = h
            ba_ref[rows, :] = _mm_nt(h, ws_ref[...])

    o_ref[...] = _mm_nt(h_ref[...], w_ref[...]).astype(o_ref.dtype)


def _in_proj(x2d, gain, scale, shift, w_big, w_small, side_weights, *, tm, tn):
    t, d = x2d.shape
    n = w_big.shape[0]
    n_i, n_j = t // tm, n // tn
    slabs = 1 << ((n_i * n_j).bit_length() - 1)
    slab_of = lambda i, j: jnp.minimum(i * n_j + j, slabs - 1)
    side_specs = [pl.BlockSpec((w.shape[0] // slabs, w.shape[1]), lambda i, j: (slab_of(i, j), 0))
                  for w in side_weights]
    assert all(w.shape[0] % (16 * slabs) == 0 for w in side_weights)
    out = pl.pallas_call(
        functools.partial(_in_proj_kernel, row_chunk=128, n_side=len(side_weights)),
        out_shape=(jax.ShapeDtypeStruct((t, n), BF16), jax.ShapeDtypeStruct((t, LANES), F32),
                   *[jax.ShapeDtypeStruct(w.shape, BF16) for w in side_weights]),
        grid=(n_i, n_j),
        in_specs=[pl.BlockSpec((tm, d), lambda i, j: (i, 0)),
                  pl.BlockSpec((1, d), lambda i, j: (0, 0)),
                  pl.BlockSpec((1, d), lambda i, j: (0, 0)),
                  pl.BlockSpec((1, d), lambda i, j: (0, 0)),
                  pl.BlockSpec((tn, d), lambda i, j: (j, 0)),
                  pl.BlockSpec((LANES, d), lambda i, j: (0, 0)),
                  *side_specs],
        out_specs=(pl.BlockSpec((tm, tn), lambda i, j: (i, j)),
                   pl.BlockSpec((tm, LANES), lambda i, j: (i, 0)),
                   *side_specs),
        scratch_shapes=[pltpu.VMEM((tm, d), BF16)],
        compiler_params=_params(("arbitrary", "arbitrary")),
        name="in_proj",
    )(x2d, gain, scale, shift, w_big, w_small, *side_weights)
    return out[0], out[1], out[2:]


def _swa_bias_kernel(tab_ref, sink_ref, o_ref):
    q = SWA_BLOCK
    qpos = lax.broadcasted_iota(jnp.int32, (q, 2 * q), 0) + q
    kpos = lax.broadcasted_iota(jnp.int32, (q, 2 * q), 1)
    dist = qpos - kpos
    in_window = (dist >= 0) & (dist < SWA_WINDOW)
    d = jnp.maximum(dist, 0)
    max_exact = REL_BUCKETS // 2
    df = jnp.maximum(d, 1).astype(F32)
    large = max_exact + (jnp.log(df / max_exact) / math.log(REL_MAX_DIST / max_exact)
                         * (REL_BUCKETS - max_exact)).astype(jnp.int32)
    large = jnp.minimum(large, REL_BUCKETS - 1)
    bucket = jnp.where(in_window, jnp.where(d < max_exact, d, large), REL_BUCKETS)
    for h in range(SWA_HEADS):
        acc = jnp.full((q, 2 * q), NEG_BIG, F32)
        for b in range(REL_BUCKETS):
            acc = jnp.where(bucket == b, tab_ref[h, b], acc)
        o_ref[h] = (jnp.where(kpos == 0, sink_ref[h], acc) * LOG2E).astype(o_ref.dtype)


def _swa_bias(rel_bias_t, sinks):
    q = SWA_BLOCK
    return pl.pallas_call(
        _swa_bias_kernel,
        out_shape=jax.ShapeDtypeStruct((SWA_HEADS, q, 2 * q), BF16),
        in_specs=[pl.BlockSpec(memory_space=pltpu.SMEM), pl.BlockSpec(memory_space=pltpu.SMEM)],
        out_specs=pl.BlockSpec(memory_space=pltpu.VMEM),
        compiler_params=pltpu.CompilerParams(vmem_limit_bytes=VMEM_LIMIT),
        name="swa_bias",
    )(rel_bias_t, sinks)


def _chunk_cumsum_rows(x):
    n = x.shape[0]
    row = lax.broadcasted_iota(jnp.int32, x.shape, 0)
    s = 1
    while s < n:
        x = x + jnp.where(row >= s, pltpu.roll(x, s, axis=0), 0.0)
        s *= 2
    return x


def _unit_lower_inverses(l_mats, masks_ref):
    eye = masks_ref[0].astype(F32)
    l_bf = [l.astype(BF16) for l in l_mats]
    m0 = [l * masks_ref[1] for l in l_bf]
    x = [eye - m for m in m0]
    p = [_mm(m, m) for m in m0]
    x = [xi + _mm(xi, pi) for xi, pi in zip(x, p)]
    p = [_mm(pi, pi) for pi in p]
    x = [xi + _mm(xi, pi) for xi, pi in zip(x, p)]
    for lvl in range(2, masks_ref.shape[0]):
        nx = [_mm(l * masks_ref[lvl], xi) for l, xi in zip(l_bf, x)]
        x = [xi - _mm(xi, ni) for xi, ni in zip(x, nx)]
    return x


def _gdn_kernel(cur_ref, prev_ref, z_ref, ba_ref, convw_ref, band_ref, alog_ref, dtb_ref, gain_ref,
                masks_ref, o_ref, s_ref, q_sc, k_sc, kb_sc, vb_sc, kbg_sc, qd_sc, kd_sc, dec_sc, el_sc):
    s = pl.program_id(0)
    c = GDN_BLOCK
    heads = range(GDN_HEADS)
    blocks = range(GDN_STEP_BLOCKS)
    items = [(b, h) for b in blocks for h in heads]
    at = lambda b, h: b * GDN_HEADS + h
    lane_of = lambda a, h: a[:, GDN_HEADS + h:GDN_HEADS + h + 1]

    @pl.when(s == 0)
    def _():
        for ref in (s_ref, q_sc, k_sc, kb_sc, vb_sc, kbg_sc, qd_sc, kd_sc, dec_sc, el_sc):
            ref[...] = jnp.zeros_like(ref)

    pad = prev_ref.shape[0]
    conv_w = convw_ref[...].astype(BF16)
    taps = []
    for b in blocks:
        if b == 0:
            ctx = jnp.where(s > 0, prev_ref[...], jnp.zeros_like(prev_ref))
        else:
            ctx = cur_ref[b * c - pad:b * c, :]
        xcat = jnp.concatenate([ctx, cur_ref[b * c:(b + 1) * c, :]], axis=0)
        taps.append(jnp.concatenate([xcat * conv_w[i:i + 1, :] for i in range(GDN_CONV)], axis=0))

    row = lax.broadcasted_iota(jnp.int32, (c, c), 0)
    col = lax.broadcasted_iota(jnp.int32, (c, c), 1)
    kk = [_mm_nt(kb_sc[at(b, h)], k_sc[at(b, h)]) for b, h in items]
    qk = [_mm_nt(q_sc[at(b, h)], k_sc[at(b, h)]) for b, h in items]
    conv_pieces = {}
    for p in range(GDN_CONV_CH // GDN_CONV_PIECE):
        for b in blocks:
            cols = slice(p * GDN_CONV_PIECE, (p + 1) * GDN_CONV_PIECE)
            conv_pieces[b, p] = jnp.dot(band_ref[...], taps[b][:, cols], preferred_element_type=F32)
    l_mat = [kk[at(b, h)] * dec_sc[at(b, h)] for b, h in items]
    attn = [(qk[at(b, h)] * dec_sc[at(b, h)]).astype(BF16) for b, h in items]
    t_inv = _unit_lower_inverses(l_mat, masks_ref)
    state = [s_ref[h] for h in heads]
    for b in blocks:
        el = el_sc[b]
        s_bf = [state[h].astype(BF16) for h in heads]
        resid = [vb_sc[at(b, h)] - _mm(kbg_sc[at(b, h)], s_bf[h]) for h in heads]
        v_new = [_mm(t_inv[at(b, h)], resid[h]).astype(BF16) for h in heads]
        o = [_mm(jnp.concatenate([qd_sc[at(b, h)], attn[at(b, h)]], axis=1),
                 jnp.concatenate([s_bf[h], v_new[h]], axis=0)) for h in heads]
        state = [state[h] * lane_of(el, h) + _mm_tn(kd_sc[at(b, h)], v_new[h]) for h in heads]
        for h in heads:
            on = o[h] * lax.rsqrt(jnp.mean(o[h] * o[h], axis=-1, keepdims=True) + EPS) * gain_ref[...]
            z = z_ref[b * c:(b + 1) * c, h * GDN_DV:(h + 1) * GDN_DV].astype(F32)
            o_ref[b * c:(b + 1) * c, h * GDN_DV:(h + 1) * GDN_DV] = (on * _silu(z)).astype(o_ref.dtype)
    for h in heads:
        s_ref[h] = state[h]

    causal = row >= col

    def l2n(a, scale=1.0):
        return a * (lax.rsqrt(jnp.sum(a * a, axis=-1, keepdims=True) + EPS) * scale)

    for b in blocks:
        ba = ba_ref[b * c:(b + 1) * c, :]
        beta_all = _sigmoid(ba)
        xg = ba + dtb_ref[...]
        softplus = jnp.maximum(xg, 0.0) + jnp.log(1.0 + jnp.exp(-jnp.abs(xg)))
        g_all = -jnp.exp(alog_ref[...]) * softplus
        gc = _chunk_cumsum_rows(g_all)
        gc_t = gc.T
        g_last = gc[c - 1:c, :]
        eg_all = jnp.exp(gc)
        ekd_all = jnp.exp(g_last - gc)
        conv_tile = lambda col0: _silu(conv_pieces[b, col0 // GDN_CONV_PIECE][
            :, col0 % GDN_CONV_PIECE:col0 % GDN_CONV_PIECE + LANES])
        for h in heads:
            q = l2n(conv_tile(h * GDN_DK), GDN_DK ** -0.5)
            k = l2n(conv_tile(GDN_QK + h * GDN_DK))
            v = conv_tile(2 * GDN_QK + h * GDN_DV)
            beta = beta_all[:, h:h + 1]
            eg = lane_of(eg_all, h)
            kb = k * beta
            gdiff = lane_of(gc, h) - gc_t[GDN_HEADS + h:GDN_HEADS + h + 1, :]
            q_sc[at(b, h)] = q.astype(BF16)
            k_sc[at(b, h)] = k.astype(BF16)
            kb_sc[at(b, h)] = kb.astype(BF16)
            vb_sc[at(b, h)] = v * beta
            kbg_sc[at(b, h)] = (kb * eg).astype(BF16)
            qd_sc[at(b, h)] = (q * eg).astype(BF16)
            kd_sc[at(b, h)] = (k * lane_of(ekd_all, h)).astype(BF16)
            dec_sc[at(b, h)] = jnp.where(causal, jnp.exp(gdiff), 0.0)
        el_sc[b] = jnp.exp(g_last)


def _gdn_masks():
    c = GDN_BLOCK
    r = np.arange(c)[:, None]
    k = np.arange(c)[None, :]
    mats = [np.eye(c), (r // 8 == k // 8) & (r > k)]
    b = 8
    while b < c:
        mats.append((r // (2 * b) == k // (2 * b)) & ((r // b) % 2 == 1) & ((k // b) % 2 == 0))
        b *= 2
    return jnp.asarray(np.stack([np.asarray(m, np.float32) for m in mats]), BF16)


def _conv_band(c, pad):
    band = np.zeros((c, GDN_CONV * (pad + c)), np.float32)
    for i in range(GDN_CONV):
        band[np.arange(c), i * (pad + c) + pad + np.arange(c) - (GDN_CONV - 1) + i] = 1.0
    return jnp.asarray(band, BF16)


def _gdn(proj, ba, conv_w, alog_lane, dtb_lane, gain):
    t = proj.shape[0]
    c = GDN_BLOCK
    rows = GDN_STEP_BLOCKS * c
    pad = 16
    masks = _gdn_masks()
    nm = masks.shape[0]
    band = _conv_band(c, pad)
    n_steps = t // rows
    staged = lambda i: jnp.minimum(i, n_steps - 1)
    chained = lambda i: jnp.maximum(i - 1, 0)
    item_sq = lambda w, dt: pltpu.VMEM((GDN_STEP_BLOCKS * GDN_HEADS, c, w), dt)
    return pl.pallas_call(
        _gdn_kernel,
        out_shape=jax.ShapeDtypeStruct((t, GDN_V), BF16),
        grid=(n_steps + 1,),
        in_specs=[pl.BlockSpec((rows, GDN_CONV_CH), lambda i: (staged(i), 0)),
                  pl.BlockSpec((pad, GDN_CONV_CH),
                               lambda i: (jnp.maximum(staged(i) * (rows // pad) - 1, 0), 0)),
                  pl.BlockSpec((rows, GDN_V), lambda i: (chained(i), COL_ZA // GDN_V)),
                  pl.BlockSpec((rows, LANES), lambda i: (staged(i), 0)),
                  pl.BlockSpec((GDN_CONV, GDN_CONV_CH), lambda i: (0, 0)),
                  pl.BlockSpec(band.shape, lambda i: (0, 0)),
                  pl.BlockSpec((1, LANES), lambda i: (0, 0)),
                  pl.BlockSpec((1, LANES), lambda i: (0, 0)),
                  pl.BlockSpec((1, GDN_DV), lambda i: (0, 0)),
                  pl.BlockSpec((nm, c, c), lambda i: (0, 0, 0))],
        out_specs=pl.BlockSpec((rows, GDN_V), lambda i: (chained(i), 0)),
        scratch_shapes=[pltpu.VMEM((GDN_HEADS, GDN_DK, GDN_DV), F32),
                        item_sq(GDN_DK, BF16), item_sq(GDN_DK, BF16), item_sq(GDN_DK, BF16),
                        item_sq(GDN_DV, F32), item_sq(GDN_DK, BF16),
                        item_sq(GDN_DK, BF16), item_sq(GDN_DK, BF16),
                        item_sq(c, F32),
                        pltpu.VMEM((GDN_STEP_BLOCKS, 1, LANES), F32)],
        compiler_params=_params(("arbitrary",)),
        name="gdn",
    )(proj, proj, proj, ba, conv_w, band, alog_lane, dtb_lane, gain, masks)


def _swa_kernel(q_ref, kc_ref, kp_ref, vc_ref, vp_ref, z_ref, bias_ref, qg_ref, kg_ref, seg_ref, o_ref):
    n = pl.program_id(0)
    qb = SWA_BLOCK
    hd = SWA_HEAD_DIM
    n_tiles = SWA_Q // LANES
    blocks = range(SWA_STEP_BLOCKS)
    items = [(j, tq) for j in blocks for tq in range(n_tiles)]
    lane = lax.broadcasted_iota(jnp.int32, (1, LANES), 1)
    lo = lane < hd
    seg = seg_ref[...].astype(BF16)

    def qk_norm(a, gain):
        ms = jnp.dot((a * a).astype(BF16), seg, preferred_element_type=F32)
        return a * (lax.rsqrt(ms + EPS) * gain)

    def two_heads(lo_part, hi_part):
        return jnp.concatenate([jnp.where(lo, lo_part, 0.0), jnp.where(lo, 0.0, hi_part)], axis=0)

    kall = jnp.concatenate([kp_ref[...], kc_ref[...]], axis=0).astype(F32)
    vall = jnp.concatenate([vp_ref[...], vc_ref[...]], axis=0).astype(F32)
    kn, vv = {}, {}
    for u in range(SWA_KV // LANES):
        knorm = qk_norm(kall[:, u * LANES:(u + 1) * LANES], kg_ref[...])
        vtile = vall[:, u * LANES:(u + 1) * LANES]
        for ci in range(SWA_STEP_BLOCKS + 1):
            kc = knorm[ci * qb:(ci + 1) * qb]
            vc = vtile[ci * qb:(ci + 1) * qb]
            kn[u, ci] = (kc, pltpu.roll(kc, hd, axis=1))
            vv[u, ci] = (vc, pltpu.roll(vc, hd, axis=1))

    row = lax.broadcasted_iota(jnp.int32, (qb, 1), 0)
    is_sink = row == 0
    k2s, rhs = {}, {}
    for j in blocks:
        prev_live = (n * SWA_STEP_BLOCKS + j) > 0
        ones_prev = jnp.where(prev_live | is_sink, 1.0, 0.0)
        ones_band = jnp.concatenate([ones_prev, jnp.ones_like(ones_prev)], axis=0)
        ones2 = two_heads(ones_band, ones_band)
        for u in range(SWA_KV // LANES):
            kband = [jnp.concatenate([jnp.where(is_sink, 0.0, kn[u, j][r]), kn[u, j + 1][r]], axis=0)
                     for r in range(2)]
            vband = [jnp.concatenate([jnp.where(prev_live & ~is_sink, vv[u, j][r], 0.0), vv[u, j + 1][r]],
                                     axis=0) for r in range(2)]
            for half in range(2):
                g = 2 * u + half
                k2s[j, g] = two_heads(kband[half], kband[1 - half]).astype(BF16)
                v2 = two_heads(vband[half], vband[1 - half])
                rhs[j, g] = jnp.concatenate([v2, ones2], axis=1).astype(BF16)

    q_gain = qg_ref[...] * (hd ** -0.5 * LOG2E)
    kv_of = lambda tq: (2 * tq) // (SWA_HEADS // SWA_KV_HEADS)
    qt = {(j, tq): qk_norm(q_ref[j * qb:(j + 1) * qb, tq * LANES:(tq + 1) * LANES].astype(F32), q_gain)
          for j, tq in items}
    logits = {(j, tq): _mm_nt(qt[j, tq], k2s[j, kv_of(tq)]) for j, tq in items}
    probs = {}
    for j, tq in items:
        ps = []
        for e in range(2):
            s = logits[j, tq][:, e * 2 * qb:(e + 1) * 2 * qb].astype(BF16) + bias_ref[2 * tq + e]
            ps.append(jnp.exp2(s - jnp.max(s, axis=-1, keepdims=True)))
        probs[j, tq] = jnp.concatenate(ps, axis=1)
    pv = {(j, tq): jnp.dot(probs[j, tq], rhs[j, kv_of(tq)], preferred_element_type=F32) for j, tq in items}
    for j, tq in items:
        out = pv[j, tq][:, :LANES] * (1.0 / pv[j, tq][:, LANES:])
        z = z_ref[j * qb:(j + 1) * qb, tq * LANES:(tq + 1) * LANES].astype(F32)
        o_ref[j * qb:(j + 1) * qb, tq * LANES:(tq + 1) * LANES] = (out * _silu(z)).astype(o_ref.dtype)


def _swa(proj, bias, qg2, kg2):
    t = proj.shape[0]
    qb = SWA_BLOCK
    rows = SWA_STEP_BLOCKS * qb
    seg = np.kron(np.eye(LANES // SWA_HEAD_DIM), np.ones((SWA_HEAD_DIM, SWA_HEAD_DIM))) / SWA_HEAD_DIM
    seg = jnp.asarray(seg, F32)
    kcol = COL_KB // SWA_KV
    vcol = COL_VB // SWA_KV
    prev = lambda i: jnp.maximum(i * SWA_STEP_BLOCKS - 1, 0)
    return pl.pallas_call(
        _swa_kernel,
        out_shape=jax.ShapeDtypeStruct((t, SWA_Q), BF16),
        grid=(t // rows,),
        in_specs=[pl.BlockSpec((rows, SWA_Q), lambda i: (i, COL_QB // SWA_Q)),
                  pl.BlockSpec((rows, SWA_KV), lambda i: (i, kcol)),
                  pl.BlockSpec((qb, SWA_KV), lambda i: (prev(i), kcol)),
                  pl.BlockSpec((rows, SWA_KV), lambda i: (i, vcol)),
                  pl.BlockSpec((qb, SWA_KV), lambda i: (prev(i), vcol)),
                  pl.BlockSpec((rows, SWA_Q), lambda i: (i, COL_ZB // SWA_Q)),
                  pl.BlockSpec((SWA_HEADS, qb, 2 * qb), lambda i: (0, 0, 0)),
                  pl.BlockSpec((1, LANES), lambda i: (0, 0)),
                  pl.BlockSpec((1, LANES), lambda i: (0, 0)),
                  pl.BlockSpec((LANES, LANES), lambda i: (0, 0))],
        out_specs=pl.BlockSpec((rows, SWA_Q), lambda i: (i, 0)),
        compiler_params=_params(("arbitrary",)),
        name="swa",
    )(proj, proj, proj, proj, proj, proj, bias, qg2, kg2, seg)


def _merge_out_kernel(x_ref, oa_ref, ob_ref, ga_ref, gb_ref, gate_ref, wa_ref, wb_ref, wo_ref, o_ref):
    ya = jnp.dot(oa_ref[...], wa_ref[...], preferred_element_type=F32)
    yb = jnp.dot(ob_ref[...], wb_ref[...], preferred_element_type=F32)
    mixed = _sigmoid(ga_ref[...].astype(F32)) * ya + _sigmoid(gb_ref[...].astype(F32)) * yb
    y = jnp.dot(mixed.astype(BF16), wo_ref[...], preferred_element_type=F32)
    o_ref[...] = x_ref[...] + gate_ref[...] * y


def _merge_out(x2d, o_a, o_b, proj, gate, w_a, w_b, w_o, *, tm):
    t, d = x2d.shape
    const = lambda shape: pl.BlockSpec(shape, lambda i: (0, 0), pipeline_mode=pl.Buffered(1))
    return pl.pallas_call(
        _merge_out_kernel,
        out_shape=jax.ShapeDtypeStruct((t, d), F32),
        grid=(t // tm,),
        in_specs=[pl.BlockSpec((tm, d), lambda i: (i, 0)),
                  pl.BlockSpec((tm, GDN_V), lambda i: (i, 0)),
                  pl.BlockSpec((tm, SWA_Q), lambda i: (i, 0)),
                  pl.BlockSpec((tm, d), lambda i: (i, COL_GA // D_MODEL)),
                  pl.BlockSpec((tm, d), lambda i: (i, COL_GB // D_MODEL)),
                  pl.BlockSpec((1, d), lambda i: (0, 0)),
                  const((GDN_V, d)), const((SWA_Q, d)), const((d, d))],
        out_specs=pl.BlockSpec((tm, d), lambda i: (i, 0)),
        compiler_params=_params(("arbitrary",)),
        name="merge_out",
    )(x2d, o_a, o_b, proj, proj, gate, w_a, w_b, w_o)


REPACK_TILE = 512
N_GATE_COLS = 2 * GDN_HEADS


def _repack_plan():
    src_of = {COL_QKV: 0, COL_ZA: GDN_CONV_CH}
    after_gates = COL_QB + N_GATE_COLS
    src_of.update({COL_QB: after_gates, COL_KB: after_gates + SWA_Q, COL_VB: after_gates + SWA_Q + SWA_KV,
                   COL_ZB: after_gates + SWA_Q + 2 * SWA_KV, COL_GA: after_gates + 2 * SWA_Q + 2 * SWA_KV,
                   COL_GB: after_gates + 2 * SWA_Q + 2 * SWA_KV + D_MODEL})
    starts = sorted(src_of)
    src = []
    for dst in range(0, PROJ_COLS, REPACK_TILE):
        grp = max(s for s in starts if s <= dst)
        src.append(src_of[grp] + dst - grp)
    return np.asarray(src, np.int32)


def _repack_kernel(tab_ref, w_ref, g_ref, big_ref, small_ref):
    del tab_ref
    big_ref[...] = w_ref[...].astype(BF16)

    @pl.when(pl.program_id(0) == 0)
    def _():
        row = lax.broadcasted_iota(jnp.int32, (LANES, 1), 0)
        small_ref[...] = jnp.where(row < N_GATE_COLS, g_ref[...], 0.0).astype(BF16)


def _repack_w_in(w_t):
    d = w_t.shape[1]
    src = _repack_plan()
    gate_row = COL_QB
    assert gate_row % LANES == 0 and np.all(src % N_GATE_COLS == 0)
    return pl.pallas_call(
        _repack_kernel,
        out_shape=(jax.ShapeDtypeStruct((PROJ_COLS, d), BF16), jax.ShapeDtypeStruct((LANES, d), BF16)),
        grid_spec=pltpu.PrefetchScalarGridSpec(
            num_scalar_prefetch=1,
            grid=(PROJ_COLS // REPACK_TILE,),
            in_specs=[pl.BlockSpec((pl.Element(REPACK_TILE), pl.Element(d)),
                                   lambda o, tab: (tab[o] * N_GATE_COLS, 0)),
                      pl.BlockSpec((LANES, d), lambda o, tab: (gate_row // LANES, 0))],
            out_specs=(pl.BlockSpec((REPACK_TILE, d), lambda o, tab: (o, 0)),
                       pl.BlockSpec((LANES, d), lambda o, tab: (0, 0)))),
        compiler_params=_params(("arbitrary",)),
        name="repack_w_in",
    )(jnp.asarray(src // N_GATE_COLS), w_t, w_t)


def _lane_row(vec, offset):
    return jnp.pad(vec.astype(F32), (offset, LANES - offset - vec.shape[0]))[None, :]


def kernel(x, c, w_ada, b_ada, norm_gain, w_in, conv_w, a_log, dt_bias, gdn_norm_gain, q_norm_gain,
           k_norm_gain, sinks, rel_bias, w_branch_gdn, w_branch_swa, w_out):
    bsz, t, d = x.shape
    depth = w_in.shape[0]
    outs = []
    for b in range(bsz):
        xb = x[b]
        c_col = c[b].astype(F32)[:, None]
        for l in range(depth):
            mod = _ada_mod(c_col, w_ada[l], b_ada[l][None, :])
            shift, scale, gate = mod[:, :d], mod[:, d:2 * d], mod[:, 2 * d:]
            w_big, w_small = _repack_w_in(w_in[l].T)
            proj, ba, (w_a, w_b, w_o) = _in_proj(
                xb, norm_gain[l][None, :], scale, shift, w_big, w_small,
                (w_branch_gdn[l], w_branch_swa[l], w_out[l]), tm=min(1024, t), tn=1792)
            o_a = _gdn(proj, ba, conv_w[l], _lane_row(a_log[l], GDN_HEADS), _lane_row(dt_bias[l], GDN_HEADS),
                       gdn_norm_gain[l][None, :])
            bias = _swa_bias(rel_bias.T.astype(F32), sinks[l].astype(F32))
            o_b = _swa(proj, bias,
                       jnp.tile(q_norm_gain[l], LANES // SWA_HEAD_DIM)[None, :],
                       jnp.tile(k_norm_gain[l], LANES // SWA_HEAD_DIM)[None, :])
            xb = _merge_out(xb, o_a, o_b, proj, gate, w_a, w_b, w_o, tm=min(512, t))
        outs.append(xb)
    return jnp.stack(outs, axis=0)
```

```python
import functools
import math

import jax
import jax.numpy as jnp
import numpy as np
from jax import lax
from jax.experimental import pallas as pl
from jax.experimental.pallas import tpu as pltpu

F32 = jnp.float32
BF16 = jnp.bfloat16

LANES = 128
D_MODEL = 2048
GDN_HEADS = 8
GDN_DK = 128
GDN_DV = 128
GDN_CONV = 4
GDN_QK = GDN_HEADS * GDN_DK
GDN_V = GDN_HEADS * GDN_DV
GDN_CONV_CH = 2 * GDN_QK + GDN_V
GDN_BLOCK = 128
GDN_STEP_BLOCKS = 2
GDN_CONV_PIECE = 512
SWA_HEADS = 16
SWA_KV_HEADS = 4
SWA_HEAD_DIM = 64
SWA_WINDOW = 128
SWA_BLOCK = 128
SWA_STEP_BLOCKS = 4
SWA_Q = SWA_HEADS * SWA_HEAD_DIM
SWA_KV = SWA_KV_HEADS * SWA_HEAD_DIM
REL_BUCKETS = 32
REL_MAX_DIST = 128
EPS = 1e-6
NEG_BIG = -1e30
LOG2E = math.log2(math.e)

COL_QKV = 0
COL_ZA = GDN_CONV_CH
COL_QB = COL_ZA + GDN_V
COL_ZB = COL_QB + SWA_Q
COL_GA = COL_ZB + SWA_Q
COL_GB = COL_GA + D_MODEL
COL_KB = COL_GB + D_MODEL
COL_VB = COL_KB + SWA_KV
PROJ_COLS = COL_VB + SWA_KV
for _col, _width in ((COL_ZA, GDN_V), (COL_QB, SWA_Q), (COL_ZB, SWA_Q), (COL_GA, D_MODEL),
                     (COL_GB, D_MODEL), (COL_KB, SWA_KV), (COL_VB, SWA_KV)):
    assert _col % _width == 0

VMEM_LIMIT = 56 * 1024 * 1024


def _sigmoid(x):
    return 0.5 + 0.5 * jnp.tanh(0.5 * x)


def _silu(x):
    half = 0.5 * x
    return half + half * jnp.tanh(half)


def _params(sem):
    return pltpu.CompilerParams(dimension_semantics=sem, vmem_limit_bytes=VMEM_LIMIT)


def _mm(a, b):
    return jnp.dot(a.astype(BF16), b.astype(BF16), preferred_element_type=F32)


def _mm_nt(a, b):
    return lax.dot_general(a.astype(BF16), b.astype(BF16), (((1,), (1,)), ((), ())),
                           preferred_element_type=F32)


def _mm_tn(a, b):
    return lax.dot_general(a.astype(BF16), b.astype(BF16), (((0,), (0,)), ((), ())),
                           preferred_element_type=F32)


def _ada_mod_kernel(c_ref, w_ref, b_ref, o_ref):
    c = c_ref[...]
    o_ref[...] = jnp.sum(_silu(c) * w_ref[...], axis=0, keepdims=True) + b_ref[...]


def _ada_mod(c_col, w_ada, b_ada):
    d, n = w_ada.shape
    tn = 1024
    return pl.pallas_call(
        _ada_mod_kernel,
        out_shape=jax.ShapeDtypeStruct((1, n), F32),
        grid=(n // tn,),
        in_specs=[pl.BlockSpec((d, 1), lambda j: (0, 0)),
                  pl.BlockSpec((d, tn), lambda j: (0, j)),
                  pl.BlockSpec((1, tn), lambda j: (0, j))],
        out_specs=pl.BlockSpec((1, tn), lambda j: (0, j)),
        compiler_params=_params(("arbitrary",)),
        name="ada_mod",
    )(c_col, w_ada, b_ada)


def _in_proj_kernel(x_ref, gain_ref, scale_ref, shift_ref, w_ref, ws_ref, *rest, row_chunk, n_side):
    side_in, (o_ref, ba_ref), side_out, h_ref = (rest[:n_side], rest[n_side:n_side + 2],
                                                 rest[n_side + 2:-1], rest[-1])
    for src, dst in zip(side_in, side_out):
        dst[...] = src[...].astype(dst.dtype)
    j = pl.program_id(1)

    @pl.when(j == 0)
    def _():
        gs = gain_ref[...] * (1.0 + scale_ref[...])
        sh = shift_ref[...]
        tm = x_ref.shape[0]
        for r in range(tm // row_chunk):
            rows = slice(r * row_chunk, (r + 1) * row_chunk)
            x = x_ref[rows, :]
            ms = jnp.mean(x * x, axis=-1, keepdims=True)
            h = ((x * lax.rsqrt(ms + EPS)) * gs + sh).astype(BF16)
            h_ref[rows, :] = h
            ba_ref[rows, :] = _mm_nt(h, ws_ref[...])

    o_ref[...] = _mm_nt(h_ref[...], w_ref[...]).astype(o_ref.dtype)


def _in_proj(x2d, gain, scale, shift, w_big, w_small, side_weights, *, tm, tn):
    t, d = x2d.shape
    n = w_big.shape[0]
    n_i, n_j = t // tm, n // tn
    slabs = 1 << ((n_i * n_j).bit_length() - 1)
    slab_of = lambda i, j: jnp.minimum(i * n_j + j, slabs - 1)
    side_specs = [pl.BlockSpec((w.shape[0] // slabs, w.shape[1]), lambda i, j: (slab_of(i, j), 0))
                  for w in side_weights]
    assert all(w.shape[0] % (16 * slabs) == 0 for w in side_weights)
    out = pl.pallas_call(
        functools.partial(_in_proj_kernel, row_chunk=128, n_side=len(side_weights)),
        out_shape=(jax.ShapeDtypeStruct((t, n), BF16), jax.ShapeDtypeStruct((t, LANES), F32),
                   *[jax.ShapeDtypeStruct(w.shape, BF16) for w in side_weights]),
        grid=(n_i, n_j),
        in_specs=[pl.BlockSpec((tm, d), lambda i, j: (i, 0)),
                  pl.BlockSpec((1, d), lambda i, j: (0, 0)),
                  pl.BlockSpec((1, d), lambda i, j: (0, 0)),
                  pl.BlockSpec((1, d), lambda i, j: (0, 0)),
                  pl.BlockSpec((tn, d), lambda i, j: (j, 0)),
                  pl.BlockSpec((LANES, d), lambda i, j: (0, 0)),
                  *side_specs],
        out_specs=(pl.BlockSpec((tm, tn), lambda i, j: (i, j)),
                   pl.BlockSpec((tm, LANES), lambda i, j: (i, 0)),
                   *side_specs),
        scratch_shapes=[pltpu.VMEM((tm, d), BF16)],
        compiler_params=_params(("arbitrary", "arbitrary")),
        name="in_proj",
    )(x2d, gain, scale, shift, w_big, w_small, *side_weights)
    return out[0], out[1], out[2:]


def _swa_bias_kernel(tab_ref, sink_ref, o_ref):
    q = SWA_BLOCK
    qpos = lax.broadcasted_iota(jnp.int32, (q, 2 * q), 0) + q
    kpos = lax.broadcasted_iota(jnp.int32, (q, 2 * q), 1)
    dist = qpos - kpos
    in_window = (dist >= 0) & (dist < SWA_WINDOW)
    d = jnp.maximum(dist, 0)
    max_exact = REL_BUCKETS // 2
    df = jnp.maximum(d, 1).astype(F32)
    large = max_exact + (jnp.log(df / max_exact) / math.log(REL_MAX_DIST / max_exact)
                         * (REL_BUCKETS - max_exact)).astype(jnp.int32)
    large = jnp.minimum(large, REL_BUCKETS - 1)
    bucket = jnp.where(in_window, jnp.where(d < max_exact, d, large), REL_BUCKETS)
    for h in range(SWA_HEADS):
        acc = jnp.full((q, 2 * q), NEG_BIG, F32)
        for b in range(REL_BUCKETS):
            acc = jnp.where(bucket == b, tab_ref[h, b], acc)
        o_ref[h] = (jnp.where(kpos == 0, sink_ref[h], acc) * LOG2E).astype(o_ref.dtype)


def _swa_bias(rel_bias_t, sinks):
    q = SWA_BLOCK
    return pl.pallas_call(
        _swa_bias_kernel,
        out_shape=jax.ShapeDtypeStruct((SWA_HEADS, q, 2 * q), BF16),
        in_specs=[pl.BlockSpec(memory_space=pltpu.SMEM), pl.BlockSpec(memory_space=pltpu.SMEM)],
        out_specs=pl.BlockSpec(memory_space=pltpu.VMEM),
        compiler_params=pltpu.CompilerParams(vmem_limit_bytes=VMEM_LIMIT),
        name="swa_bias",
    )(rel_bias_t, sinks)


def _chunk_cumsum_rows(x):
    n = x.shape[0]
    row = lax.broadcasted_iota(jnp.int32, x.shape, 0)
    s = 1
    while s < n:
        x = x + jnp.where(row >= s, pltpu.roll(x, s, axis=0), 0.0)
        s *= 2
    return x


def _unit_lower_inverses(l_mats, masks_ref):
    eye = masks_ref[0].astype(F32)
    l_bf = [l.astype(BF16) for l in l_mats]
    m0 = [l * masks_ref[1] for l in l_bf]
    x = [eye - m for m in m0]
    p = [_mm(m, m) for m in m0]
    x = [xi + _mm(xi, pi) for xi, pi in zip(x, p)]
    p = [_mm(pi, pi) for pi in p]
    x = [xi + _mm(xi, pi) for xi, pi in zip(x, p)]
    for lvl in range(2, masks_ref.shape[0]):
        nx = [_mm(l * masks_ref[lvl], xi) for l, xi in zip(l_bf, x)]
        x = [xi - _mm(xi, ni) for xi, ni in zip(x, nx)]
    return x


def _gdn_kernel(cur_ref, prev_ref, z_ref, ba_ref, convw_ref, band_ref, alog_ref, dtb_ref, gain_ref,
                masks_ref, o_ref, s_ref):
    s = pl.program_id(0)
    c = GDN_BLOCK
    heads = range(GDN_HEADS)
    blocks = range(GDN_STEP_BLOCKS)
    items = [(b, h) for b in blocks for h in heads]
    lane_of = lambda a, h: a[:, GDN_HEADS + h:GDN_HEADS + h + 1]

    @pl.when(s == 0)
    def _():
        s_ref[...] = jnp.zeros_like(s_ref)

    pad = prev_ref.shape[0]
    conv_w = convw_ref[...].astype(BF16)
    conv_pieces = {}
    for b in blocks:
        if b == 0:
            ctx = jnp.where(s > 0, prev_ref[...], jnp.zeros_like(prev_ref))
        else:
            ctx = cur_ref[b * c - pad:b * c, :]
        xcat = jnp.concatenate([ctx, cur_ref[b * c:(b + 1) * c, :]], axis=0)
        taps = jnp.concatenate([xcat * conv_w[i:i + 1, :] for i in range(GDN_CONV)], axis=0)
        for p in range(GDN_CONV_CH // GDN_CONV_PIECE):
            cols = slice(p * GDN_CONV_PIECE, (p + 1) * GDN_CONV_PIECE)
            conv_pieces[b, p] = jnp.dot(band_ref[...], taps[:, cols], preferred_element_type=F32)

    row = lax.broadcasted_iota(jnp.int32, (c, c), 0)
    col = lax.broadcasted_iota(jnp.int32, (c, c), 1)
    causal = row >= col

    def l2n(a, scale=1.0):
        return a * (lax.rsqrt(jnp.sum(a * a, axis=-1, keepdims=True) + EPS) * scale)

    q, k, kb, vb, kbg, qd, kd, dec, el = ({} for _ in range(9))
    for b in blocks:
        ba = ba_ref[b * c:(b + 1) * c, :]
        beta_all = _sigmoid(ba)
        xg = ba + dtb_ref[...]
        softplus = jnp.maximum(xg, 0.0) + jnp.log(1.0 + jnp.exp(-jnp.abs(xg)))
        g_all = -jnp.exp(alog_ref[...]) * softplus
        gc = _chunk_cumsum_rows(g_all)
        gc_t = gc.T
        g_last = gc[c - 1:c, :]
        eg_all = jnp.exp(gc)
        ekd_all = jnp.exp(g_last - gc)
        el[b] = jnp.exp(g_last)
        conv_tile = lambda col0: _silu(conv_pieces[b, col0 // GDN_CONV_PIECE][
            :, col0 % GDN_CONV_PIECE:col0 % GDN_CONV_PIECE + LANES])
        for h in heads:
            qf = l2n(conv_tile(h * GDN_DK), GDN_DK ** -0.5)
            kf = l2n(conv_tile(GDN_QK + h * GDN_DK))
            vf = conv_tile(2 * GDN_QK + h * GDN_DV)
            beta = beta_all[:, h:h + 1]
            eg = lane_of(eg_all, h)
            kbf = kf * beta
            gdiff = lane_of(gc, h) - gc_t[GDN_HEADS + h:GDN_HEADS + h + 1, :]
            q[b, h] = qf.astype(BF16)
            k[b, h] = kf.astype(BF16)
            kb[b, h] = kbf.astype(BF16)
            vb[b, h] = vf * beta
            kbg[b, h] = (kbf * eg).astype(BF16)
            qd[b, h] = (qf * eg).astype(BF16)
            kd[b, h] = (kf * lane_of(ekd_all, h)).astype(BF16)
            dec[b, h] = jnp.where(causal, jnp.exp(gdiff), 0.0)

    kk = [_mm_nt(kb[i], k[i]) for i in items]
    qk = [_mm_nt(q[i], k[i]) for i in items]
    l_mat = [kk[n] * dec[i] for n, i in enumerate(items)]
    attn = {i: (qk[n] * dec[i]).astype(BF16) for n, i in enumerate(items)}
    t_inv = dict(zip(items, _unit_lower_inverses(l_mat, masks_ref)))
    state = [s_ref[h] for h in heads]
    for b in blocks:
        s_bf = [state[h].astype(BF16) for h in heads]
        resid = [vb[b, h] - _mm(kbg[b, h], s_bf[h]) for h in heads]
        v_new = [_mm(t_inv[b, h], resid[h]).astype(BF16) for h in heads]
        o = [_mm(jnp.concatenate([qd[b, h], attn[b, h]], axis=1),
                 jnp.concatenate([s_bf[h], v_new[h]], axis=0)) for h in heads]
        state = [state[h] * lane_of(el[b], h) + _mm_tn(kd[b, h], v_new[h]) for h in heads]
        for h in heads:
            on = o[h] * lax.rsqrt(jnp.mean(o[h] * o[h], axis=-1, keepdims=True) + EPS) * gain_ref[...]
            z = z_ref[b * c:(b + 1) * c, h * GDN_DV:(h + 1) * GDN_DV].astype(F32)
            o_ref[b * c:(b + 1) * c, h * GDN_DV:(h + 1) * GDN_DV] = (on * _silu(z)).astype(o_ref.dtype)
    for h in heads:
        s_ref[h] = state[h]


def _gdn_masks():
    c = GDN_BLOCK
    r = np.arange(c)[:, None]
    k = np.arange(c)[None, :]
    mats = [np.eye(c), (r // 8 == k // 8) & (r > k)]
    b = 8
    while b < c:
        mats.append((r // (2 * b) == k // (2 * b)) & ((r // b) % 2 == 1) & ((k // b) % 2 == 0))
        b *= 2
    return jnp.asarray(np.stack([np.asarray(m, np.float32) for m in mats]), BF16)


def _conv_band(c, pad):
    band = np.zeros((c, GDN_CONV * (pad + c)), np.float32)
    for i in range(GDN_CONV):
        band[np.arange(c), i * (pad + c) + pad + np.arange(c) - (GDN_CONV - 1) + i] = 1.0
    return jnp.asarray(band, BF16)


def _gdn(proj, ba, conv_w, alog_lane, dtb_lane, gain):
    t = proj.shape[0]
    c = GDN_BLOCK
    rows = GDN_STEP_BLOCKS * c
    pad = 16
    masks = _gdn_masks()
    nm = masks.shape[0]
    band = _conv_band(c, pad)
    return pl.pallas_call(
        _gdn_kernel,
        out_shape=jax.ShapeDtypeStruct((t, GDN_V), BF16),
        grid=(t // rows,),
        in_specs=[pl.BlockSpec((rows, GDN_CONV_CH), lambda i: (i, 0)),
                  pl.BlockSpec((pad, GDN_CONV_CH), lambda i: (jnp.maximum(i * (rows // pad) - 1, 0), 0)),
                  pl.BlockSpec((rows, GDN_V), lambda i: (i, COL_ZA // GDN_V)),
                  pl.BlockSpec((rows, LANES), lambda i: (i, 0)),
                  pl.BlockSpec((GDN_CONV, GDN_CONV_CH), lambda i: (0, 0)),
                  pl.BlockSpec(band.shape, lambda i: (0, 0)),
                  pl.BlockSpec((1, LANES), lambda i: (0, 0)),
                  pl.BlockSpec((1, LANES), lambda i: (0, 0)),
                  pl.BlockSpec((1, GDN_DV), lambda i: (0, 0)),
                  pl.BlockSpec((nm, c, c), lambda i: (0, 0, 0))],
        out_specs=pl.BlockSpec((rows, GDN_V), lambda i: (i, 0)),
        scratch_shapes=[pltpu.VMEM((GDN_HEADS, GDN_DK, GDN_DV), F32)],
        compiler_params=_params(("arbitrary",)),
        name="gdn",
    )(proj, proj, proj, ba, conv_w, band, alog_lane, dtb_lane, gain, masks)


def _swa_kernel(q_ref, kc_ref, kp_ref, vc_ref, vp_ref, z_ref, bias_ref, qg_ref, kg_ref, seg_ref, o_ref):
    n = pl.program_id(0)
    qb = SWA_BLOCK
    hd = SWA_HEAD_DIM
    n_tiles = SWA_Q // LANES
    blocks = range(SWA_STEP_BLOCKS)
    items = [(j, tq) for j in blocks for tq in range(n_tiles)]
    lane = lax.broadcasted_iota(jnp.int32, (1, LANES), 1)
    lo = lane < hd
    seg = seg_ref[...].astype(BF16)

    def qk_norm(a, gain):
        ms = jnp.dot((a * a).astype(BF16), seg, preferred_element_type=F32)
        return a * (lax.rsqrt(ms + EPS) * gain)

    def two_heads(lo_part, hi_part):
        return jnp.concatenate([jnp.where(lo, lo_part, 0.0), jnp.where(lo, 0.0, hi_part)], axis=0)

    kall = jnp.concatenate([kp_ref[...], kc_ref[...]], axis=0).astype(F32)
    vall = jnp.concatenate([vp_ref[...], vc_ref[...]], axis=0).astype(F32)
    kn, vv = {}, {}
    for u in range(SWA_KV // LANES):
        knorm = qk_norm(kall[:, u * LANES:(u + 1) * LANES], kg_ref[...])
        vtile = vall[:, u * LANES:(u + 1) * LANES]
        for ci in range(SWA_STEP_BLOCKS + 1):
            kc = knorm[ci * qb:(ci + 1) * qb]
            vc = vtile[ci * qb:(ci + 1) * qb]
            kn[u, ci] = (kc, pltpu.roll(kc, hd, axis=1))
            vv[u, ci] = (vc, pltpu.roll(vc, hd, axis=1))

    row = lax.broadcasted_iota(jnp.int32, (qb, 1), 0)
    is_sink = row == 0
    k2s, rhs = {}, {}
    for j in blocks:
        prev_live = (n * SWA_STEP_BLOCKS + j) > 0
        ones_prev = jnp.where(prev_live | is_sink, 1.0, 0.0)
        ones_band = jnp.concatenate([ones_prev, jnp.ones_like(ones_prev)], axis=0)
        ones2 = two_heads(ones_band, ones_band)
        for u in range(SWA_KV // LANES):
            kband = [jnp.concatenate([jnp.where(is_sink, 0.0, kn[u, j][r]), kn[u, j + 1][r]], axis=0)
                     for r in range(2)]
            vband = [jnp.concatenate([jnp.where(prev_live & ~is_sink, vv[u, j][r], 0.0), vv[u, j + 1][r]],
                                     axis=0) for r in range(2)]
            for half in range(2):
                g = 2 * u + half
                k2s[j, g] = two_heads(kband[half], kband[1 - half]).astype(BF16)
                v2 = two_heads(vband[half], vband[1 - half])
                rhs[j, g] = jnp.concatenate([v2, ones2], axis=1).astype(BF16)

    q_gain = qg_ref[...] * (hd ** -0.5 * LOG2E)
    kv_of = lambda tq: (2 * tq) // (SWA_HEADS // SWA_KV_HEADS)
    qt = {(j, tq): qk_norm(q_ref[j * qb:(j + 1) * qb, tq * LANES:(tq + 1) * LANES].astype(F32), q_gain)
          for j, tq in items}
    logits = {(j, tq): _mm_nt(qt[j, tq], k2s[j, kv_of(tq)]) for j, tq in items}
    probs = {}
    for j, tq in items:
        ps = []
        for e in range(2):
            s = logits[j, tq][:, e * 2 * qb:(e + 1) * 2 * qb].astype(BF16) + bias_ref[2 * tq + e]
            ps.append(jnp.exp2(s - jnp.max(s, axis=-1, keepdims=True)))
        probs[j, tq] = jnp.concatenate(ps, axis=1)
    pv = {(j, tq): jnp.dot(probs[j, tq], rhs[j, kv_of(tq)], preferred_element_type=F32) for j, tq in items}
    for j, tq in items:
        out = pv[j, tq][:, :LANES] * (1.0 / pv[j, tq][:, LANES:])
        z = z_ref[j * qb:(j + 1) * qb, tq * LANES:(tq + 1) * LANES].astype(F32)
        o_ref[j * qb:(j + 1) * qb, tq * LANES:(tq + 1) * LANES] = (out * _silu(z)).astype(o_ref.dtype)


def _swa(proj, bias, qg2, kg2):
    t = proj.shape[0]
    qb = SWA_BLOCK
    rows = SWA_STEP_BLOCKS * qb
    seg = np.kron(np.eye(LANES // SWA_HEAD_DIM), np.ones((SWA_HEAD_DIM, SWA_HEAD_DIM))) / SWA_HEAD_DIM
    seg = jnp.asarray(seg, F32)
    kcol = COL_KB // SWA_KV
    vcol = COL_VB // SWA_KV
    prev = lambda i: jnp.maximum(i * SWA_STEP_BLOCKS - 1, 0)
    return pl.pallas_call(
        _swa_kernel,
        out_shape=jax.ShapeDtypeStruct((t, SWA_Q), BF16),
        grid=(t // rows,),
        in_specs=[pl.BlockSpec((rows, SWA_Q), lambda i: (i, COL_QB // SWA_Q)),
                  pl.BlockSpec((rows, SWA_KV), lambda i: (i, kcol)),
                  pl.BlockSpec((qb, SWA_KV), lambda i: (prev(i), kcol)),
                  pl.BlockSpec((rows, SWA_KV), lambda i: (i, vcol)),
                  pl.BlockSpec((qb, SWA_KV), lambda i: (prev(i), vcol)),
                  pl.BlockSpec((rows, SWA_Q), lambda i: (i, COL_ZB // SWA_Q)),
                  pl.BlockSpec((SWA_HEADS, qb, 2 * qb), lambda i: (0, 0, 0)),
                  pl.BlockSpec((1, LANES), lambda i: (0, 0)),
                  pl.BlockSpec((1, LANES), lambda i: (0, 0)),
                  pl.BlockSpec((LANES, LANES), lambda i: (0, 0))],
        out_specs=pl.BlockSpec((rows, SWA_Q), lambda i: (i, 0)),
        compiler_params=_params(("arbitrary",)),
        name="swa",
    )(proj, proj, proj, proj, proj, proj, bias, qg2, kg2, seg)


def _merge_out_kernel(x_ref, oa_ref, ob_ref, ga_ref, gb_ref, gate_ref, wa_ref, wb_ref, wo_ref, o_ref):
    ya = jnp.dot(oa_ref[...], wa_ref[...], preferred_element_type=F32)
    yb = jnp.dot(ob_ref[...], wb_ref[...], preferred_element_type=F32)
    mixed = _sigmoid(ga_ref[...].astype(F32)) * ya + _sigmoid(gb_ref[...].astype(F32)) * yb
    y = jnp.dot(mixed.astype(BF16), wo_ref[...], preferred_element_type=F32)
    o_ref[...] = x_ref[...] + gate_ref[...] * y


def _merge_out(x2d, o_a, o_b, proj, gate, w_a, w_b, w_o, *, tm):
    t, d = x2d.shape
    const = lambda shape: pl.BlockSpec(shape, lambda i: (0, 0), pipeline_mode=pl.Buffered(1))
    return pl.pallas_call(
        _merge_out_kernel,
        out_shape=jax.ShapeDtypeStruct((t, d), F32),
        grid=(t // tm,),
        in_specs=[pl.BlockSpec((tm, d), lambda i: (i, 0)),
                  pl.BlockSpec((tm, GDN_V), lambda i: (i, 0)),
                  pl.BlockSpec((tm, SWA_Q), lambda i: (i, 0)),
                  pl.BlockSpec((tm, d), lambda i: (i, COL_GA // D_MODEL)),
                  pl.BlockSpec((tm, d), lambda i: (i, COL_GB // D_MODEL)),
                  pl.BlockSpec((1, d), lambda i: (0, 0)),
                  const((GDN_V, d)), const((SWA_Q, d)), const((d, d))],
        out_specs=pl.BlockSpec((tm, d), lambda i: (i, 0)),
        compiler_params=_params(("arbitrary",)),
        name="merge_out",
    )(x2d, o_a, o_b, proj, proj, gate, w_a, w_b, w_o)


REPACK_TILE = 512
N_GATE_COLS = 2 * GDN_HEADS


def _repack_plan():
    src_of = {COL_QKV: 0, COL_ZA: GDN_CONV_CH}
    after_gates = COL_QB + N_GATE_COLS
    src_of.update({COL_QB: after_gates, COL_KB: after_gates + SWA_Q, COL_VB: after_gates + SWA_Q + SWA_KV,
                   COL_ZB: after_gates + SWA_Q + 2 * SWA_KV, COL_GA: after_gates + 2 * SWA_Q + 2 * SWA_KV,
                   COL_GB: after_gates + 2 * SWA_Q + 2 * SWA_KV + D_MODEL})
    starts = sorted(src_of)
    src = []
    for dst in range(0, PROJ_COLS, REPACK_TILE):
        grp = max(s for s in starts if s <= dst)
        src.append(src_of[grp] + dst - grp)
    return np.asarray(src, np.int32)


def _repack_kernel(tab_ref, w_ref, g_ref, big_ref, small_ref):
    del tab_ref
    big_ref[...] = w_ref[...].astype(BF16)

    @pl.when(pl.program_id(0) == 0)
    def _():
        row = lax.broadcasted_iota(jnp.int32, (LANES, 1), 0)
        small_ref[...] = jnp.where(row < N_GATE_COLS, g_ref[...], 0.0).astype(BF16)


def _repack_w_in(w_t):
    d = w_t.shape[1]
    src = _repack_plan()
    gate_row = COL_QB
    assert gate_row % LANES == 0 and np.all(src % N_GATE_COLS == 0)
    return pl.pallas_call(
        _repack_kernel,
        out_shape=(jax.ShapeDtypeStruct((PROJ_COLS, d), BF16), jax.ShapeDtypeStruct((LANES, d), BF16)),
        grid_spec=pltpu.PrefetchScalarGridSpec(
            num_scalar_prefetch=1,
            grid=(PROJ_COLS // REPACK_TILE,),
            in_specs=[pl.BlockSpec((pl.Element(REPACK_TILE), pl.Element(d)),
                                   lambda o, tab: (tab[o] * N_GATE_COLS, 0)),
                      pl.BlockSpec((LANES, d), lambda o, tab: (gate_row // LANES, 0))],
            out_specs=(pl.BlockSpec((REPACK_TILE, d), lambda o, tab: (o, 0)),
                       pl.BlockSpec((LANES, d), lambda o, tab: (0, 0)))),
        compiler_params=_params(("arbitrary",)),
        name="repack_w_in",
    )(jnp.asarray(src // N_GATE_COLS), w_t, w_t)


def _lane_row(vec, offset):
    return jnp.pad(vec.astype(F32), (offset, LANES - offset - vec.shape[0]))[None, :]


def kernel(x, c, w_ada, b_ada, norm_gain, w_in, conv_w, a_log, dt_bias, gdn_norm_gain, q_norm_gain,
           k_norm_gain, sinks, rel_bias, w_branch_gdn, w_branch_swa, w_out):
    bsz, t, d = x.shape
    depth = w_in.shape[0]
    outs = []
    for b in range(bsz):
        xb = x[b]
        c_col = c[b].astype(F32)[:, None]
        for l in range(depth):
            mod = _ada_mod(c_col, w_ada[l], b_ada[l][None, :])
            shift, scale, gate = mod[:, :d], mod[:, d:2 * d], mod[:, 2 * d:]
            w_big, w_small = _repack_w_in(w_in[l].T)
            proj, ba, (w_a, w_b, w_o) = _in_proj(
                xb, norm_gain[l][None, :], scale, shift, w_big, w_small,
                (w_branch_gdn[l], w_branch_swa[l], w_out[l]), tm=min(1024, t), tn=1792)
            o_a = _gdn(proj, ba, conv_w[l], _lane_row(a_log[l], GDN_HEADS), _lane_row(dt_bias[l], GDN_HEADS),
                       gdn_norm_gain[l][None, :])
            bias = _swa_bias(rel_bias.T.astype(F32), sinks[l].astype(F32))
            o_b = _swa(proj, bias,
                       jnp.tile(q_norm_gain[l], LANES // SWA_HEAD_DIM)[None, :],
                       jnp.tile(k_norm_gain[l], LANES // SWA_HEAD_DIM)[None, :])
            xb = _merge_out(xb, o_a, o_b, proj, gate, w_a, w_b, w_o, tm=min(512, t))
        outs.append(xb)
    return jnp.stack(outs, axis=0)
```

```python
import functools
import math

import jax
import jax.numpy as jnp
import numpy as np
from jax import lax
from jax.experimental import pallas as pl
from jax.experimental.pallas import tpu as pltpu

F32 = jnp.float32
BF16 = jnp.bfloat16

LANES = 128
D_MODEL = 2048
GDN_HEADS = 8
GDN_DK = 128
GDN_DV = 128
GDN_CONV = 4
GDN_QK = GDN_HEADS * GDN_DK
GDN_V = GDN_HEADS * GDN_DV
GDN_CONV_CH = 2 * GDN_QK + GDN_V
GDN_BLOCK = 128
GDN_STEP_BLOCKS = 2
GDN_CONV_PIECE = 512
SWA_HEADS = 16
SWA_KV_HEADS = 4
SWA_HEAD_DIM = 64
SWA_WINDOW = 128
SWA_BLOCK = 128
SWA_STEP_BLOCKS = 4
SWA_LOGITS_AHEAD = 4
SWA_Q = SWA_HEADS * SWA_HEAD_DIM
SWA_KV = SWA_KV_HEADS * SWA_HEAD_DIM
REL_BUCKETS = 32
REL_MAX_DIST = 128
EPS = 1e-6
NEG_BIG = -1e30
LOG2E = math.log2(math.e)

COL_QKV = 0
COL_ZA = GDN_CONV_CH
COL_QB = COL_ZA + GDN_V
COL_ZB = COL_QB + SWA_Q
COL_GA = COL_ZB + SWA_Q
COL_GB = COL_GA + D_MODEL
COL_KB = COL_GB + D_MODEL
COL_VB = COL_KB + SWA_KV
PROJ_COLS = COL_VB + SWA_KV
for _col, _width in ((COL_ZA, GDN_V), (COL_QB, SWA_Q), (COL_ZB, SWA_Q), (COL_GA, D_MODEL),
                     (COL_GB, D_MODEL), (COL_KB, SWA_KV), (COL_VB, SWA_KV)):
    assert _col % _width == 0

VMEM_LIMIT = 56 * 1024 * 1024


def _sigmoid(x):
    return 0.5 + 0.5 * jnp.tanh(0.5 * x)


def _silu(x):
    half = 0.5 * x
    return half + half * jnp.tanh(half)


def _params(sem):
    return pltpu.CompilerParams(dimension_semantics=sem, vmem_limit_bytes=VMEM_LIMIT)


def _mm(a, b):
    return jnp.dot(a.astype(BF16), b.astype(BF16), preferred_element_type=F32)


def _mm_nt(a, b):
    return lax.dot_general(a.astype(BF16), b.astype(BF16), (((1,), (1,)), ((), ())),
                           preferred_element_type=F32)


def _mm_tn(a, b):
    return lax.dot_general(a.astype(BF16), b.astype(BF16), (((0,), (0,)), ((), ())),
                           preferred_element_type=F32)


def _ada_mod_kernel(c_ref, w_ref, b_ref, o_ref):
    c = c_ref[...]
    o_ref[...] = jnp.sum(_silu(c) * w_ref[...], axis=0, keepdims=True) + b_ref[...]


def _ada_mod(c_col, w_ada, b_ada):
    d, n = w_ada.shape
    tn = 1024
    return pl.pallas_call(
        _ada_mod_kernel,
        out_shape=jax.ShapeDtypeStruct((1, n), F32),
        grid=(n // tn,),
        in_specs=[pl.BlockSpec((d, 1), lambda j: (0, 0)),
                  pl.BlockSpec((d, tn), lambda j: (0, j)),
                  pl.BlockSpec((1, tn), lambda j: (0, j))],
        out_specs=pl.BlockSpec((1, tn), lambda j: (0, j)),
        compiler_params=_params(("arbitrary",)),
        name="ada_mod",
    )(c_col, w_ada, b_ada)


def _in_proj_kernel(x_ref, gain_ref, scale_ref, shift_ref, w_ref, ws_ref, *rest, row_chunk, n_side):
    side_in, (o_ref, ba_ref), side_out, h_ref = (rest[:n_side], rest[n_side:n_side + 2],
                                                 rest[n_side + 2:-1], rest[-1])
    for src, dst in zip(side_in, side_out):
        dst[...] = src[...].astype(dst.dtype)
    j = pl.program_id(1)

    @pl.when(j == 0)
    def _():
        gs = gain_ref[...] * (1.0 + scale_ref[...])
        sh = shift_ref[...]
        tm = x_ref.shape[0]
        for r in range(tm // row_chunk):
            rows = slice(r * row_chunk, (r + 1) * row_chunk)
            x = x_ref[rows, :]
            ms = jnp.mean(x * x, axis=-1, keepdims=True)
            h = ((x * lax.rsqrt(ms + EPS)) * gs + sh).astype(BF16)
            h_ref[rows, :] = h
            ba_ref[rows, :] = _mm_nt(h, ws_ref[...])

    o_ref[...] = _mm_nt(h_ref[...], w_ref[...]).astype(o_ref.dtype)


def _in_proj(x2d, gain, scale, shift, w_big, w_small, side_weights, *, tm, tn):
    t, d = x2d.shape
    n = w_big.shape[0]
    n_i, n_j = t // tm, n // tn
    slabs = 1 << ((n_i * n_j).bit_length() - 1)
    slab_of = lambda i, j: jnp.minimum(i * n_j + j, slabs - 1)
    side_specs = [pl.BlockSpec((w.shape[0] // slabs, w.shape[1]), lambda i, j: (slab_of(i, j), 0))
                  for w in side_weights]
    assert all(w.shape[0] % (16 * slabs) == 0 for w in side_weights)
    out = pl.pallas_call(
        functools.partial(_in_proj_kernel, row_chunk=128, n_side=len(side_weights)),
        out_shape=(jax.ShapeDtypeStruct((t, n), BF16), jax.ShapeDtypeStruct((t, LANES), F32),
                   *[jax.ShapeDtypeStruct(w.shape, BF16) for w in side_weights]),
        grid=(n_i, n_j),
        in_specs=[pl.BlockSpec((tm, d), lambda i, j: (i, 0)),
                  pl.BlockSpec((1, d), lambda i, j: (0, 0)),
                  pl.BlockSpec((1, d), lambda i, j: (0, 0)),
                  pl.BlockSpec((1, d), lambda i, j: (0, 0)),
                  pl.BlockSpec((tn, d), lambda i, j: (j, 0)),
                  pl.BlockSpec((LANES, d), lambda i, j: (0, 0)),
                  *side_specs],
        out_specs=(pl.BlockSpec((tm, tn), lambda i, j: (i, j)),
                   pl.BlockSpec((tm, LANES), lambda i, j: (i, 0)),
                   *side_specs),
        scratch_shapes=[pltpu.VMEM((tm, d), BF16)],
        compiler_params=_params(("arbitrary", "arbitrary")),
        name="in_proj",
    )(x2d, gain, scale, shift, w_big, w_small, *side_weights)
    return out[0], out[1], out[2:]


def _swa_bias_kernel(tab_ref, sink_ref, o_ref):
    q = SWA_BLOCK
    qpos = lax.broadcasted_iota(jnp.int32, (q, 2 * q), 0) + q
    kpos = lax.broadcasted_iota(jnp.int32, (q, 2 * q), 1)
    dist = qpos - kpos
    in_window = (dist >= 0) & (dist < SWA_WINDOW)
    d = jnp.maximum(dist, 0)
    max_exact = REL_BUCKETS // 2
    df = jnp.maximum(d, 1).astype(F32)
    large = max_exact + (jnp.log(df / max_exact) / math.log(REL_MAX_DIST / max_exact)
                         * (REL_BUCKETS - max_exact)).astype(jnp.int32)
    large = jnp.minimum(large, REL_BUCKETS - 1)
    bucket = jnp.where(in_window, jnp.where(d < max_exact, d, large), REL_BUCKETS)
    for h in range(SWA_HEADS):
        acc = jnp.full((q, 2 * q), NEG_BIG, F32)
        for b in range(REL_BUCKETS):
            acc = jnp.where(bucket == b, tab_ref[h, b], acc)
        o_ref[h] = (jnp.where(kpos == 0, sink_ref[h], acc) * LOG2E).astype(o_ref.dtype)


def _swa_bias(rel_bias_t, sinks):
    q = SWA_BLOCK
    return pl.pallas_call(
        _swa_bias_kernel,
        out_shape=jax.ShapeDtypeStruct((SWA_HEADS, q, 2 * q), BF16),
        in_specs=[pl.BlockSpec(memory_space=pltpu.SMEM), pl.BlockSpec(memory_space=pltpu.SMEM)],
        out_specs=pl.BlockSpec(memory_space=pltpu.VMEM),
        compiler_params=pltpu.CompilerParams(vmem_limit_bytes=VMEM_LIMIT),
        name="swa_bias",
    )(rel_bias_t, sinks)


def _chunk_cumsum_rows(x):
    n = x.shape[0]
    row = lax.broadcasted_iota(jnp.int32, x.shape, 0)
    s = 1
    while s < n:
        x = x + jnp.where(row >= s, pltpu.roll(x, s, axis=0), 0.0)
        s *= 2
    return x


def _unit_lower_inverses(l_mats, masks_ref):
    eye = masks_ref[0].astype(F32)
    l_bf = [l.astype(BF16) for l in l_mats]
    m0 = [l * masks_ref[1] for l in l_bf]
    x = [eye - m for m in m0]
    p = [_mm(m, m) for m in m0]
    x = [xi + _mm(xi, pi) for xi, pi in zip(x, p)]
    p = [_mm(pi, pi) for pi in p]
    x = [xi + _mm(xi, pi) for xi, pi in zip(x, p)]
    for lvl in range(2, masks_ref.shape[0]):
        nx = [_mm(l * masks_ref[lvl], xi) for l, xi in zip(l_bf, x)]
        x = [xi - _mm(xi, ni) for xi, ni in zip(x, nx)]
    return x


def _gdn_kernel(cur_ref, prev_ref, z_ref, ba_ref, convw_ref, band_ref, alog_ref, dtb_ref, gain_ref,
                masks_ref, o_ref, s_ref):
    s = pl.program_id(0)
    c = GDN_BLOCK
    heads = range(GDN_HEADS)
    blocks = range(GDN_STEP_BLOCKS)
    items = [(b, h) for b in blocks for h in heads]
    lane_of = lambda a, h: a[:, GDN_HEADS + h:GDN_HEADS + h + 1]

    @pl.when(s == 0)
    def _():
        s_ref[...] = jnp.zeros_like(s_ref)

    pad = prev_ref.shape[0]
    conv_w = convw_ref[...].astype(BF16)
    conv_pieces = {}
    for b in blocks:
        if b == 0:
            ctx = jnp.where(s > 0, prev_ref[...], jnp.zeros_like(prev_ref))
        else:
            ctx = cur_ref[b * c - pad:b * c, :]
        xcat = jnp.concatenate([ctx, cur_ref[b * c:(b + 1) * c, :]], axis=0)
        taps = jnp.concatenate([xcat * conv_w[i:i + 1, :] for i in range(GDN_CONV)], axis=0)
        for p in range(GDN_CONV_CH // GDN_CONV_PIECE):
            cols = slice(p * GDN_CONV_PIECE, (p + 1) * GDN_CONV_PIECE)
            conv_pieces[b, p] = jnp.dot(band_ref[...], taps[:, cols], preferred_element_type=F32)

    row = lax.broadcasted_iota(jnp.int32, (c, c), 0)
    col = lax.broadcasted_iota(jnp.int32, (c, c), 1)
    causal = row >= col

    def l2n(a, scale=1.0):
        return a * (lax.rsqrt(jnp.sum(a * a, axis=-1, keepdims=True) + EPS) * scale)

    q, k, kb, vb, kbg, qd, kd, dec, el = ({} for _ in range(9))
    for b in blocks:
        ba = ba_ref[b * c:(b + 1) * c, :]
        beta_all = _sigmoid(ba)
        xg = ba + dtb_ref[...]
        softplus = jnp.maximum(xg, 0.0) + jnp.log(1.0 + jnp.exp(-jnp.abs(xg)))
        g_all = -jnp.exp(alog_ref[...]) * softplus
        gc = _chunk_cumsum_rows(g_all)
        gc_t = gc.T
        g_last = gc[c - 1:c, :]
        eg_all = jnp.exp(gc)
        ekd_all = jnp.exp(g_last - gc)
        el[b] = jnp.exp(g_last)
        conv_tile = lambda col0: _silu(conv_pieces[b, col0 // GDN_CONV_PIECE][
            :, col0 % GDN_CONV_PIECE:col0 % GDN_CONV_PIECE + LANES])
        for h in heads:
            qf = l2n(conv_tile(h * GDN_DK), GDN_DK ** -0.5)
            kf = l2n(conv_tile(GDN_QK + h * GDN_DK))
            vf = conv_tile(2 * GDN_QK + h * GDN_DV)
            beta = beta_all[:, h:h + 1]
            eg = lane_of(eg_all, h)
            kbf = kf * beta
            gdiff = lane_of(gc, h) - gc_t[GDN_HEADS + h:GDN_HEADS + h + 1, :]
            q[b, h] = qf.astype(BF16)
            k[b, h] = kf.astype(BF16)
            kb[b, h] = kbf.astype(BF16)
            vb[b, h] = vf * beta
            kbg[b, h] = (kbf * eg).astype(BF16)
            qd[b, h] = (qf * eg).astype(BF16)
            kd[b, h] = (kf * lane_of(ekd_all, h)).astype(BF16)
            dec[b, h] = jnp.where(causal, jnp.exp(gdiff), 0.0)

    kk = [_mm_nt(kb[i], k[i]) for i in items]
    qk = [_mm_nt(q[i], k[i]) for i in items]
    l_mat = [kk[n] * dec[i] for n, i in enumerate(items)]
    attn = {i: (qk[n] * dec[i]).astype(BF16) for n, i in enumerate(items)}
    t_inv = dict(zip(items, _unit_lower_inverses(l_mat, masks_ref)))
    state = [s_ref[h] for h in heads]
    for b in blocks:
        s_bf = [state[h].astype(BF16) for h in heads]
        resid = [vb[b, h] - _mm(kbg[b, h], s_bf[h]) for h in heads]
        v_new = [_mm(t_inv[b, h], resid[h]).astype(BF16) for h in heads]
        o = [_mm(jnp.concatenate([qd[b, h], attn[b, h]], axis=1),
                 jnp.concatenate([s_bf[h], v_new[h]], axis=0)) for h in heads]
        state = [state[h] * lane_of(el[b], h) + _mm_tn(kd[b, h], v_new[h]) for h in heads]
        for h in heads:
            on = o[h] * lax.rsqrt(jnp.mean(o[h] * o[h], axis=-1, keepdims=True) + EPS) * gain_ref[...]
            z = z_ref[b * c:(b + 1) * c, h * GDN_DV:(h + 1) * GDN_DV].astype(F32)
            o_ref[b * c:(b + 1) * c, h * GDN_DV:(h + 1) * GDN_DV] = (on * _silu(z)).astype(o_ref.dtype)
    for h in heads:
        s_ref[h] = state[h]


def _gdn_masks():
    c = GDN_BLOCK
    r = np.arange(c)[:, None]
    k = np.arange(c)[None, :]
    mats = [np.eye(c), (r // 8 == k // 8) & (r > k)]
    b = 8
    while b < c:
        mats.append((r // (2 * b) == k // (2 * b)) & ((r // b) % 2 == 1) & ((k // b) % 2 == 0))
        b *= 2
    return jnp.asarray(np.stack([np.asarray(m, np.float32) for m in mats]), BF16)


def _conv_band(c, pad):
    band = np.zeros((c, GDN_CONV * (pad + c)), np.float32)
    for i in range(GDN_CONV):
        band[np.arange(c), i * (pad + c) + pad + np.arange(c) - (GDN_CONV - 1) + i] = 1.0
    return jnp.asarray(band, BF16)


def _gdn(proj, ba, conv_w, alog_lane, dtb_lane, gain):
    t = proj.shape[0]
    c = GDN_BLOCK
    rows = GDN_STEP_BLOCKS * c
    pad = 16
    masks = _gdn_masks()
    nm = masks.shape[0]
    band = _conv_band(c, pad)
    return pl.pallas_call(
        _gdn_kernel,
        out_shape=jax.ShapeDtypeStruct((t, GDN_V), BF16),
        grid=(t // rows,),
        in_specs=[pl.BlockSpec((rows, GDN_CONV_CH), lambda i: (i, 0)),
                  pl.BlockSpec((pad, GDN_CONV_CH), lambda i: (jnp.maximum(i * (rows // pad) - 1, 0), 0)),
                  pl.BlockSpec((rows, GDN_V), lambda i: (i, COL_ZA // GDN_V)),
                  pl.BlockSpec((rows, LANES), lambda i: (i, 0)),
                  pl.BlockSpec((GDN_CONV, GDN_CONV_CH), lambda i: (0, 0)),
                  pl.BlockSpec(band.shape, lambda i: (0, 0)),
                  pl.BlockSpec((1, LANES), lambda i: (0, 0)),
                  pl.BlockSpec((1, LANES), lambda i: (0, 0)),
                  pl.BlockSpec((1, GDN_DV), lambda i: (0, 0)),
                  pl.BlockSpec((nm, c, c), lambda i: (0, 0, 0))],
        out_specs=pl.BlockSpec((rows, GDN_V), lambda i: (i, 0)),
        scratch_shapes=[pltpu.VMEM((GDN_HEADS, GDN_DK, GDN_DV), F32)],
        compiler_params=_params(("arbitrary",)),
        name="gdn",
    )(proj, proj, proj, ba, conv_w, band, alog_lane, dtb_lane, gain, masks)


def _swa_kernel(q_ref, kc_ref, kp_ref, vc_ref, vp_ref, z_ref, bias_ref, qg_ref, kg_ref, seg_ref, o_ref):
    n = pl.program_id(0)
    qb = SWA_BLOCK
    hd = SWA_HEAD_DIM
    n_tiles = SWA_Q // LANES
    blocks = range(SWA_STEP_BLOCKS)
    items = [(j, tq) for j in blocks for tq in range(n_tiles)]
    lane = lax.broadcasted_iota(jnp.int32, (1, LANES), 1)
    lo = lane < hd
    seg = seg_ref[...].astype(BF16)

    def qk_norm(a, gain):
        ms = jnp.dot((a * a).astype(BF16), seg, preferred_element_type=F32)
        return a * (lax.rsqrt(ms + EPS) * gain)

    def two_heads(lo_part, hi_part):
        return jnp.concatenate([jnp.where(lo, lo_part, 0.0), jnp.where(lo, 0.0, hi_part)], axis=0)

    kall = jnp.concatenate([kp_ref[...], kc_ref[...]], axis=0).astype(F32)
    vall = jnp.concatenate([vp_ref[...], vc_ref[...]], axis=0).astype(F32)
    kn, vv = {}, {}
    for u in range(SWA_KV // LANES):
        knorm = qk_norm(kall[:, u * LANES:(u + 1) * LANES], kg_ref[...])
        vtile = vall[:, u * LANES:(u + 1) * LANES]
        for ci in range(SWA_STEP_BLOCKS + 1):
            kc = knorm[ci * qb:(ci + 1) * qb]
            vc = vtile[ci * qb:(ci + 1) * qb]
            kn[u, ci] = (kc, pltpu.roll(kc, hd, axis=1))
            vv[u, ci] = (vc, pltpu.roll(vc, hd, axis=1))

    row = lax.broadcasted_iota(jnp.int32, (qb, 1), 0)
    is_sink = row == 0
    k2s, rhs = {}, {}
    for j in blocks:
        prev_live = (n * SWA_STEP_BLOCKS + j) > 0
        ones_prev = jnp.where(prev_live | is_sink, 1.0, 0.0)
        ones_band = jnp.concatenate([ones_prev, jnp.ones_like(ones_prev)], axis=0)
        ones2 = two_heads(ones_band, ones_band)
        for u in range(SWA_KV // LANES):
            kband = [jnp.concatenate([jnp.where(is_sink, 0.0, kn[u, j][r]), kn[u, j + 1][r]], axis=0)
                     for r in range(2)]
            vband = [jnp.concatenate([jnp.where(prev_live & ~is_sink, vv[u, j][r], 0.0), vv[u, j + 1][r]],
                                     axis=0) for r in range(2)]
            for half in range(2):
                g = 2 * u + half
                k2s[j, g] = two_heads(kband[half], kband[1 - half]).astype(BF16)
                v2 = two_heads(vband[half], vband[1 - half])
                rhs[j, g] = jnp.concatenate([v2, ones2], axis=1).astype(BF16)

    q_gain = qg_ref[...] * (hd ** -0.5 * LOG2E)
    kv_of = lambda tq: (2 * tq) // (SWA_HEADS // SWA_KV_HEADS)
    qt = {(j, tq): qk_norm(q_ref[j * qb:(j + 1) * qb, tq * LANES:(tq + 1) * LANES].astype(F32), q_gain)
          for j, tq in items}
    logits = {}
    for n in range(len(items) + SWA_LOGITS_AHEAD):
        if n < len(items):
            j, tq = items[n]
            logits[j, tq] = _mm_nt(qt[j, tq], k2s[j, kv_of(tq)])
        if n >= SWA_LOGITS_AHEAD:
            j, tq = items[n - SWA_LOGITS_AHEAD]
            both = logits.pop((j, tq))
            ps = []
            for e in range(2):
                s = both[:, e * 2 * qb:(e + 1) * 2 * qb].astype(BF16) + bias_ref[2 * tq + e]
                ps.append(jnp.exp2(s - jnp.max(s, axis=-1, keepdims=True)))
            pv = jnp.dot(jnp.concatenate(ps, axis=1), rhs[j, kv_of(tq)], preferred_element_type=F32)
            out = pv[:, :LANES] * (1.0 / pv[:, LANES:])
            z = z_ref[j * qb:(j + 1) * qb, tq * LANES:(tq + 1) * LANES].astype(F32)
            o_ref[j * qb:(j + 1) * qb, tq * LANES:(tq + 1) * LANES] = (out * _silu(z)).astype(o_ref.dtype)


def _swa(proj, bias, qg2, kg2):
    t = proj.shape[0]
    qb = SWA_BLOCK
    rows = SWA_STEP_BLOCKS * qb
    seg = np.kron(np.eye(LANES // SWA_HEAD_DIM), np.ones((SWA_HEAD_DIM, SWA_HEAD_DIM))) / SWA_HEAD_DIM
    seg = jnp.asarray(seg, F32)
    kcol = COL_KB // SWA_KV
    vcol = COL_VB // SWA_KV
    prev = lambda i: jnp.maximum(i * SWA_STEP_BLOCKS - 1, 0)
    return pl.pallas_call(
        _swa_kernel,
        out_shape=jax.ShapeDtypeStruct((t, SWA_Q), BF16),
        grid=(t // rows,),
        in_specs=[pl.BlockSpec((rows, SWA_Q), lambda i: (i, COL_QB // SWA_Q)),
                  pl.BlockSpec((rows, SWA_KV), lambda i: (i, kcol)),
                  pl.BlockSpec((qb, SWA_KV), lambda i: (prev(i), kcol)),
                  pl.BlockSpec((rows, SWA_KV), lambda i: (i, vcol)),
                  pl.BlockSpec((qb, SWA_KV), lambda i: (prev(i), vcol)),
                  pl.BlockSpec((rows, SWA_Q), lambda i: (i, COL_ZB // SWA_Q)),
                  pl.BlockSpec((SWA_HEADS, qb, 2 * qb), lambda i: (0, 0, 0)),
                  pl.BlockSpec((1, LANES), lambda i: (0, 0)),
                  pl.BlockSpec((1, LANES), lambda i: (0, 0)),
                  pl.BlockSpec((LANES, LANES), lambda i: (0, 0))],
        out_specs=pl.BlockSpec((rows, SWA_Q), lambda i: (i, 0)),
        compiler_params=_params(("arbitrary",)),
        name="swa",
    )(proj, proj, proj, proj, proj, proj, bias, qg2, kg2, seg)


def _merge_out_kernel(x_ref, oa_ref, ob_ref, ga_ref, gb_ref, gate_ref, wa_ref, wb_ref, wo_ref, o_ref):
    ya = jnp.dot(oa_ref[...], wa_ref[...], preferred_element_type=F32)
    yb = jnp.dot(ob_ref[...], wb_ref[...], preferred_element_type=F32)
    mixed = _sigmoid(ga_ref[...].astype(F32)) * ya + _sigmoid(gb_ref[...].astype(F32)) * yb
    y = jnp.dot(mixed.astype(BF16), wo_ref[...], preferred_element_type=F32)
    o_ref[...] = x_ref[...] + gate_ref[...] * y


def _merge_out(x2d, o_a, o_b, proj, gate, w_a, w_b, w_o, *, tm):
    t, d = x2d.shape
    const = lambda shape: pl.BlockSpec(shape, lambda i: (0, 0), pipeline_mode=pl.Buffered(1))
    return pl.pallas_call(
        _merge_out_kernel,
        out_shape=jax.ShapeDtypeStruct((t, d), F32),
        grid=(t // tm,),
        in_specs=[pl.BlockSpec((tm, d), lambda i: (i, 0)),
                  pl.BlockSpec((tm, GDN_V), lambda i: (i, 0)),
                  pl.BlockSpec((tm, SWA_Q), lambda i: (i, 0)),
                  pl.BlockSpec((tm, d), lambda i: (i, COL_GA // D_MODEL)),
                  pl.BlockSpec((tm, d), lambda i: (i, COL_GB // D_MODEL)),
                  pl.BlockSpec((1, d), lambda i: (0, 0)),
                  const((GDN_V, d)), const((SWA_Q, d)), const((d, d))],
        out_specs=pl.BlockSpec((tm, d), lambda i: (i, 0)),
        compiler_params=_params(("arbitrary",)),
        name="merge_out",
    )(x2d, o_a, o_b, proj, proj, gate, w_a, w_b, w_o)


REPACK_TILE = 512
N_GATE_COLS = 2 * GDN_HEADS


def _repack_plan():
    src_of = {COL_QKV: 0, COL_ZA: GDN_CONV_CH}
    after_gates = COL_QB + N_GATE_COLS
    src_of.update({COL_QB: after_gates, COL_KB: after_gates + SWA_Q, COL_VB: after_gates + SWA_Q + SWA_KV,
                   COL_ZB: after_gates + SWA_Q + 2 * SWA_KV, COL_GA: after_gates + 2 * SWA_Q + 2 * SWA_KV,
                   COL_GB: after_gates + 2 * SWA_Q + 2 * SWA_KV + D_MODEL})
    starts = sorted(src_of)
    src = []
    for dst in range(0, PROJ_COLS, REPACK_TILE):
        grp = max(s for s in starts if s <= dst)
        src.append(src_of[grp] + dst - grp)
    return np.asarray(src, np.int32)


def _repack_kernel(tab_ref, w_ref, g_ref, big_ref, small_ref):
    del tab_ref
    big_ref[...] = w_ref[...].astype(BF16)

    @pl.when(pl.program_id(0) == 0)
    def _():
        row = lax.broadcasted_iota(jnp.int32, (LANES, 1), 0)
        small_ref[...] = jnp.where(row < N_GATE_COLS, g_ref[...], 0.0).astype(BF16)


def _repack_w_in(w_t):
    d = w_t.shape[1]
    src = _repack_plan()
    gate_row = COL_QB
    assert gate_row % LANES == 0 and np.all(src % N_GATE_COLS == 0)
    return pl.pallas_call(
        _repack_kernel,
        out_shape=(jax.ShapeDtypeStruct((PROJ_COLS, d), BF16), jax.ShapeDtypeStruct((LANES, d), BF16)),
        grid_spec=pltpu.PrefetchScalarGridSpec(
            num_scalar_prefetch=1,
            grid=(PROJ_COLS // REPACK_TILE,),
            in_specs=[pl.BlockSpec((pl.Element(REPACK_TILE), pl.Element(d)),
                                   lambda o, tab: (tab[o] * N_GATE_COLS, 0)),
                      pl.BlockSpec((LANES, d), lambda o, tab: (gate_row // LANES, 0))],
            out_specs=(pl.BlockSpec((REPACK_TILE, d), lambda o, tab: (o, 0)),
                       pl.BlockSpec((LANES, d), lambda o, tab: (0, 0)))),
        compiler_params=_params(("arbitrary",)),
        name="repack_w_in",
    )(jnp.asarray(src // N_GATE_COLS), w_t, w_t)


def _lane_row(vec, offset):
    return jnp.pad(vec.astype(F32), (offset, LANES - offset - vec.shape[0]))[None, :]


def kernel(x, c, w_ada, b_ada, norm_gain, w_in, conv_w, a_log, dt_bias, gdn_norm_gain, q_norm_gain,
           k_norm_gain, sinks, rel_bias, w_branch_gdn, w_branch_swa, w_out):
    bsz, t, d = x.shape
    depth = w_in.shape[0]
    outs = []
    for b in range(bsz):
        xb = x[b]
        c_col = c[b].astype(F32)[:, None]
        for l in range(depth):
            mod = _ada_mod(c_col, w_ada[l], b_ada[l][None, :])
            shift, scale, gate = mod[:, :d], mod[:, d:2 * d], mod[:, 2 * d:]
            w_big, w_small = _repack_w_in(w_in[l].T)
            proj, ba, (w_a, w_b, w_o) = _in_proj(
                xb, norm_gain[l][None, :], scale, shift, w_big, w_small,
                (w_branch_gdn[l], w_branch_swa[l], w_out[l]), tm=min(1024, t), tn=1792)
            o_a = _gdn(proj, ba, conv_w[l], _lane_row(a_log[l], GDN_HEADS), _lane_row(dt_bias[l], GDN_HEADS),
                       gdn_norm_gain[l][None, :])
            bias = _swa_bias(rel_bias.T.astype(F32), sinks[l].astype(F32))
            o_b = _swa(proj, bias,
                       jnp.tile(q_norm_gain[l], LANES // SWA_HEAD_DIM)[None, :],
                       jnp.tile(k_norm_gain[l], LANES // SWA_HEAD_DIM)[None, :])
            xb = _merge_out(xb, o_a, o_b, proj, gate, w_a, w_b, w_o, tm=min(512, t))
        outs.append(xb)
    return jnp.stack(outs, axis=0)
```

```python
import functools
import math

import jax
import jax.numpy as jnp
import numpy as np
from jax import lax
from jax.experimental import pallas as pl
from jax.experimental.pallas import tpu as pltpu

F32 = jnp.float32
BF16 = jnp.bfloat16

LANES = 128
D_MODEL = 2048
GDN_HEADS = 8
GDN_DK = 128
GDN_DV = 128
GDN_CONV = 4
GDN_QK = GDN_HEADS * GDN_DK
GDN_V = GDN_HEADS * GDN_DV
GDN_CONV_CH = 2 * GDN_QK + GDN_V
GDN_BLOCK = 128
GDN_STEP_BLOCKS = 2
GDN_CONV_PIECE = 512
SWA_HEADS = 16
SWA_KV_HEADS = 4
SWA_HEAD_DIM = 64
SWA_WINDOW = 128
SWA_BLOCK = 128
SWA_STEP_BLOCKS = 8
SWA_LOGITS_AHEAD = 4
SWA_Q = SWA_HEADS * SWA_HEAD_DIM
SWA_KV = SWA_KV_HEADS * SWA_HEAD_DIM
REL_BUCKETS = 32
REL_MAX_DIST = 128
EPS = 1e-6
NEG_BIG = -1e30
LOG2E = math.log2(math.e)

COL_QKV = 0
COL_ZA = GDN_CONV_CH
COL_QB = COL_ZA + GDN_V
COL_ZB = COL_QB + SWA_Q
COL_GA = COL_ZB + SWA_Q
COL_GB = COL_GA + D_MODEL
COL_KB = COL_GB + D_MODEL
COL_VB = COL_KB + SWA_KV
PROJ_COLS = COL_VB + SWA_KV
for _col, _width in ((COL_ZA, GDN_V), (COL_QB, SWA_Q), (COL_ZB, SWA_Q), (COL_GA, D_MODEL),
                     (COL_GB, D_MODEL), (COL_KB, SWA_KV), (COL_VB, SWA_KV)):
    assert _col % _width == 0

VMEM_LIMIT = 56 * 1024 * 1024


def _sigmoid(x):
    return 0.5 + 0.5 * jnp.tanh(0.5 * x)


def _silu(x):
    half = 0.5 * x
    return half + half * jnp.tanh(half)


def _params(sem):
    return pltpu.CompilerParams(dimension_semantics=sem, vmem_limit_bytes=VMEM_LIMIT)


def _mm(a, b):
    return jnp.dot(a.astype(BF16), b.astype(BF16), preferred_element_type=F32)


def _mm_nt(a, b):
    return lax.dot_general(a.astype(BF16), b.astype(BF16), (((1,), (1,)), ((), ())),
                           preferred_element_type=F32)


def _mm_tn(a, b):
    return lax.dot_general(a.astype(BF16), b.astype(BF16), (((0,), (0,)), ((), ())),
                           preferred_element_type=F32)


def _ada_mod_kernel(c_ref, w_ref, b_ref, o_ref):
    c = c_ref[...]
    o_ref[...] = jnp.sum(_silu(c) * w_ref[...], axis=0, keepdims=True) + b_ref[...]


def _ada_mod(c_col, w_ada, b_ada):
    d, n = w_ada.shape
    tn = 1024
    return pl.pallas_call(
        _ada_mod_kernel,
        out_shape=jax.ShapeDtypeStruct((1, n), F32),
        grid=(n // tn,),
        in_specs=[pl.BlockSpec((d, 1), lambda j: (0, 0)),
                  pl.BlockSpec((d, tn), lambda j: (0, j)),
                  pl.BlockSpec((1, tn), lambda j: (0, j))],
        out_specs=pl.BlockSpec((1, tn), lambda j: (0, j)),
        compiler_params=_params(("arbitrary",)),
        name="ada_mod",
    )(c_col, w_ada, b_ada)


def _in_proj_kernel(x_ref, gain_ref, scale_ref, shift_ref, w_ref, ws_ref, *rest, row_chunk, n_side):
    side_in, (o_ref, ba_ref), side_out, h_ref = (rest[:n_side], rest[n_side:n_side + 2],
                                                 rest[n_side + 2:-1], rest[-1])
    for src, dst in zip(side_in, side_out):
        dst[...] = src[...].astype(dst.dtype)
    j = pl.program_id(1)

    @pl.when(j == 0)
    def _():
        gs = gain_ref[...] * (1.0 + scale_ref[...])
        sh = shift_ref[...]
        tm = x_ref.shape[0]
        for r in range(tm // row_chunk):
            rows = slice(r * row_chunk, (r + 1) * row_chunk)
            x = x_ref[rows, :]
            ms = jnp.mean(x * x, axis=-1, keepdims=True)
            h = ((x * lax.rsqrt(ms + EPS)) * gs + sh).astype(BF16)
            h_ref[rows, :] = h
            ba_ref[rows, :] = _mm_nt(h, ws_ref[...])

    o_ref[...] = _mm_nt(h_ref[...], w_ref[...]).astype(o_ref.dtype)


def _in_proj(x2d, gain, scale, shift, w_big, w_small, side_weights, *, tm, tn):
    t, d = x2d.shape
    n = w_big.shape[0]
    n_i, n_j = t // tm, n // tn
    slabs = 1 << ((n_i * n_j).bit_length() - 1)
    slab_of = lambda i, j: jnp.minimum(i * n_j + j, slabs - 1)
    side_specs = [pl.BlockSpec((w.shape[0] // slabs, w.shape[1]), lambda i, j: (slab_of(i, j), 0))
                  for w in side_weights]
    assert all(w.shape[0] % (16 * slabs) == 0 for w in side_weights)
    out = pl.pallas_call(
        functools.partial(_in_proj_kernel, row_chunk=128, n_side=len(side_weights)),
        out_shape=(jax.ShapeDtypeStruct((t, n), BF16), jax.ShapeDtypeStruct((t, LANES), F32),
                   *[jax.ShapeDtypeStruct(w.shape, BF16) for w in side_weights]),
        grid=(n_i, n_j),
        in_specs=[pl.BlockSpec((tm, d), lambda i, j: (i, 0)),
                  pl.BlockSpec((1, d), lambda i, j: (0, 0)),
                  pl.BlockSpec((1, d), lambda i, j: (0, 0)),
                  pl.BlockSpec((1, d), lambda i, j: (0, 0)),
                  pl.BlockSpec((tn, d), lambda i, j: (j, 0)),
                  pl.BlockSpec((LANES, d), lambda i, j: (0, 0)),
                  *side_specs],
        out_specs=(pl.BlockSpec((tm, tn), lambda i, j: (i, j)),
                   pl.BlockSpec((tm, LANES), lambda i, j: (i, 0)),
                   *side_specs),
        scratch_shapes=[pltpu.VMEM((tm, d), BF16)],
        compiler_params=_params(("arbitrary", "arbitrary")),
        name="in_proj",
    )(x2d, gain, scale, shift, w_big, w_small, *side_weights)
    return out[0], out[1], out[2:]


def _swa_bias_kernel(tab_ref, sink_ref, o_ref):
    q = SWA_BLOCK
    qpos = lax.broadcasted_iota(jnp.int32, (q, 2 * q), 0) + q
    kpos = lax.broadcasted_iota(jnp.int32, (q, 2 * q), 1)
    dist = qpos - kpos
    in_window = (dist >= 0) & (dist < SWA_WINDOW)
    d = jnp.maximum(dist, 0)
    max_exact = REL_BUCKETS // 2
    df = jnp.maximum(d, 1).astype(F32)
    large = max_exact + (jnp.log(df / max_exact) / math.log(REL_MAX_DIST / max_exact)
                         * (REL_BUCKETS - max_exact)).astype(jnp.int32)
    large = jnp.minimum(large, REL_BUCKETS - 1)
    bucket = jnp.where(in_window, jnp.where(d < max_exact, d, large), REL_BUCKETS)
    for h in range(SWA_HEADS):
        acc = jnp.full((q, 2 * q), NEG_BIG, F32)
        for b in range(REL_BUCKETS):
            acc = jnp.where(bucket == b, tab_ref[h, b], acc)
        o_ref[h] = (jnp.where(kpos == 0, sink_ref[h], acc) * LOG2E).astype(o_ref.dtype)


def _swa_bias(rel_bias_t, sinks):
    q = SWA_BLOCK
    return pl.pallas_call(
        _swa_bias_kernel,
        out_shape=jax.ShapeDtypeStruct((SWA_HEADS, q, 2 * q), BF16),
        in_specs=[pl.BlockSpec(memory_space=pltpu.SMEM), pl.BlockSpec(memory_space=pltpu.SMEM)],
        out_specs=pl.BlockSpec(memory_space=pltpu.VMEM),
        compiler_params=pltpu.CompilerParams(vmem_limit_bytes=VMEM_LIMIT),
        name="swa_bias",
    )(rel_bias_t, sinks)


def _chunk_cumsum_rows(x):
    n = x.shape[0]
    row = lax.broadcasted_iota(jnp.int32, x.shape, 0)
    s = 1
    while s < n:
        x = x + jnp.where(row >= s, pltpu.roll(x, s, axis=0), 0.0)
        s *= 2
    return x


def _unit_lower_inverses(l_mats, masks_ref):
    eye = masks_ref[0].astype(F32)
    l_bf = [l.astype(BF16) for l in l_mats]
    m0 = [l * masks_ref[1] for l in l_bf]
    x = [eye - m for m in m0]
    p = [_mm(m, m) for m in m0]
    x = [xi + _mm(xi, pi) for xi, pi in zip(x, p)]
    p = [_mm(pi, pi) for pi in p]
    x = [xi + _mm(xi, pi) for xi, pi in zip(x, p)]
    for lvl in range(2, masks_ref.shape[0]):
        nx = [_mm(l * masks_ref[lvl], xi) for l, xi in zip(l_bf, x)]
        x = [xi - _mm(xi, ni) for xi, ni in zip(x, nx)]
    return x


def _gdn_kernel(cur_ref, prev_ref, z_ref, ba_ref, convw_ref, band_ref, alog_ref, dtb_ref, gain_ref,
                masks_ref, o_ref, s_ref):
    s = pl.program_id(0)
    c = GDN_BLOCK
    heads = range(GDN_HEADS)
    blocks = range(GDN_STEP_BLOCKS)
    items = [(b, h) for b in blocks for h in heads]
    lane_of = lambda a, h: a[:, GDN_HEADS + h:GDN_HEADS + h + 1]

    @pl.when(s == 0)
    def _():
        s_ref[...] = jnp.zeros_like(s_ref)

    pad = prev_ref.shape[0]
    conv_w = convw_ref[...].astype(BF16)
    conv_pieces = {}
    for b in blocks:
        if b == 0:
            ctx = jnp.where(s > 0, prev_ref[...], jnp.zeros_like(prev_ref))
        else:
            ctx = cur_ref[b * c - pad:b * c, :]
        xcat = jnp.concatenate([ctx, cur_ref[b * c:(b + 1) * c, :]], axis=0)
        taps = jnp.concatenate([xcat * conv_w[i:i + 1, :] for i in range(GDN_CONV)], axis=0)
        for p in range(GDN_CONV_CH // GDN_CONV_PIECE):
            cols = slice(p * GDN_CONV_PIECE, (p + 1) * GDN_CONV_PIECE)
            conv_pieces[b, p] = jnp.dot(band_ref[...], taps[:, cols], preferred_element_type=F32)

    row = lax.broadcasted_iota(jnp.int32, (c, c), 0)
    col = lax.broadcasted_iota(jnp.int32, (c, c), 1)
    causal = row >= col

    def l2n(a, scale=1.0):
        return a * (lax.rsqrt(jnp.sum(a * a, axis=-1, keepdims=True) + EPS) * scale)

    q, k, kb, vb, kbg, qd, kd, dec, el = ({} for _ in range(9))
    for b in blocks:
        ba = ba_ref[b * c:(b + 1) * c, :]
        beta_all = _sigmoid(ba)
        xg = ba + dtb_ref[...]
        softplus = jnp.maximum(xg, 0.0) + jnp.log(1.0 + jnp.exp(-jnp.abs(xg)))
        g_all = -jnp.exp(alog_ref[...]) * softplus
        gc = _chunk_cumsum_rows(g_all)
        gc_t = gc.T
        g_last = gc[c - 1:c, :]
        eg_all = jnp.exp(gc)
        ekd_all = jnp.exp(g_last - gc)
        el[b] = jnp.exp(g_last)
        conv_tile = lambda col0: _silu(conv_pieces[b, col0 // GDN_CONV_PIECE][
            :, col0 % GDN_CONV_PIECE:col0 % GDN_CONV_PIECE + LANES])
        for h in heads:
            qf = l2n(conv_tile(h * GDN_DK), GDN_DK ** -0.5)
            kf = l2n(conv_tile(GDN_QK + h * GDN_DK))
            vf = conv_tile(2 * GDN_QK + h * GDN_DV)
            beta = beta_all[:, h:h + 1]
            eg = lane_of(eg_all, h)
            kbf = kf * beta
            gdiff = lane_of(gc, h) - gc_t[GDN_HEADS + h:GDN_HEADS + h + 1, :]
            q[b, h] = qf.astype(BF16)
            k[b, h] = kf.astype(BF16)
            kb[b, h] = kbf.astype(BF16)
            vb[b, h] = vf * beta
            kbg[b, h] = (kbf * eg).astype(BF16)
            qd[b, h] = (qf * eg).astype(BF16)
            kd[b, h] = (kf * lane_of(ekd_all, h)).astype(BF16)
            dec[b, h] = jnp.where(causal, jnp.exp(gdiff), 0.0)

    kk = [_mm_nt(kb[i], k[i]) for i in items]
    qk = [_mm_nt(q[i], k[i]) for i in items]
    l_mat = [kk[n] * dec[i] for n, i in enumerate(items)]
    attn = {i: (qk[n] * dec[i]).astype(BF16) for n, i in enumerate(items)}
    t_inv = dict(zip(items, _unit_lower_inverses(l_mat, masks_ref)))
    state = [s_ref[h] for h in heads]
    for b in blocks:
        s_bf = [state[h].astype(BF16) for h in heads]
        resid = [vb[b, h] - _mm(kbg[b, h], s_bf[h]) for h in heads]
        v_new = [_mm(t_inv[b, h], resid[h]).astype(BF16) for h in heads]
        o = [_mm(jnp.concatenate([qd[b, h], attn[b, h]], axis=1),
                 jnp.concatenate([s_bf[h], v_new[h]], axis=0)) for h in heads]
        state = [state[h] * lane_of(el[b], h) + _mm_tn(kd[b, h], v_new[h]) for h in heads]
        for h in heads:
            on = o[h] * lax.rsqrt(jnp.mean(o[h] * o[h], axis=-1, keepdims=True) + EPS) * gain_ref[...]
            z = z_ref[b * c:(b + 1) * c, h * GDN_DV:(h + 1) * GDN_DV].astype(F32)
            o_ref[b * c:(b + 1) * c, h * GDN_DV:(h + 1) * GDN_DV] = (on * _silu(z)).astype(o_ref.dtype)
    for h in heads:
        s_ref[h] = state[h]


def _gdn_masks():
    c = GDN_BLOCK
    r = np.arange(c)[:, None]
    k = np.arange(c)[None, :]
    mats = [np.eye(c), (r // 8 == k // 8) & (r > k)]
    b = 8
    while b < c:
        mats.append((r // (2 * b) == k // (2 * b)) & ((r // b) % 2 == 1) & ((k // b) % 2 == 0))
        b *= 2
    return jnp.asarray(np.stack([np.asarray(m, np.float32) for m in mats]), BF16)


def _conv_band(c, pad):
    band = np.zeros((c, GDN_CONV * (pad + c)), np.float32)
    for i in range(GDN_CONV):
        band[np.arange(c), i * (pad + c) + pad + np.arange(c) - (GDN_CONV - 1) + i] = 1.0
    return jnp.asarray(band, BF16)


def _gdn(proj, ba, conv_w, alog_lane, dtb_lane, gain):
    t = proj.shape[0]
    c = GDN_BLOCK
    rows = GDN_STEP_BLOCKS * c
    pad = 16
    masks = _gdn_masks()
    nm = masks.shape[0]
    band = _conv_band(c, pad)
    return pl.pallas_call(
        _gdn_kernel,
        out_shape=jax.ShapeDtypeStruct((t, GDN_V), BF16),
        grid=(t // rows,),
        in_specs=[pl.BlockSpec((rows, GDN_CONV_CH), lambda i: (i, 0)),
                  pl.BlockSpec((pad, GDN_CONV_CH), lambda i: (jnp.maximum(i * (rows // pad) - 1, 0), 0)),
                  pl.BlockSpec((rows, GDN_V), lambda i: (i, COL_ZA // GDN_V)),
                  pl.BlockSpec((rows, LANES), lambda i: (i, 0)),
                  pl.BlockSpec((GDN_CONV, GDN_CONV_CH), lambda i: (0, 0)),
                  pl.BlockSpec(band.shape, lambda i: (0, 0)),
                  pl.BlockSpec((1, LANES), lambda i: (0, 0)),
                  pl.BlockSpec((1, LANES), lambda i: (0, 0)),
                  pl.BlockSpec((1, GDN_DV), lambda i: (0, 0)),
                  pl.BlockSpec((nm, c, c), lambda i: (0, 0, 0))],
        out_specs=pl.BlockSpec((rows, GDN_V), lambda i: (i, 0)),
        scratch_shapes=[pltpu.VMEM((GDN_HEADS, GDN_DK, GDN_DV), F32)],
        compiler_params=_params(("arbitrary",)),
        name="gdn",
    )(proj, proj, proj, ba, conv_w, band, alog_lane, dtb_lane, gain, masks)


def _swa_kernel(q_ref, kc_ref, kp_ref, vc_ref, vp_ref, z_ref, bias_ref, qg_ref, kg_ref, seg_ref, o_ref):
    n = pl.program_id(0)
    qb = SWA_BLOCK
    hd = SWA_HEAD_DIM
    n_tiles = SWA_Q // LANES
    blocks = range(SWA_STEP_BLOCKS)
    items = [(j, tq) for j in blocks for tq in range(n_tiles)]
    lane = lax.broadcasted_iota(jnp.int32, (1, LANES), 1)
    lo = lane < hd
    seg = seg_ref[...].astype(BF16)

    def qk_norm(a, gain):
        ms = jnp.dot((a * a).astype(BF16), seg, preferred_element_type=F32)
        return a * (lax.rsqrt(ms + EPS) * gain)

    def two_heads(lo_part, hi_part):
        return jnp.concatenate([jnp.where(lo, lo_part, 0.0), jnp.where(lo, 0.0, hi_part)], axis=0)

    kall = jnp.concatenate([kp_ref[...], kc_ref[...]], axis=0).astype(F32)
    vall = jnp.concatenate([vp_ref[...], vc_ref[...]], axis=0).astype(F32)
    kn, vv = {}, {}
    for u in range(SWA_KV // LANES):
        knorm = qk_norm(kall[:, u * LANES:(u + 1) * LANES], kg_ref[...])
        vtile = vall[:, u * LANES:(u + 1) * LANES]
        for ci in range(SWA_STEP_BLOCKS + 1):
            kc = knorm[ci * qb:(ci + 1) * qb]
            vc = vtile[ci * qb:(ci + 1) * qb]
            kn[u, ci] = (kc, pltpu.roll(kc, hd, axis=1))
            vv[u, ci] = (vc, pltpu.roll(vc, hd, axis=1))

    row = lax.broadcasted_iota(jnp.int32, (qb, 1), 0)
    is_sink = row == 0
    k2s, rhs = {}, {}
    for j in blocks:
        prev_live = (n * SWA_STEP_BLOCKS + j) > 0
        ones_prev = jnp.where(prev_live | is_sink, 1.0, 0.0)
        ones_band = jnp.concatenate([ones_prev, jnp.ones_like(ones_prev)], axis=0)
        ones2 = two_heads(ones_band, ones_band)
        for u in range(SWA_KV // LANES):
            kband = [jnp.concatenate([jnp.where(is_sink, 0.0, kn[u, j][r]), kn[u, j + 1][r]], axis=0)
                     for r in range(2)]
            vband = [jnp.concatenate([jnp.where(prev_live & ~is_sink, vv[u, j][r], 0.0), vv[u, j + 1][r]],
                                     axis=0) for r in range(2)]
            for half in range(2):
                g = 2 * u + half
                k2s[j, g] = two_heads(kband[half], kband[1 - half]).astype(BF16)
                v2 = two_heads(vband[half], vband[1 - half])
                rhs[j, g] = jnp.concatenate([v2, ones2], axis=1).astype(BF16)

    q_gain = qg_ref[...] * (hd ** -0.5 * LOG2E)
    kv_of = lambda tq: (2 * tq) // (SWA_HEADS // SWA_KV_HEADS)
    qt = {(j, tq): qk_norm(q_ref[j * qb:(j + 1) * qb, tq * LANES:(tq + 1) * LANES].astype(F32), q_gain)
          for j, tq in items}
    logits = {}
    for n in range(len(items) + SWA_LOGITS_AHEAD):
        if n < len(items):
            j, tq = items[n]
            logits[j, tq] = _mm_nt(qt[j, tq], k2s[j, kv_of(tq)])
        if n >= SWA_LOGITS_AHEAD:
            j, tq = items[n - SWA_LOGITS_AHEAD]
            both = logits.pop((j, tq))
            ps = []
            for e in range(2):
                s = both[:, e * 2 * qb:(e + 1) * 2 * qb].astype(BF16) + bias_ref[2 * tq + e]
                ps.append(jnp.exp2(s - jnp.max(s, axis=-1, keepdims=True)))
            pv = jnp.dot(jnp.concatenate(ps, axis=1), rhs[j, kv_of(tq)], preferred_element_type=F32)
            out = pv[:, :LANES] * (1.0 / pv[:, LANES:])
            z = z_ref[j * qb:(j + 1) * qb, tq * LANES:(tq + 1) * LANES].astype(F32)
            o_ref[j * qb:(j + 1) * qb, tq * LANES:(tq + 1) * LANES] = (out * _silu(z)).astype(o_ref.dtype)


def _swa(proj, bias, qg2, kg2):
    t = proj.shape[0]
    qb = SWA_BLOCK
    rows = SWA_STEP_BLOCKS * qb
    seg = np.kron(np.eye(LANES // SWA_HEAD_DIM), np.ones((SWA_HEAD_DIM, SWA_HEAD_DIM))) / SWA_HEAD_DIM
    seg = jnp.asarray(seg, F32)
    kcol = COL_KB // SWA_KV
    vcol = COL_VB // SWA_KV
    prev = lambda i: jnp.maximum(i * SWA_STEP_BLOCKS - 1, 0)
    return pl.pallas_call(
        _swa_kernel,
        out_shape=jax.ShapeDtypeStruct((t, SWA_Q), BF16),
        grid=(t // rows,),
        in_specs=[pl.BlockSpec((rows, SWA_Q), lambda i: (i, COL_QB // SWA_Q)),
                  pl.BlockSpec((rows, SWA_KV), lambda i: (i, kcol)),
                  pl.BlockSpec((qb, SWA_KV), lambda i: (prev(i), kcol)),
                  pl.BlockSpec((rows, SWA_KV), lambda i: (i, vcol)),
                  pl.BlockSpec((qb, SWA_KV), lambda i: (prev(i), vcol)),
                  pl.BlockSpec((rows, SWA_Q), lambda i: (i, COL_ZB // SWA_Q)),
                  pl.BlockSpec((SWA_HEADS, qb, 2 * qb), lambda i: (0, 0, 0)),
                  pl.BlockSpec((1, LANES), lambda i: (0, 0)),
                  pl.BlockSpec((1, LANES), lambda i: (0, 0)),
                  pl.BlockSpec((LANES, LANES), lambda i: (0, 0))],
        out_specs=pl.BlockSpec((rows, SWA_Q), lambda i: (i, 0)),
        compiler_params=_params(("arbitrary",)),
        name="swa",
    )(proj, proj, proj, proj, proj, proj, bias, qg2, kg2, seg)


def _merge_out_kernel(x_ref, oa_ref, ob_ref, ga_ref, gb_ref, gate_ref, wa_ref, wb_ref, wo_ref, o_ref):
    ya = jnp.dot(oa_ref[...], wa_ref[...], preferred_element_type=F32)
    yb = jnp.dot(ob_ref[...], wb_ref[...], preferred_element_type=F32)
    mixed = _sigmoid(ga_ref[...].astype(F32)) * ya + _sigmoid(gb_ref[...].astype(F32)) * yb
    y = jnp.dot(mixed.astype(BF16), wo_ref[...], preferred_element_type=F32)
    o_ref[...] = x_ref[...] + gate_ref[...] * y


def _merge_out(x2d, o_a, o_b, proj, gate, w_a, w_b, w_o, *, tm):
    t, d = x2d.shape
    const = lambda shape: pl.BlockSpec(shape, lambda i: (0, 0), pipeline_mode=pl.Buffered(1))
    return pl.pallas_call(
        _merge_out_kernel,
        out_shape=jax.ShapeDtypeStruct((t, d), F32),
        grid=(t // tm,),
        in_specs=[pl.BlockSpec((tm, d), lambda i: (i, 0)),
                  pl.BlockSpec((tm, GDN_V), lambda i: (i, 0)),
                  pl.BlockSpec((tm, SWA_Q), lambda i: (i, 0)),
                  pl.BlockSpec((tm, d), lambda i: (i, COL_GA // D_MODEL)),
                  pl.BlockSpec((tm, d), lambda i: (i, COL_GB // D_MODEL)),
                  pl.BlockSpec((1, d), lambda i: (0, 0)),
                  const((GDN_V, d)), const((SWA_Q, d)), const((d, d))],
        out_specs=pl.BlockSpec((tm, d), lambda i: (i, 0)),
        compiler_params=_params(("arbitrary",)),
        name="merge_out",
    )(x2d, o_a, o_b, proj, proj, gate, w_a, w_b, w_o)


REPACK_TILE = 512
N_GATE_COLS = 2 * GDN_HEADS


def _repack_plan():
    src_of = {COL_QKV: 0, COL_ZA: GDN_CONV_CH}
    after_gates = COL_QB + N_GATE_COLS
    src_of.update({COL_QB: after_gates, COL_KB: after_gates + SWA_Q, COL_VB: after_gates + SWA_Q + SWA_KV,
                   COL_ZB: after_gates + SWA_Q + 2 * SWA_KV, COL_GA: after_gates + 2 * SWA_Q + 2 * SWA_KV,
                   COL_GB: after_gates + 2 * SWA_Q + 2 * SWA_KV + D_MODEL})
    starts = sorted(src_of)
    src = []
    for dst in range(0, PROJ_COLS, REPACK_TILE):
        grp = max(s for s in starts if s <= dst)
        src.append(src_of[grp] + dst - grp)
    return np.asarray(src, np.int32)


def _repack_kernel(tab_ref, w_ref, g_ref, big_ref, small_ref):
    del tab_ref
    big_ref[...] = w_ref[...].astype(BF16)

    @pl.when(pl.program_id(0) == 0)
    def _():
        row = lax.broadcasted_iota(jnp.int32, (LANES, 1), 0)
        small_ref[...] = jnp.where(row < N_GATE_COLS, g_ref[...], 0.0).astype(BF16)


def _repack_w_in(w_t):
    d = w_t.shape[1]
    src = _repack_plan()
    gate_row = COL_QB
    assert gate_row % LANES == 0 and np.all(src % N_GATE_COLS == 0)
    return pl.pallas_call(
        _repack_kernel,
        out_shape=(jax.ShapeDtypeStruct((PROJ_COLS, d), BF16), jax.ShapeDtypeStruct((LANES, d), BF16)),
        grid_spec=pltpu.PrefetchScalarGridSpec(
            num_scalar_prefetch=1,
            grid=(PROJ_COLS // REPACK_TILE,),
            in_specs=[pl.BlockSpec((pl.Element(REPACK_TILE), pl.Element(d)),
                                   lambda o, tab: (tab[o] * N_GATE_COLS, 0)),
                      pl.BlockSpec((LANES, d), lambda o, tab: (gate_row // LANES, 0))],
            out_specs=(pl.BlockSpec((REPACK_TILE, d), lambda o, tab: (o, 0)),
                       pl.BlockSpec((LANES, d), lambda o, tab: (0, 0)))),
        compiler_params=_params(("arbitrary",)),
        name="repack_w_in",
    )(jnp.asarray(src // N_GATE_COLS), w_t, w_t)


def _lane_row(vec, offset):
    return jnp.pad(vec.astype(F32), (offset, LANES - offset - vec.shape[0]))[None, :]


def kernel(x, c, w_ada, b_ada, norm_gain, w_in, conv_w, a_log, dt_bias, gdn_norm_gain, q_norm_gain,
           k_norm_gain, sinks, rel_bias, w_branch_gdn, w_branch_swa, w_out):
    bsz, t, d = x.shape
    depth = w_in.shape[0]
    outs = []
    for b in range(bsz):
        xb = x[b]
        c_col = c[b].astype(F32)[:, None]
        for l in range(depth):
            mod = _ada_mod(c_col, w_ada[l], b_ada[l][None, :])
            shift, scale, gate = mod[:, :d], mod[:, d:2 * d], mod[:, 2 * d:]
            w_big, w_small = _repack_w_in(w_in[l].T)
            proj, ba, (w_a, w_b, w_o) = _in_proj(
                xb, norm_gain[l][None, :], scale, shift, w_big, w_small,
                (w_branch_gdn[l], w_branch_swa[l], w_out[l]), tm=min(1024, t), tn=1792)
            o_a = _gdn(proj, ba, conv_w[l], _lane_row(a_log[l], GDN_HEADS), _lane_row(dt_bias[l], GDN_HEADS),
                       gdn_norm_gain[l][None, :])
            bias = _swa_bias(rel_bias.T.astype(F32), sinks[l].astype(F32))
            o_b = _swa(proj, bias,
                       jnp.tile(q_norm_gain[l], LANES // SWA_HEAD_DIM)[None, :],
                       jnp.tile(k_norm_gain[l], LANES // SWA_HEAD_DIM)[None, :])
            xb = _merge_out(xb, o_a, o_b, proj, gate, w_a, w_b, w_o, tm=min(512, t))
        outs.append(xb)
    return jnp.stack(outs, axis=0)
```

```python
import functools
import math

import jax
import jax.numpy as jnp
import numpy as np
from jax import lax
from jax.experimental import pallas as pl
from jax.experimental.pallas import tpu as pltpu

F32 = jnp.float32
BF16 = jnp.bfloat16

LANES = 128
D_MODEL = 2048
GDN_HEADS = 8
GDN_DK = 128
GDN_DV = 128
GDN_CONV = 4
GDN_QK = GDN_HEADS * GDN_DK
GDN_V = GDN_HEADS * GDN_DV
GDN_CONV_CH = 2 * GDN_QK + GDN_V
GDN_BLOCK = 128
GDN_STEP_BLOCKS = 4
GDN_CONV_PIECE = 512
GDN_BLOCK_LAG = 3
SWA_HEADS = 16
SWA_KV_HEADS = 4
SWA_HEAD_DIM = 64
SWA_WINDOW = 128
SWA_BLOCK = 128
SWA_STEP_BLOCKS = 8
SWA_LOGITS_AHEAD = 4
SWA_Q = SWA_HEADS * SWA_HEAD_DIM
SWA_KV = SWA_KV_HEADS * SWA_HEAD_DIM
REL_BUCKETS = 32
REL_MAX_DIST = 128
EPS = 1e-6
NEG_BIG = -1e30
LOG2E = math.log2(math.e)

COL_QKV = 0
COL_ZA = GDN_CONV_CH
COL_QB = COL_ZA + GDN_V
COL_ZB = COL_QB + SWA_Q
COL_GA = COL_ZB + SWA_Q
COL_GB = COL_GA + D_MODEL
COL_KB = COL_GB + D_MODEL
COL_VB = COL_KB + SWA_KV
PROJ_COLS = COL_VB + SWA_KV
for _col, _width in ((COL_ZA, GDN_V), (COL_QB, SWA_Q), (COL_ZB, SWA_Q), (COL_GA, D_MODEL),
                     (COL_GB, D_MODEL), (COL_KB, SWA_KV), (COL_VB, SWA_KV)):
    assert _col % _width == 0

VMEM_LIMIT = 56 * 1024 * 1024


def _sigmoid(x):
    return 0.5 + 0.5 * jnp.tanh(0.5 * x)


def _silu(x):
    half = 0.5 * x
    return half + half * jnp.tanh(half)


def _params(sem):
    return pltpu.CompilerParams(dimension_semantics=sem, vmem_limit_bytes=VMEM_LIMIT)


def _mm(a, b):
    return jnp.dot(a.astype(BF16), b.astype(BF16), preferred_element_type=F32)


def _mm_nt(a, b):
    return lax.dot_general(a.astype(BF16), b.astype(BF16), (((1,), (1,)), ((), ())),
                           preferred_element_type=F32)


def _mm_tn(a, b):
    return lax.dot_general(a.astype(BF16), b.astype(BF16), (((0,), (0,)), ((), ())),
                           preferred_element_type=F32)


def _ada_mod_kernel(c_ref, w_ref, b_ref, o_ref):
    c = c_ref[...]
    o_ref[...] = jnp.sum(_silu(c) * w_ref[...], axis=0, keepdims=True) + b_ref[...]


def _ada_mod(c_col, w_ada, b_ada):
    d, n = w_ada.shape
    tn = 1024
    return pl.pallas_call(
        _ada_mod_kernel,
        out_shape=jax.ShapeDtypeStruct((1, n), F32),
        grid=(n // tn,),
        in_specs=[pl.BlockSpec((d, 1), lambda j: (0, 0)),
                  pl.BlockSpec((d, tn), lambda j: (0, j)),
                  pl.BlockSpec((1, tn), lambda j: (0, j))],
        out_specs=pl.BlockSpec((1, tn), lambda j: (0, j)),
        compiler_params=_params(("arbitrary",)),
        name="ada_mod",
    )(c_col, w_ada, b_ada)


def _in_proj_kernel(x_ref, gain_ref, scale_ref, shift_ref, w_ref, ws_ref, *rest, row_chunk, n_side):
    side_in, (o_ref, ba_ref), side_out, h_ref = (rest[:n_side], rest[n_side:n_side + 2],
                                                 rest[n_side + 2:-1], rest[-1])
    for src, dst in zip(side_in, side_out):
        dst[...] = src[...].astype(dst.dtype)
    j = pl.program_id(1)

    @pl.when(j == 0)
    def _():
        gs = gain_ref[...] * (1.0 + scale_ref[...])
        sh = shift_ref[...]
        tm = x_ref.shape[0]
        for r in range(tm // row_chunk):
            rows = slice(r * row_chunk, (r + 1) * row_chunk)
            x = x_ref[rows, :]
            ms = jnp.mean(x * x, axis=-1, keepdims=True)
            h = ((x * lax.rsqrt(ms + EPS)) * gs + sh).astype(BF16)
            h_ref[rows, :] = h
            ba_ref[rows, :] = _mm_nt(h, ws_ref[...])

    o_ref[...] = _mm_nt(h_ref[...], w_ref[...]).astype(o_ref.dtype)


def _in_proj(x2d, gain, scale, shift, w_big, w_small, side_weights, *, tm, tn):
    t, d = x2d.shape
    n = w_big.shape[0]
    n_i, n_j = t // tm, n // tn
    slabs = 1 << ((n_i * n_j).bit_length() - 1)
    slab_of = lambda i, j: jnp.minimum(i * n_j + j, slabs - 1)
    side_specs = [pl.BlockSpec((w.shape[0] // slabs, w.shape[1]), lambda i, j: (slab_of(i, j), 0))
                  for w in side_weights]
    assert all(w.shape[0] % (16 * slabs) == 0 for w in side_weights)
    out = pl.pallas_call(
        functools.partial(_in_proj_kernel, row_chunk=128, n_side=len(side_weights)),
        out_shape=(jax.ShapeDtypeStruct((t, n), BF16), jax.ShapeDtypeStruct((t, LANES), F32),
                   *[jax.ShapeDtypeStruct(w.shape, BF16) for w in side_weights]),
        grid=(n_i, n_j),
        in_specs=[pl.BlockSpec((tm, d), lambda i, j: (i, 0)),
                  pl.BlockSpec((1, d), lambda i, j: (0, 0)),
                  pl.BlockSpec((1, d), lambda i, j: (0, 0)),
                  pl.BlockSpec((1, d), lambda i, j: (0, 0)),
                  pl.BlockSpec((tn, d), lambda i, j: (j, 0)),
                  pl.BlockSpec((LANES, d), lambda i, j: (0, 0)),
                  *side_specs],
        out_specs=(pl.BlockSpec((tm, tn), lambda i, j: (i, j)),
                   pl.BlockSpec((tm, LANES), lambda i, j: (i, 0)),
                   *side_specs),
        scratch_shapes=[pltpu.VMEM((tm, d), BF16)],
        compiler_params=_params(("arbitrary", "arbitrary")),
        name="in_proj",
    )(x2d, gain, scale, shift, w_big, w_small, *side_weights)
    return out[0], out[1], out[2:]


def _swa_bias_kernel(tab_ref, sink_ref, o_ref):
    q = SWA_BLOCK
    qpos = lax.broadcasted_iota(jnp.int32, (q, 2 * q), 0) + q
    kpos = lax.broadcasted_iota(jnp.int32, (q, 2 * q), 1)
    dist = qpos - kpos
    in_window = (dist >= 0) & (dist < SWA_WINDOW)
    d = jnp.maximum(dist, 0)
    max_exact = REL_BUCKETS // 2
    df = jnp.maximum(d, 1).astype(F32)
    large = max_exact + (jnp.log(df / max_exact) / math.log(REL_MAX_DIST / max_exact)
                         * (REL_BUCKETS - max_exact)).astype(jnp.int32)
    large = jnp.minimum(large, REL_BUCKETS - 1)
    bucket = jnp.where(in_window, jnp.where(d < max_exact, d, large), REL_BUCKETS)
    for h in range(SWA_HEADS):
        acc = jnp.full((q, 2 * q), NEG_BIG, F32)
        for b in range(REL_BUCKETS):
            acc = jnp.where(bucket == b, tab_ref[h, b], acc)
        o_ref[h] = (jnp.where(kpos == 0, sink_ref[h], acc) * LOG2E).astype(o_ref.dtype)


def _swa_bias(rel_bias_t, sinks):
    q = SWA_BLOCK
    return pl.pallas_call(
        _swa_bias_kernel,
        out_shape=jax.ShapeDtypeStruct((SWA_HEADS, q, 2 * q), BF16),
        in_specs=[pl.BlockSpec(memory_space=pltpu.SMEM), pl.BlockSpec(memory_space=pltpu.SMEM)],
        out_specs=pl.BlockSpec(memory_space=pltpu.VMEM),
        compiler_params=pltpu.CompilerParams(vmem_limit_bytes=VMEM_LIMIT),
        name="swa_bias",
    )(rel_bias_t, sinks)


def _chunk_cumsum_rows(x):
    n = x.shape[0]
    row = lax.broadcasted_iota(jnp.int32, x.shape, 0)
    s = 1
    while s < n:
        x = x + jnp.where(row >= s, pltpu.roll(x, s, axis=0), 0.0)
        s *= 2
    return x


def _unit_lower_inverse_stages(l_mats, masks_ref, out):
    eye = masks_ref[0].astype(F32)
    l_bf = [l.astype(BF16) for l in l_mats]
    m0 = [l * masks_ref[1] for l in l_bf]
    x = [eye - m for m in m0]
    p = [_mm(m, m) for m in m0]
    yield
    x = [xi + _mm(xi, pi) for xi, pi in zip(x, p)]
    yield
    p = [_mm(pi, pi) for pi in p]
    yield
    x = [xi + _mm(xi, pi) for xi, pi in zip(x, p)]
    yield
    for lvl in range(2, masks_ref.shape[0]):
        nx = [_mm(l * masks_ref[lvl], xi) for l, xi in zip(l_bf, x)]
        yield
        x = [xi - _mm(xi, ni) for xi, ni in zip(x, nx)]
        yield
    out.extend(x)


def _gdn_kernel(cur_ref, prev_ref, z_ref, ba_ref, convw_ref, band_ref, alog_ref, dtb_ref, gain_ref,
                masks_ref, o_ref, s_ref):
    s = pl.program_id(0)
    c = GDN_BLOCK
    heads = range(GDN_HEADS)
    blocks = range(GDN_STEP_BLOCKS)
    items = [(b, h) for b in blocks for h in heads]
    lane_of = lambda a, h: a[:, GDN_HEADS + h:GDN_HEADS + h + 1]

    @pl.when(s == 0)
    def _():
        s_ref[...] = jnp.zeros_like(s_ref)

    pad = prev_ref.shape[0]
    conv_w = convw_ref[...].astype(BF16)
    conv_pieces = {}
    for b in blocks:
        if b == 0:
            ctx = jnp.where(s > 0, prev_ref[...], jnp.zeros_like(prev_ref))
        else:
            ctx = cur_ref[b * c - pad:b * c, :]
        xcat = jnp.concatenate([ctx, cur_ref[b * c:(b + 1) * c, :]], axis=0)
        taps = jnp.concatenate([xcat * conv_w[i:i + 1, :] for i in range(GDN_CONV)], axis=0)
        for p in range(GDN_CONV_CH // GDN_CONV_PIECE):
            cols = slice(p * GDN_CONV_PIECE, (p + 1) * GDN_CONV_PIECE)
            conv_pieces[b, p] = jnp.dot(band_ref[...], taps[:, cols], preferred_element_type=F32)

    row = lax.broadcasted_iota(jnp.int32, (c, c), 0)
    col = lax.broadcasted_iota(jnp.int32, (c, c), 1)
    causal = row >= col

    def l2n(a, scale=1.0):
        return a * (lax.rsqrt(jnp.sum(a * a, axis=-1, keepdims=True) + EPS) * scale)

    q, k, kb, vb, kbg, qd, kd, dec, el = ({} for _ in range(9))
    for b in blocks:
        ba = ba_ref[b * c:(b + 1) * c, :]
        beta_all = _sigmoid(ba)
        xg = ba + dtb_ref[...]
        softplus = jnp.maximum(xg, 0.0) + jnp.log(1.0 + jnp.exp(-jnp.abs(xg)))
        g_all = -jnp.exp(alog_ref[...]) * softplus
        gc = _chunk_cumsum_rows(g_all)
        gc_t = gc.T
        g_last = gc[c - 1:c, :]
        eg_all = jnp.exp(gc)
        ekd_all = jnp.exp(g_last - gc)
        el[b] = jnp.exp(g_last)
        conv_tile = lambda col0: _silu(conv_pieces[b, col0 // GDN_CONV_PIECE][
            :, col0 % GDN_CONV_PIECE:col0 % GDN_CONV_PIECE + LANES])
        for h in heads:
            qf = l2n(conv_tile(h * GDN_DK), GDN_DK ** -0.5)
            kf = l2n(conv_tile(GDN_QK + h * GDN_DK))
            vf = conv_tile(2 * GDN_QK + h * GDN_DV)
            beta = beta_all[:, h:h + 1]
            eg = lane_of(eg_all, h)
            kbf = kf * beta
            gdiff = lane_of(gc, h) - gc_t[GDN_HEADS + h:GDN_HEADS + h + 1, :]
            q[b, h] = qf.astype(BF16)
            k[b, h] = kf.astype(BF16)
            kb[b, h] = kbf.astype(BF16)
            vb[b, h] = vf * beta
            kbg[b, h] = (kbf * eg).astype(BF16)
            qd[b, h] = (qf * eg).astype(BF16)
            kd[b, h] = (kf * lane_of(ekd_all, h)).astype(BF16)
            dec[b, h] = jnp.where(causal, jnp.exp(gdiff), 0.0)

    attn, t_inv, o_tiles = {}, {}, {}
    state = [s_ref[h] for h in heads]

    def local_chain(b):
        kk = [_mm_nt(kb[b, h], k[b, h]) for h in heads]
        yield
        qk = [_mm_nt(q[b, h], k[b, h]) for h in heads]
        yield
        l_mat = [kk[h] * dec[b, h] for h in heads]
        for h in heads:
            attn[b, h] = (qk[h] * dec[b, h]).astype(BF16)
        inv = []
        yield from _unit_lower_inverse_stages(l_mat, masks_ref, inv)
        for h in heads:
            t_inv[b, h] = inv[h]

    def recurrence(b):
        s_bf = [state[h].astype(BF16) for h in heads]
        resid = [vb[b, h] - _mm(kbg[b, h], s_bf[h]) for h in heads]
        yield
        v_new = [_mm(t_inv[b, h], resid[h]).astype(BF16) for h in heads]
        yield
        for h in heads:
            o_tiles[b, h] = _mm(jnp.concatenate([qd[b, h], attn[b, h]], axis=1),
                                jnp.concatenate([s_bf[h], v_new[h]], axis=0))
            state[h] = state[h] * lane_of(el[b], h) + _mm_tn(kd[b, h], v_new[h])
        yield
        for h in heads:
            o = o_tiles[b, h]
            on = o * lax.rsqrt(jnp.mean(o * o, axis=-1, keepdims=True) + EPS) * gain_ref[...]
            z = z_ref[b * c:(b + 1) * c, h * GDN_DV:(h + 1) * GDN_DV].astype(F32)
            o_ref[b * c:(b + 1) * c, h * GDN_DV:(h + 1) * GDN_DV] = (on * _silu(z)).astype(o_ref.dtype)

    def advance(gen):
        return next(gen, StopIteration) is not StopIteration

    chains = [local_chain(b) for b in blocks]
    alive = [True] * len(chains)
    wave = 0
    recs = []
    while any(alive) or recs:
        for b in blocks:
            if alive[b] and wave >= b * GDN_BLOCK_LAG:
                alive[b] = advance(chains[b])
                if not alive[b]:
                    recs.append(recurrence(b))
        if recs and (len(recs) == 1 or not alive[0]):
            if not advance(recs[0]):
                recs.pop(0)
        wave += 1

    for h in heads:
        s_ref[h] = state[h]


def _gdn_masks():
    c = GDN_BLOCK
    r = np.arange(c)[:, None]
    k = np.arange(c)[None, :]
    mats = [np.eye(c), (r // 8 == k // 8) & (r > k)]
    b = 8
    while b < c:
        mats.append((r // (2 * b) == k // (2 * b)) & ((r // b) % 2 == 1) & ((k // b) % 2 == 0))
        b *= 2
    return jnp.asarray(np.stack([np.asarray(m, np.float32) for m in mats]), BF16)


def _conv_band(c, pad):
    band = np.zeros((c, GDN_CONV * (pad + c)), np.float32)
    for i in range(GDN_CONV):
        band[np.arange(c), i * (pad + c) + pad + np.arange(c) - (GDN_CONV - 1) + i] = 1.0
    return jnp.asarray(band, BF16)


def _gdn(proj, ba, conv_w, alog_lane, dtb_lane, gain):
    t = proj.shape[0]
    c = GDN_BLOCK
    rows = GDN_STEP_BLOCKS * c
    pad = 16
    masks = _gdn_masks()
    nm = masks.shape[0]
    band = _conv_band(c, pad)
    return pl.pallas_call(
        _gdn_kernel,
        out_shape=jax.ShapeDtypeStruct((t, GDN_V), BF16),
        grid=(t // rows,),
        in_specs=[pl.BlockSpec((rows, GDN_CONV_CH), lambda i: (i, 0)),
                  pl.BlockSpec((pad, GDN_CONV_CH), lambda i: (jnp.maximum(i * (rows // pad) - 1, 0), 0)),
                  pl.BlockSpec((rows, GDN_V), lambda i: (i, COL_ZA // GDN_V)),
                  pl.BlockSpec((rows, LANES), lambda i: (i, 0)),
                  pl.BlockSpec((GDN_CONV, GDN_CONV_CH), lambda i: (0, 0)),
                  pl.BlockSpec(band.shape, lambda i: (0, 0)),
                  pl.BlockSpec((1, LANES), lambda i: (0, 0)),
                  pl.BlockSpec((1, LANES), lambda i: (0, 0)),
                  pl.BlockSpec((1, GDN_DV), lambda i: (0, 0)),
                  pl.BlockSpec((nm, c, c), lambda i: (0, 0, 0))],
        out_specs=pl.BlockSpec((rows, GDN_V), lambda i: (i, 0)),
        scratch_shapes=[pltpu.VMEM((GDN_HEADS, GDN_DK, GDN_DV), F32)],
        compiler_params=_params(("arbitrary",)),
        name="gdn",
    )(proj, proj, proj, ba, conv_w, band, alog_lane, dtb_lane, gain, masks)


def _swa_kernel(q_ref, kc_ref, kp_ref, vc_ref, vp_ref, z_ref, bias_ref, qg_ref, kg_ref, seg_ref, o_ref):
    n = pl.program_id(0)
    qb = SWA_BLOCK
    hd = SWA_HEAD_DIM
    n_tiles = SWA_Q // LANES
    blocks = range(SWA_STEP_BLOCKS)
    items = [(j, tq) for j in blocks for tq in range(n_tiles)]
    lane = lax.broadcasted_iota(jnp.int32, (1, LANES), 1)
    lo = lane < hd
    seg = seg_ref[...].astype(BF16)

    def qk_norm(a, gain):
        ms = jnp.dot((a * a).astype(BF16), seg, preferred_element_type=F32)
        return a * (lax.rsqrt(ms + EPS) * gain)

    def two_heads(lo_part, hi_part):
        return jnp.concatenate([jnp.where(lo, lo_part, 0.0), jnp.where(lo, 0.0, hi_part)], axis=0)

    kall = jnp.concatenate([kp_ref[...], kc_ref[...]], axis=0).astype(F32)
    vall = jnp.concatenate([vp_ref[...], vc_ref[...]], axis=0).astype(F32)
    kn, vv = {}, {}
    for u in range(SWA_KV // LANES):
        knorm = qk_norm(kall[:, u * LANES:(u + 1) * LANES], kg_ref[...])
        vtile = vall[:, u * LANES:(u + 1) * LANES]
        for ci in range(SWA_STEP_BLOCKS + 1):
            kc = knorm[ci * qb:(ci + 1) * qb]
            vc = vtile[ci * qb:(ci + 1) * qb]
            kn[u, ci] = (kc, pltpu.roll(kc, hd, axis=1))
            vv[u, ci] = (vc, pltpu.roll(vc, hd, axis=1))

    row = lax.broadcasted_iota(jnp.int32, (qb, 1), 0)
    is_sink = row == 0
    k2s, rhs = {}, {}
    for j in blocks:
        prev_live = (n * SWA_STEP_BLOCKS + j) > 0
        ones_prev = jnp.where(prev_live | is_sink, 1.0, 0.0)
        ones_band = jnp.concatenate([ones_prev, jnp.ones_like(ones_prev)], axis=0)
        ones2 = two_heads(ones_band, ones_band)
        for u in range(SWA_KV // LANES):
            kband = [jnp.concatenate([jnp.where(is_sink, 0.0, kn[u, j][r]), kn[u, j + 1][r]], axis=0)
                     for r in range(2)]
            vband = [jnp.concatenate([jnp.where(prev_live & ~is_sink, vv[u, j][r], 0.0), vv[u, j + 1][r]],
                                     axis=0) for r in range(2)]
            for half in range(2):
                g = 2 * u + half
                k2s[j, g] = two_heads(kband[half], kband[1 - half]).astype(BF16)
                v2 = two_heads(vband[half], vband[1 - half])
                rhs[j, g] = jnp.concatenate([v2, ones2], axis=1).astype(BF16)

    q_gain = qg_ref[...] * (hd ** -0.5 * LOG2E)
    kv_of = lambda tq: (2 * tq) // (SWA_HEADS // SWA_KV_HEADS)
    qt = {(j, tq): qk_norm(q_ref[j * qb:(j + 1) * qb, tq * LANES:(tq + 1) * LANES].astype(F32), q_gain)
          for j, tq in items}
    logits = {}
    for n in range(len(items) + SWA_LOGITS_AHEAD):
        if n < len(items):
            j, tq = items[n]
            logits[j, tq] = _mm_nt(qt[j, tq], k2s[j, kv_of(tq)])
        if n >= SWA_LOGITS_AHEAD:
            j, tq = items[n - SWA_LOGITS_AHEAD]
            both = logits.pop((j, tq))
            ps = []
            for e in range(2):
                s = both[:, e * 2 * qb:(e + 1) * 2 * qb].astype(BF16) + bias_ref[2 * tq + e]
                ps.append(jnp.exp2(s - jnp.max(s, axis=-1, keepdims=True)))
            pv = jnp.dot(jnp.concatenate(ps, axis=1), rhs[j, kv_of(tq)], preferred_element_type=F32)
            out = pv[:, :LANES] * (1.0 / pv[:, LANES:])
            z = z_ref[j * qb:(j + 1) * qb, tq * LANES:(tq + 1) * LANES].astype(F32)
            o_ref[j * qb:(j + 1) * qb, tq * LANES:(tq + 1) * LANES] = (out * _silu(z)).astype(o_ref.dtype)


def _swa(proj, bias, qg2, kg2):
    t = proj.shape[0]
    qb = SWA_BLOCK
    rows = SWA_STEP_BLOCKS * qb
    seg = np.kron(np.eye(LANES // SWA_HEAD_DIM), np.ones((SWA_HEAD_DIM, SWA_HEAD_DIM))) / SWA_HEAD_DIM
    seg = jnp.asarray(seg, F32)
    kcol = COL_KB // SWA_KV
    vcol = COL_VB // SWA_KV
    prev = lambda i: jnp.maximum(i * SWA_STEP_BLOCKS - 1, 0)
    return pl.pallas_call(
        _swa_kernel,
        out_shape=jax.ShapeDtypeStruct((t, SWA_Q), BF16),
        grid=(t // rows,),
        in_specs=[pl.BlockSpec((rows, SWA_Q), lambda i: (i, COL_QB // SWA_Q)),
                  pl.BlockSpec((rows, SWA_KV), lambda i: (i, kcol)),
                  pl.BlockSpec((qb, SWA_KV), lambda i: (prev(i), kcol)),
                  pl.BlockSpec((rows, SWA_KV), lambda i: (i, vcol)),
                  pl.BlockSpec((qb, SWA_KV), lambda i: (prev(i), vcol)),
                  pl.BlockSpec((rows, SWA_Q), lambda i: (i, COL_ZB // SWA_Q)),
                  pl.BlockSpec((SWA_HEADS, qb, 2 * qb), lambda i: (0, 0, 0)),
                  pl.BlockSpec((1, LANES), lambda i: (0, 0)),
                  pl.BlockSpec((1, LANES), lambda i: (0, 0)),
                  pl.BlockSpec((LANES, LANES), lambda i: (0, 0))],
        out_specs=pl.BlockSpec((rows, SWA_Q), lambda i: (i, 0)),
        compiler_params=_params(("arbitrary",)),
        name="swa",
    )(proj, proj, proj, proj, proj, proj, bias, qg2, kg2, seg)


def _merge_out_kernel(x_ref, oa_ref, ob_ref, ga_ref, gb_ref, gate_ref, wa_ref, wb_ref, wo_ref, o_ref):
    ya = jnp.dot(oa_ref[...], wa_ref[...], preferred_element_type=F32)
    yb = jnp.dot(ob_ref[...], wb_ref[...], preferred_element_type=F32)
    mixed = _sigmoid(ga_ref[...].astype(F32)) * ya + _sigmoid(gb_ref[...].astype(F32)) * yb
    y = jnp.dot(mixed.astype(BF16), wo_ref[...], preferred_element_type=F32)
    o_ref[...] = x_ref[...] + gate_ref[...] * y


def _merge_out(x2d, o_a, o_b, proj, gate, w_a, w_b, w_o, *, tm):
    t, d = x2d.shape
    const = lambda shape: pl.BlockSpec(shape, lambda i: (0, 0), pipeline_mode=pl.Buffered(1))
    return pl.pallas_call(
        _merge_out_kernel,
        out_shape=jax.ShapeDtypeStruct((t, d), F32),
        grid=(t // tm,),
        in_specs=[pl.BlockSpec((tm, d), lambda i: (i, 0)),
                  pl.BlockSpec((tm, GDN_V), lambda i: (i, 0)),
                  pl.BlockSpec((tm, SWA_Q), lambda i: (i, 0)),
                  pl.BlockSpec((tm, d), lambda i: (i, COL_GA // D_MODEL)),
                  pl.BlockSpec((tm, d), lambda i: (i, COL_GB // D_MODEL)),
                  pl.BlockSpec((1, d), lambda i: (0, 0)),
                  const((GDN_V, d)), const((SWA_Q, d)), const((d, d))],
        out_specs=pl.BlockSpec((tm, d), lambda i: (i, 0)),
        compiler_params=_params(("arbitrary",)),
        name="merge_out",
    )(x2d, o_a, o_b, proj, proj, gate, w_a, w_b, w_o)


REPACK_TILE = 512
N_GATE_COLS = 2 * GDN_HEADS


def _repack_plan():
    src_of = {COL_QKV: 0, COL_ZA: GDN_CONV_CH}
    after_gates = COL_QB + N_GATE_COLS
    src_of.update({COL_QB: after_gates, COL_KB: after_gates + SWA_Q, COL_VB: after_gates + SWA_Q + SWA_KV,
                   COL_ZB: after_gates + SWA_Q + 2 * SWA_KV, COL_GA: after_gates + 2 * SWA_Q + 2 * SWA_KV,
                   COL_GB: after_gates + 2 * SWA_Q + 2 * SWA_KV + D_MODEL})
    starts = sorted(src_of)
    src = []
    for dst in range(0, PROJ_COLS, REPACK_TILE):
        grp = max(s for s in starts if s <= dst)
        src.append(src_of[grp] + dst - grp)
    return np.asarray(src, np.int32)


def _repack_kernel(tab_ref, w_ref, g_ref, big_ref, small_ref):
    del tab_ref
    big_ref[...] = w_ref[...].astype(BF16)

    @pl.when(pl.program_id(0) == 0)
    def _():
        row = lax.broadcasted_iota(jnp.int32, (LANES, 1), 0)
        small_ref[...] = jnp.where(row < N_GATE_COLS, g_ref[...], 0.0).astype(BF16)


def _repack_w_in(w_t):
    d = w_t.shape[1]
    src = _repack_plan()
    gate_row = COL_QB
    assert gate_row % LANES == 0 and np.all(src % N_GATE_COLS == 0)
    return pl.pallas_call(
        _repack_kernel,
        out_shape=(jax.ShapeDtypeStruct((PROJ_COLS, d), BF16), jax.ShapeDtypeStruct((LANES, d), BF16)),
        grid_spec=pltpu.PrefetchScalarGridSpec(
            num_scalar_prefetch=1,
            grid=(PROJ_COLS // REPACK_TILE,),
            in_specs=[pl.BlockSpec((pl.Element(REPACK_TILE), pl.Element(d)),
                                   lambda o, tab: (tab[o] * N_GATE_COLS, 0)),
                      pl.BlockSpec((LANES, d), lambda o, tab: (gate_row // LANES, 0))],
            out_specs=(pl.BlockSpec((REPACK_TILE, d), lambda o, tab: (o, 0)),
                       pl.BlockSpec((LANES, d), lambda o, tab: (0, 0)))),
        compiler_params=_params(("arbitrary",)),
        name="repack_w_in",
    )(jnp.asarray(src // N_GATE_COLS), w_t, w_t)


def _lane_row(vec, offset):
    return jnp.pad(vec.astype(F32), (offset, LANES - offset - vec.shape[0]))[None, :]


def kernel(x, c, w_ada, b_ada, norm_gain, w_in, conv_w, a_log, dt_bias, gdn_norm_gain, q_norm_gain,
           k_norm_gain, sinks, rel_bias, w_branch_gdn, w_branch_swa, w_out):
    bsz, t, d = x.shape
    depth = w_in.shape[0]
    outs = []
    for b in range(bsz):
        xb = x[b]
        c_col = c[b].astype(F32)[:, None]
        for l in range(depth):
            mod = _ada_mod(c_col, w_ada[l], b_ada[l][None, :])
            shift, scale, gate = mod[:, :d], mod[:, d:2 * d], mod[:, 2 * d:]
            w_big, w_small = _repack_w_in(w_in[l].T)
            proj, ba, (w_a, w_b, w_o) = _in_proj(
                xb, norm_gain[l][None, :], scale, shift, w_big, w_small,
                (w_branch_gdn[l], w_branch_swa[l], w_out[l]), tm=min(1024, t), tn=1792)
            o_a = _gdn(proj, ba, conv_w[l], _lane_row(a_log[l], GDN_HEADS), _lane_row(dt_bias[l], GDN_HEADS),
                       gdn_norm_gain[l][None, :])
            bias = _swa_bias(rel_bias.T.astype(F32), sinks[l].astype(F32))
            o_b = _swa(proj, bias,
                       jnp.tile(q_norm_gain[l], LANES // SWA_HEAD_DIM)[None, :],
                       jnp.tile(k_norm_gain[l], LANES // SWA_HEAD_DIM)[None, :])
            xb = _merge_out(xb, o_a, o_b, proj, gate, w_a, w_b, w_o, tm=min(512, t))
        outs.append(xb)
    return jnp.stack(outs, axis=0)
```

```python
import functools
import math

import jax
import jax.numpy as jnp
import numpy as np
from jax import lax
from jax.experimental import pallas as pl
from jax.experimental.pallas import tpu as pltpu

F32 = jnp.float32
BF16 = jnp.bfloat16

LANES = 128
D_MODEL = 2048
GDN_HEADS = 8
GDN_DK = 128
GDN_DV = 128
GDN_CONV = 4
GDN_QK = GDN_HEADS * GDN_DK
GDN_V = GDN_HEADS * GDN_DV
GDN_CONV_CH = 2 * GDN_QK + GDN_V
GDN_BLOCK = 128
GDN_STEP_BLOCKS = 4
GDN_CONV_PIECE = 512
GDN_BLOCK_LAG = 4
SWA_HEADS = 16
SWA_KV_HEADS = 4
SWA_HEAD_DIM = 64
SWA_WINDOW = 128
SWA_BLOCK = 128
SWA_STEP_BLOCKS = 8
SWA_LOGITS_AHEAD = 4
SWA_Q = SWA_HEADS * SWA_HEAD_DIM
SWA_KV = SWA_KV_HEADS * SWA_HEAD_DIM
REL_BUCKETS = 32
REL_MAX_DIST = 128
EPS = 1e-6
NEG_BIG = -1e30
LOG2E = math.log2(math.e)

COL_QKV = 0
COL_ZA = GDN_CONV_CH
COL_QB = COL_ZA + GDN_V
COL_ZB = COL_QB + SWA_Q
COL_GA = COL_ZB + SWA_Q
COL_GB = COL_GA + D_MODEL
COL_KB = COL_GB + D_MODEL
COL_VB = COL_KB + SWA_KV
PROJ_COLS = COL_VB + SWA_KV
for _col, _width in ((COL_ZA, GDN_V), (COL_QB, SWA_Q), (COL_ZB, SWA_Q), (COL_GA, D_MODEL),
                     (COL_GB, D_MODEL), (COL_KB, SWA_KV), (COL_VB, SWA_KV)):
    assert _col % _width == 0

VMEM_LIMIT = 56 * 1024 * 1024


def _sigmoid(x):
    return 0.5 + 0.5 * jnp.tanh(0.5 * x)


def _silu(x):
    half = 0.5 * x
    return half + half * jnp.tanh(half)


def _params(sem):
    return pltpu.CompilerParams(dimension_semantics=sem, vmem_limit_bytes=VMEM_LIMIT)


def _mm(a, b):
    return jnp.dot(a.astype(BF16), b.astype(BF16), preferred_element_type=F32)


def _mm_nt(a, b):
    return lax.dot_general(a.astype(BF16), b.astype(BF16), (((1,), (1,)), ((), ())),
                           preferred_element_type=F32)


def _mm_tn(a, b):
    return lax.dot_general(a.astype(BF16), b.astype(BF16), (((0,), (0,)), ((), ())),
                           preferred_element_type=F32)


def _ada_mod_kernel(c_ref, w_ref, b_ref, o_ref):
    c = c_ref[...]
    o_ref[...] = jnp.sum(_silu(c) * w_ref[...], axis=0, keepdims=True) + b_ref[...]


def _ada_mod(c_col, w_ada, b_ada):
    d, n = w_ada.shape
    tn = 1024
    return pl.pallas_call(
        _ada_mod_kernel,
        out_shape=jax.ShapeDtypeStruct((1, n), F32),
        grid=(n // tn,),
        in_specs=[pl.BlockSpec((d, 1), lambda j: (0, 0)),
                  pl.BlockSpec((d, tn), lambda j: (0, j)),
                  pl.BlockSpec((1, tn), lambda j: (0, j))],
        out_specs=pl.BlockSpec((1, tn), lambda j: (0, j)),
        compiler_params=_params(("arbitrary",)),
        name="ada_mod",
    )(c_col, w_ada, b_ada)


def _in_proj_kernel(x_ref, gain_ref, scale_ref, shift_ref, w_ref, ws_ref, *rest, row_chunk, n_side):
    side_in, (o_ref, ba_ref), side_out, h_ref = (rest[:n_side], rest[n_side:n_side + 2],
                                                 rest[n_side + 2:-1], rest[-1])
    for src, dst in zip(side_in, side_out):
        dst[...] = src[...].astype(dst.dtype)
    j = pl.program_id(1)

    @pl.when(j == 0)
    def _():
        gs = gain_ref[...] * (1.0 + scale_ref[...])
        sh = shift_ref[...]
        tm = x_ref.shape[0]
        for r in range(tm // row_chunk):
            rows = slice(r * row_chunk, (r + 1) * row_chunk)
            x = x_ref[rows, :]
            ms = jnp.mean(x * x, axis=-1, keepdims=True)
            h = ((x * lax.rsqrt(ms + EPS)) * gs + sh).astype(BF16)
            h_ref[rows, :] = h
            ba_ref[rows, :] = _mm_nt(h, ws_ref[...])

    o_ref[...] = _mm_nt(h_ref[...], w_ref[...]).astype(o_ref.dtype)


def _in_proj(x2d, gain, scale, shift, w_big, w_small, side_weights, *, tm, tn):
    t, d = x2d.shape
    n = w_big.shape[0]
    n_i, n_j = t // tm, n // tn
    slabs = 1 << ((n_i * n_j).bit_length() - 1)
    slab_of = lambda i, j: jnp.minimum(i * n_j + j, slabs - 1)
    side_specs = [pl.BlockSpec((w.shape[0] // slabs, w.shape[1]), lambda i, j: (slab_of(i, j), 0))
                  for w in side_weights]
    assert all(w.shape[0] % (16 * slabs) == 0 for w in side_weights)
    out = pl.pallas_call(
        functools.partial(_in_proj_kernel, row_chunk=128, n_side=len(side_weights)),
        out_shape=(jax.ShapeDtypeStruct((t, n), BF16), jax.ShapeDtypeStruct((t, LANES), F32),
                   *[jax.ShapeDtypeStruct(w.shape, BF16) for w in side_weights]),
        grid=(n_i, n_j),
        in_specs=[pl.BlockSpec((tm, d), lambda i, j: (i, 0)),
                  pl.BlockSpec((1, d), lambda i, j: (0, 0)),
                  pl.BlockSpec((1, d), lambda i, j: (0, 0)),
                  pl.BlockSpec((1, d), lambda i, j: (0, 0)),
                  pl.BlockSpec((tn, d), lambda i, j: (j, 0)),
                  pl.BlockSpec((LANES, d), lambda i, j: (0, 0)),
                  *side_specs],
        out_specs=(pl.BlockSpec((tm, tn), lambda i, j: (i, j)),
                   pl.BlockSpec((tm, LANES), lambda i, j: (i, 0)),
                   *side_specs),
        scratch_shapes=[pltpu.VMEM((tm, d), BF16)],
        compiler_params=_params(("arbitrary", "arbitrary")),
        name="in_proj",
    )(x2d, gain, scale, shift, w_big, w_small, *side_weights)
    return out[0], out[1], out[2:]


def _swa_bias_kernel(tab_ref, sink_ref, o_ref):
    q = SWA_BLOCK
    qpos = lax.broadcasted_iota(jnp.int32, (q, 2 * q), 0) + q
    kpos = lax.broadcasted_iota(jnp.int32, (q, 2 * q), 1)
    dist = qpos - kpos
    in_window = (dist >= 0) & (dist < SWA_WINDOW)
    d = jnp.maximum(dist, 0)
    max_exact = REL_BUCKETS // 2
    df = jnp.maximum(d, 1).astype(F32)
    large = max_exact + (jnp.log(df / max_exact) / math.log(REL_MAX_DIST / max_exact)
                         * (REL_BUCKETS - max_exact)).astype(jnp.int32)
    large = jnp.minimum(large, REL_BUCKETS - 1)
    bucket = jnp.where(in_window, jnp.where(d < max_exact, d, large), REL_BUCKETS)
    for h in range(SWA_HEADS):
        acc = jnp.full((q, 2 * q), NEG_BIG, F32)
        for b in range(REL_BUCKETS):
            acc = jnp.where(bucket == b, tab_ref[h, b], acc)
        o_ref[h] = (jnp.where(kpos == 0, sink_ref[h], acc) * LOG2E).astype(o_ref.dtype)


def _swa_bias(rel_bias_t, sinks):
    q = SWA_BLOCK
    return pl.pallas_call(
        _swa_bias_kernel,
        out_shape=jax.ShapeDtypeStruct((SWA_HEADS, q, 2 * q), BF16),
        in_specs=[pl.BlockSpec(memory_space=pltpu.SMEM), pl.BlockSpec(memory_space=pltpu.SMEM)],
        out_specs=pl.BlockSpec(memory_space=pltpu.VMEM),
        compiler_params=pltpu.CompilerParams(vmem_limit_bytes=VMEM_LIMIT),
        name="swa_bias",
    )(rel_bias_t, sinks)


def _chunk_cumsum_rows(x):
    n = x.shape[0]
    row = lax.broadcasted_iota(jnp.int32, x.shape, 0)
    s = 1
    while s < n:
        x = x + jnp.where(row >= s, pltpu.roll(x, s, axis=0), 0.0)
        s *= 2
    return x


def _unit_lower_inverse_stages(l_mats, masks_ref, out):
    eye = masks_ref[0].astype(F32)
    l_bf = [l.astype(BF16) for l in l_mats]
    m0 = [l * masks_ref[1] for l in l_bf]
    x = [eye - m for m in m0]
    p = [_mm(m, m) for m in m0]
    yield
    x = [xi + _mm(xi, pi) for xi, pi in zip(x, p)]
    yield
    p = [_mm(pi, pi) for pi in p]
    yield
    x = [xi + _mm(xi, pi) for xi, pi in zip(x, p)]
    yield
    for lvl in range(2, masks_ref.shape[0]):
        nx = [_mm(l * masks_ref[lvl], xi) for l, xi in zip(l_bf, x)]
        yield
        x = [xi - _mm(xi, ni) for xi, ni in zip(x, nx)]
        yield
    out.extend(x)


def _gdn_kernel(cur_ref, prev_ref, z_ref, ba_ref, convw_ref, band_ref, alog_ref, dtb_ref, gain_ref,
                masks_ref, o_ref, s_ref):
    s = pl.program_id(0)
    c = GDN_BLOCK
    heads = range(GDN_HEADS)
    blocks = range(GDN_STEP_BLOCKS)
    items = [(b, h) for b in blocks for h in heads]
    lane_of = lambda a, h: a[:, GDN_HEADS + h:GDN_HEADS + h + 1]

    @pl.when(s == 0)
    def _():
        s_ref[...] = jnp.zeros_like(s_ref)

    pad = prev_ref.shape[0]
    conv_w = convw_ref[...].astype(BF16)
    conv_pieces = {}
    for b in blocks:
        if b == 0:
            ctx = jnp.where(s > 0, prev_ref[...], jnp.zeros_like(prev_ref))
        else:
            ctx = cur_ref[b * c - pad:b * c, :]
        xcat = jnp.concatenate([ctx, cur_ref[b * c:(b + 1) * c, :]], axis=0)
        taps = jnp.concatenate([xcat * conv_w[i:i + 1, :] for i in range(GDN_CONV)], axis=0)
        for p in range(GDN_CONV_CH // GDN_CONV_PIECE):
            cols = slice(p * GDN_CONV_PIECE, (p + 1) * GDN_CONV_PIECE)
            conv_pieces[b, p] = jnp.dot(band_ref[...], taps[:, cols], preferred_element_type=F32)

    row = lax.broadcasted_iota(jnp.int32, (c, c), 0)
    col = lax.broadcasted_iota(jnp.int32, (c, c), 1)
    causal = row >= col

    def l2n(a, scale=1.0):
        return a * (lax.rsqrt(jnp.sum(a * a, axis=-1, keepdims=True) + EPS) * scale)

    q, k, kb, vb, kbg, qd, kd, dec, el = ({} for _ in range(9))
    for b in blocks:
        ba = ba_ref[b * c:(b + 1) * c, :]
        beta_all = _sigmoid(ba)
        xg = ba + dtb_ref[...]
        softplus = jnp.maximum(xg, 0.0) + jnp.log(1.0 + jnp.exp(-jnp.abs(xg)))
        g_all = -jnp.exp(alog_ref[...]) * softplus
        gc = _chunk_cumsum_rows(g_all)
        gc_t = gc.T
        g_last = gc[c - 1:c, :]
        eg_all = jnp.exp(gc)
        ekd_all = jnp.exp(g_last - gc)
        el[b] = jnp.exp(g_last)
        conv_tile = lambda col0: _silu(conv_pieces[b, col0 // GDN_CONV_PIECE][
            :, col0 % GDN_CONV_PIECE:col0 % GDN_CONV_PIECE + LANES])
        for h in heads:
            qf = l2n(conv_tile(h * GDN_DK), GDN_DK ** -0.5)
            kf = l2n(conv_tile(GDN_QK + h * GDN_DK))
            vf = conv_tile(2 * GDN_QK + h * GDN_DV)
            beta = beta_all[:, h:h + 1]
            eg = lane_of(eg_all, h)
            kbf = kf * beta
            gdiff = lane_of(gc, h) - gc_t[GDN_HEADS + h:GDN_HEADS + h + 1, :]
            q[b, h] = qf.astype(BF16)
            k[b, h] = kf.astype(BF16)
            kb[b, h] = kbf.astype(BF16)
            vb[b, h] = vf * beta
            kbg[b, h] = (kbf * eg).astype(BF16)
            qd[b, h] = (qf * eg).astype(BF16)
            kd[b, h] = (kf * lane_of(ekd_all, h)).astype(BF16)
            dec[b, h] = jnp.where(causal, jnp.exp(gdiff), 0.0)

    attn, t_inv, o_tiles = {}, {}, {}
    state = [s_ref[h] for h in heads]

    def local_chain(b):
        kk = [_mm_nt(kb[b, h], k[b, h]) for h in heads]
        yield
        qk = [_mm_nt(q[b, h], k[b, h]) for h in heads]
        yield
        l_mat = [kk[h] * dec[b, h] for h in heads]
        for h in heads:
            attn[b, h] = (qk[h] * dec[b, h]).astype(BF16)
        inv = []
        yield from _unit_lower_inverse_stages(l_mat, masks_ref, inv)
        for h in heads:
            t_inv[b, h] = inv[h]

    def recurrence(b):
        s_bf = [state[h].astype(BF16) for h in heads]
        resid = [vb[b, h] - _mm(kbg[b, h], s_bf[h]) for h in heads]
        yield
        v_new = [_mm(t_inv[b, h], resid[h]).astype(BF16) for h in heads]
        yield
        for h in heads:
            o_tiles[b, h] = _mm(jnp.concatenate([qd[b, h], attn[b, h]], axis=1),
                                jnp.concatenate([s_bf[h], v_new[h]], axis=0))
            state[h] = state[h] * lane_of(el[b], h) + _mm_tn(kd[b, h], v_new[h])
        yield
        for h in heads:
            o = o_tiles[b, h]
            on = o * lax.rsqrt(jnp.mean(o * o, axis=-1, keepdims=True) + EPS) * gain_ref[...]
            z = z_ref[b * c:(b + 1) * c, h * GDN_DV:(h + 1) * GDN_DV].astype(F32)
            o_ref[b * c:(b + 1) * c, h * GDN_DV:(h + 1) * GDN_DV] = (on * _silu(z)).astype(o_ref.dtype)

    def advance(gen):
        return next(gen, StopIteration) is not StopIteration

    chains = [local_chain(b) for b in blocks]
    alive = [True] * len(chains)
    wave = 0
    recs = []
    while any(alive) or recs:
        for b in blocks:
            if alive[b] and wave >= b * GDN_BLOCK_LAG:
                alive[b] = advance(chains[b])
                if not alive[b]:
                    recs.append(recurrence(b))
        if recs and (len(recs) == 1 or not alive[0]):
            if not advance(recs[0]):
                recs.pop(0)
        wave += 1

    for h in heads:
        s_ref[h] = state[h]


def _gdn_masks():
    c = GDN_BLOCK
    r = np.arange(c)[:, None]
    k = np.arange(c)[None, :]
    mats = [np.eye(c), (r // 8 == k // 8) & (r > k)]
    b = 8
    while b < c:
        mats.append((r // (2 * b) == k // (2 * b)) & ((r // b) % 2 == 1) & ((k // b) % 2 == 0))
        b *= 2
    return jnp.asarray(np.stack([np.asarray(m, np.float32) for m in mats]), BF16)


def _conv_band(c, pad):
    band = np.zeros((c, GDN_CONV * (pad + c)), np.float32)
    for i in range(GDN_CONV):
        band[np.arange(c), i * (pad + c) + pad + np.arange(c) - (GDN_CONV - 1) + i] = 1.0
    return jnp.asarray(band, BF16)


def _gdn(proj, ba, conv_w, alog_lane, dtb_lane, gain):
    t = proj.shape[0]
    c = GDN_BLOCK
    rows = GDN_STEP_BLOCKS * c
    pad = 16
    masks = _gdn_masks()
    nm = masks.shape[0]
    band = _conv_band(c, pad)
    return pl.pallas_call(
        _gdn_kernel,
        out_shape=jax.ShapeDtypeStruct((t, GDN_V), BF16),
        grid=(t // rows,),
        in_specs=[pl.BlockSpec((rows, GDN_CONV_CH), lambda i: (i, 0)),
                  pl.BlockSpec((pad, GDN_CONV_CH), lambda i: (jnp.maximum(i * (rows // pad) - 1, 0), 0)),
                  pl.BlockSpec((rows, GDN_V), lambda i: (i, COL_ZA // GDN_V)),
                  pl.BlockSpec((rows, LANES), lambda i: (i, 0)),
                  pl.BlockSpec((GDN_CONV, GDN_CONV_CH), lambda i: (0, 0)),
                  pl.BlockSpec(band.shape, lambda i: (0, 0)),
                  pl.BlockSpec((1, LANES), lambda i: (0, 0)),
                  pl.BlockSpec((1, LANES), lambda i: (0, 0)),
                  pl.BlockSpec((1, GDN_DV), lambda i: (0, 0)),
                  pl.BlockSpec((nm, c, c), lambda i: (0, 0, 0))],
        out_specs=pl.BlockSpec((rows, GDN_V), lambda i: (i, 0)),
        scratch_shapes=[pltpu.VMEM((GDN_HEADS, GDN_DK, GDN_DV), F32)],
        compiler_params=_params(("arbitrary",)),
        name="gdn",
    )(proj, proj, proj, ba, conv_w, band, alog_lane, dtb_lane, gain, masks)


def _swa_kernel(q_ref, kc_ref, kp_ref, vc_ref, vp_ref, z_ref, bias_ref, qg_ref, kg_ref, seg_ref, o_ref):
    n = pl.program_id(0)
    qb = SWA_BLOCK
    hd = SWA_HEAD_DIM
    n_tiles = SWA_Q // LANES
    blocks = range(SWA_STEP_BLOCKS)
    items = [(j, tq) for j in blocks for tq in range(n_tiles)]
    lane = lax.broadcasted_iota(jnp.int32, (1, LANES), 1)
    lo = lane < hd
    seg = seg_ref[...].astype(BF16)

    def qk_norm(a, gain):
        ms = jnp.dot((a * a).astype(BF16), seg, preferred_element_type=F32)
        return a * (lax.rsqrt(ms + EPS) * gain)

    def two_heads(lo_part, hi_part):
        return jnp.concatenate([jnp.where(lo, lo_part, 0.0), jnp.where(lo, 0.0, hi_part)], axis=0)

    kall = jnp.concatenate([kp_ref[...], kc_ref[...]], axis=0).astype(F32)
    vall = jnp.concatenate([vp_ref[...], vc_ref[...]], axis=0).astype(F32)
    kn, vv = {}, {}
    for u in range(SWA_KV // LANES):
        knorm = qk_norm(kall[:, u * LANES:(u + 1) * LANES], kg_ref[...])
        vtile = vall[:, u * LANES:(u + 1) * LANES]
        for ci in range(SWA_STEP_BLOCKS + 1):
            kc = knorm[ci * qb:(ci + 1) * qb]
            vc = vtile[ci * qb:(ci + 1) * qb]
            kn[u, ci] = (kc, pltpu.roll(kc, hd, axis=1))
            vv[u, ci] = (vc, pltpu.roll(vc, hd, axis=1))

    row = lax.broadcasted_iota(jnp.int32, (qb, 1), 0)
    is_sink = row == 0
    k2s, rhs = {}, {}
    for j in blocks:
        prev_live = (n * SWA_STEP_BLOCKS + j) > 0
        ones_prev = jnp.where(prev_live | is_sink, 1.0, 0.0)
        ones_band = jnp.concatenate([ones_prev, jnp.ones_like(ones_prev)], axis=0)
        ones2 = two_heads(ones_band, ones_band)
        for u in range(SWA_KV // LANES):
            kband = [jnp.concatenate([jnp.where(is_sink, 0.0, kn[u, j][r]), kn[u, j + 1][r]], axis=0)
                     for r in range(2)]
            vband = [jnp.concatenate([jnp.where(prev_live & ~is_sink, vv[u, j][r], 0.0), vv[u, j + 1][r]],
                                     axis=0) for r in range(2)]
            for half in range(2):
                g = 2 * u + half
                k2s[j, g] = two_heads(kband[half], kband[1 - half]).astype(BF16)
                v2 = two_heads(vband[half], vband[1 - half])
                rhs[j, g] = jnp.concatenate([v2, ones2], axis=1).astype(BF16)

    q_gain = qg_ref[...] * (hd ** -0.5 * LOG2E)
    kv_of = lambda tq: (2 * tq) // (SWA_HEADS // SWA_KV_HEADS)
    qt = {(j, tq): qk_norm(q_ref[j * qb:(j + 1) * qb, tq * LANES:(tq + 1) * LANES].astype(F32), q_gain)
          for j, tq in items}
    logits = {}
    for n in range(len(items) + SWA_LOGITS_AHEAD):
        if n < len(items):
            j, tq = items[n]
            logits[j, tq] = _mm_nt(qt[j, tq], k2s[j, kv_of(tq)])
        if n >= SWA_LOGITS_AHEAD:
            j, tq = items[n - SWA_LOGITS_AHEAD]
            both = logits.pop((j, tq))
            ps = []
            for e in range(2):
                s = both[:, e * 2 * qb:(e + 1) * 2 * qb].astype(BF16) + bias_ref[2 * tq + e]
                ps.append(jnp.exp2(s - jnp.max(s, axis=-1, keepdims=True)))
            pv = jnp.dot(jnp.concatenate(ps, axis=1), rhs[j, kv_of(tq)], preferred_element_type=F32)
            out = pv[:, :LANES] * (1.0 / pv[:, LANES:])
            z = z_ref[j * qb:(j + 1) * qb, tq * LANES:(tq + 1) * LANES].astype(F32)
            o_ref[j * qb:(j + 1) * qb, tq * LANES:(tq + 1) * LANES] = (out * _silu(z)).astype(o_ref.dtype)


def _swa(proj, bias, qg2, kg2):
    t = proj.shape[0]
    qb = SWA_BLOCK
    rows = SWA_STEP_BLOCKS * qb
    seg = np.kron(np.eye(LANES // SWA_HEAD_DIM), np.ones((SWA_HEAD_DIM, SWA_HEAD_DIM))) / SWA_HEAD_DIM
    seg = jnp.asarray(seg, F32)
    kcol = COL_KB // SWA_KV
    vcol = COL_VB // SWA_KV
    prev = lambda i: jnp.maximum(i * SWA_STEP_BLOCKS - 1, 0)
    return pl.pallas_call(
        _swa_kernel,
        out_shape=jax.ShapeDtypeStruct((t, SWA_Q), BF16),
        grid=(t // rows,),
        in_specs=[pl.BlockSpec((rows, SWA_Q), lambda i: (i, COL_QB // SWA_Q)),
                  pl.BlockSpec((rows, SWA_KV), lambda i: (i, kcol)),
                  pl.BlockSpec((qb, SWA_KV), lambda i: (prev(i), kcol)),
                  pl.BlockSpec((rows, SWA_KV), lambda i: (i, vcol)),
                  pl.BlockSpec((qb, SWA_KV), lambda i: (prev(i), vcol)),
                  pl.BlockSpec((rows, SWA_Q), lambda i: (i, COL_ZB // SWA_Q)),
                  pl.BlockSpec((SWA_HEADS, qb, 2 * qb), lambda i: (0, 0, 0)),
                  pl.BlockSpec((1, LANES), lambda i: (0, 0)),
                  pl.BlockSpec((1, LANES), lambda i: (0, 0)),
                  pl.BlockSpec((LANES, LANES), lambda i: (0, 0))],
        out_specs=pl.BlockSpec((rows, SWA_Q), lambda i: (i, 0)),
        compiler_params=_params(("arbitrary",)),
        name="swa",
    )(proj, proj, proj, proj, proj, proj, bias, qg2, kg2, seg)


def _merge_out_kernel(x_ref, oa_ref, ob_ref, ga_ref, gb_ref, gate_ref, wa_ref, wb_ref, wo_ref, o_ref):
    ya = jnp.dot(oa_ref[...], wa_ref[...], preferred_element_type=F32)
    yb = jnp.dot(ob_ref[...], wb_ref[...], preferred_element_type=F32)
    mixed = _sigmoid(ga_ref[...].astype(F32)) * ya + _sigmoid(gb_ref[...].astype(F32)) * yb
    y = jnp.dot(mixed.astype(BF16), wo_ref[...], preferred_element_type=F32)
    o_ref[...] = x_ref[...] + gate_ref[...] * y


def _merge_out(x2d, o_a, o_b, proj, gate, w_a, w_b, w_o, *, tm):
    t, d = x2d.shape
    const = lambda shape: pl.BlockSpec(shape, lambda i: (0, 0), pipeline_mode=pl.Buffered(1))
    return pl.pallas_call(
        _merge_out_kernel,
        out_shape=jax.ShapeDtypeStruct((t, d), F32),
        grid=(t // tm,),
        in_specs=[pl.BlockSpec((tm, d), lambda i: (i, 0)),
                  pl.BlockSpec((tm, GDN_V), lambda i: (i, 0)),
                  pl.BlockSpec((tm, SWA_Q), lambda i: (i, 0)),
                  pl.BlockSpec((tm, d), lambda i: (i, COL_GA // D_MODEL)),
                  pl.BlockSpec((tm, d), lambda i: (i, COL_GB // D_MODEL)),
                  pl.BlockSpec((1, d), lambda i: (0, 0)),
                  const((GDN_V, d)), const((SWA_Q, d)), const((d, d))],
        out_specs=pl.BlockSpec((tm, d), lambda i: (i, 0)),
        compiler_params=_params(("arbitrary",)),
        name="merge_out",
    )(x2d, o_a, o_b, proj, proj, gate, w_a, w_b, w_o)


REPACK_TILE = 512
N_GATE_COLS = 2 * GDN_HEADS


def _repack_plan():
    src_of = {COL_QKV: 0, COL_ZA: GDN_CONV_CH}
    after_gates = COL_QB + N_GATE_COLS
    src_of.update({COL_QB: after_gates, COL_KB: after_gates + SWA_Q, COL_VB: after_gates + SWA_Q + SWA_KV,
                   COL_ZB: after_gates + SWA_Q + 2 * SWA_KV, COL_GA: after_gates + 2 * SWA_Q + 2 * SWA_KV,
                   COL_GB: after_gates + 2 * SWA_Q + 2 * SWA_KV + D_MODEL})
    starts = sorted(src_of)
    src = []
    for dst in range(0, PROJ_COLS, REPACK_TILE):
        grp = max(s for s in starts if s <= dst)
        src.append(src_of[grp] + dst - grp)
    return np.asarray(src, np.int32)


def _repack_kernel(tab_ref, w_ref, g_ref, big_ref, small_ref):
    del tab_ref
    big_ref[...] = w_ref[...].astype(BF16)

    @pl.when(pl.program_id(0) == 0)
    def _():
        row = lax.broadcasted_iota(jnp.int32, (LANES, 1), 0)
        small_ref[...] = jnp.where(row < N_GATE_COLS, g_ref[...], 0.0).astype(BF16)


def _repack_w_in(w_t):
    d = w_t.shape[1]
    src = _repack_plan()
    gate_row = COL_QB
    assert gate_row % LANES == 0 and np.all(src % N_GATE_COLS == 0)
    return pl.pallas_call(
        _repack_kernel,
        out_shape=(jax.ShapeDtypeStruct((PROJ_COLS, d), BF16), jax.ShapeDtypeStruct((LANES, d), BF16)),
        grid_spec=pltpu.PrefetchScalarGridSpec(
            num_scalar_prefetch=1,
            grid=(PROJ_COLS // REPACK_TILE,),
            in_specs=[pl.BlockSpec((pl.Element(REPACK_TILE), pl.Element(d)),
                                   lambda o, tab: (tab[o] * N_GATE_COLS, 0)),
                      pl.BlockSpec((LANES, d), lambda o, tab: (gate_row // LANES, 0))],
            out_specs=(pl.BlockSpec((REPACK_TILE, d), lambda o, tab: (o, 0)),
                       pl.BlockSpec((LANES, d), lambda o, tab: (0, 0)))),
        compiler_params=_params(("arbitrary",)),
        name="repack_w_in",
    )(jnp.asarray(src // N_GATE_COLS), w_t, w_t)


def _lane_row(vec, offset):
    return jnp.pad(vec.astype(F32), (offset, LANES - offset - vec.shape[0]))[None, :]


def kernel(x, c, w_ada, b_ada, norm_gain, w_in, conv_w, a_log, dt_bias, gdn_norm_gain, q_norm_gain,
           k_norm_gain, sinks, rel_bias, w_branch_gdn, w_branch_swa, w_out):
    bsz, t, d = x.shape
    depth = w_in.shape[0]
    outs = []
    for b in range(bsz):
        xb = x[b]
        c_col = c[b].astype(F32)[:, None]
        for l in range(depth):
            mod = _ada_mod(c_col, w_ada[l], b_ada[l][None, :])
            shift, scale, gate = mod[:, :d], mod[:, d:2 * d], mod[:, 2 * d:]
            w_big, w_small = _repack_w_in(w_in[l].T)
            proj, ba, (w_a, w_b, w_o) = _in_proj(
                xb, norm_gain[l][None, :], scale, shift, w_big, w_small,
                (w_branch_gdn[l], w_branch_swa[l], w_out[l]), tm=min(1024, t), tn=1792)
            o_a = _gdn(proj, ba, conv_w[l], _lane_row(a_log[l], GDN_HEADS), _lane_row(dt_bias[l], GDN_HEADS),
                       gdn_norm_gain[l][None, :])
            bias = _swa_bias(rel_bias.T.astype(F32), sinks[l].astype(F32))
            o_b = _swa(proj, bias,
                       jnp.tile(q_norm_gain[l], LANES // SWA_HEAD_DIM)[None, :],
                       jnp.tile(k_norm_gain[l], LANES // SWA_HEAD_DIM)[None, :])
            xb = _merge_out(xb, o_a, o_b, proj, gate, w_a, w_b, w_o, tm=min(512, t))
        outs.append(xb)
    return jnp.stack(outs, axis=0)
```

```python
import functools
import math

import jax
import jax.numpy as jnp
import numpy as np
from jax import lax
from jax.experimental import pallas as pl
from jax.experimental.pallas import tpu as pltpu

F32 = jnp.float32
BF16 = jnp.bfloat16

LANES = 128
D_MODEL = 2048
GDN_HEADS = 8
GDN_DK = 128
GDN_DV = 128
GDN_CONV = 4
GDN_QK = GDN_HEADS * GDN_DK
GDN_V = GDN_HEADS * GDN_DV
GDN_CONV_CH = 2 * GDN_QK + GDN_V
GDN_BLOCK = 128
GDN_STEP_BLOCKS = 4
GDN_CONV_PIECE = 512
GDN_GROUP_HEADS = 8
GDN_UNIT_LAG = 4
SWA_HEADS = 16
SWA_KV_HEADS = 4
SWA_HEAD_DIM = 64
SWA_WINDOW = 128
SWA_BLOCK = 128
SWA_STEP_BLOCKS = 8
SWA_LOGITS_AHEAD = 4
SWA_Q = SWA_HEADS * SWA_HEAD_DIM
SWA_KV = SWA_KV_HEADS * SWA_HEAD_DIM
REL_BUCKETS = 32
REL_MAX_DIST = 128
EPS = 1e-6
NEG_BIG = -1e30
LOG2E = math.log2(math.e)

COL_QKV = 0
COL_ZA = GDN_CONV_CH
COL_QB = COL_ZA + GDN_V
COL_ZB = COL_QB + SWA_Q
COL_GA = COL_ZB + SWA_Q
COL_GB = COL_GA + D_MODEL
COL_KB = COL_GB + D_MODEL
COL_VB = COL_KB + SWA_KV
PROJ_COLS = COL_VB + SWA_KV
for _col, _width in ((COL_ZA, GDN_V), (COL_QB, SWA_Q), (COL_ZB, SWA_Q), (COL_GA, D_MODEL),
                     (COL_GB, D_MODEL), (COL_KB, SWA_KV), (COL_VB, SWA_KV)):
    assert _col % _width == 0

VMEM_LIMIT = 56 * 1024 * 1024


def _sigmoid(x):
    return 0.5 + 0.5 * jnp.tanh(0.5 * x)


def _silu(x):
    half = 0.5 * x
    return half + half * jnp.tanh(half)


def _params(sem):
    return pltpu.CompilerParams(dimension_semantics=sem, vmem_limit_bytes=VMEM_LIMIT)


def _mm(a, b):
    return jnp.dot(a.astype(BF16), b.astype(BF16), preferred_element_type=F32)


def _mm_nt(a, b):
    return lax.dot_general(a.astype(BF16), b.astype(BF16), (((1,), (1,)), ((), ())),
                           preferred_element_type=F32)


def _mm_tn(a, b):
    return lax.dot_general(a.astype(BF16), b.astype(BF16), (((0,), (0,)), ((), ())),
                           preferred_element_type=F32)


def _ada_mod_kernel(c_ref, w_ref, b_ref, o_ref):
    c = c_ref[...]
    o_ref[...] = jnp.sum(_silu(c) * w_ref[...], axis=0, keepdims=True) + b_ref[...]


def _ada_mod(c_col, w_ada, b_ada):
    d, n = w_ada.shape
    tn = 1024
    return pl.pallas_call(
        _ada_mod_kernel,
        out_shape=jax.ShapeDtypeStruct((1, n), F32),
        grid=(n // tn,),
        in_specs=[pl.BlockSpec((d, 1), lambda j: (0, 0)),
                  pl.BlockSpec((d, tn), lambda j: (0, j)),
                  pl.BlockSpec((1, tn), lambda j: (0, j))],
        out_specs=pl.BlockSpec((1, tn), lambda j: (0, j)),
        compiler_params=_params(("arbitrary",)),
        name="ada_mod",
    )(c_col, w_ada, b_ada)


def _in_proj_kernel(x_ref, gain_ref, scale_ref, shift_ref, w_ref, ws_ref, *rest, row_chunk, n_side):
    side_in, (o_ref, ba_ref), side_out, h_ref = (rest[:n_side], rest[n_side:n_side + 2],
                                                 rest[n_side + 2:-1], rest[-1])
    for src, dst in zip(side_in, side_out):
        dst[...] = src[...].astype(dst.dtype)
    j = pl.program_id(1)

    @pl.when(j == 0)
    def _():
        gs = gain_ref[...] * (1.0 + scale_ref[...])
        sh = shift_ref[...]
        tm = x_ref.shape[0]
        for r in range(tm // row_chunk):
            rows = slice(r * row_chunk, (r + 1) * row_chunk)
            x = x_ref[rows, :]
            ms = jnp.mean(x * x, axis=-1, keepdims=True)
            h = ((x * lax.rsqrt(ms + EPS)) * gs + sh).astype(BF16)
            h_ref[rows, :] = h
            ba_ref[rows, :] = _mm_nt(h, ws_ref[...])

    o_ref[...] = _mm_nt(h_ref[...], w_ref[...]).astype(o_ref.dtype)


def _in_proj(x2d, gain, scale, shift, w_big, w_small, side_weights, *, tm, tn):
    t, d = x2d.shape
    n = w_big.shape[0]
    n_i, n_j = t // tm, n // tn
    slabs = 1 << ((n_i * n_j).bit_length() - 1)
    slab_of = lambda i, j: jnp.minimum(i * n_j + j, slabs - 1)
    side_specs = [pl.BlockSpec((w.shape[0] // slabs, w.shape[1]), lambda i, j: (slab_of(i, j), 0))
                  for w in side_weights]
    assert all(w.shape[0] % (16 * slabs) == 0 for w in side_weights)
    out = pl.pallas_call(
        functools.partial(_in_proj_kernel, row_chunk=128, n_side=len(side_weights)),
        out_shape=(jax.ShapeDtypeStruct((t, n), BF16), jax.ShapeDtypeStruct((t, LANES), F32),
                   *[jax.ShapeDtypeStruct(w.shape, BF16) for w in side_weights]),
        grid=(n_i, n_j),
        in_specs=[pl.BlockSpec((tm, d), lambda i, j: (i, 0)),
                  pl.BlockSpec((1, d), lambda i, j: (0, 0)),
                  pl.BlockSpec((1, d), lambda i, j: (0, 0)),
                  pl.BlockSpec((1, d), lambda i, j: (0, 0)),
                  pl.BlockSpec((tn, d), lambda i, j: (j, 0)),
                  pl.BlockSpec((LANES, d), lambda i, j: (0, 0)),
                  *side_specs],
        out_specs=(pl.BlockSpec((tm, tn), lambda i, j: (i, j)),
                   pl.BlockSpec((tm, LANES), lambda i, j: (i, 0)),
                   *side_specs),
        scratch_shapes=[pltpu.VMEM((tm, d), BF16)],
        compiler_params=_params(("arbitrary", "arbitrary")),
        name="in_proj",
    )(x2d, gain, scale, shift, w_big, w_small, *side_weights)
    return out[0], out[1], out[2:]


def _swa_bias_kernel(tab_ref, sink_ref, o_ref):
    q = SWA_BLOCK
    qpos = lax.broadcasted_iota(jnp.int32, (q, 2 * q), 0) + q
    kpos = lax.broadcasted_iota(jnp.int32, (q, 2 * q), 1)
    dist = qpos - kpos
    in_window = (dist >= 0) & (dist < SWA_WINDOW)
    d = jnp.maximum(dist, 0)
    max_exact = REL_BUCKETS // 2
    df = jnp.maximum(d, 1).astype(F32)
    large = max_exact + (jnp.log(df / max_exact) / math.log(REL_MAX_DIST / max_exact)
                         * (REL_BUCKETS - max_exact)).astype(jnp.int32)
    large = jnp.minimum(large, REL_BUCKETS - 1)
    bucket = jnp.where(in_window, jnp.where(d < max_exact, d, large), REL_BUCKETS)
    for h in range(SWA_HEADS):
        acc = jnp.full((q, 2 * q), NEG_BIG, F32)
        for b in range(REL_BUCKETS):
            acc = jnp.where(bucket == b, tab_ref[h, b], acc)
        o_ref[h] = (jnp.where(kpos == 0, sink_ref[h], acc) * LOG2E).astype(o_ref.dtype)


def _swa_bias(rel_bias_t, sinks):
    q = SWA_BLOCK
    return pl.pallas_call(
        _swa_bias_kernel,
        out_shape=jax.ShapeDtypeStruct((SWA_HEADS, q, 2 * q), BF16),
        in_specs=[pl.BlockSpec(memory_space=pltpu.SMEM), pl.BlockSpec(memory_space=pltpu.SMEM)],
        out_specs=pl.BlockSpec(memory_space=pltpu.VMEM),
        compiler_params=pltpu.CompilerParams(vmem_limit_bytes=VMEM_LIMIT),
        name="swa_bias",
    )(rel_bias_t, sinks)


def _chunk_cumsum_rows(x):
    n = x.shape[0]
    row = lax.broadcasted_iota(jnp.int32, x.shape, 0)
    s = 1
    while s < n:
        x = x + jnp.where(row >= s, pltpu.roll(x, s, axis=0), 0.0)
        s *= 2
    return x


def _unit_lower_inverse_stages(l_mats, masks_ref, out):
    eye = masks_ref[0].astype(F32)
    l_bf = [l.astype(BF16) for l in l_mats]
    m0 = [l * masks_ref[1] for l in l_bf]
    x = [eye - m for m in m0]
    p = [_mm(m, m) for m in m0]
    yield
    x = [xi + _mm(xi, pi) for xi, pi in zip(x, p)]
    yield
    p = [_mm(pi, pi) for pi in p]
    yield
    x = [xi + _mm(xi, pi) for xi, pi in zip(x, p)]
    yield
    for lvl in range(2, masks_ref.shape[0]):
        nx = [_mm(l * masks_ref[lvl], xi) for l, xi in zip(l_bf, x)]
        yield
        x = [xi - _mm(xi, ni) for xi, ni in zip(x, nx)]
        yield
    out.extend(x)


def _gdn_kernel(cur_ref, prev_ref, z_ref, ba_ref, convw_ref, band_ref, alog_ref, dtb_ref, gain_ref,
                masks_ref, o_ref, s_ref):
    s = pl.program_id(0)
    c = GDN_BLOCK
    heads = range(GDN_HEADS)
    blocks = range(GDN_STEP_BLOCKS)
    items = [(b, h) for b in blocks for h in heads]
    lane_of = lambda a, h: a[:, GDN_HEADS + h:GDN_HEADS + h + 1]

    @pl.when(s == 0)
    def _():
        s_ref[...] = jnp.zeros_like(s_ref)

    pad = prev_ref.shape[0]
    conv_w = convw_ref[...].astype(BF16)
    conv_pieces = {}
    for b in blocks:
        if b == 0:
            ctx = jnp.where(s > 0, prev_ref[...], jnp.zeros_like(prev_ref))
        else:
            ctx = cur_ref[b * c - pad:b * c, :]
        xcat = jnp.concatenate([ctx, cur_ref[b * c:(b + 1) * c, :]], axis=0)
        taps = jnp.concatenate([xcat * conv_w[i:i + 1, :] for i in range(GDN_CONV)], axis=0)
        for p in range(GDN_CONV_CH // GDN_CONV_PIECE):
            cols = slice(p * GDN_CONV_PIECE, (p + 1) * GDN_CONV_PIECE)
            conv_pieces[b, p] = jnp.dot(band_ref[...], taps[:, cols], preferred_element_type=F32)

    row = lax.broadcasted_iota(jnp.int32, (c, c), 0)
    col = lax.broadcasted_iota(jnp.int32, (c, c), 1)
    causal = row >= col

    def l2n(a, scale=1.0):
        return a * (lax.rsqrt(jnp.sum(a * a, axis=-1, keepdims=True) + EPS) * scale)

    q, k, kb, vb, kbg, qd, kd, dec, el = ({} for _ in range(9))
    for b in blocks:
        ba = ba_ref[b * c:(b + 1) * c, :]
        beta_all = _sigmoid(ba)
        xg = ba + dtb_ref[...]
        softplus = jnp.maximum(xg, 0.0) + jnp.log(1.0 + jnp.exp(-jnp.abs(xg)))
        g_all = -jnp.exp(alog_ref[...]) * softplus
        gc = _chunk_cumsum_rows(g_all)
        gc_t = gc.T
        g_last = gc[c - 1:c, :]
        eg_all = jnp.exp(gc)
        ekd_all = jnp.exp(g_last - gc)
        el[b] = jnp.exp(g_last)
        conv_tile = lambda col0: _silu(conv_pieces[b, col0 // GDN_CONV_PIECE][
            :, col0 % GDN_CONV_PIECE:col0 % GDN_CONV_PIECE + LANES])
        for h in heads:
            qf = l2n(conv_tile(h * GDN_DK), GDN_DK ** -0.5)
            kf = l2n(conv_tile(GDN_QK + h * GDN_DK))
            vf = conv_tile(2 * GDN_QK + h * GDN_DV)
            beta = beta_all[:, h:h + 1]
            eg = lane_of(eg_all, h)
            kbf = kf * beta
            gdiff = lane_of(gc, h) - gc_t[GDN_HEADS + h:GDN_HEADS + h + 1, :]
            q[b, h] = qf.astype(BF16)
            k[b, h] = kf.astype(BF16)
            kb[b, h] = kbf.astype(BF16)
            vb[b, h] = vf * beta
            kbg[b, h] = (kbf * eg).astype(BF16)
            qd[b, h] = (qf * eg).astype(BF16)
            kd[b, h] = (kf * lane_of(ekd_all, h)).astype(BF16)
            dec[b, h] = jnp.where(causal, jnp.exp(gdiff), 0.0)

    attn, t_inv, o_tiles = {}, {}, {}
    state = [s_ref[h] for h in heads]
    groups = [range(g, g + GDN_GROUP_HEADS) for g in range(0, GDN_HEADS, GDN_GROUP_HEADS)]

    def local_chain(b, hs):
        kk = {h: _mm_nt(kb[b, h], k[b, h]) for h in hs}
        yield
        qk = {h: _mm_nt(q[b, h], k[b, h]) for h in hs}
        yield
        l_mat = [kk[h] * dec[b, h] for h in hs]
        for h in hs:
            attn[b, h] = (qk[h] * dec[b, h]).astype(BF16)
        inv = []
        yield from _unit_lower_inverse_stages(l_mat, masks_ref, inv)
        for n, h in enumerate(hs):
            t_inv[b, h] = inv[n]

    def recurrence(b, hs):
        s_bf = {h: state[h].astype(BF16) for h in hs}
        resid = {h: vb[b, h] - _mm(kbg[b, h], s_bf[h]) for h in hs}
        yield
        v_new = {h: _mm(t_inv[b, h], resid[h]).astype(BF16) for h in hs}
        yield
        for h in hs:
            o_tiles[b, h] = _mm(jnp.concatenate([qd[b, h], attn[b, h]], axis=1),
                                jnp.concatenate([s_bf[h], v_new[h]], axis=0))
            state[h] = state[h] * lane_of(el[b], h) + _mm_tn(kd[b, h], v_new[h])
        yield
        for h in hs:
            o = o_tiles[b, h]
            on = o * lax.rsqrt(jnp.mean(o * o, axis=-1, keepdims=True) + EPS) * gain_ref[...]
            z = z_ref[b * c:(b + 1) * c, h * GDN_DV:(h + 1) * GDN_DV].astype(F32)
            o_ref[b * c:(b + 1) * c, h * GDN_DV:(h + 1) * GDN_DV] = (on * _silu(z)).astype(o_ref.dtype)

    def advance(gen):
        return next(gen, StopIteration) is not StopIteration

    units = [(b, g) for b in blocks for g in range(len(groups))]
    chains = [local_chain(b, groups[g]) for b, g in units]
    alive = [True] * len(units)
    recs = [[] for _ in groups]
    wave = 0
    while any(alive) or any(recs):
        for n, (b, g) in enumerate(units):
            if alive[n] and wave >= n * GDN_UNIT_LAG:
                alive[n] = advance(chains[n])
                if not alive[n]:
                    recs[g].append(recurrence(b, groups[g]))
        for pending in recs:
            if pending and not advance(pending[0]):
                pending.pop(0)
        wave += 1

    for h in heads:
        s_ref[h] = state[h]


def _gdn_masks():
    c = GDN_BLOCK
    r = np.arange(c)[:, None]
    k = np.arange(c)[None, :]
    mats = [np.eye(c), (r // 8 == k // 8) & (r > k)]
    b = 8
    while b < c:
        mats.append((r // (2 * b) == k // (2 * b)) & ((r // b) % 2 == 1) & ((k // b) % 2 == 0))
        b *= 2
    return jnp.asarray(np.stack([np.asarray(m, np.float32) for m in mats]), BF16)


def _conv_band(c, pad):
    band = np.zeros((c, GDN_CONV * (pad + c)), np.float32)
    for i in range(GDN_CONV):
        band[np.arange(c), i * (pad + c) + pad + np.arange(c) - (GDN_CONV - 1) + i] = 1.0
    return jnp.asarray(band, BF16)


def _gdn(proj, ba, conv_w, alog_lane, dtb_lane, gain):
    t = proj.shape[0]
    c = GDN_BLOCK
    rows = GDN_STEP_BLOCKS * c
    pad = 16
    masks = _gdn_masks()
    nm = masks.shape[0]
    band = _conv_band(c, pad)
    return pl.pallas_call(
        _gdn_kernel,
        out_shape=jax.ShapeDtypeStruct((t, GDN_V), BF16),
        grid=(t // rows,),
        in_specs=[pl.BlockSpec((rows, GDN_CONV_CH), lambda i: (i, 0)),
                  pl.BlockSpec((pad, GDN_CONV_CH), lambda i: (jnp.maximum(i * (rows // pad) - 1, 0), 0)),
                  pl.BlockSpec((rows, GDN_V), lambda i: (i, COL_ZA // GDN_V)),
                  pl.BlockSpec((rows, LANES), lambda i: (i, 0)),
                  pl.BlockSpec((GDN_CONV, GDN_CONV_CH), lambda i: (0, 0)),
                  pl.BlockSpec(band.shape, lambda i: (0, 0)),
                  pl.BlockSpec((1, LANES), lambda i: (0, 0)),
                  pl.BlockSpec((1, LANES), lambda i: (0, 0)),
                  pl.BlockSpec((1, GDN_DV), lambda i: (0, 0)),
                  pl.BlockSpec((nm, c, c), lambda i: (0, 0, 0))],
        out_specs=pl.BlockSpec((rows, GDN_V), lambda i: (i, 0)),
        scratch_shapes=[pltpu.VMEM((GDN_HEADS, GDN_DK, GDN_DV), F32)],
        compiler_params=_params(("arbitrary",)),
        name="gdn",
    )(proj, proj, proj, ba, conv_w, band, alog_lane, dtb_lane, gain, masks)


def _swa_kernel(q_ref, kc_ref, kp_ref, vc_ref, vp_ref, z_ref, bias_ref, qg_ref, kg_ref, seg_ref, o_ref):
    n = pl.program_id(0)
    qb = SWA_BLOCK
    hd = SWA_HEAD_DIM
    n_tiles = SWA_Q // LANES
    blocks = range(SWA_STEP_BLOCKS)
    items = [(j, tq) for j in blocks for tq in range(n_tiles)]
    lane = lax.broadcasted_iota(jnp.int32, (1, LANES), 1)
    lo = lane < hd
    seg = seg_ref[...].astype(BF16)

    def qk_norm(a, gain):
        ms = jnp.dot((a * a).astype(BF16), seg, preferred_element_type=F32)
        return a * (lax.rsqrt(ms + EPS) * gain)

    def two_heads(lo_part, hi_part):
        return jnp.concatenate([jnp.where(lo, lo_part, 0.0), jnp.where(lo, 0.0, hi_part)], axis=0)

    kall = jnp.concatenate([kp_ref[...], kc_ref[...]], axis=0).astype(F32)
    vall = jnp.concatenate([vp_ref[...], vc_ref[...]], axis=0).astype(F32)
    kn, vv = {}, {}
    for u in range(SWA_KV // LANES):
        knorm = qk_norm(kall[:, u * LANES:(u + 1) * LANES], kg_ref[...])
        vtile = vall[:, u * LANES:(u + 1) * LANES]
        for ci in range(SWA_STEP_BLOCKS + 1):
            kc = knorm[ci * qb:(ci + 1) * qb]
            vc = vtile[ci * qb:(ci + 1) * qb]
            kn[u, ci] = (kc, pltpu.roll(kc, hd, axis=1))
            vv[u, ci] = (vc, pltpu.roll(vc, hd, axis=1))

    row = lax.broadcasted_iota(jnp.int32, (qb, 1), 0)
    is_sink = row == 0
    k2s, rhs = {}, {}
    for j in blocks:
        prev_live = (n * SWA_STEP_BLOCKS + j) > 0
        ones_prev = jnp.where(prev_live | is_sink, 1.0, 0.0)
        ones_band = jnp.concatenate([ones_prev, jnp.ones_like(ones_prev)], axis=0)
        ones2 = two_heads(ones_band, ones_band)
        for u in range(SWA_KV // LANES):
            kband = [jnp.concatenate([jnp.where(is_sink, 0.0, kn[u, j][r]), kn[u, j + 1][r]], axis=0)
                     for r in range(2)]
            vband = [jnp.concatenate([jnp.where(prev_live & ~is_sink, vv[u, j][r], 0.0), vv[u, j + 1][r]],
                                     axis=0) for r in range(2)]
            for half in range(2):
                g = 2 * u + half
                k2s[j, g] = two_heads(kband[half], kband[1 - half]).astype(BF16)
                v2 = two_heads(vband[half], vband[1 - half])
                rhs[j, g] = jnp.concatenate([v2, ones2], axis=1).astype(BF16)

    q_gain = qg_ref[...] * (hd ** -0.5 * LOG2E)
    kv_of = lambda tq: (2 * tq) // (SWA_HEADS // SWA_KV_HEADS)
    qt = {(j, tq): qk_norm(q_ref[j * qb:(j + 1) * qb, tq * LANES:(tq + 1) * LANES].astype(F32), q_gain)
          for j, tq in items}
    logits = {}
    for n in range(len(items) + SWA_LOGITS_AHEAD):
        if n < len(items):
            j, tq = items[n]
            logits[j, tq] = _mm_nt(qt[j, tq], k2s[j, kv_of(tq)])
        if n >= SWA_LOGITS_AHEAD:
            j, tq = items[n - SWA_LOGITS_AHEAD]
            both = logits.pop((j, tq))
            ps = []
            for e in range(2):
                s = both[:, e * 2 * qb:(e + 1) * 2 * qb].astype(BF16) + bias_ref[2 * tq + e]
                ps.append(jnp.exp2(s - jnp.max(s, axis=-1, keepdims=True)))
            pv = jnp.dot(jnp.concatenate(ps, axis=1), rhs[j, kv_of(tq)], preferred_element_type=F32)
            out = pv[:, :LANES] * (1.0 / pv[:, LANES:])
            z = z_ref[j * qb:(j + 1) * qb, tq * LANES:(tq + 1) * LANES].astype(F32)
            o_ref[j * qb:(j + 1) * qb, tq * LANES:(tq + 1) * LANES] = (out * _silu(z)).astype(o_ref.dtype)


def _swa(proj, bias, qg2, kg2):
    t = proj.shape[0]
    qb = SWA_BLOCK
    rows = SWA_STEP_BLOCKS * qb
    seg = np.kron(np.eye(LANES // SWA_HEAD_DIM), np.ones((SWA_HEAD_DIM, SWA_HEAD_DIM))) / SWA_HEAD_DIM
    seg = jnp.asarray(seg, F32)
    kcol = COL_KB // SWA_KV
    vcol = COL_VB // SWA_KV
    prev = lambda i: jnp.maximum(i * SWA_STEP_BLOCKS - 1, 0)
    return pl.pallas_call(
        _swa_kernel,
        out_shape=jax.ShapeDtypeStruct((t, SWA_Q), BF16),
        grid=(t // rows,),
        in_specs=[pl.BlockSpec((rows, SWA_Q), lambda i: (i, COL_QB // SWA_Q)),
                  pl.BlockSpec((rows, SWA_KV), lambda i: (i, kcol)),
                  pl.BlockSpec((qb, SWA_KV), lambda i: (prev(i), kcol)),
                  pl.BlockSpec((rows, SWA_KV), lambda i: (i, vcol)),
                  pl.BlockSpec((qb, SWA_KV), lambda i: (prev(i), vcol)),
                  pl.BlockSpec((rows, SWA_Q), lambda i: (i, COL_ZB // SWA_Q)),
                  pl.BlockSpec((SWA_HEADS, qb, 2 * qb), lambda i: (0, 0, 0)),
                  pl.BlockSpec((1, LANES), lambda i: (0, 0)),
                  pl.BlockSpec((1, LANES), lambda i: (0, 0)),
                  pl.BlockSpec((LANES, LANES), lambda i: (0, 0))],
        out_specs=pl.BlockSpec((rows, SWA_Q), lambda i: (i, 0)),
        compiler_params=_params(("arbitrary",)),
        name="swa",
    )(proj, proj, proj, proj, proj, proj, bias, qg2, kg2, seg)


def _merge_out_kernel(x_ref, oa_ref, ob_ref, ga_ref, gb_ref, gate_ref, wa_ref, wb_ref, wo_ref, o_ref):
    ya = jnp.dot(oa_ref[...], wa_ref[...], preferred_element_type=F32)
    yb = jnp.dot(ob_ref[...], wb_ref[...], preferred_element_type=F32)
    mixed = _sigmoid(ga_ref[...].astype(F32)) * ya + _sigmoid(gb_ref[...].astype(F32)) * yb
    y = jnp.dot(mixed.astype(BF16), wo_ref[...], preferred_element_type=F32)
    o_ref[...] = x_ref[...] + gate_ref[...] * y


def _merge_out(x2d, o_a, o_b, proj, gate, w_a, w_b, w_o, *, tm):
    t, d = x2d.shape
    const = lambda shape: pl.BlockSpec(shape, lambda i: (0, 0), pipeline_mode=pl.Buffered(1))
    return pl.pallas_call(
        _merge_out_kernel,
        out_shape=jax.ShapeDtypeStruct((t, d), F32),
        grid=(t // tm,),
        in_specs=[pl.BlockSpec((tm, d), lambda i: (i, 0)),
                  pl.BlockSpec((tm, GDN_V), lambda i: (i, 0)),
                  pl.BlockSpec((tm, SWA_Q), lambda i: (i, 0)),
                  pl.BlockSpec((tm, d), lambda i: (i, COL_GA // D_MODEL)),
                  pl.BlockSpec((tm, d), lambda i: (i, COL_GB // D_MODEL)),
                  pl.BlockSpec((1, d), lambda i: (0, 0)),
                  const((GDN_V, d)), const((SWA_Q, d)), const((d, d))],
        out_specs=pl.BlockSpec((tm, d), lambda i: (i, 0)),
        compiler_params=_params(("arbitrary",)),
        name="merge_out",
    )(x2d, o_a, o_b, proj, proj, gate, w_a, w_b, w_o)


REPACK_TILE = 512
N_GATE_COLS = 2 * GDN_HEADS


def _repack_plan():
    src_of = {COL_QKV: 0, COL_ZA: GDN_CONV_CH}
    after_gates = COL_QB + N_GATE_COLS
    src_of.update({COL_QB: after_gates, COL_KB: after_gates + SWA_Q, COL_VB: after_gates + SWA_Q + SWA_KV,
                   COL_ZB: after_gates + SWA_Q + 2 * SWA_KV, COL_GA: after_gates + 2 * SWA_Q + 2 * SWA_KV,
                   COL_GB: after_gates + 2 * SWA_Q + 2 * SWA_KV + D_MODEL})
    starts = sorted(src_of)
    src = []
    for dst in range(0, PROJ_COLS, REPACK_TILE):
        grp = max(s for s in starts if s <= dst)
        src.append(src_of[grp] + dst - grp)
    return np.asarray(src, np.int32)


def _repack_kernel(tab_ref, w_ref, g_ref, big_ref, small_ref):
    del tab_ref
    big_ref[...] = w_ref[...].astype(BF16)

    @pl.when(pl.program_id(0) == 0)
    def _():
        row = lax.broadcasted_iota(jnp.int32, (LANES, 1), 0)
        small_ref[...] = jnp.where(row < N_GATE_COLS, g_ref[...], 0.0).astype(BF16)


def _repack_w_in(w_t):
    d = w_t.shape[1]
    src = _repack_plan()
    gate_row = COL_QB
    assert gate_row % LANES == 0 and np.all(src % N_GATE_COLS == 0)
    return pl.pallas_call(
        _repack_kernel,
        out_shape=(jax.ShapeDtypeStruct((PROJ_COLS, d), BF16), jax.ShapeDtypeStruct((LANES, d), BF16)),
        grid_spec=pltpu.PrefetchScalarGridSpec(
            num_scalar_prefetch=1,
            grid=(PROJ_COLS // REPACK_TILE,),
            in_specs=[pl.BlockSpec((pl.Element(REPACK_TILE), pl.Element(d)),
                                   lambda o, tab: (tab[o] * N_GATE_COLS, 0)),
                      pl.BlockSpec((LANES, d), lambda o, tab: (gate_row // LANES, 0))],
            out_specs=(pl.BlockSpec((REPACK_TILE, d), lambda o, tab: (o, 0)),
                       pl.BlockSpec((LANES, d), lambda o, tab: (0, 0)))),
        compiler_params=_params(("arbitrary",)),
        name="repack_w_in",
    )(jnp.asarray(src // N_GATE_COLS), w_t, w_t)


def _lane_row(vec, offset):
    return jnp.pad(vec.astype(F32), (offset, LANES - offset - vec.shape[0]))[None, :]


def kernel(x, c, w_ada, b_ada, norm_gain, w_in, conv_w, a_log, dt_bias, gdn_norm_gain, q_norm_gain,
           k_norm_gain, sinks, rel_bias, w_branch_gdn, w_branch_swa, w_out):
    bsz, t, d = x.shape
    depth = w_in.shape[0]
    outs = []
    for b in range(bsz):
        xb = x[b]
        c_col = c[b].astype(F32)[:, None]
        for l in range(depth):
            mod = _ada_mod(c_col, w_ada[l], b_ada[l][None, :])
            shift, scale, gate = mod[:, :d], mod[:, d:2 * d], mod[:, 2 * d:]
            w_big, w_small = _repack_w_in(w_in[l].T)
            proj, ba, (w_a, w_b, w_o) = _in_proj(
                xb, norm_gain[l][None, :], scale, shift, w_big, w_small,
                (w_branch_gdn[l], w_branch_swa[l], w_out[l]), tm=min(1024, t), tn=1792)
            o_a = _gdn(proj, ba, conv_w[l], _lane_row(a_log[l], GDN_HEADS), _lane_row(dt_bias[l], GDN_HEADS),
                       gdn_norm_gain[l][None, :])
            bias = _swa_bias(rel_bias.T.astype(F32), sinks[l].astype(F32))
            o_b = _swa(proj, bias,
                       jnp.tile(q_norm_gain[l], LANES // SWA_HEAD_DIM)[None, :],
                       jnp.tile(k_norm_gain[l], LANES // SWA_HEAD_DIM)[None, :])
            xb = _merge_out(xb, o_a, o_b, proj, gate, w_a, w_b, w_o, tm=min(512, t))
        outs.append(xb)
    return jnp.stack(outs, axis=0)
```

```python
import functools
import math

import jax
import jax.numpy as jnp
import numpy as np
from jax import lax
from jax.experimental import pallas as pl
from jax.experimental.pallas import tpu as pltpu

F32 = jnp.float32
BF16 = jnp.bfloat16

LANES = 128
D_MODEL = 2048
GDN_HEADS = 8
GDN_DK = 128
GDN_DV = 128
GDN_CONV = 4
GDN_QK = GDN_HEADS * GDN_DK
GDN_V = GDN_HEADS * GDN_DV
GDN_CONV_CH = 2 * GDN_QK + GDN_V
GDN_BLOCK = 128
GDN_STEP_BLOCKS = 4
GDN_CONV_PIECE = 512
GDN_GROUP_HEADS = 8
GDN_UNIT_LAG = 4
SWA_HEADS = 16
SWA_KV_HEADS = 4
SWA_HEAD_DIM = 64
SWA_WINDOW = 128
SWA_BLOCK = 128
SWA_STEP_BLOCKS = 8
SWA_LOGITS_AHEAD = 4
SWA_Q = SWA_HEADS * SWA_HEAD_DIM
SWA_KV = SWA_KV_HEADS * SWA_HEAD_DIM
REL_BUCKETS = 32
REL_MAX_DIST = 128
EPS = 1e-6
NEG_BIG = -1e30
LOG2E = math.log2(math.e)

COL_QKV = 0
COL_ZA = GDN_CONV_CH
COL_QB = COL_ZA + GDN_V
COL_ZB = COL_QB + SWA_Q
COL_GA = COL_ZB + SWA_Q
COL_GB = COL_GA + D_MODEL
COL_KB = COL_GB + D_MODEL
COL_VB = COL_KB + SWA_KV
PROJ_COLS = COL_VB + SWA_KV
for _col, _width in ((COL_ZA, GDN_V), (COL_QB, SWA_Q), (COL_ZB, SWA_Q), (COL_GA, D_MODEL),
                     (COL_GB, D_MODEL), (COL_KB, SWA_KV), (COL_VB, SWA_KV)):
    assert _col % _width == 0

VMEM_LIMIT = 56 * 1024 * 1024


def _sigmoid(x):
    return 0.5 + 0.5 * jnp.tanh(0.5 * x)


def _silu(x):
    half = 0.5 * x
    return half + half * jnp.tanh(half)


def _params(sem):
    return pltpu.CompilerParams(dimension_semantics=sem, vmem_limit_bytes=VMEM_LIMIT)


def _mm(a, b):
    return jnp.dot(a.astype(BF16), b.astype(BF16), preferred_element_type=F32)


def _mm_nt(a, b):
    return lax.dot_general(a.astype(BF16), b.astype(BF16), (((1,), (1,)), ((), ())),
                           preferred_element_type=F32)


def _mm_tn(a, b):
    return lax.dot_general(a.astype(BF16), b.astype(BF16), (((0,), (0,)), ((), ())),
                           preferred_element_type=F32)


def _ada_mod_kernel(c_ref, w_ref, b_ref, o_ref):
    c = c_ref[...]
    o_ref[...] = jnp.sum(_silu(c) * w_ref[...], axis=0, keepdims=True) + b_ref[...]


def _ada_mod(c_col, w_ada, b_ada):
    d, n = w_ada.shape
    tn = 1024
    return pl.pallas_call(
        _ada_mod_kernel,
        out_shape=jax.ShapeDtypeStruct((1, n), F32),
        grid=(n // tn,),
        in_specs=[pl.BlockSpec((d, 1), lambda j: (0, 0)),
                  pl.BlockSpec((d, tn), lambda j: (0, j)),
                  pl.BlockSpec((1, tn), lambda j: (0, j))],
        out_specs=pl.BlockSpec((1, tn), lambda j: (0, j)),
        compiler_params=_params(("arbitrary",)),
        name="ada_mod",
    )(c_col, w_ada, b_ada)


def _swa_bias_head(tab_ref, sink_ref, h):
    q = SWA_BLOCK
    qpos = lax.broadcasted_iota(jnp.int32, (q, 2 * q), 0) + q
    kpos = lax.broadcasted_iota(jnp.int32, (q, 2 * q), 1)
    dist = qpos - kpos
    in_window = (dist >= 0) & (dist < SWA_WINDOW)
    d = jnp.maximum(dist, 0)
    max_exact = REL_BUCKETS // 2
    df = jnp.maximum(d, 1).astype(F32)
    large = max_exact + (jnp.log(df / max_exact) / math.log(REL_MAX_DIST / max_exact)
                         * (REL_BUCKETS - max_exact)).astype(jnp.int32)
    large = jnp.minimum(large, REL_BUCKETS - 1)
    bucket = jnp.where(in_window, jnp.where(d < max_exact, d, large), REL_BUCKETS)
    acc = jnp.full((q, 2 * q), NEG_BIG, F32)
    for b in range(REL_BUCKETS):
        acc = jnp.where(bucket == b, tab_ref[h, b], acc)
    return jnp.where(kpos == 0, sink_ref[h], acc) * LOG2E


def _in_proj_kernel(x_ref, gain_ref, scale_ref, shift_ref, w_ref, ws_ref, tab_ref, sink_ref, *rest,
                    row_chunk, n_side):
    side_in, (o_ref, ba_ref, bias_ref), side_out, h_ref = (rest[:n_side], rest[n_side:n_side + 3],
                                                           rest[n_side + 3:-1], rest[-1])
    for src, dst in zip(side_in, side_out):
        dst[...] = src[...].astype(dst.dtype)
    j = pl.program_id(1)
    per_step = bias_ref.shape[0]
    group = jnp.minimum(pl.program_id(0) * pl.num_programs(1) + j, SWA_HEADS // per_step - 1)
    for p in range(per_step):
        bias_ref[p] = _swa_bias_head(tab_ref, sink_ref, group * per_step + p).astype(bias_ref.dtype)

    @pl.when(j == 0)
    def _():
        gs = gain_ref[...] * (1.0 + scale_ref[...])
        sh = shift_ref[...]
        tm = x_ref.shape[0]
        for r in range(tm // row_chunk):
            rows = slice(r * row_chunk, (r + 1) * row_chunk)
            x = x_ref[rows, :]
            ms = jnp.mean(x * x, axis=-1, keepdims=True)
            h = ((x * lax.rsqrt(ms + EPS)) * gs + sh).astype(BF16)
            h_ref[rows, :] = h
            ba_ref[rows, :] = _mm_nt(h, ws_ref[...])

    o_ref[...] = _mm_nt(h_ref[...], w_ref[...]).astype(o_ref.dtype)


def _in_proj(x2d, gain, scale, shift, w_big, w_small, rel_bias_t, sinks, side_weights, *, tm, tn):
    t, d = x2d.shape
    n = w_big.shape[0]
    n_i, n_j = t // tm, n // tn
    slabs = 1 << ((n_i * n_j).bit_length() - 1)
    slab_of = lambda i, j: jnp.minimum(i * n_j + j, slabs - 1)
    side_specs = [pl.BlockSpec((w.shape[0] // slabs, w.shape[1]), lambda i, j: (slab_of(i, j), 0))
                  for w in side_weights]
    assert all(w.shape[0] % (16 * slabs) == 0 for w in side_weights)
    per_step = max(SWA_HEADS // slabs, 1)
    bias_spec = pl.BlockSpec((per_step, SWA_BLOCK, 2 * SWA_BLOCK),
                             lambda i, j: (jnp.minimum(i * n_j + j, SWA_HEADS // per_step - 1), 0, 0))
    out = pl.pallas_call(
        functools.partial(_in_proj_kernel, row_chunk=128, n_side=len(side_weights)),
        out_shape=(jax.ShapeDtypeStruct((t, n), BF16), jax.ShapeDtypeStruct((t, LANES), F32),
                   jax.ShapeDtypeStruct((SWA_HEADS, SWA_BLOCK, 2 * SWA_BLOCK), BF16),
                   *[jax.ShapeDtypeStruct(w.shape, BF16) for w in side_weights]),
        grid=(n_i, n_j),
        in_specs=[pl.BlockSpec((tm, d), lambda i, j: (i, 0)),
                  pl.BlockSpec((1, d), lambda i, j: (0, 0)),
                  pl.BlockSpec((1, d), lambda i, j: (0, 0)),
                  pl.BlockSpec((1, d), lambda i, j: (0, 0)),
                  pl.BlockSpec((tn, d), lambda i, j: (j, 0)),
                  pl.BlockSpec((LANES, d), lambda i, j: (0, 0)),
                  pl.BlockSpec(memory_space=pltpu.SMEM),
                  pl.BlockSpec(memory_space=pltpu.SMEM),
                  *side_specs],
        out_specs=(pl.BlockSpec((tm, tn), lambda i, j: (i, j)),
                   pl.BlockSpec((tm, LANES), lambda i, j: (i, 0)),
                   bias_spec,
                   *side_specs),
        scratch_shapes=[pltpu.VMEM((tm, d), BF16)],
        compiler_params=_params(("arbitrary", "arbitrary")),
        name="in_proj",
    )(x2d, gain, scale, shift, w_big, w_small, rel_bias_t, sinks, *side_weights)
    return out[0], out[1], out[2], out[3:]


def _chunk_cumsum_rows(x):
    n = x.shape[0]
    row = lax.broadcasted_iota(jnp.int32, x.shape, 0)
    s = 1
    while s < n:
        x = x + jnp.where(row >= s, pltpu.roll(x, s, axis=0), 0.0)
        s *= 2
    return x


def _unit_lower_inverse_stages(l_mats, masks_ref, out):
    eye = masks_ref[0].astype(F32)
    l_bf = [l.astype(BF16) for l in l_mats]
    m0 = [l * masks_ref[1] for l in l_bf]
    x = [eye - m for m in m0]
    p = [_mm(m, m) for m in m0]
    yield
    x = [xi + _mm(xi, pi) for xi, pi in zip(x, p)]
    yield
    p = [_mm(pi, pi) for pi in p]
    yield
    x = [xi + _mm(xi, pi) for xi, pi in zip(x, p)]
    yield
    for lvl in range(2, masks_ref.shape[0]):
        nx = [_mm(l * masks_ref[lvl], xi) for l, xi in zip(l_bf, x)]
        yield
        x = [xi - _mm(xi, ni) for xi, ni in zip(x, nx)]
        yield
    out.extend(x)


def _gdn_kernel(cur_ref, prev_ref, z_ref, ba_ref, convw_ref, band_ref, alog_ref, dtb_ref, gain_ref,
                masks_ref, o_ref, s_ref):
    s = pl.program_id(0)
    c = GDN_BLOCK
    heads = range(GDN_HEADS)
    blocks = range(GDN_STEP_BLOCKS)
    items = [(b, h) for b in blocks for h in heads]
    lane_of = lambda a, h: a[:, GDN_HEADS + h:GDN_HEADS + h + 1]

    @pl.when(s == 0)
    def _():
        s_ref[...] = jnp.zeros_like(s_ref)

    pad = prev_ref.shape[0]
    conv_w = convw_ref[...].astype(BF16)
    conv_pieces = {}
    for b in blocks:
        if b == 0:
            ctx = jnp.where(s > 0, prev_ref[...], jnp.zeros_like(prev_ref))
        else:
            ctx = cur_ref[b * c - pad:b * c, :]
        xcat = jnp.concatenate([ctx, cur_ref[b * c:(b + 1) * c, :]], axis=0)
        taps = jnp.concatenate([xcat * conv_w[i:i + 1, :] for i in range(GDN_CONV)], axis=0)
        for p in range(GDN_CONV_CH // GDN_CONV_PIECE):
            cols = slice(p * GDN_CONV_PIECE, (p + 1) * GDN_CONV_PIECE)
            conv_pieces[b, p] = jnp.dot(band_ref[...], taps[:, cols], preferred_element_type=F32)

    row = lax.broadcasted_iota(jnp.int32, (c, c), 0)
    col = lax.broadcasted_iota(jnp.int32, (c, c), 1)
    causal = row >= col

    def l2n(a, scale=1.0):
        return a * (lax.rsqrt(jnp.sum(a * a, axis=-1, keepdims=True) + EPS) * scale)

    q, k, kb, vb, kbg, qd, kd, dec, el = ({} for _ in range(9))
    for b in blocks:
        ba = ba_ref[b * c:(b + 1) * c, :]
        beta_all = _sigmoid(ba)
        xg = ba + dtb_ref[...]
        softplus = jnp.maximum(xg, 0.0) + jnp.log(1.0 + jnp.exp(-jnp.abs(xg)))
        g_all = -jnp.exp(alog_ref[...]) * softplus
        gc = _chunk_cumsum_rows(g_all)
        gc_t = gc.T
        g_last = gc[c - 1:c, :]
        eg_all = jnp.exp(gc)
        ekd_all = jnp.exp(g_last - gc)
        el[b] = jnp.exp(g_last)
        conv_tile = lambda col0: _silu(conv_pieces[b, col0 // GDN_CONV_PIECE][
            :, col0 % GDN_CONV_PIECE:col0 % GDN_CONV_PIECE + LANES])
        for h in heads:
            qf = l2n(conv_tile(h * GDN_DK), GDN_DK ** -0.5)
            kf = l2n(conv_tile(GDN_QK + h * GDN_DK))
            vf = conv_tile(2 * GDN_QK + h * GDN_DV)
            beta = beta_all[:, h:h + 1]
            eg = lane_of(eg_all, h)
            kbf = kf * beta
            gdiff = lane_of(gc, h) - gc_t[GDN_HEADS + h:GDN_HEADS + h + 1, :]
            q[b, h] = qf.astype(BF16)
            k[b, h] = kf.astype(BF16)
            kb[b, h] = kbf.astype(BF16)
            vb[b, h] = vf * beta
            kbg[b, h] = (kbf * eg).astype(BF16)
            qd[b, h] = (qf * eg).astype(BF16)
            kd[b, h] = (kf * lane_of(ekd_all, h)).astype(BF16)
            dec[b, h] = jnp.where(causal, jnp.exp(gdiff), 0.0)

    attn, t_inv, o_tiles = {}, {}, {}
    state = [s_ref[h] for h in heads]
    groups = [range(g, g + GDN_GROUP_HEADS) for g in range(0, GDN_HEADS, GDN_GROUP_HEADS)]

    def local_chain(b, hs):
        kk = {h: _mm_nt(kb[b, h], k[b, h]) for h in hs}
        yield
        qk = {h: _mm_nt(q[b, h], k[b, h]) for h in hs}
        yield
        l_mat = [kk[h] * dec[b, h] for h in hs]
        for h in hs:
            attn[b, h] = (qk[h] * dec[b, h]).astype(BF16)
        inv = []
        yield from _unit_lower_inverse_stages(l_mat, masks_ref, inv)
        for n, h in enumerate(hs):
            t_inv[b, h] = inv[n]

    def recurrence(b, hs):
        s_bf = {h: state[h].astype(BF16) for h in hs}
        resid = {h: vb[b, h] - _mm(kbg[b, h], s_bf[h]) for h in hs}
        yield
        v_new = {h: _mm(t_inv[b, h], resid[h]).astype(BF16) for h in hs}
        yield
        for h in hs:
            o_tiles[b, h] = _mm(jnp.concatenate([qd[b, h], attn[b, h]], axis=1),
                                jnp.concatenate([s_bf[h], v_new[h]], axis=0))
            state[h] = state[h] * lane_of(el[b], h) + _mm_tn(kd[b, h], v_new[h])
        yield
        for h in hs:
            o = o_tiles[b, h]
            on = o * lax.rsqrt(jnp.mean(o * o, axis=-1, keepdims=True) + EPS) * gain_ref[...]
            z = z_ref[b * c:(b + 1) * c, h * GDN_DV:(h + 1) * GDN_DV].astype(F32)
            o_ref[b * c:(b + 1) * c, h * GDN_DV:(h + 1) * GDN_DV] = (on * _silu(z)).astype(o_ref.dtype)

    def advance(gen):
        return next(gen, StopIteration) is not StopIteration

    units = [(b, g) for b in blocks for g in range(len(groups))]
    chains = [local_chain(b, groups[g]) for b, g in units]
    alive = [True] * len(units)
    recs = [[] for _ in groups]
    wave = 0
    while any(alive) or any(recs):
        for n, (b, g) in enumerate(units):
            if alive[n] and wave >= n * GDN_UNIT_LAG:
                alive[n] = advance(chains[n])
                if not alive[n]:
                    recs[g].append(recurrence(b, groups[g]))
        for pending in recs:
            if pending and not advance(pending[0]):
                pending.pop(0)
        wave += 1

    for h in heads:
        s_ref[h] = state[h]


def _gdn_masks():
    c = GDN_BLOCK
    r = np.arange(c)[:, None]
    k = np.arange(c)[None, :]
    mats = [np.eye(c), (r // 8 == k // 8) & (r > k)]
    b = 8
    while b < c:
        mats.append((r // (2 * b) == k // (2 * b)) & ((r // b) % 2 == 1) & ((k // b) % 2 == 0))
        b *= 2
    return jnp.asarray(np.stack([np.asarray(m, np.float32) for m in mats]), BF16)


def _conv_band(c, pad):
    band = np.zeros((c, GDN_CONV * (pad + c)), np.float32)
    for i in range(GDN_CONV):
        band[np.arange(c), i * (pad + c) + pad + np.arange(c) - (GDN_CONV - 1) + i] = 1.0
    return jnp.asarray(band, BF16)


def _gdn(proj, ba, conv_w, alog_lane, dtb_lane, gain):
    t = proj.shape[0]
    c = GDN_BLOCK
    rows = GDN_STEP_BLOCKS * c
    pad = 16
    masks = _gdn_masks()
    nm = masks.shape[0]
    band = _conv_band(c, pad)
    return pl.pallas_call(
        _gdn_kernel,
        out_shape=jax.ShapeDtypeStruct((t, GDN_V), BF16),
        grid=(t // rows,),
        in_specs=[pl.BlockSpec((rows, GDN_CONV_CH), lambda i: (i, 0)),
                  pl.BlockSpec((pad, GDN_CONV_CH), lambda i: (jnp.maximum(i * (rows // pad) - 1, 0), 0)),
                  pl.BlockSpec((rows, GDN_V), lambda i: (i, COL_ZA // GDN_V)),
                  pl.BlockSpec((rows, LANES), lambda i: (i, 0)),
                  pl.BlockSpec((GDN_CONV, GDN_CONV_CH), lambda i: (0, 0)),
                  pl.BlockSpec(band.shape, lambda i: (0, 0)),
                  pl.BlockSpec((1, LANES), lambda i: (0, 0)),
                  pl.BlockSpec((1, LANES), lambda i: (0, 0)),
                  pl.BlockSpec((1, GDN_DV), lambda i: (0, 0)),
                  pl.BlockSpec((nm, c, c), lambda i: (0, 0, 0))],
        out_specs=pl.BlockSpec((rows, GDN_V), lambda i: (i, 0)),
        scratch_shapes=[pltpu.VMEM((GDN_HEADS, GDN_DK, GDN_DV), F32)],
        compiler_params=_params(("arbitrary",)),
        name="gdn",
    )(proj, proj, proj, ba, conv_w, band, alog_lane, dtb_lane, gain, masks)


def _swa_kernel(q_ref, kc_ref, kp_ref, vc_ref, vp_ref, z_ref, bias_ref, qg_ref, kg_ref, seg_ref, o_ref):
    n = pl.program_id(0)
    qb = SWA_BLOCK
    hd = SWA_HEAD_DIM
    n_tiles = SWA_Q // LANES
    blocks = range(SWA_STEP_BLOCKS)
    items = [(j, tq) for j in blocks for tq in range(n_tiles)]
    lane = lax.broadcasted_iota(jnp.int32, (1, LANES), 1)
    lo = lane < hd
    seg = seg_ref[...].astype(BF16)

    def qk_norm(a, gain):
        ms = jnp.dot((a * a).astype(BF16), seg, preferred_element_type=F32)
        return a * (lax.rsqrt(ms + EPS) * gain)

    def two_heads(lo_part, hi_part):
        return jnp.concatenate([jnp.where(lo, lo_part, 0.0), jnp.where(lo, 0.0, hi_part)], axis=0)

    kall = jnp.concatenate([kp_ref[...], kc_ref[...]], axis=0).astype(F32)
    vall = jnp.concatenate([vp_ref[...], vc_ref[...]], axis=0).astype(F32)
    kn, vv = {}, {}
    for u in range(SWA_KV // LANES):
        knorm = qk_norm(kall[:, u * LANES:(u + 1) * LANES], kg_ref[...])
        vtile = vall[:, u * LANES:(u + 1) * LANES]
        for ci in range(SWA_STEP_BLOCKS + 1):
            kc = knorm[ci * qb:(ci + 1) * qb]
            vc = vtile[ci * qb:(ci + 1) * qb]
            kn[u, ci] = (kc, pltpu.roll(kc, hd, axis=1))
            vv[u, ci] = (vc, pltpu.roll(vc, hd, axis=1))

    row = lax.broadcasted_iota(jnp.int32, (qb, 1), 0)
    is_sink = row == 0
    k2s, rhs = {}, {}
    for j in blocks:
        prev_live = (n * SWA_STEP_BLOCKS + j) > 0
        ones_prev = jnp.where(prev_live | is_sink, 1.0, 0.0)
        ones_band = jnp.concatenate([ones_prev, jnp.ones_like(ones_prev)], axis=0)
        ones2 = two_heads(ones_band, ones_band)
        for u in range(SWA_KV // LANES):
            kband = [jnp.concatenate([jnp.where(is_sink, 0.0, kn[u, j][r]), kn[u, j + 1][r]], axis=0)
                     for r in range(2)]
            vband = [jnp.concatenate([jnp.where(prev_live & ~is_sink, vv[u, j][r], 0.0), vv[u, j + 1][r]],
                                     axis=0) for r in range(2)]
            for half in range(2):
                g = 2 * u + half
                k2s[j, g] = two_heads(kband[half], kband[1 - half]).astype(BF16)
                v2 = two_heads(vband[half], vband[1 - half])
                rhs[j, g] = jnp.concatenate([v2, ones2], axis=1).astype(BF16)

    q_gain = qg_ref[...] * (hd ** -0.5 * LOG2E)
    kv_of = lambda tq: (2 * tq) // (SWA_HEADS // SWA_KV_HEADS)
    qt = {(j, tq): qk_norm(q_ref[j * qb:(j + 1) * qb, tq * LANES:(tq + 1) * LANES].astype(F32), q_gain)
          for j, tq in items}
    logits = {}
    for n in range(len(items) + SWA_LOGITS_AHEAD):
        if n < len(items):
            j, tq = items[n]
            logits[j, tq] = _mm_nt(qt[j, tq], k2s[j, kv_of(tq)])
        if n >= SWA_LOGITS_AHEAD:
            j, tq = items[n - SWA_LOGITS_AHEAD]
            both = logits.pop((j, tq))
            ps = []
            for e in range(2):
                s = both[:, e * 2 * qb:(e + 1) * 2 * qb].astype(BF16) + bias_ref[2 * tq + e]
                ps.append(jnp.exp2(s - jnp.max(s, axis=-1, keepdims=True)))
            pv = jnp.dot(jnp.concatenate(ps, axis=1), rhs[j, kv_of(tq)], preferred_element_type=F32)
            out = pv[:, :LANES] * (1.0 / pv[:, LANES:])
            z = z_ref[j * qb:(j + 1) * qb, tq * LANES:(tq + 1) * LANES].astype(F32)
            o_ref[j * qb:(j + 1) * qb, tq * LANES:(tq + 1) * LANES] = (out * _silu(z)).astype(o_ref.dtype)


def _swa(proj, bias, qg2, kg2):
    t = proj.shape[0]
    qb = SWA_BLOCK
    rows = SWA_STEP_BLOCKS * qb
    seg = np.kron(np.eye(LANES // SWA_HEAD_DIM), np.ones((SWA_HEAD_DIM, SWA_HEAD_DIM))) / SWA_HEAD_DIM
    seg = jnp.asarray(seg, F32)
    kcol = COL_KB // SWA_KV
    vcol = COL_VB // SWA_KV
    prev = lambda i: jnp.maximum(i * SWA_STEP_BLOCKS - 1, 0)
    return pl.pallas_call(
        _swa_kernel,
        out_shape=jax.ShapeDtypeStruct((t, SWA_Q), BF16),
        grid=(t // rows,),
        in_specs=[pl.BlockSpec((rows, SWA_Q), lambda i: (i, COL_QB // SWA_Q)),
                  pl.BlockSpec((rows, SWA_KV), lambda i: (i, kcol)),
                  pl.BlockSpec((qb, SWA_KV), lambda i: (prev(i), kcol)),
                  pl.BlockSpec((rows, SWA_KV), lambda i: (i, vcol)),
                  pl.BlockSpec((qb, SWA_KV), lambda i: (prev(i), vcol)),
                  pl.BlockSpec((rows, SWA_Q), lambda i: (i, COL_ZB // SWA_Q)),
                  pl.BlockSpec((SWA_HEADS, qb, 2 * qb), lambda i: (0, 0, 0)),
                  pl.BlockSpec((1, LANES), lambda i: (0, 0)),
                  pl.BlockSpec((1, LANES), lambda i: (0, 0)),
                  pl.BlockSpec((LANES, LANES), lambda i: (0, 0))],
        out_specs=pl.BlockSpec((rows, SWA_Q), lambda i: (i, 0)),
        compiler_params=_params(("arbitrary",)),
        name="swa",
    )(proj, proj, proj, proj, proj, proj, bias, qg2, kg2, seg)


def _merge_out_kernel(x_ref, oa_ref, ob_ref, ga_ref, gb_ref, gate_ref, wa_ref, wb_ref, wo_ref, o_ref):
    ya = jnp.dot(oa_ref[...], wa_ref[...], preferred_element_type=F32)
    yb = jnp.dot(ob_ref[...], wb_ref[...], preferred_element_type=F32)
    mixed = _sigmoid(ga_ref[...].astype(F32)) * ya + _sigmoid(gb_ref[...].astype(F32)) * yb
    y = jnp.dot(mixed.astype(BF16), wo_ref[...], preferred_element_type=F32)
    o_ref[...] = x_ref[...] + gate_ref[...] * y


def _merge_out(x2d, o_a, o_b, proj, gate, w_a, w_b, w_o, *, tm):
    t, d = x2d.shape
    const = lambda shape: pl.BlockSpec(shape, lambda i: (0, 0), pipeline_mode=pl.Buffered(1))
    return pl.pallas_call(
        _merge_out_kernel,
        out_shape=jax.ShapeDtypeStruct((t, d), F32),
        grid=(t // tm,),
        in_specs=[pl.BlockSpec((tm, d), lambda i: (i, 0)),
                  pl.BlockSpec((tm, GDN_V), lambda i: (i, 0)),
                  pl.BlockSpec((tm, SWA_Q), lambda i: (i, 0)),
                  pl.BlockSpec((tm, d), lambda i: (i, COL_GA // D_MODEL)),
                  pl.BlockSpec((tm, d), lambda i: (i, COL_GB // D_MODEL)),
                  pl.BlockSpec((1, d), lambda i: (0, 0)),
                  const((GDN_V, d)), const((SWA_Q, d)), const((d, d))],
        out_specs=pl.BlockSpec((tm, d), lambda i: (i, 0)),
        compiler_params=_params(("arbitrary",)),
        name="merge_out",
    )(x2d, o_a, o_b, proj, proj, gate, w_a, w_b, w_o)


REPACK_TILE = 512
N_GATE_COLS = 2 * GDN_HEADS


def _repack_plan():
    src_of = {COL_QKV: 0, COL_ZA: GDN_CONV_CH}
    after_gates = COL_QB + N_GATE_COLS
    src_of.update({COL_QB: after_gates, COL_KB: after_gates + SWA_Q, COL_VB: after_gates + SWA_Q + SWA_KV,
                   COL_ZB: after_gates + SWA_Q + 2 * SWA_KV, COL_GA: after_gates + 2 * SWA_Q + 2 * SWA_KV,
                   COL_GB: after_gates + 2 * SWA_Q + 2 * SWA_KV + D_MODEL})
    starts = sorted(src_of)
    src = []
    for dst in range(0, PROJ_COLS, REPACK_TILE):
        grp = max(s for s in starts if s <= dst)
        src.append(src_of[grp] + dst - grp)
    return np.asarray(src, np.int32)


def _repack_kernel(tab_ref, w_ref, g_ref, big_ref, small_ref):
    del tab_ref
    big_ref[...] = w_ref[...].astype(BF16)

    @pl.when(pl.program_id(0) == 0)
    def _():
        row = lax.broadcasted_iota(jnp.int32, (LANES, 1), 0)
        small_ref[...] = jnp.where(row < N_GATE_COLS, g_ref[...], 0.0).astype(BF16)


def _repack_w_in(w_t):
    d = w_t.shape[1]
    src = _repack_plan()
    gate_row = COL_QB
    assert gate_row % LANES == 0 and np.all(src % N_GATE_COLS == 0)
    return pl.pallas_call(
        _repack_kernel,
        out_shape=(jax.ShapeDtypeStruct((PROJ_COLS, d), BF16), jax.ShapeDtypeStruct((LANES, d), BF16)),
        grid_spec=pltpu.PrefetchScalarGridSpec(
            num_scalar_prefetch=1,
            grid=(PROJ_COLS // REPACK_TILE,),
            in_specs=[pl.BlockSpec((pl.Element(REPACK_TILE), pl.Element(d)),
                                   lambda o, tab: (tab[o] * N_GATE_COLS, 0)),
                      pl.BlockSpec((LANES, d), lambda o, tab: (gate_row // LANES, 0))],
            out_specs=(pl.BlockSpec((REPACK_TILE, d), lambda o, tab: (o, 0)),
                       pl.BlockSpec((LANES, d), lambda o, tab: (0, 0)))),
        compiler_params=_params(("arbitrary",)),
        name="repack_w_in",
    )(jnp.asarray(src // N_GATE_COLS), w_t, w_t)


def _lane_row(vec, offset):
    return jnp.pad(vec.astype(F32), (offset, LANES - offset - vec.shape[0]))[None, :]


def kernel(x, c, w_ada, b_ada, norm_gain, w_in, conv_w, a_log, dt_bias, gdn_norm_gain, q_norm_gain,
           k_norm_gain, sinks, rel_bias, w_branch_gdn, w_branch_swa, w_out):
    bsz, t, d = x.shape
    depth = w_in.shape[0]
    outs = []
    for b in range(bsz):
        xb = x[b]
        c_col = c[b].astype(F32)[:, None]
        for l in range(depth):
            mod = _ada_mod(c_col, w_ada[l], b_ada[l][None, :])
            shift, scale, gate = mod[:, :d], mod[:, d:2 * d], mod[:, 2 * d:]
            w_big, w_small = _repack_w_in(w_in[l].T)
            proj, ba, bias, (w_a, w_b, w_o) = _in_proj(
                xb, norm_gain[l][None, :], scale, shift, w_big, w_small,
                rel_bias.T.astype(F32), sinks[l].astype(F32),
                (w_branch_gdn[l], w_branch_swa[l], w_out[l]), tm=min(1024, t), tn=1792)
            o_a = _gdn(proj, ba, conv_w[l], _lane_row(a_log[l], GDN_HEADS), _lane_row(dt_bias[l], GDN_HEADS),
                       gdn_norm_gain[l][None, :])
            o_b = _swa(proj, bias,
                       jnp.tile(q_norm_gain[l], LANES // SWA_HEAD_DIM)[None, :],
                       jnp.tile(k_norm_gain[l], LANES // SWA_HEAD_DIM)[None, :])
            xb = _merge_out(xb, o_a, o_b, proj, gate, w_a, w_b, w_o, tm=min(512, t))
        outs.append(xb)
    return jnp.stack(outs, axis=0)
```

```python
import functools
import math

import jax
import jax.numpy as jnp
import numpy as np
from jax import lax
from jax.experimental import pallas as pl
from jax.experimental.pallas import tpu as pltpu

F32 = jnp.float32
BF16 = jnp.bfloat16

LANES = 128
D_MODEL = 2048
GDN_HEADS = 8
GDN_DK = 128
GDN_DV = 128
GDN_CONV = 4
GDN_QK = GDN_HEADS * GDN_DK
GDN_V = GDN_HEADS * GDN_DV
GDN_CONV_CH = 2 * GDN_QK + GDN_V
GDN_BLOCK = 128
GDN_STEP_BLOCKS = 4
GDN_CONV_PIECE = 512
GDN_GROUP_HEADS = 8
GDN_UNIT_LAG = 4
SWA_HEADS = 16
SWA_KV_HEADS = 4
SWA_HEAD_DIM = 64
SWA_WINDOW = 128
SWA_BLOCK = 128
SWA_STEP_BLOCKS = 8
SWA_LOGITS_AHEAD = 4
SWA_Q = SWA_HEADS * SWA_HEAD_DIM
SWA_KV = SWA_KV_HEADS * SWA_HEAD_DIM
REL_BUCKETS = 32
REL_MAX_DIST = 128
EPS = 1e-6
NEG_BIG = -1e30
LOG2E = math.log2(math.e)

COL_QKV = 0
COL_ZA = GDN_CONV_CH
COL_QB = COL_ZA + GDN_V
COL_ZB = COL_QB + SWA_Q
COL_GA = COL_ZB + SWA_Q
COL_GB = COL_GA + D_MODEL
COL_KB = COL_GB + D_MODEL
COL_VB = COL_KB + SWA_KV
PROJ_COLS = COL_VB + SWA_KV
for _col, _width in ((COL_ZA, GDN_V), (COL_QB, SWA_Q), (COL_ZB, SWA_Q), (COL_GA, D_MODEL),
                     (COL_GB, D_MODEL), (COL_KB, SWA_KV), (COL_VB, SWA_KV)):
    assert _col % _width == 0

VMEM_LIMIT = 56 * 1024 * 1024


def _sigmoid(x):
    return 0.5 + 0.5 * jnp.tanh(0.5 * x)


def _silu(x):
    half = 0.5 * x
    return half + half * jnp.tanh(half)


def _params(sem):
    return pltpu.CompilerParams(dimension_semantics=sem, vmem_limit_bytes=VMEM_LIMIT)


def _mm(a, b):
    return jnp.dot(a.astype(BF16), b.astype(BF16), preferred_element_type=F32)


def _mm_nt(a, b):
    return lax.dot_general(a.astype(BF16), b.astype(BF16), (((1,), (1,)), ((), ())),
                           preferred_element_type=F32)


def _mm_tn(a, b):
    return lax.dot_general(a.astype(BF16), b.astype(BF16), (((0,), (0,)), ((), ())),
                           preferred_element_type=F32)


def _swa_bias_head(tab_ref, sink_ref, h):
    q = SWA_BLOCK
    qpos = lax.broadcasted_iota(jnp.int32, (q, 2 * q), 0) + q
    kpos = lax.broadcasted_iota(jnp.int32, (q, 2 * q), 1)
    dist = qpos - kpos
    in_window = (dist >= 0) & (dist < SWA_WINDOW)
    d = jnp.maximum(dist, 0)
    max_exact = REL_BUCKETS // 2
    df = jnp.maximum(d, 1).astype(F32)
    large = max_exact + (jnp.log(df / max_exact) / math.log(REL_MAX_DIST / max_exact)
                         * (REL_BUCKETS - max_exact)).astype(jnp.int32)
    large = jnp.minimum(large, REL_BUCKETS - 1)
    bucket = jnp.where(in_window, jnp.where(d < max_exact, d, large), REL_BUCKETS)
    acc = jnp.full((q, 2 * q), NEG_BIG, F32)
    for b in range(REL_BUCKETS):
        acc = jnp.where(bucket == b, tab_ref[h, b], acc)
    return jnp.where(kpos == 0, sink_ref[h], acc) * LOG2E


def _ada_mod_kernel(c_ref, w_ref, b_ref, tab_ref, sink_ref, o_ref, bias_ref):
    c = c_ref[...]
    o_ref[...] = jnp.sum(_silu(c) * w_ref[...], axis=0, keepdims=True) + b_ref[...]
    per_step = bias_ref.shape[0]
    group = jnp.minimum(pl.program_id(0), SWA_HEADS // per_step - 1)
    for p in range(per_step):
        bias_ref[p] = _swa_bias_head(tab_ref, sink_ref, group * per_step + p).astype(bias_ref.dtype)


def _ada_mod(c_col, w_ada, b_ada, rel_bias_t, sinks):
    d, n = w_ada.shape
    tn = 1024
    steps = n // tn
    per_step = next(p for p in (1, 2, 4, 8, 16) if SWA_HEADS // p <= steps)
    return pl.pallas_call(
        _ada_mod_kernel,
        out_shape=(jax.ShapeDtypeStruct((1, n), F32),
                   jax.ShapeDtypeStruct((SWA_HEADS, SWA_BLOCK, 2 * SWA_BLOCK), BF16)),
        grid=(steps,),
        in_specs=[pl.BlockSpec((d, 1), lambda j: (0, 0)),
                  pl.BlockSpec((d, tn), lambda j: (0, j)),
                  pl.BlockSpec((1, tn), lambda j: (0, j)),
                  pl.BlockSpec(memory_space=pltpu.SMEM),
                  pl.BlockSpec(memory_space=pltpu.SMEM)],
        out_specs=(pl.BlockSpec((1, tn), lambda j: (0, j)),
                   pl.BlockSpec((per_step, SWA_BLOCK, 2 * SWA_BLOCK),
                                lambda j: (jnp.minimum(j, SWA_HEADS // per_step - 1), 0, 0))),
        compiler_params=_params(("arbitrary",)),
        name="ada_mod",
    )(c_col, w_ada, b_ada, rel_bias_t, sinks)


def _in_proj_kernel(x_ref, gain_ref, scale_ref, shift_ref, w_ref, ws_ref, *rest, row_chunk, n_side):
    side_in, (o_ref, ba_ref), side_out, h_ref = (rest[:n_side], rest[n_side:n_side + 2],
                                                 rest[n_side + 2:-1], rest[-1])
    for src, dst in zip(side_in, side_out):
        dst[...] = src[...].astype(dst.dtype)
    j = pl.program_id(1)

    @pl.when(j == 0)
    def _():
        gs = gain_ref[...] * (1.0 + scale_ref[...])
        sh = shift_ref[...]
        tm = x_ref.shape[0]
        for r in range(tm // row_chunk):
            rows = slice(r * row_chunk, (r + 1) * row_chunk)
            x = x_ref[rows, :]
            ms = jnp.mean(x * x, axis=-1, keepdims=True)
            h = ((x * lax.rsqrt(ms + EPS)) * gs + sh).astype(BF16)
            h_ref[rows, :] = h
            ba_ref[rows, :] = _mm_nt(h, ws_ref[...])

    o_ref[...] = _mm_nt(h_ref[...], w_ref[...]).astype(o_ref.dtype)


def _in_proj(x2d, gain, scale, shift, w_big, w_small, side_weights, *, tm, tn):
    t, d = x2d.shape
    n = w_big.shape[0]
    n_i, n_j = t // tm, n // tn
    slabs = 1 << ((n_i * n_j).bit_length() - 1)
    slab_of = lambda i, j: jnp.minimum(i * n_j + j, slabs - 1)
    side_specs = [pl.BlockSpec((w.shape[0] // slabs, w.shape[1]), lambda i, j: (slab_of(i, j), 0))
                  for w in side_weights]
    assert all(w.shape[0] % (16 * slabs) == 0 for w in side_weights)
    out = pl.pallas_call(
        functools.partial(_in_proj_kernel, row_chunk=128, n_side=len(side_weights)),
        out_shape=(jax.ShapeDtypeStruct((t, n), BF16), jax.ShapeDtypeStruct((t, LANES), F32),
                   *[jax.ShapeDtypeStruct(w.shape, BF16) for w in side_weights]),
        grid=(n_i, n_j),
        in_specs=[pl.BlockSpec((tm, d), lambda i, j: (i, 0)),
                  pl.BlockSpec((1, d), lambda i, j: (0, 0)),
                  pl.BlockSpec((1, d), lambda i, j: (0, 0)),
                  pl.BlockSpec((1, d), lambda i, j: (0, 0)),
                  pl.BlockSpec((tn, d), lambda i, j: (j, 0)),
                  pl.BlockSpec((LANES, d), lambda i, j: (0, 0)),
                  *side_specs],
        out_specs=(pl.BlockSpec((tm, tn), lambda i, j: (i, j)),
                   pl.BlockSpec((tm, LANES), lambda i, j: (i, 0)),
                   *side_specs),
        scratch_shapes=[pltpu.VMEM((tm, d), BF16)],
        compiler_params=_params(("arbitrary", "arbitrary")),
        name="in_proj",
    )(x2d, gain, scale, shift, w_big, w_small, *side_weights)
    return out[0], out[1], out[2:]


def _chunk_cumsum_rows(x):
    n = x.shape[0]
    row = lax.broadcasted_iota(jnp.int32, x.shape, 0)
    s = 1
    while s < n:
        x = x + jnp.where(row >= s, pltpu.roll(x, s, axis=0), 0.0)
        s *= 2
    return x


def _unit_lower_inverse_stages(l_mats, masks_ref, out):
    eye = masks_ref[0].astype(F32)
    l_bf = [l.astype(BF16) for l in l_mats]
    m0 = [l * masks_ref[1] for l in l_bf]
    x = [eye - m for m in m0]
    p = [_mm(m, m) for m in m0]
    yield
    x = [xi + _mm(xi, pi) for xi, pi in zip(x, p)]
    yield
    p = [_mm(pi, pi) for pi in p]
    yield
    x = [xi + _mm(xi, pi) for xi, pi in zip(x, p)]
    yield
    for lvl in range(2, masks_ref.shape[0]):
        nx = [_mm(l * masks_ref[lvl], xi) for l, xi in zip(l_bf, x)]
        yield
        x = [xi - _mm(xi, ni) for xi, ni in zip(x, nx)]
        yield
    out.extend(x)


def _gdn_kernel(cur_ref, prev_ref, z_ref, ba_ref, convw_ref, band_ref, alog_ref, dtb_ref, gain_ref,
                masks_ref, o_ref, s_ref):
    s = pl.program_id(0)
    c = GDN_BLOCK
    heads = range(GDN_HEADS)
    blocks = range(GDN_STEP_BLOCKS)
    items = [(b, h) for b in blocks for h in heads]
    lane_of = lambda a, h: a[:, GDN_HEADS + h:GDN_HEADS + h + 1]

    @pl.when(s == 0)
    def _():
        s_ref[...] = jnp.zeros_like(s_ref)

    pad = prev_ref.shape[0]
    conv_w = convw_ref[...].astype(BF16)
    conv_pieces = {}
    for b in blocks:
        if b == 0:
            ctx = jnp.where(s > 0, prev_ref[...], jnp.zeros_like(prev_ref))
        else:
            ctx = cur_ref[b * c - pad:b * c, :]
        xcat = jnp.concatenate([ctx, cur_ref[b * c:(b + 1) * c, :]], axis=0)
        taps = jnp.concatenate([xcat * conv_w[i:i + 1, :] for i in range(GDN_CONV)], axis=0)
        for p in range(GDN_CONV_CH // GDN_CONV_PIECE):
            cols = slice(p * GDN_CONV_PIECE, (p + 1) * GDN_CONV_PIECE)
            conv_pieces[b, p] = jnp.dot(band_ref[...], taps[:, cols], preferred_element_type=F32)

    row = lax.broadcasted_iota(jnp.int32, (c, c), 0)
    col = lax.broadcasted_iota(jnp.int32, (c, c), 1)
    causal = row >= col

    def l2n(a, scale=1.0):
        return a * (lax.rsqrt(jnp.sum(a * a, axis=-1, keepdims=True) + EPS) * scale)

    q, k, kb, vb, kbg, qd, kd, dec, el = ({} for _ in range(9))
    for b in blocks:
        ba = ba_ref[b * c:(b + 1) * c, :]
        beta_all = _sigmoid(ba)
        xg = ba + dtb_ref[...]
        softplus = jnp.maximum(xg, 0.0) + jnp.log(1.0 + jnp.exp(-jnp.abs(xg)))
        g_all = -jnp.exp(alog_ref[...]) * softplus
        gc = _chunk_cumsum_rows(g_all)
        gc_t = gc.T
        g_last = gc[c - 1:c, :]
        eg_all = jnp.exp(gc)
        ekd_all = jnp.exp(g_last - gc)
        el[b] = jnp.exp(g_last)
        conv_tile = lambda col0: _silu(conv_pieces[b, col0 // GDN_CONV_PIECE][
            :, col0 % GDN_CONV_PIECE:col0 % GDN_CONV_PIECE + LANES])
        for h in heads:
            qf = l2n(conv_tile(h * GDN_DK), GDN_DK ** -0.5)
            kf = l2n(conv_tile(GDN_QK + h * GDN_DK))
            vf = conv_tile(2 * GDN_QK + h * GDN_DV)
            beta = beta_all[:, h:h + 1]
            eg = lane_of(eg_all, h)
            kbf = kf * beta
            gdiff = lane_of(gc, h) - gc_t[GDN_HEADS + h:GDN_HEADS + h + 1, :]
            q[b, h] = qf.astype(BF16)
            k[b, h] = kf.astype(BF16)
            kb[b, h] = kbf.astype(BF16)
            vb[b, h] = vf * beta
            kbg[b, h] = (kbf * eg).astype(BF16)
            qd[b, h] = (qf * eg).astype(BF16)
            kd[b, h] = (kf * lane_of(ekd_all, h)).astype(BF16)
            dec[b, h] = jnp.where(causal, jnp.exp(gdiff), 0.0)

    attn, t_inv, o_tiles = {}, {}, {}
    state = [s_ref[h] for h in heads]
    groups = [range(g, g + GDN_GROUP_HEADS) for g in range(0, GDN_HEADS, GDN_GROUP_HEADS)]

    def local_chain(b, hs):
        kk = {h: _mm_nt(kb[b, h], k[b, h]) for h in hs}
        yield
        qk = {h: _mm_nt(q[b, h], k[b, h]) for h in hs}
        yield
        l_mat = [kk[h] * dec[b, h] for h in hs]
        for h in hs:
            attn[b, h] = (qk[h] * dec[b, h]).astype(BF16)
        inv = []
        yield from _unit_lower_inverse_stages(l_mat, masks_ref, inv)
        for n, h in enumerate(hs):
            t_inv[b, h] = inv[n]

    def recurrence(b, hs):
        s_bf = {h: state[h].astype(BF16) for h in hs}
        resid = {h: vb[b, h] - _mm(kbg[b, h], s_bf[h]) for h in hs}
        yield
        v_new = {h: _mm(t_inv[b, h], resid[h]).astype(BF16) for h in hs}
        yield
        for h in hs:
            o_tiles[b, h] = _mm(jnp.concatenate([qd[b, h], attn[b, h]], axis=1),
                                jnp.concatenate([s_bf[h], v_new[h]], axis=0))
            state[h] = state[h] * lane_of(el[b], h) + _mm_tn(kd[b, h], v_new[h])
        yield
        for h in hs:
            o = o_tiles[b, h]
            on = o * lax.rsqrt(jnp.mean(o * o, axis=-1, keepdims=True) + EPS) * gain_ref[...]
            z = z_ref[b * c:(b + 1) * c, h * GDN_DV:(h + 1) * GDN_DV].astype(F32)
            o_ref[b * c:(b + 1) * c, h * GDN_DV:(h + 1) * GDN_DV] = (on * _silu(z)).astype(o_ref.dtype)

    def advance(gen):
        return next(gen, StopIteration) is not StopIteration

    units = [(b, g) for b in blocks for g in range(len(groups))]
    chains = [local_chain(b, groups[g]) for b, g in units]
    alive = [True] * len(units)
    recs = [[] for _ in groups]
    wave = 0
    while any(alive) or any(recs):
        for n, (b, g) in enumerate(units):
            if alive[n] and wave >= n * GDN_UNIT_LAG:
                alive[n] = advance(chains[n])
                if not alive[n]:
                    recs[g].append(recurrence(b, groups[g]))
        for pending in recs:
            if pending and not advance(pending[0]):
                pending.pop(0)
        wave += 1

    for h in heads:
        s_ref[h] = state[h]


def _gdn_masks():
    c = GDN_BLOCK
    r = np.arange(c)[:, None]
    k = np.arange(c)[None, :]
    mats = [np.eye(c), (r // 8 == k // 8) & (r > k)]
    b = 8
    while b < c:
        mats.append((r // (2 * b) == k // (2 * b)) & ((r // b) % 2 == 1) & ((k // b) % 2 == 0))
        b *= 2
    return jnp.asarray(np.stack([np.asarray(m, np.float32) for m in mats]), BF16)


def _conv_band(c, pad):
    band = np.zeros((c, GDN_CONV * (pad + c)), np.float32)
    for i in range(GDN_CONV):
        band[np.arange(c), i * (pad + c) + pad + np.arange(c) - (GDN_CONV - 1) + i] = 1.0
    return jnp.asarray(band, BF16)


def _gdn(proj, ba, conv_w, alog_lane, dtb_lane, gain):
    t = proj.shape[0]
    c = GDN_BLOCK
    rows = GDN_STEP_BLOCKS * c
    pad = 16
    masks = _gdn_masks()
    nm = masks.shape[0]
    band = _conv_band(c, pad)
    return pl.pallas_call(
        _gdn_kernel,
        out_shape=jax.ShapeDtypeStruct((t, GDN_V), BF16),
        grid=(t // rows,),
        in_specs=[pl.BlockSpec((rows, GDN_CONV_CH), lambda i: (i, 0)),
                  pl.BlockSpec((pad, GDN_CONV_CH), lambda i: (jnp.maximum(i * (rows // pad) - 1, 0), 0)),
                  pl.BlockSpec((rows, GDN_V), lambda i: (i, COL_ZA // GDN_V)),
                  pl.BlockSpec((rows, LANES), lambda i: (i, 0)),
                  pl.BlockSpec((GDN_CONV, GDN_CONV_CH), lambda i: (0, 0)),
                  pl.BlockSpec(band.shape, lambda i: (0, 0)),
                  pl.BlockSpec((1, LANES), lambda i: (0, 0)),
                  pl.BlockSpec((1, LANES), lambda i: (0, 0)),
                  pl.BlockSpec((1, GDN_DV), lambda i: (0, 0)),
                  pl.BlockSpec((nm, c, c), lambda i: (0, 0, 0))],
        out_specs=pl.BlockSpec((rows, GDN_V), lambda i: (i, 0)),
        scratch_shapes=[pltpu.VMEM((GDN_HEADS, GDN_DK, GDN_DV), F32)],
        compiler_params=_params(("arbitrary",)),
        name="gdn",
    )(proj, proj, proj, ba, conv_w, band, alog_lane, dtb_lane, gain, masks)


def _swa_kernel(q_ref, kc_ref, kp_ref, vc_ref, vp_ref, z_ref, bias_ref, qg_ref, kg_ref, seg_ref, o_ref):
    n = pl.program_id(0)
    qb = SWA_BLOCK
    hd = SWA_HEAD_DIM
    n_tiles = SWA_Q // LANES
    blocks = range(SWA_STEP_BLOCKS)
    items = [(j, tq) for j in blocks for tq in range(n_tiles)]
    lane = lax.broadcasted_iota(jnp.int32, (1, LANES), 1)
    lo = lane < hd
    seg = seg_ref[...].astype(BF16)

    def qk_norm(a, gain):
        ms = jnp.dot((a * a).astype(BF16), seg, preferred_element_type=F32)
        return a * (lax.rsqrt(ms + EPS) * gain)

    def two_heads(lo_part, hi_part):
        return jnp.concatenate([jnp.where(lo, lo_part, 0.0), jnp.where(lo, 0.0, hi_part)], axis=0)

    kall = jnp.concatenate([kp_ref[...], kc_ref[...]], axis=0).astype(F32)
    vall = jnp.concatenate([vp_ref[...], vc_ref[...]], axis=0).astype(F32)
    kn, vv = {}, {}
    for u in range(SWA_KV // LANES):
        knorm = qk_norm(kall[:, u * LANES:(u + 1) * LANES], kg_ref[...])
        vtile = vall[:, u * LANES:(u + 1) * LANES]
        for ci in range(SWA_STEP_BLOCKS + 1):
            kc = knorm[ci * qb:(ci + 1) * qb]
            vc = vtile[ci * qb:(ci + 1) * qb]
            kn[u, ci] = (kc, pltpu.roll(kc, hd, axis=1))
            vv[u, ci] = (vc, pltpu.roll(vc, hd, axis=1))

    row = lax.broadcasted_iota(jnp.int32, (qb, 1), 0)
    is_sink = row == 0
    k2s, rhs = {}, {}
    for j in blocks:
        prev_live = (n * SWA_STEP_BLOCKS + j) > 0
        ones_prev = jnp.where(prev_live | is_sink, 1.0, 0.0)
        ones_band = jnp.concatenate([ones_prev, jnp.ones_like(ones_prev)], axis=0)
        ones2 = two_heads(ones_band, ones_band)
        for u in range(SWA_KV // LANES):
            kband = [jnp.concatenate([jnp.where(is_sink, 0.0, kn[u, j][r]), kn[u, j + 1][r]], axis=0)
                     for r in range(2)]
            vband = [jnp.concatenate([jnp.where(prev_live & ~is_sink, vv[u, j][r], 0.0), vv[u, j + 1][r]],
                                     axis=0) for r in range(2)]
            for half in range(2):
                g = 2 * u + half
                k2s[j, g] = two_heads(kband[half], kband[1 - half]).astype(BF16)
                v2 = two_heads(vband[half], vband[1 - half])
                rhs[j, g] = jnp.concatenate([v2, ones2], axis=1).astype(BF16)

    q_gain = qg_ref[...] * (hd ** -0.5 * LOG2E)
    kv_of = lambda tq: (2 * tq) // (SWA_HEADS // SWA_KV_HEADS)
    qt = {(j, tq): qk_norm(q_ref[j * qb:(j + 1) * qb, tq * LANES:(tq + 1) * LANES].astype(F32), q_gain)
          for j, tq in items}
    logits = {}
    for n in range(len(items) + SWA_LOGITS_AHEAD):
        if n < len(items):
            j, tq = items[n]
            logits[j, tq] = _mm_nt(qt[j, tq], k2s[j, kv_of(tq)])
        if n >= SWA_LOGITS_AHEAD:
            j, tq = items[n - SWA_LOGITS_AHEAD]
            both = logits.pop((j, tq))
            ps = []
            for e in range(2):
                s = both[:, e * 2 * qb:(e + 1) * 2 * qb].astype(BF16) + bias_ref[2 * tq + e]
                ps.append(jnp.exp2(s - jnp.max(s, axis=-1, keepdims=True)))
            pv = jnp.dot(jnp.concatenate(ps, axis=1), rhs[j, kv_of(tq)], preferred_element_type=F32)
            out = pv[:, :LANES] * (1.0 / pv[:, LANES:])
            z = z_ref[j * qb:(j + 1) * qb, tq * LANES:(tq + 1) * LANES].astype(F32)
            o_ref[j * qb:(j + 1) * qb, tq * LANES:(tq + 1) * LANES] = (out * _silu(z)).astype(o_ref.dtype)


def _swa(proj, bias, qg2, kg2):
    t = proj.shape[0]
    qb = SWA_BLOCK
    rows = SWA_STEP_BLOCKS * qb
    seg = np.kron(np.eye(LANES // SWA_HEAD_DIM), np.ones((SWA_HEAD_DIM, SWA_HEAD_DIM))) / SWA_HEAD_DIM
    seg = jnp.asarray(seg, F32)
    kcol = COL_KB // SWA_KV
    vcol = COL_VB // SWA_KV
    prev = lambda i: jnp.maximum(i * SWA_STEP_BLOCKS - 1, 0)
    return pl.pallas_call(
        _swa_kernel,
        out_shape=jax.ShapeDtypeStruct((t, SWA_Q), BF16),
        grid=(t // rows,),
        in_specs=[pl.BlockSpec((rows, SWA_Q), lambda i: (i, COL_QB // SWA_Q)),
                  pl.BlockSpec((rows, SWA_KV), lambda i: (i, kcol)),
                  pl.BlockSpec((qb, SWA_KV), lambda i: (prev(i), kcol)),
                  pl.BlockSpec((rows, SWA_KV), lambda i: (i, vcol)),
                  pl.BlockSpec((qb, SWA_KV), lambda i: (prev(i), vcol)),
                  pl.BlockSpec((rows, SWA_Q), lambda i: (i, COL_ZB // SWA_Q)),
                  pl.BlockSpec((SWA_HEADS, qb, 2 * qb), lambda i: (0, 0, 0)),
                  pl.BlockSpec((1, LANES), lambda i: (0, 0)),
                  pl.BlockSpec((1, LANES), lambda i: (0, 0)),
                  pl.BlockSpec((LANES, LANES), lambda i: (0, 0))],
        out_specs=pl.BlockSpec((rows, SWA_Q), lambda i: (i, 0)),
        compiler_params=_params(("arbitrary",)),
        name="swa",
    )(proj, proj, proj, proj, proj, proj, bias, qg2, kg2, seg)


def _merge_out_kernel(x_ref, oa_ref, ob_ref, ga_ref, gb_ref, gate_ref, wa_ref, wb_ref, wo_ref, o_ref):
    ya = jnp.dot(oa_ref[...], wa_ref[...], preferred_element_type=F32)
    yb = jnp.dot(ob_ref[...], wb_ref[...], preferred_element_type=F32)
    mixed = _sigmoid(ga_ref[...].astype(F32)) * ya + _sigmoid(gb_ref[...].astype(F32)) * yb
    y = jnp.dot(mixed.astype(BF16), wo_ref[...], preferred_element_type=F32)
    o_ref[...] = x_ref[...] + gate_ref[...] * y


def _merge_out(x2d, o_a, o_b, proj, gate, w_a, w_b, w_o, *, tm):
    t, d = x2d.shape
    const = lambda shape: pl.BlockSpec(shape, lambda i: (0, 0), pipeline_mode=pl.Buffered(1))
    return pl.pallas_call(
        _merge_out_kernel,
        out_shape=jax.ShapeDtypeStruct((t, d), F32),
        grid=(t // tm,),
        in_specs=[pl.BlockSpec((tm, d), lambda i: (i, 0)),
                  pl.BlockSpec((tm, GDN_V), lambda i: (i, 0)),
                  pl.BlockSpec((tm, SWA_Q), lambda i: (i, 0)),
                  pl.BlockSpec((tm, d), lambda i: (i, COL_GA // D_MODEL)),
                  pl.BlockSpec((tm, d), lambda i: (i, COL_GB // D_MODEL)),
                  pl.BlockSpec((1, d), lambda i: (0, 0)),
                  const((GDN_V, d)), const((SWA_Q, d)), const((d, d))],
        out_specs=pl.BlockSpec((tm, d), lambda i: (i, 0)),
        compiler_params=_params(("arbitrary",)),
        name="merge_out",
    )(x2d, o_a, o_b, proj, proj, gate, w_a, w_b, w_o)


REPACK_TILE = 512
N_GATE_COLS = 2 * GDN_HEADS


def _repack_plan():
    src_of = {COL_QKV: 0, COL_ZA: GDN_CONV_CH}
    after_gates = COL_QB + N_GATE_COLS
    src_of.update({COL_QB: after_gates, COL_KB: after_gates + SWA_Q, COL_VB: after_gates + SWA_Q + SWA_KV,
                   COL_ZB: after_gates + SWA_Q + 2 * SWA_KV, COL_GA: after_gates + 2 * SWA_Q + 2 * SWA_KV,
                   COL_GB: after_gates + 2 * SWA_Q + 2 * SWA_KV + D_MODEL})
    starts = sorted(src_of)
    src = []
    for dst in range(0, PROJ_COLS, REPACK_TILE):
        grp = max(s for s in starts if s <= dst)
        src.append(src_of[grp] + dst - grp)
    return np.asarray(src, np.int32)


def _repack_kernel(tab_ref, w_ref, g_ref, big_ref, small_ref):
    del tab_ref
    big_ref[...] = w_ref[...].astype(BF16)

    @pl.when(pl.program_id(0) == 0)
    def _():
        row = lax.broadcasted_iota(jnp.int32, (LANES, 1), 0)
        small_ref[...] = jnp.where(row < N_GATE_COLS, g_ref[...], 0.0).astype(BF16)


def _repack_w_in(w_t):
    d = w_t.shape[1]
    src = _repack_plan()
    gate_row = COL_QB
    assert gate_row % LANES == 0 and np.all(src % N_GATE_COLS == 0)
    return pl.pallas_call(
        _repack_kernel,
        out_shape=(jax.ShapeDtypeStruct((PROJ_COLS, d), BF16), jax.ShapeDtypeStruct((LANES, d), BF16)),
        grid_spec=pltpu.PrefetchScalarGridSpec(
            num_scalar_prefetch=1,
            grid=(PROJ_COLS // REPACK_TILE,),
            in_specs=[pl.BlockSpec((pl.Element(REPACK_TILE), pl.Element(d)),
                                   lambda o, tab: (tab[o] * N_GATE_COLS, 0)),
                      pl.BlockSpec((LANES, d), lambda o, tab: (gate_row // LANES, 0))],
            out_specs=(pl.BlockSpec((REPACK_TILE, d), lambda o, tab: (o, 0)),
                       pl.BlockSpec((LANES, d), lambda o, tab: (0, 0)))),
        compiler_params=_params(("arbitrary",)),
        name="repack_w_in",
    )(jnp.asarray(src // N_GATE_COLS), w_t, w_t)


def _lane_row(vec, offset):
    return jnp.pad(vec.astype(F32), (offset, LANES - offset - vec.shape[0]))[None, :]


def kernel(x, c, w_ada, b_ada, norm_gain, w_in, conv_w, a_log, dt_bias, gdn_norm_gain, q_norm_gain,
           k_norm_gain, sinks, rel_bias, w_branch_gdn, w_branch_swa, w_out):
    bsz, t, d = x.shape
    depth = w_in.shape[0]
    outs = []
    for b in range(bsz):
        xb = x[b]
        c_col = c[b].astype(F32)[:, None]
        for l in range(depth):
            mod, bias = _ada_mod(c_col, w_ada[l], b_ada[l][None, :],
                                 rel_bias.T.astype(F32), sinks[l].astype(F32))
            shift, scale, gate = mod[:, :d], mod[:, d:2 * d], mod[:, 2 * d:]
            w_big, w_small = _repack_w_in(w_in[l].T)
            proj, ba, (w_a, w_b, w_o) = _in_proj(
                xb, norm_gain[l][None, :], scale, shift, w_big, w_small,
                (w_branch_gdn[l], w_branch_swa[l], w_out[l]), tm=min(1024, t), tn=1792)
            o_a = _gdn(proj, ba, conv_w[l], _lane_row(a_log[l], GDN_HEADS), _lane_row(dt_bias[l], GDN_HEADS),
                       gdn_norm_gain[l][None, :])
            o_b = _swa(proj, bias,
                       jnp.tile(q_norm_gain[l], LANES // SWA_HEAD_DIM)[None, :],
                       jnp.tile(k_norm_gain[l], LANES // SWA_HEAD_DIM)[None, :])
            xb = _merge_out(xb, o_a, o_b, proj, gate, w_a, w_b, w_o, tm=min(512, t))
        outs.append(xb)
    return jnp.stack(outs, axis=0)
```
